```python
import jax, jax.numpy as jnp
from jax import lax
import numpy as np

D_MODEL = 2048
BATCH = 2
SEQ = 4096
DEPTH = 2
DEC_BATCH = 4
DEC_SEQ = 2048
PAST_LEN = 128

HEAD_DIM = 128
NA_HEADS = D_MODEL // HEAD_DIM
NA_ROWS = 8
NA_COLS = 16
GRID_W = 64
DIL_CONFIGS = ((128, 1), (512, 4), (2048, 16))
DIL_GROUPS = len(DIL_CONFIGS)
DIL_HEADS = D_MODEL // (2 * HEAD_DIM)
DIL_WIDTH = DIL_HEADS * HEAD_DIM
D_FF = 11 * D_MODEL // 4
N_EXPERTS = 8
TOP_K = 2
D_FF_EXPERT = 7 * D_MODEL // 2
MOE_BLOCK = 256
ROPE_THETA = 10000.0
EPS = 1e-6
NEG_INF = -1e30
ATTN_SCALE = HEAD_DIM ** -0.5
EVEN_LAYERS = (DEPTH + 1) // 2
ODD_LAYERS = DEPTH // 2

kernel_name = "hybrid_natten_dilated_moe_encoder"


def rms_norm(x, gain):
    xf = x.astype(jnp.float32)
    y = xf * lax.rsqrt(jnp.mean(xf * xf, axis=-1, keepdims=True) + EPS)
    return (y * gain.astype(jnp.float32)).astype(x.dtype)


def rope_tables(seq_len):
    half = HEAD_DIM // 2
    inv_freq = ROPE_THETA ** (-jnp.arange(half, dtype=jnp.float32) / half)
    ang = jnp.arange(seq_len, dtype=jnp.float32)[:, None] * inv_freq[None, :]
    return jnp.cos(ang), jnp.sin(ang)


def rotary(x, cos, sin):
    half = HEAD_DIM // 2
    xf = x.astype(jnp.float32)
    x1, x2 = xf[..., :half], xf[..., half:]
    c = cos[:, None, None, :]
    s = sin[:, None, None, :]
    return jnp.concatenate([x1 * c - x2 * s, x2 * c + x1 * s], axis=-1).astype(x.dtype)


def swiglu(h, w_gate, w_up, w_down):
    return (jax.nn.silu(h @ w_gate) * (h @ w_up)) @ w_down


def neighborhood_attention(h, w_qkv, q_gain, k_gain, rpb, w_o):
    B, S, _ = h.shape
    rows = S // GRID_W
    kr = min(NA_ROWS, rows)
    qkv = (h @ w_qkv).reshape(B, S, 3, NA_HEADS, HEAD_DIM)
    q = rms_norm(qkv[:, :, 0], q_gain)
    k = rms_norm(qkv[:, :, 1], k_gain)
    v = qkv[:, :, 2]

    def to_grid(t):
        return t.reshape(B, rows, GRID_W, NA_HEADS, HEAD_DIM).transpose(0, 3, 1, 2, 4)

    qg, kg, vg = to_grid(q), to_grid(k), to_grid(v)
    r = jnp.arange(rows)
    row_idx = jnp.clip(r - kr // 2, 0, rows - kr)[:, None] + jnp.arange(kr)[None, :]
    kw = kg[:, :, row_idx].reshape(B, NA_HEADS, rows, kr * GRID_W, HEAD_DIM)
    vw = vg[:, :, row_idx].reshape(B, NA_HEADS, rows, kr * GRID_W, HEAD_DIM)

    c = jnp.arange(GRID_W)
    col_start = jnp.clip(c - NA_COLS // 2, 0, GRID_W - NA_COLS)
    col_mask = (c[None, :] >= col_start[:, None]) & (c[None, :] < col_start[:, None] + NA_COLS)
    mask = jnp.broadcast_to(col_mask[:, None, :], (GRID_W, kr, GRID_W)).reshape(GRID_W, kr * GRID_W)

    dr = row_idx - r[:, None]
    dc = jnp.clip(c[None, :] - c[:, None], 1 - NA_COLS, NA_COLS - 1)
    bias = rpb[:, dr[:, None, :, None] + NA_ROWS - 1, dc[None, :, None, :] + NA_COLS - 1]
    bias = bias.reshape(NA_HEADS, rows, GRID_W, kr * GRID_W).astype(jnp.float32)

    s = jnp.einsum('bhrqd,bhrnd->bhrqn', qg, kw, preferred_element_type=jnp.float32) * ATTN_SCALE + bias[None]
    s = jnp.where(mask, s, NEG_INF)
    p = jax.nn.softmax(s, axis=-1)
    o = jnp.einsum('bhrqn,bhrnd->bhrqd', p.astype(vw.dtype), vw)
    o = o.transpose(0, 2, 3, 1, 4).reshape(B, S, NA_HEADS * HEAD_DIM)
    return o @ w_o


def strided_band_attention(q, k, v, half, dil):
    B, S, H, hd = q.shape
    L = S // dil
    blk = half
    nb = -(-L // blk)
    Lp = nb * blk

    def to_sub(t):
        return t.reshape(B, L, dil, H, hd).transpose(0, 2, 3, 1, 4)

    zpad = ((0, 0), (0, 0), (0, 0))
    qs = jnp.pad(to_sub(q), zpad + ((0, Lp - L), (0, 0)))
    ks = jnp.pad(to_sub(k), zpad + ((blk, Lp - L + blk), (0, 0)))
    vs = jnp.pad(to_sub(v), zpad + ((blk, Lp - L + blk), (0, 0)))
    qb = qs.reshape(B, dil, H, nb, blk, hd)
    kb = ks.reshape(B, dil, H, nb + 2, blk, hd)
    vb = vs.reshape(B, dil, H, nb + 2, blk, hd)
    kw = jnp.concatenate([kb[:, :, :, :-2], kb[:, :, :, 1:-1], kb[:, :, :, 2:]], axis=-2)
    vw = jnp.concatenate([vb[:, :, :, :-2], vb[:, :, :, 1:-1], vb[:, :, :, 2:]], axis=-2)

    qpos = jnp.arange(nb)[:, None] * blk + jnp.arange(blk)[None, :]
    kpos = (jnp.arange(nb)[:, None] - 1) * blk + jnp.arange(3 * blk)[None, :]
    mask = (jnp.abs(qpos[:, :, None] - kpos[:, None, :]) <= half) & (kpos[:, None, :] >= 0) & (kpos[:, None, :] < L)

    s = jnp.einsum('bghnqd,bghnkd->bghnqk', qb, kw, preferred_element_type=jnp.float32) * ATTN_SCALE
    s = jnp.where(mask, s, NEG_INF)
    m = jnp.max(s, axis=-1)
    p = jnp.exp(s - m[..., None])
    den = jnp.sum(p, axis=-1)
    o = jnp.einsum('bghnqk,bghnkd->bghnqd', p.astype(vw.dtype), vw, preferred_element_type=jnp.float32) / den[..., None]
    lse = m + jnp.log(den)
    o = o.reshape(B, dil, H, Lp, hd)[:, :, :, :L].transpose(0, 3, 1, 2, 4).reshape(B, S, H, hd)
    lse = lse.reshape(B, dil, H, Lp)[:, :, :, :L].transpose(0, 3, 1, 2).reshape(B, S, H)
    return o, lse


def dilated_attention(h, w_qkv, q_gain, k_gain, w_o):
    B, S, _ = h.shape
    qkv = (h @ w_qkv).reshape(B, S, DIL_GROUPS, 3, DIL_HEADS, HEAD_DIM)
    q = rms_norm(qkv[:, :, :, 0], q_gain[:, None, :])
    k = rms_norm(qkv[:, :, :, 1], k_gain[:, None, :])
    v = qkv[:, :, :, 2]
    cos, sin = rope_tables(S)
    q = rotary(q, cos, sin)
    k = rotary(k, cos, sin)
    outs, lses = [], []
    for g, (window, dil) in enumerate(DIL_CONFIGS):
        o, lse = strided_band_attention(q[:, :, g], k[:, :, g], v[:, :, g], window // (2 * dil), dil)
        outs.append(o)
        lses.append(lse)
    wts = jax.nn.softmax(jnp.stack(lses), axis=0)
    o = jnp.sum(wts[..., None] * jnp.stack(outs), axis=0)
    return o.reshape(B, S, DIL_WIDTH).astype(h.dtype) @ w_o


def moe_swiglu(h, w_router, w_gate, w_up, w_down):
    B, S, D = h.shape
    xt = h.reshape(-1, D)
    n = xt.shape[0]
    nk = n * TOP_K
    logits = xt.astype(jnp.float32) @ w_router.astype(jnp.float32)
    top_logit, top_idx = lax.top_k(logits, TOP_K)
    gates = jax.nn.softmax(top_logit, axis=-1)
    e_flat = top_idx.reshape(-1).astype(jnp.int32)
    tok_flat = jnp.arange(nk, dtype=jnp.int32) // TOP_K
    g_flat = gates.reshape(-1)
    e_s, tok_s, g_s = lax.sort((e_flat, tok_flat, g_flat), num_keys=1, is_stable=True)

    counts = jnp.bincount(e_flat, length=N_EXPERTS)
    padded = (counts + MOE_BLOCK - 1) // MOE_BLOCK * MOE_BLOCK
    pad_end = jnp.cumsum(padded)
    pad_start = pad_end - padded
    start = jnp.cumsum(counts) - counts
    dest = pad_start[e_s] + jnp.arange(nk, dtype=jnp.int32) - start[e_s]
    n_blocks = -(-nk // MOE_BLOCK) + N_EXPERTS
    xs = jnp.zeros((n_blocks * MOE_BLOCK, D), h.dtype).at[dest].set(xt[tok_s])
    block_expert = jnp.minimum(
        jnp.searchsorted(pad_end, jnp.arange(n_blocks) * MOE_BLOCK, side='right'), N_EXPERTS - 1)

    def expert_block(args):
        xb, e = args
        return swiglu(xb, w_gate[e], w_up[e], w_down[e])

    ys = lax.map(expert_block, (xs.reshape(n_blocks, MOE_BLOCK, D), block_expert)).reshape(-1, D)
    out = jnp.zeros((n, D), jnp.float32).at[tok_s].add(ys[dest].astype(jnp.float32) * g_s[:, None])
    return out.astype(h.dtype).reshape(B, S, D)


def trunk(x, norm_mix, norm_ffn, na_w_qkv, na_q_gain, na_k_gain, na_rpb, na_w_o,
          da_w_qkv, da_q_gain, da_k_gain, da_w_o, ffn_w_gate, ffn_w_up, ffn_w_down,
          moe_w_router, moe_w_gate, moe_w_up, moe_w_down):
    for i in range(DEPTH):
        j = i // 2
        h = rms_norm(x, norm_mix[i])
        if i % 2 == 0:
            x = x + neighborhood_attention(h, na_w_qkv[j], na_q_gain[j], na_k_gain[j], na_rpb[j], na_w_o[j])
        else:
            x = x + dilated_attention(h, da_w_qkv[j], da_q_gain[j], da_k_gain[j], da_w_o[j])
        h = rms_norm(x, norm_ffn[i])
        if i % 2 == 0:
            x = x + swiglu(h, ffn_w_gate[j], ffn_w_up[j], ffn_w_down[j])
        else:
            x = x + moe_swiglu(h, moe_w_router[j], moe_w_gate[j], moe_w_up[j], moe_w_down[j])
    return x


def setup_inputs(seed: int = 0) -> dict:
    key = jax.random.key(seed)
    ks = jax.random.split(key, 24)
    f32 = jnp.float32

    def nrm(k, shape, fan_in):
        return jax.random.normal(k, shape, f32) * fan_in ** -0.5

    def gain(k, shape):
        return 1.0 + 0.05 * jax.random.normal(k, shape, f32)

    return {
        "x_prompt": jax.random.normal(ks[0], (BATCH, SEQ, D_MODEL), f32),
        "x_sample": jax.random.normal(ks[1], (DEC_BATCH, DEC_SEQ, D_MODEL), f32),
        "norm_mix": gain(ks[2], (DEPTH, D_MODEL)),
        "norm_ffn": gain(ks[3], (DEPTH, D_MODEL)),
        "na_w_qkv": nrm(ks[4], (EVEN_LAYERS, D_MODEL, 3 * NA_HEADS * HEAD_DIM), D_MODEL),
        "na_q_gain": gain(ks[5], (EVEN_LAYERS, HEAD_DIM)),
        "na_k_gain": gain(ks[6], (EVEN_LAYERS, HEAD_DIM)),
        "na_rpb": 0.02 * jax.random.normal(ks[7], (EVEN_LAYERS, NA_HEADS, 2 * NA_ROWS - 1, 2 * NA_COLS - 1), f32),
        "na_w_o": nrm(ks[8], (EVEN_LAYERS, NA_HEADS * HEAD_DIM, D_MODEL), NA_HEADS * HEAD_DIM),
        "da_w_qkv": nrm(ks[9], (ODD_LAYERS, D_MODEL, DIL_GROUPS * 3 * DIL_WIDTH), D_MODEL),
        "da_q_gain": gain(ks[10], (ODD_LAYERS, DIL_GROUPS, HEAD_DIM)),
        "da_k_gain": gain(ks[11], (ODD_LAYERS, DIL_GROUPS, HEAD_DIM)),
        "da_w_o": nrm(ks[12], (ODD_LAYERS, DIL_WIDTH, D_MODEL), DIL_WIDTH),
        "ffn_w_gate": nrm(ks[13], (EVEN_LAYERS, D_MODEL, D_FF), D_MODEL),
        "ffn_w_up": nrm(ks[14], (EVEN_LAYERS, D_MODEL, D_FF), D_MODEL),
        "ffn_w_down": nrm(ks[15], (EVEN_LAYERS, D_FF, D_MODEL), D_FF),
        "moe_w_router": nrm(ks[16], (ODD_LAYERS, D_MODEL, N_EXPERTS), D_MODEL),
        "moe_w_gate": nrm(ks[17], (ODD_LAYERS, N_EXPERTS, D_MODEL, D_FF_EXPERT), D_MODEL),
        "moe_w_up": nrm(ks[18], (ODD_LAYERS, N_EXPERTS, D_MODEL, D_FF_EXPERT), D_MODEL),
        "moe_w_down": nrm(ks[19], (ODD_LAYERS, N_EXPERTS, D_FF_EXPERT, D_MODEL), D_FF_EXPERT),
    }


def reference(x_prompt, x_sample, norm_mix, norm_ffn, na_w_qkv, na_q_gain, na_k_gain, na_rpb, na_w_o,
              da_w_qkv, da_q_gain, da_k_gain, da_w_o, ffn_w_gate, ffn_w_up, ffn_w_down,
              moe_w_router, moe_w_gate, moe_w_up, moe_w_down):
    y_prompt = trunk(x_prompt, norm_mix, norm_ffn, na_w_qkv, na_q_gain, na_k_gain, na_rpb, na_w_o,
                     da_w_qkv, da_q_gain, da_k_gain, da_w_o, ffn_w_gate, ffn_w_up, ffn_w_down,
                     moe_w_router, moe_w_gate, moe_w_up, moe_w_down)
    y_sample = trunk(x_sample, norm_mix, norm_ffn, na_w_qkv, na_q_gain, na_k_gain, na_rpb, na_w_o,
                     da_w_qkv, da_q_gain, da_k_gain, da_w_o, ffn_w_gate, ffn_w_up, ffn_w_down,
                     moe_w_router, moe_w_gate, moe_w_up, moe_w_down)
    return (y_prompt, y_sample)
```

```python
import functools

import jax
import jax.numpy as jnp
from jax import lax
from jax.experimental import pallas as pl
from jax.experimental.pallas import tpu as pltpu

F32 = jnp.float32
BF16 = jnp.bfloat16

HEAD_DIM = 128
GRID_W = 64
NA_ROWS = 8
NA_COLS = 16
NA_GROUP = NA_ROWS // 2
NA_KEY_ROWS = NA_GROUP + NA_ROWS - 1
DIL_CONFIGS = ((128, 1), (512, 4), (2048, 16))
N_EXPERTS = 8
TOP_K = 2
ROPE_THETA = 10000.0
EPS = 1e-6
NEG_INF = -1e30
ATTN_SCALE = HEAD_DIM ** -0.5

LANES = 128
VMEM_LIMIT = 52 * 1024 * 1024

TOKEN_BLOCK = 1024
FFN_TOKEN_BLOCK = 512
FFN_F_BLOCK = 512
BAND_Q_BLOCK = 128
MOE_BLOCK = 512
MOE_F_BLOCK = 512
COMBINE_BLOCK = 256


def _params(*sem):
    return pltpu.CompilerParams(dimension_semantics=sem, vmem_limit_bytes=VMEM_LIMIT)


def _rms_rows(x, gain):
    ms = jnp.mean(x * x, axis=-1, keepdims=True)
    return x * lax.rsqrt(ms + EPS) * gain


def _qkv_kernel(*refs, v_every, v_from, rope):
    if rope:
        x_ref, gain_ref, w_ref, hg_ref, cos_ref, sin_ref, o_ref, h_ref = refs
    else:
        x_ref, gain_ref, w_ref, hg_ref, o_ref, h_ref = refs
    j = pl.program_id(1)

    @pl.when(j == 0)
    def _():
        h_ref[...] = _rms_rows(x_ref[...], gain_ref[...]).astype(BF16)

    acc = jnp.dot(h_ref[...], w_ref[...], preferred_element_type=F32)
    is_qk = (j % v_every) < v_from

    @pl.when(is_qk)
    def _():
        for h in range(acc.shape[1] // HEAD_DIM):
            sl = slice(h * HEAD_DIM, (h + 1) * HEAD_DIM)
            y = _rms_rows(acc[:, sl], hg_ref[0])
            if rope:
                y = y * cos_ref[...] + pltpu.roll(y, HEAD_DIM // 2, 1) * sin_ref[...]
            o_ref[:, sl] = y.astype(o_ref.dtype)

    @pl.when(jnp.logical_not(is_qk))
    def _():
        o_ref[...] = acc.astype(o_ref.dtype)


def _qkv_project(x, gain, w, head_gain, v_every, v_from, rope_tabs=None, pos_map=None):
    n, d = x.shape
    n_out = w.shape[1]
    tm = min(TOKEN_BLOCK, n)
    tn = n_out // head_gain.shape[0]
    in_specs = [
        pl.BlockSpec((tm, d), lambda i, j: (i, 0)),
        pl.BlockSpec((1, d), lambda i, j: (0, 0)),
        pl.BlockSpec((d, tn), lambda i, j: (0, j)),
        pl.BlockSpec((1, 1, HEAD_DIM), lambda i, j: (j, 0, 0)),
    ]
    args = [x, gain.reshape(1, d), w, head_gain]
    if rope_tabs is not None:
        in_specs += [pl.BlockSpec((tm, HEAD_DIM), lambda i, j: (pos_map(i), 0))] * 2
        args += list(rope_tabs)
    return pl.pallas_call(
        functools.partial(_qkv_kernel, v_every=v_every, v_from=v_from, rope=rope_tabs is not None),
        grid=(n // tm, n_out // tn),
        in_specs=in_specs,
        out_specs=pl.BlockSpec((tm, tn), lambda i, j: (i, j)),
        out_shape=jax.ShapeDtypeStruct((n, n_out), BF16),
        scratch_shapes=[pltpu.VMEM((tm, d), BF16)],
        compiler_params=_params("arbitrary", "arbitrary"),
        name="qkv_project",
    )(*args)


def _na_bias_tables(rpb):
    half = NA_ROWS // 2
    qi = jnp.arange(NA_GROUP)[:, None]
    kj = jnp.arange(NA_KEY_ROWS)[None, :]
    dr = jnp.stack([kj - qi, kj - half - qi, kj - qi - (NA_KEY_ROWS - NA_GROUP)])
    rvalid = jnp.stack([
        jnp.broadcast_to(kj < NA_ROWS, (NA_GROUP, NA_KEY_ROWS)),
        (kj - qi >= 0) & (kj - qi < NA_ROWS),
        jnp.broadcast_to(kj >= NA_KEY_ROWS - NA_ROWS, (NA_GROUP, NA_KEY_ROWS)),
    ])
    c = jnp.arange(GRID_W)
    cs = jnp.clip(c - NA_COLS // 2, 0, GRID_W - NA_COLS)
    cvalid = (c[None, :] >= cs[:, None]) & (c[None, :] < cs[:, None] + NA_COLS)
    dc = jnp.clip(c[None, :] - c[:, None], 1 - NA_COLS, NA_COLS - 1)
    dri = jnp.clip(dr, 1 - NA_ROWS, NA_ROWS - 1) + NA_ROWS - 1
    bias = rpb[:, dri[:, :, None, :, None], (dc + NA_COLS - 1)[None, None, :, None, :]]
    valid = rvalid[:, :, None, :, None] & cvalid[None, None, :, None, :]
    bias = jnp.where(valid[None], bias.astype(F32), NEG_INF)
    h = rpb.shape[0]
    return bias.transpose(1, 0, 2, 3, 4, 5).reshape(3, h, NA_GROUP * GRID_W, NA_KEY_ROWS * GRID_W)


def _na_kernel(q_ref, k_ref, v_ref, tab_ref, prev_ref, o_ref, *, rows):
    del prev_ref
    g = pl.program_id(2)
    ks = jnp.clip(g * NA_GROUP - NA_ROWS // 2, 0, rows - NA_KEY_ROWS)
    start = pl.multiple_of(ks * GRID_W, GRID_W)
    k = k_ref[pl.ds(start, NA_KEY_ROWS * GRID_W), :]
    v = v_ref[pl.ds(start, NA_KEY_ROWS * GRID_W), :]
    s = lax.dot_general(q_ref[...], k, (((1,), (1,)), ((), ())), preferred_element_type=F32)
    s = s * ATTN_SCALE + tab_ref[0, 0]
    m = jnp.max(s, axis=-1, keepdims=True)
    p = jnp.exp(s - m)
    den = jnp.sum(p, axis=-1, keepdims=True)
    o = jnp.dot(p.astype(BF16), v, preferred_element_type=F32) / den
    o_ref[...] = o.astype(o_ref.dtype)


def _na_attention(qkv, tables, out_prev, tok_off, batch, seq, heads):
    rows = seq // GRID_W
    groups = rows // NA_GROUP
    assert rows % NA_GROUP == 0 and rows >= NA_KEY_ROWS and tok_off % seq == 0
    gq = NA_GROUP * GRID_W
    seq0 = tok_off // seq
    blk0 = tok_off // gq

    def variant(g):
        return jnp.where(g == 0, 0, jnp.where(g == groups - 1, 2, 1))

    return pl.pallas_call(
        functools.partial(_na_kernel, rows=rows),
        grid=(batch, heads, groups),
        in_specs=[
            pl.BlockSpec((gq, HEAD_DIM), lambda b, h, g: (blk0 + b * groups + g, h)),
            pl.BlockSpec((seq, HEAD_DIM), lambda b, h, g: (seq0 + b, heads + h)),
            pl.BlockSpec((seq, HEAD_DIM), lambda b, h, g: (seq0 + b, 2 * heads + h)),
            pl.BlockSpec((1, 1, gq, NA_KEY_ROWS * GRID_W), lambda b, h, g: (variant(g), h, 0, 0)),
            pl.BlockSpec(memory_space=pl.ANY),
        ],
        out_specs=pl.BlockSpec((gq, HEAD_DIM), lambda b, h, g: (blk0 + b * groups + g, h)),
        out_shape=jax.ShapeDtypeStruct(out_prev.shape, out_prev.dtype),
        input_output_aliases={4: 0},
        compiler_params=_params("arbitrary", "arbitrary", "arbitrary"),
        name="na_attention",
    )(qkv, qkv, qkv, tables, out_prev)


def _proj_residual_kernel(a_ref, w_ref, x_ref, o_ref):
    o_ref[...] = x_ref[...] + jnp.dot(a_ref[...], w_ref[...], preferred_element_type=F32)


def _proj_residual(a, w, x):
    n, k = a.shape
    d = w.shape[1]
    tm = min(TOKEN_BLOCK, n)
    tn = min(1024, d)
    return pl.pallas_call(
        _proj_residual_kernel,
        grid=(n // tm, d // tn),
        in_specs=[
            pl.BlockSpec((tm, k), lambda i, j: (i, 0)),
            pl.BlockSpec((k, tn), lambda i, j: (0, j)),
            pl.BlockSpec((tm, tn), lambda i, j: (i, j)),
        ],
        out_specs=pl.BlockSpec((tm, tn), lambda i, j: (i, j)),
        out_shape=jax.ShapeDtypeStruct((n, d), F32),
        compiler_params=_params("arbitrary", "arbitrary"),
        name="proj_residual",
    )(a, w, x)


def _silu_mul(g, u):
    return g * (1.0 / (1.0 + jnp.exp(-g))) * u


def _ffn_kernel(x_ref, gain_ref, wg_ref, wu_ref, wd_ref, o_ref, h_ref, acc_ref):
    j = pl.program_id(1)

    @pl.when(j == 0)
    def _():
        h_ref[...] = _rms_rows(x_ref[...], gain_ref[...]).astype(BF16)
        acc_ref[...] = jnp.zeros_like(acc_ref)

    h = h_ref[...]
    g = jnp.dot(h, wg_ref[...], preferred_element_type=F32)
    u = jnp.dot(h, wu_ref[...], preferred_element_type=F32)
    acc_ref[...] += jnp.dot(_silu_mul(g, u).astype(BF16), wd_ref[...], preferred_element_type=F32)

    @pl.when(j == pl.num_programs(1) - 1)
    def _():
        o_ref[...] = x_ref[...] + acc_ref[...]


def _dense_ffn(x, gain, w_gate, w_up, w_down):
    n, d = x.shape
    f = w_gate.shape[1]
    tm = min(FFN_TOKEN_BLOCK, n)
    tf = FFN_F_BLOCK
    assert f % tf == 0
    return pl.pallas_call(
        _ffn_kernel,
        grid=(n // tm, f // tf),
        in_specs=[
            pl.BlockSpec((tm, d), lambda i, j: (i, 0)),
            pl.BlockSpec((1, d), lambda i, j: (0, 0)),
            pl.BlockSpec((d, tf), lambda i, j: (0, j)),
            pl.BlockSpec((d, tf), lambda i, j: (0, j)),
            pl.BlockSpec((tf, d), lambda i, j: (j, 0)),
        ],
        out_specs=pl.BlockSpec((tm, d), lambda i, j: (i, 0)),
        out_shape=jax.ShapeDtypeStruct((n, d), F32),
        scratch_shapes=[pltpu.VMEM((tm, d), BF16), pltpu.VMEM((tm, d), F32)],
        compiler_params=_params("arbitrary", "arbitrary"),
        name="dense_ffn",
    )(x, gain.reshape(1, d), w_gate, w_up, w_down)


def _band_kernel(q_ref, k_ref, v_ref, o_prev, lse_prev, o_ref, lse_ref, *, length, tq, kw, half):
    del o_prev, lse_prev
    h = pl.program_id(2)
    lane = lax.broadcasted_iota(jnp.int32, (tq, LANES), 1)

    @pl.when(h == 0)
    def _():
        lse_ref[...] = jnp.zeros_like(lse_ref)

    def body(t, carry):
        q0 = pl.multiple_of(t * tq, tq)
        ks = pl.multiple_of(jnp.clip(q0 - half, 0, length - kw), half)
        q = q_ref[pl.ds(q0, tq), :]
        k = k_ref[pl.ds(ks, kw), :]
        v = v_ref[pl.ds(ks, kw), :]
        s = lax.dot_general(q, k, (((1,), (1,)), ((), ())), preferred_element_type=F32) * ATTN_SCALE
        qpos = q0 + lax.broadcasted_iota(jnp.int32, (tq, kw), 0)
        kpos = ks + lax.broadcasted_iota(jnp.int32, (tq, kw), 1)
        s = jnp.where(jnp.abs(qpos - kpos) <= half, s, NEG_INF)
        m = jnp.max(s, axis=-1, keepdims=True)
        p = jnp.exp(s - m)
        den = jnp.sum(p, axis=-1, keepdims=True)
        o = jnp.dot(p.astype(BF16), v, preferred_element_type=F32) / den
        o_ref[pl.ds(q0, tq), :] = o
        lse = m + jnp.log(den)
        lse_ref[pl.ds(q0, tq), :] = jnp.where(lane == h, lse, lse_ref[pl.ds(q0, tq), :])
        return carry

    lax.fori_loop(0, length // tq, body, 0)


def _band_attention(qkv, o_prev, lse_prev, tok_off, batch, seq, group, heads):
    window, dil = DIL_CONFIGS[group]
    half = window // (2 * dil)
    length = seq // dil
    assert seq % dil == 0 and length % half == 0 and tok_off % seq == 0
    tq = min(BAND_Q_BLOCK, length)
    kw = min(tq + 2 * half, length)
    n, c = qkv.shape
    cb = c // HEAD_DIM
    gb = group * 3 * heads
    seq0 = tok_off // seq
    width = heads * HEAD_DIM
    qkv_v = qkv.reshape(n // dil, dil * c)
    o_v = o_prev.reshape(n // dil, dil * width)
    lse_v = lse_prev.reshape(n // dil, dil * LANES)

    def col(which):
        return lambda b, r, h: (seq0 + b, r * cb + gb + which * heads + h)

    o, lse = pl.pallas_call(
        functools.partial(_band_kernel, length=length, tq=tq, kw=kw, half=half),
        grid=(batch, dil, heads),
        in_specs=[
            pl.BlockSpec((length, HEAD_DIM), col(0)),
            pl.BlockSpec((length, HEAD_DIM), col(1)),
            pl.BlockSpec((length, HEAD_DIM), col(2)),
            pl.BlockSpec(memory_space=pl.ANY),
            pl.BlockSpec(memory_space=pl.ANY),
        ],
        out_specs=[
            pl.BlockSpec((length, HEAD_DIM), lambda b, r, h: (seq0 + b, r * heads + h)),
            pl.BlockSpec((length, LANES), lambda b, r, h: (seq0 + b, r)),
        ],
        out_shape=[jax.ShapeDtypeStruct(o_v.shape, F32), jax.ShapeDtypeStruct(lse_v.shape, F32)],
        input_output_aliases={3: 0, 4: 1},
        compiler_params=_params("arbitrary", "arbitrary", "arbitrary"),
        name="band_attention",
    )(qkv_v, qkv_v, qkv_v, o_v, lse_v)
    return o.reshape(n, width), lse.reshape(n, LANES)


def _merge_proj_kernel(o0_ref, o1_ref, o2_ref, l0_ref, l1_ref, l2_ref, w_ref, x_ref, out_ref, a_ref):
    l0, l1, l2 = l0_ref[...], l1_ref[...], l2_ref[...]
    m = jnp.maximum(jnp.maximum(l0, l1), l2)
    e0, e1, e2 = jnp.exp(l0 - m), jnp.exp(l1 - m), jnp.exp(l2 - m)
    tot = e0 + e1 + e2
    w0, w1, w2 = e0 / tot, e1 / tot, e2 / tot
    for h in range(o0_ref.shape[1] // HEAD_DIM):
        sl = slice(h * HEAD_DIM, (h + 1) * HEAD_DIM)
        a = (w0[:, h:h + 1] * o0_ref[:, sl] + w1[:, h:h + 1] * o1_ref[:, sl]) + w2[:, h:h + 1] * o2_ref[:, sl]
        a_ref[:, sl] = a.astype(BF16)
    out_ref[...] = x_ref[...] + jnp.dot(a_ref[...], w_ref[...], preferred_element_type=F32)


def _merge_proj(outs, lses, w, x):
    n, width = outs[0].shape
    d = w.shape[1]
    tm = min(512, n)
    row = lambda i: (i, 0)
    return pl.pallas_call(
        _merge_proj_kernel,
        grid=(n // tm,),
        in_specs=[pl.BlockSpec((tm, width), row)] * 3 + [pl.BlockSpec((tm, LANES), row)] * 3 + [
            pl.BlockSpec((width, d), lambda i: (0, 0)),
            pl.BlockSpec((tm, d), row),
        ],
        out_specs=pl.BlockSpec((tm, d), row),
        out_shape=jax.ShapeDtypeStruct((n, d), F32),
        scratch_shapes=[pltpu.VMEM((tm, width), BF16)],
        compiler_params=_params("arbitrary"),
        name="merge_proj",
    )(*outs, *lses, w, x)


def _router_kernel(x_ref, gain_ref, wr_ref, h_ref, idx_ref, gate_ref):
    h = _rms_rows(x_ref[...], gain_ref[...])
    h_ref[...] = h
    logits = jnp.dot(h, wr_ref[...], preferred_element_type=F32, precision=lax.Precision.HIGHEST)
    lane = lax.broadcasted_iota(jnp.int32, logits.shape, 1)
    lane_f = lane.astype(F32)
    logits = jnp.where(lane < N_EXPERTS, logits, -jnp.inf)
    m1 = jnp.max(logits, axis=-1, keepdims=True)
    i1 = jnp.min(jnp.where(logits == m1, lane_f, float(LANES)), axis=-1, keepdims=True)
    rest = jnp.where(lane_f == i1, -jnp.inf, logits)
    m2 = jnp.max(rest, axis=-1, keepdims=True)
    i2 = jnp.min(jnp.where(rest == m2, lane_f, float(LANES)), axis=-1, keepdims=True)
    e = jnp.exp(m2 - m1)
    tot = 1.0 + e
    idx_ref[...] = jnp.where(lane == 0, i1, jnp.where(lane == 1, i2, 0.0)).astype(jnp.int32)
    gate_ref[...] = jnp.where(lane == 0, 1.0 / tot, jnp.where(lane == 1, e / tot, 0.0))


def _router(x, gain, w_router):
    n, d = x.shape
    tm = min(512, n)
    wr = jnp.zeros((d, LANES), F32).at[:, :N_EXPERTS].set(w_router.astype(F32))
    row = lambda i: (i, 0)
    return pl.pallas_call(
        _router_kernel,
        grid=(n // tm,),
        in_specs=[
            pl.BlockSpec((tm, d), row),
            pl.BlockSpec((1, d), lambda i: (0, 0)),
            pl.BlockSpec((d, LANES), lambda i: (0, 0)),
        ],
        out_specs=[pl.BlockSpec((tm, d), row), pl.BlockSpec((tm, LANES), row), pl.BlockSpec((tm, LANES), row)],
        out_shape=[
            jax.ShapeDtypeStruct((n, d), F32),
            jax.ShapeDtypeStruct((n, LANES), jnp.int32),
            jax.ShapeDtypeStruct((n, LANES), F32),
        ],
        compiler_params=_params("arbitrary"),
        name="router",
    )(x, gain.reshape(1, d), wr)


def _expert_kernel(bexp_ref, nused_ref, xs_ref, wg_ref, wu_ref, wd_ref, gate_ref, y_ref, acc_ref):
    del bexp_ref
    i = pl.program_id(0)
    j = pl.program_id(1)
    n_used = nused_ref[0]

    @pl.when(i < n_used)
    def _():
        @pl.when(j == 0)
        def _():
            acc_ref[...] = jnp.zeros_like(acc_ref)

        x = xs_ref[...]
        g = jnp.dot(x, wg_ref[0], preferred_element_type=F32)
        u = jnp.dot(x, wu_ref[0], preferred_element_type=F32)
        acc_ref[...] += jnp.dot(_silu_mul(g, u).astype(BF16), wd_ref[0], preferred_element_type=F32)

    last = j == pl.num_programs(1) - 1

    @pl.when(last & (i < n_used))
    def _():
        y_ref[...] = acc_ref[...] * gate_ref[...]

    @pl.when(last & (i >= n_used))
    def _():
        y_ref[...] = jnp.zeros_like(y_ref)


def _expert_ffn(xs, block_expert, n_used, row_gate, w_gate, w_up, w_down):
    d = xs.shape[1]
    f = w_gate.shape[2]
    tmb = MOE_BLOCK
    tf = MOE_F_BLOCK
    n_blocks = block_expert.shape[0]
    nf = f // tf
    assert f % tf == 0

    def fblk(i, j, nu):
        return jnp.where(i < nu[0], j, nf - 1)

    grid_spec = pltpu.PrefetchScalarGridSpec(
        num_scalar_prefetch=2,
        grid=(n_blocks, nf),
        in_specs=[
            pl.BlockSpec((tmb, d), lambda i, j, be, nu: (i, 0)),
            pl.BlockSpec((1, d, tf), lambda i, j, be, nu: (be[i], 0, fblk(i, j, nu))),
            pl.BlockSpec((1, d, tf), lambda i, j, be, nu: (be[i], 0, fblk(i, j, nu))),
            pl.BlockSpec((1, tf, d), lambda i, j, be, nu: (be[i], fblk(i, j, nu), 0)),
            pl.BlockSpec((tmb, 1), lambda i, j, be, nu: (i, 0)),
        ],
        out_specs=pl.BlockSpec((tmb, d), lambda i, j, be, nu: (i, 0)),
        scratch_shapes=[pltpu.VMEM((tmb, d), F32)],
    )
    return pl.pallas_call(
        _expert_kernel,
        grid_spec=grid_spec,
        out_shape=jax.ShapeDtypeStruct((n_blocks * tmb, d), F32),
        compiler_params=_params("arbitrary", "arbitrary"),
        name="expert_ffn",
    )(block_expert, n_used, xs, w_gate, w_up, w_down, row_gate)


def _combine_kernel(dest_ref, ys_hbm, x_ref, o_ref, buf_ref, sem):
    i = pl.program_id(0)
    nsteps = pl.num_programs(0)
    tc = x_ref.shape[0]

    def start_gather(step, slot):
        def issue(r, carry):
            row = dest_ref[step * (2 * tc) + r]
            dst = (r % 2) * tc + r // 2
            pltpu.make_async_copy(ys_hbm.at[pl.ds(row, 1), :], buf_ref.at[slot, pl.ds(dst, 1), :], sem.at[slot]).start()
            return carry
        lax.fori_loop(0, 2 * tc, issue, 0)

    def wait_gather(slot):
        def drain(r, carry):
            pltpu.make_async_copy(ys_hbm.at[pl.ds(0, 1), :], buf_ref.at[slot, pl.ds(r, 1), :], sem.at[slot]).wait()
            return carry
        lax.fori_loop(0, 2 * tc, drain, 0)

    slot = i % 2

    @pl.when(i == 0)
    def _():
        start_gather(0, 0)

    wait_gather(slot)

    @pl.when(i + 1 < nsteps)
    def _():
        start_gather(i + 1, 1 - slot)

    o_ref[...] = x_ref[...] + (buf_ref[slot, :tc, :] + buf_ref[slot, tc:, :])


def _combine(x, ys, dest):
    n, d = x.shape
    tc = min(COMBINE_BLOCK, n)
    grid_spec = pltpu.PrefetchScalarGridSpec(
        num_scalar_prefetch=1,
        grid=(n // tc,),
        in_specs=[
            pl.BlockSpec(memory_space=pl.ANY),
            pl.BlockSpec((tc, d), lambda i, dr: (i, 0)),
        ],
        out_specs=pl.BlockSpec((tc, d), lambda i, dr: (i, 0)),
        scratch_shapes=[pltpu.VMEM((2, 2 * tc, d), F32), pltpu.SemaphoreType.DMA((2,))],
    )
    return pl.pallas_call(
        _combine_kernel,
        grid_spec=grid_spec,
        out_shape=jax.ShapeDtypeStruct((n, d), F32),
        compiler_params=_params("arbitrary"),
        name="moe_combine",
    )(dest, ys, x)


def _routing_tables(idx, gate, tmb):
    n = idx.shape[0]
    nk = n * TOP_K
    e_flat = idx[:, :TOP_K].reshape(-1)
    g_flat = gate[:, :TOP_K].reshape(-1)
    onehot = (e_flat[:, None] == jnp.arange(N_EXPERTS, dtype=jnp.int32)[None, :]).astype(jnp.int32)
    csum = jnp.cumsum(onehot, axis=0)
    rank = jnp.sum(onehot * csum, axis=1) - 1
    counts = csum[-1]
    padded = (counts + tmb - 1) // tmb * tmb
    pad_end = jnp.cumsum(padded)
    pad_start = pad_end - padded
    dest = (jnp.sum(onehot * pad_start[None, :], axis=1) + rank).astype(jnp.int32)
    n_blocks = -(-nk // tmb) + N_EXPERTS
    block_expert = jnp.minimum(
        jnp.searchsorted(pad_end, jnp.arange(n_blocks, dtype=jnp.int32) * tmb, side="right"), N_EXPERTS - 1
    ).astype(jnp.int32)
    n_used = (pad_end[-1] // tmb).astype(jnp.int32).reshape(1)
    row_tok = jnp.zeros((n_blocks * tmb,), jnp.int32).at[dest].set(jnp.arange(nk, dtype=jnp.int32) // TOP_K)
    row_gate = jnp.zeros((n_blocks * tmb,), F32).at[dest].set(g_flat).reshape(-1, 1)
    return dest, block_expert, n_used, row_tok, row_gate


def _rope_tables(seq_len):
    half = HEAD_DIM // 2
    inv_freq = ROPE_THETA ** (-jnp.arange(half, dtype=F32) / half)
    ang = jnp.arange(seq_len, dtype=F32)[:, None] * inv_freq[None, :]
    cos, sin = jnp.cos(ang), jnp.sin(ang)
    return jnp.concatenate([cos, cos], axis=-1), jnp.concatenate([-sin, sin], axis=-1)


def kernel(x_prompt, x_sample, norm_mix, norm_ffn, na_w_qkv, na_q_gain, na_k_gain, na_rpb, na_w_o,
           da_w_qkv, da_q_gain, da_k_gain, da_w_o, ffn_w_gate, ffn_w_up, ffn_w_down,
           moe_w_router, moe_w_gate, moe_w_up, moe_w_down):
    d = x_prompt.shape[-1]
    trunks = []
    off = 0
    for xin in (x_prompt, x_sample):
        b, s, _ = xin.shape
        trunks.append((off, b, s))
        off += b * s
    n = off
    x = jnp.concatenate([x_prompt.reshape(-1, d), x_sample.reshape(-1, d)], axis=0)
    depth = norm_mix.shape[0]
    tm = min(TOKEN_BLOCK, n)

    for layer in range(depth):
        lj = layer // 2
        if layer % 2 == 0:
            heads = na_w_qkv.shape[2] // (3 * HEAD_DIM)
            tn = 1024
            per = heads * HEAD_DIM // tn
            hg = jnp.concatenate([
                jnp.broadcast_to(na_q_gain[lj], (per, HEAD_DIM)),
                jnp.broadcast_to(na_k_gain[lj], (per, HEAD_DIM)),
                jnp.ones((per, HEAD_DIM), F32),
            ]).reshape(3 * per, 1, HEAD_DIM)
            qkv = _qkv_project(x, norm_mix[layer], na_w_qkv[lj].astype(BF16), hg, 3 * per, 2 * per)
            tables = _na_bias_tables(na_rpb[lj])
            att = jnp.zeros((n, heads * HEAD_DIM), BF16)
            for (toff, b, s) in trunks:
                att = _na_attention(qkv, tables, att, toff, b, s, heads)
            x = _proj_residual(att, na_w_o[lj].astype(BF16), x)
            x = _dense_ffn(x, norm_ffn[layer], ffn_w_gate[lj].astype(BF16), ffn_w_up[lj].astype(BF16),
                           ffn_w_down[lj].astype(BF16))
        else:
            groups = len(DIL_CONFIGS)
            heads = da_w_qkv.shape[2] // (groups * 3 * HEAD_DIM)
            ones = jnp.ones((HEAD_DIM,), F32)
            hg = jnp.stack([t for g in range(groups) for t in (da_q_gain[lj, g], da_k_gain[lj, g], ones)])
            hg = hg.reshape(groups * 3, 1, HEAD_DIM)
            (o0, b0, s0), (o1, b1, s1) = trunks
            assert s0 % tm == 0 and s1 % tm == 0 and o1 % tm == 0
            nb0, p0, p1 = o1 // tm, s0 // tm, s1 // tm
            pos_map = lambda i: jnp.where(i < nb0, i % p0, (i - nb0) % p1)
            qkv = _qkv_project(x, norm_mix[layer], da_w_qkv[lj].astype(BF16), hg, 3, 2,
                               rope_tabs=_rope_tables(max(s0, s1)), pos_map=pos_map)
            outs, lses = [], []
            for g in range(groups):
                o_g = jnp.zeros((n, heads * HEAD_DIM), F32)
                lse_g = jnp.zeros((n, LANES), F32)
                for (toff, b, s) in trunks:
                    o_g, lse_g = _band_attention(qkv, o_g, lse_g, toff, b, s, g, heads)
                outs.append(o_g)
                lses.append(lse_g)
            x = _merge_proj(outs, lses, da_w_o[lj].astype(BF16), x)
            h, idx, gate = _router(x, norm_ffn[layer], moe_w_router[lj])
            dest, block_expert, n_used, row_tok, row_gate = _routing_tables(idx, gate, MOE_BLOCK)
            xs = h.astype(BF16)[row_tok]
            ys = _expert_ffn(xs, block_expert, n_used, row_gate, moe_w_gate[lj].astype(BF16),
                             moe_w_up[lj].astype(BF16), moe_w_down[lj].astype(BF16))
            x = x + (ys[dest[0::2]] + ys[dest[1::2]])

    outs = []
    for (toff, b, s), xin in zip(trunks, (x_prompt, x_sample)):
        outs.append(x[toff:toff + b * s].reshape(xin.shape))
    return tuple(outs)
```

```python
import functools

import jax
import jax.numpy as jnp
from jax import lax
from jax.experimental import pallas as pl
from jax.experimental.pallas import tpu as pltpu

F32 = jnp.float32
BF16 = jnp.bfloat16

HEAD_DIM = 128
GRID_W = 64
NA_ROWS = 8
NA_COLS = 16
NA_GROUP = NA_ROWS // 2
NA_KEY_ROWS = NA_GROUP + NA_ROWS - 1
DIL_CONFIGS = ((128, 1), (512, 4), (2048, 16))
N_EXPERTS = 8
TOP_K = 2
ROPE_THETA = 10000.0
EPS = 1e-6
NEG_INF = -1e30
ATTN_SCALE = HEAD_DIM ** -0.5

LANES = 128
VMEM_LIMIT = 52 * 1024 * 1024

TOKEN_BLOCK = 1024
FFN_TOKEN_BLOCK = 512
FFN_F_BLOCK = 512
BAND_Q_BLOCK = 128
MOE_BLOCK = 512
MOE_F_BLOCK = 512
COMBINE_BLOCK = 256


def _params(*sem):
    return pltpu.CompilerParams(dimension_semantics=sem, vmem_limit_bytes=VMEM_LIMIT)


def _rms_rows(x, gain):
    ms = jnp.mean(x * x, axis=-1, keepdims=True)
    return x * lax.rsqrt(ms + EPS) * gain


def _qkv_kernel(*refs, v_every, v_from, rope):
    if rope:
        x_ref, gain_ref, w_ref, hg_ref, cos_ref, sin_ref, o_ref, h_ref = refs
    else:
        x_ref, gain_ref, w_ref, hg_ref, o_ref, h_ref = refs
    j = pl.program_id(1)

    @pl.when(j == 0)
    def _():
        h_ref[...] = _rms_rows(x_ref[...], gain_ref[...]).astype(BF16)

    acc = jnp.dot(h_ref[...], w_ref[...], preferred_element_type=F32)
    is_qk = (j % v_every) < v_from

    @pl.when(is_qk)
    def _():
        for h in range(acc.shape[1] // HEAD_DIM):
            sl = slice(h * HEAD_DIM, (h + 1) * HEAD_DIM)
            y = _rms_rows(acc[:, sl], hg_ref[0])
            if rope:
                y = y * cos_ref[...] + pltpu.roll(y, HEAD_DIM // 2, 1) * sin_ref[...]
            o_ref[:, sl] = y.astype(o_ref.dtype)

    @pl.when(jnp.logical_not(is_qk))
    def _():
        o_ref[...] = acc.astype(o_ref.dtype)


def _qkv_project(x, gain, w, head_gain, v_every, v_from, rope_tabs=None, pos_map=None):
    n, d = x.shape
    n_out = w.shape[1]
    tm = min(TOKEN_BLOCK, n)
    tn = n_out // head_gain.shape[0]
    in_specs = [
        pl.BlockSpec((tm, d), lambda i, j: (i, 0)),
        pl.BlockSpec((1, d), lambda i, j: (0, 0)),
        pl.BlockSpec((d, tn), lambda i, j: (0, j)),
        pl.BlockSpec((1, 1, HEAD_DIM), lambda i, j: (j, 0, 0)),
    ]
    args = [x, gain.reshape(1, d), w, head_gain]
    if rope_tabs is not None:
        in_specs += [pl.BlockSpec((tm, HEAD_DIM), lambda i, j: (pos_map(i), 0))] * 2
        args += list(rope_tabs)
    return pl.pallas_call(
        functools.partial(_qkv_kernel, v_every=v_every, v_from=v_from, rope=rope_tabs is not None),
        grid=(n // tm, n_out // tn),
        in_specs=in_specs,
        out_specs=pl.BlockSpec((tm, tn), lambda i, j: (i, j)),
        out_shape=jax.ShapeDtypeStruct((n, n_out), BF16),
        scratch_shapes=[pltpu.VMEM((tm, d), BF16)],
        compiler_params=_params("arbitrary", "arbitrary"),
        name="qkv_project",
    )(*args)


def _na_bias_tables(rpb):
    half = NA_ROWS // 2
    qi = jnp.arange(NA_GROUP)[:, None]
    kj = jnp.arange(NA_KEY_ROWS)[None, :]
    dr = jnp.stack([kj - qi, kj - half - qi, kj - qi - (NA_KEY_ROWS - NA_GROUP)])
    rvalid = jnp.stack([
        jnp.broadcast_to(kj < NA_ROWS, (NA_GROUP, NA_KEY_ROWS)),
        (kj - qi >= 0) & (kj - qi < NA_ROWS),
        jnp.broadcast_to(kj >= NA_KEY_ROWS - NA_ROWS, (NA_GROUP, NA_KEY_ROWS)),
    ])
    c = jnp.arange(GRID_W)
    cs = jnp.clip(c - NA_COLS // 2, 0, GRID_W - NA_COLS)
    cvalid = (c[None, :] >= cs[:, None]) & (c[None, :] < cs[:, None] + NA_COLS)
    dc = jnp.clip(c[None, :] - c[:, None], 1 - NA_COLS, NA_COLS - 1)
    dri = jnp.clip(dr, 1 - NA_ROWS, NA_ROWS - 1) + NA_ROWS - 1
    bias = rpb[:, dri[:, :, None, :, None], (dc + NA_COLS - 1)[None, None, :, None, :]]
    valid = rvalid[:, :, None, :, None] & cvalid[None, None, :, None, :]
    bias = jnp.where(valid[None], bias.astype(F32), NEG_INF)
    h = rpb.shape[0]
    return bias.transpose(1, 0, 2, 3, 4, 5).reshape(3, h, NA_GROUP * GRID_W, NA_KEY_ROWS * GRID_W)


def _na_kernel(q_ref, k_ref, v_ref, tab_ref, prev_ref, o_ref, *, rows):
    del prev_ref
    g = pl.program_id(2)
    ks = jnp.clip(g * NA_GROUP - NA_ROWS // 2, 0, rows - NA_KEY_ROWS)
    start = pl.multiple_of(ks * GRID_W, GRID_W)
    k = k_ref[pl.ds(start, NA_KEY_ROWS * GRID_W), :]
    v = v_ref[pl.ds(start, NA_KEY_ROWS * GRID_W), :]
    s = lax.dot_general(q_ref[...], k, (((1,), (1,)), ((), ())), preferred_element_type=F32)
    s = s * ATTN_SCALE + tab_ref[0, 0]
    m = jnp.max(s, axis=-1, keepdims=True)
    p = jnp.exp(s - m)
    den = jnp.sum(p, axis=-1, keepdims=True)
    o = jnp.dot(p.astype(BF16), v, preferred_element_type=F32) / den
    o_ref[...] = o.astype(o_ref.dtype)


def _na_attention(qkv, tables, out_prev, tok_off, batch, seq, heads):
    rows = seq // GRID_W
    groups = rows // NA_GROUP
    assert rows % NA_GROUP == 0 and rows >= NA_KEY_ROWS and tok_off % seq == 0
    gq = NA_GROUP * GRID_W
    seq0 = tok_off // seq
    blk0 = tok_off // gq

    def variant(g):
        return jnp.where(g == 0, 0, jnp.where(g == groups - 1, 2, 1))

    return pl.pallas_call(
        functools.partial(_na_kernel, rows=rows),
        grid=(batch, heads, groups),
        in_specs=[
            pl.BlockSpec((gq, HEAD_DIM), lambda b, h, g: (blk0 + b * groups + g, h)),
            pl.BlockSpec((seq, HEAD_DIM), lambda b, h, g: (seq0 + b, heads + h)),
            pl.BlockSpec((seq, HEAD_DIM), lambda b, h, g: (seq0 + b, 2 * heads + h)),
            pl.BlockSpec((1, 1, gq, NA_KEY_ROWS * GRID_W), lambda b, h, g: (variant(g), h, 0, 0)),
            pl.BlockSpec(memory_space=pl.ANY),
        ],
        out_specs=pl.BlockSpec((gq, HEAD_DIM), lambda b, h, g: (blk0 + b * groups + g, h)),
        out_shape=jax.ShapeDtypeStruct(out_prev.shape, out_prev.dtype),
        input_output_aliases={4: 0},
        compiler_params=_params("arbitrary", "arbitrary", "arbitrary"),
        name="na_attention",
    )(qkv, qkv, qkv, tables, out_prev)


def _proj_residual_kernel(a_ref, w_ref, x_ref, o_ref):
    o_ref[...] = x_ref[...] + jnp.dot(a_ref[...], w_ref[...], preferred_element_type=F32)


def _proj_residual(a, w, x):
    n, k = a.shape
    d = w.shape[1]
    tm = min(TOKEN_BLOCK, n)
    tn = min(1024, d)
    return pl.pallas_call(
        _proj_residual_kernel,
        grid=(n // tm, d // tn),
        in_specs=[
            pl.BlockSpec((tm, k), lambda i, j: (i, 0)),
            pl.BlockSpec((k, tn), lambda i, j: (0, j)),
            pl.BlockSpec((tm, tn), lambda i, j: (i, j)),
        ],
        out_specs=pl.BlockSpec((tm, tn), lambda i, j: (i, j)),
        out_shape=jax.ShapeDtypeStruct((n, d), F32),
        compiler_params=_params("arbitrary", "arbitrary"),
        name="proj_residual",
    )(a, w, x)


def _silu_mul(g, u):
    return g * (1.0 / (1.0 + jnp.exp(-g))) * u


def _ffn_kernel(x_ref, gain_ref, wg_ref, wu_ref, wd_ref, o_ref, h_ref, acc_ref):
    j = pl.program_id(1)

    @pl.when(j == 0)
    def _():
        h_ref[...] = _rms_rows(x_ref[...], gain_ref[...]).astype(BF16)
        acc_ref[...] = jnp.zeros_like(acc_ref)

    h = h_ref[...]
    g = jnp.dot(h, wg_ref[...], preferred_element_type=F32)
    u = jnp.dot(h, wu_ref[...], preferred_element_type=F32)
    acc_ref[...] += jnp.dot(_silu_mul(g, u).astype(BF16), wd_ref[...], preferred_element_type=F32)

    @pl.when(j == pl.num_programs(1) - 1)
    def _():
        o_ref[...] = x_ref[...] + acc_ref[...]


def _dense_ffn(x, gain, w_gate, w_up, w_down):
    n, d = x.shape
    f = w_gate.shape[1]
    tm = min(FFN_TOKEN_BLOCK, n)
    tf = FFN_F_BLOCK
    assert f % tf == 0
    return pl.pallas_call(
        _ffn_kernel,
        grid=(n // tm, f // tf),
        in_specs=[
            pl.BlockSpec((tm, d), lambda i, j: (i, 0)),
            pl.BlockSpec((1, d), lambda i, j: (0, 0)),
            pl.BlockSpec((d, tf), lambda i, j: (0, j)),
            pl.BlockSpec((d, tf), lambda i, j: (0, j)),
            pl.BlockSpec((tf, d), lambda i, j: (j, 0)),
        ],
        out_specs=pl.BlockSpec((tm, d), lambda i, j: (i, 0)),
        out_shape=jax.ShapeDtypeStruct((n, d), F32),
        scratch_shapes=[pltpu.VMEM((tm, d), BF16), pltpu.VMEM((tm, d), F32)],
        compiler_params=_params("arbitrary", "arbitrary"),
        name="dense_ffn",
    )(x, gain.reshape(1, d), w_gate, w_up, w_down)


def _band_kernel(q_ref, k_ref, v_ref, o_prev, lse_prev, o_ref, lse_ref, *, length, tq, kw, half):
    del o_prev, lse_prev
    h = pl.program_id(2)
    lane = lax.broadcasted_iota(jnp.int32, (tq, LANES), 1)

    @pl.when(h == 0)
    def _():
        lse_ref[...] = jnp.zeros_like(lse_ref)

    def body(t, carry):
        q0 = pl.multiple_of(t * tq, tq)
        ks = pl.multiple_of(jnp.clip(q0 - half, 0, length - kw), half)
        q = q_ref[pl.ds(q0, tq), :]
        k = k_ref[pl.ds(ks, kw), :]
        v = v_ref[pl.ds(ks, kw), :]
        s = lax.dot_general(q, k, (((1,), (1,)), ((), ())), preferred_element_type=F32) * ATTN_SCALE
        qpos = q0 + lax.broadcasted_iota(jnp.int32, (tq, kw), 0)
        kpos = ks + lax.broadcasted_iota(jnp.int32, (tq, kw), 1)
        s = jnp.where(jnp.abs(qpos - kpos) <= half, s, NEG_INF)
        m = jnp.max(s, axis=-1, keepdims=True)
        p = jnp.exp(s - m)
        den = jnp.sum(p, axis=-1, keepdims=True)
        o = jnp.dot(p.astype(BF16), v, preferred_element_type=F32) / den
        o_ref[pl.ds(q0, tq), :] = o
        lse = m + jnp.log(den)
        lse_ref[pl.ds(q0, tq), :] = jnp.where(lane == h, lse, lse_ref[pl.ds(q0, tq), :])
        return carry

    lax.fori_loop(0, length // tq, body, 0)


def _band_attention(qkv, o_prev, lse_prev, tok_off, batch, seq, group, heads):
    window, dil = DIL_CONFIGS[group]
    half = window // (2 * dil)
    length = seq // dil
    assert seq % dil == 0 and length % half == 0 and tok_off % seq == 0
    tq = min(BAND_Q_BLOCK, length)
    kw = min(tq + 2 * half, length)
    n, c = qkv.shape
    cb = c // HEAD_DIM
    gb = group * 3 * heads
    seq0 = tok_off // seq
    width = heads * HEAD_DIM
    qkv_v = qkv.reshape(n // dil, dil * c)
    o_v = o_prev.reshape(n // dil, dil * width)
    lse_v = lse_prev.reshape(n // dil, dil * LANES)

    def col(which):
        return lambda b, r, h: (seq0 + b, r * cb + gb + which * heads + h)

    o, lse = pl.pallas_call(
        functools.partial(_band_kernel, length=length, tq=tq, kw=kw, half=half),
        grid=(batch, dil, heads),
        in_specs=[
            pl.BlockSpec((length, HEAD_DIM), col(0)),
            pl.BlockSpec((length, HEAD_DIM), col(1)),
            pl.BlockSpec((length, HEAD_DIM), col(2)),
            pl.BlockSpec(memory_space=pl.ANY),
            pl.BlockSpec(memory_space=pl.ANY),
        ],
        out_specs=[
            pl.BlockSpec((length, HEAD_DIM), lambda b, r, h: (seq0 + b, r * heads + h)),
            pl.BlockSpec((length, LANES), lambda b, r, h: (seq0 + b, r)),
        ],
        out_shape=[jax.ShapeDtypeStruct(o_v.shape, F32), jax.ShapeDtypeStruct(lse_v.shape, F32)],
        input_output_aliases={3: 0, 4: 1},
        compiler_params=_params("arbitrary", "arbitrary", "arbitrary"),
        name="band_attention",
    )(qkv_v, qkv_v, qkv_v, o_v, lse_v)
    return o.reshape(n, width), lse.reshape(n, LANES)


def _merge_proj_kernel(o0_ref, o1_ref, o2_ref, l0_ref, l1_ref, l2_ref, w_ref, x_ref, out_ref, a_ref):
    l0, l1, l2 = l0_ref[...], l1_ref[...], l2_ref[...]
    m = jnp.maximum(jnp.maximum(l0, l1), l2)
    e0, e1, e2 = jnp.exp(l0 - m), jnp.exp(l1 - m), jnp.exp(l2 - m)
    tot = e0 + e1 + e2
    w0, w1, w2 = e0 / tot, e1 / tot, e2 / tot
    for h in range(o0_ref.shape[1] // HEAD_DIM):
        sl = slice(h * HEAD_DIM, (h + 1) * HEAD_DIM)
        a = (w0[:, h:h + 1] * o0_ref[:, sl] + w1[:, h:h + 1] * o1_ref[:, sl]) + w2[:, h:h + 1] * o2_ref[:, sl]
        a_ref[:, sl] = a.astype(BF16)
    out_ref[...] = x_ref[...] + jnp.dot(a_ref[...], w_ref[...], preferred_element_type=F32)


def _merge_proj(outs, lses, w, x):
    n, width = outs[0].shape
    d = w.shape[1]
    tm = min(512, n)
    row = lambda i: (i, 0)
    return pl.pallas_call(
        _merge_proj_kernel,
        grid=(n // tm,),
        in_specs=[pl.BlockSpec((tm, width), row)] * 3 + [pl.BlockSpec((tm, LANES), row)] * 3 + [
            pl.BlockSpec((width, d), lambda i: (0, 0)),
            pl.BlockSpec((tm, d), row),
        ],
        out_specs=pl.BlockSpec((tm, d), row),
        out_shape=jax.ShapeDtypeStruct((n, d), F32),
        scratch_shapes=[pltpu.VMEM((tm, width), BF16)],
        compiler_params=_params("arbitrary"),
        name="merge_proj",
    )(*outs, *lses, w, x)


def _store_chunk_rows(ref, x):
    rows, d = x.shape
    nc = d // LANES
    for c in range(nc):
        ref[pl.ds(c, rows, stride=nc), :] = x[:, c * LANES:(c + 1) * LANES]


def _router_kernel(x_ref, gain_ref, wr_ref, h_ref, idx_ref, gate_ref):
    h = _rms_rows(x_ref[...], gain_ref[...])
    _store_chunk_rows(h_ref, h)
    logits = jnp.dot(h, wr_ref[...], preferred_element_type=F32, precision=lax.Precision.HIGHEST)
    lane = lax.broadcasted_iota(jnp.int32, logits.shape, 1)
    lane_f = lane.astype(F32)
    logits = jnp.where(lane < N_EXPERTS, logits, -jnp.inf)
    m1 = jnp.max(logits, axis=-1, keepdims=True)
    i1 = jnp.min(jnp.where(logits == m1, lane_f, float(LANES)), axis=-1, keepdims=True)
    rest = jnp.where(lane_f == i1, -jnp.inf, logits)
    m2 = jnp.max(rest, axis=-1, keepdims=True)
    i2 = jnp.min(jnp.where(rest == m2, lane_f, float(LANES)), axis=-1, keepdims=True)
    e = jnp.exp(m2 - m1)
    tot = 1.0 + e
    idx_ref[...] = jnp.where(lane == 0, i1, jnp.where(lane == 1, i2, 0.0)).astype(jnp.int32)
    gate_ref[...] = jnp.where(lane == 0, 1.0 / tot, jnp.where(lane == 1, e / tot, 0.0))


def _router(x, gain, w_router):
    n, d = x.shape
    tm = min(512, n)
    wr = jnp.zeros((d, LANES), F32).at[:, :N_EXPERTS].set(w_router.astype(F32))
    row = lambda i: (i, 0)
    return pl.pallas_call(
        _router_kernel,
        grid=(n // tm,),
        in_specs=[
            pl.BlockSpec((tm, d), row),
            pl.BlockSpec((1, d), lambda i: (0, 0)),
            pl.BlockSpec((d, LANES), lambda i: (0, 0)),
        ],
        out_specs=[pl.BlockSpec((tm * (d // LANES), LANES), row), pl.BlockSpec((tm, LANES), row),
                   pl.BlockSpec((tm, LANES), row)],
        out_shape=[
            jax.ShapeDtypeStruct((n * (d // LANES), LANES), F32),
            jax.ShapeDtypeStruct((n, LANES), jnp.int32),
            jax.ShapeDtypeStruct((n, LANES), F32),
        ],
        compiler_params=_params("arbitrary"),
        name="router",
    )(x, gain.reshape(1, d), wr)


def _expert_kernel(bexp_ref, nused_ref, rtok_ref, h_hbm, wg_ref, wu_ref, wd_ref, gate_ref, y_ref,
                   xs_ref, xb_ref, acc_ref, sem):
    del bexp_ref
    i = pl.program_id(0)
    j = pl.program_id(1)
    tmb, d = xb_ref.shape
    nc = d // LANES
    n_used = nused_ref[0]

    def row_copy(tok, slot, r):
        return pltpu.make_async_copy(h_hbm.at[pl.ds(pl.multiple_of(tok * nc, nc), nc), :],
                                     xs_ref.at[slot, pl.ds(pl.multiple_of(r * nc, nc), nc), :], sem.at[slot])

    def start_gather(blk, slot):
        def issue(r, carry):
            row_copy(rtok_ref[blk * tmb + r], slot, r).start()
            return carry
        lax.fori_loop(0, tmb, issue, 0)

    def wait_gather(slot):
        def drain(r, carry):
            row_copy(0, slot, r).wait()
            return carry
        lax.fori_loop(0, tmb, drain, 0)

    @pl.when(i < n_used)
    def _():
        slot = i % 2

        @pl.when(j == 0)
        def _():
            @pl.when(i == 0)
            def _():
                start_gather(0, 0)

            wait_gather(slot)

            @pl.when(i + 1 < n_used)
            def _():
                start_gather(i + 1, 1 - slot)

            for c in range(nc):
                xb_ref[:, c * LANES:(c + 1) * LANES] = xs_ref[slot, pl.ds(c, tmb, stride=nc), :].astype(BF16)
            acc_ref[...] = jnp.zeros_like(acc_ref)

        x = xb_ref[...]
        g = jnp.dot(x, wg_ref[0], preferred_element_type=F32)
        u = jnp.dot(x, wu_ref[0], preferred_element_type=F32)
        acc_ref[...] += jnp.dot(_silu_mul(g, u).astype(BF16), wd_ref[0], preferred_element_type=F32)

    last = j == pl.num_programs(1) - 1

    @pl.when(last & (i < n_used))
    def _():
        _store_chunk_rows(y_ref, acc_ref[...] * gate_ref[...])

    @pl.when(last & (i >= n_used))
    def _():
        y_ref[...] = jnp.zeros_like(y_ref)


def _expert_ffn(h_rows, block_expert, n_used, row_tok, row_gate, w_gate, w_up, w_down):
    d = w_gate.shape[1]
    f = w_gate.shape[2]
    tmb = MOE_BLOCK
    tf = MOE_F_BLOCK
    n_blocks = block_expert.shape[0]
    nf = f // tf
    assert f % tf == 0

    def fblk(i, j, nu):
        return jnp.where(i < nu[0], j, nf - 1)

    grid_spec = pltpu.PrefetchScalarGridSpec(
        num_scalar_prefetch=3,
        grid=(n_blocks, nf),
        in_specs=[
            pl.BlockSpec(memory_space=pl.ANY),
            pl.BlockSpec((1, d, tf), lambda i, j, be, nu, rt: (be[i], 0, fblk(i, j, nu))),
            pl.BlockSpec((1, d, tf), lambda i, j, be, nu, rt: (be[i], 0, fblk(i, j, nu))),
            pl.BlockSpec((1, tf, d), lambda i, j, be, nu, rt: (be[i], fblk(i, j, nu), 0)),
            pl.BlockSpec((tmb, 1), lambda i, j, be, nu, rt: (i, 0)),
        ],
        out_specs=pl.BlockSpec((tmb * (d // LANES), LANES), lambda i, j, be, nu, rt: (i, 0)),
        scratch_shapes=[
            pltpu.VMEM((2, tmb * (d // LANES), LANES), F32),
            pltpu.VMEM((tmb, d), BF16),
            pltpu.VMEM((tmb, d), F32),
            pltpu.SemaphoreType.DMA((2,)),
        ],
    )
    return pl.pallas_call(
        _expert_kernel,
        grid_spec=grid_spec,
        out_shape=jax.ShapeDtypeStruct((n_blocks * tmb * (d // LANES), LANES), F32),
        compiler_params=_params("arbitrary", "arbitrary"),
        name="expert_ffn",
    )(block_expert, n_used, row_tok, h_rows, w_gate, w_up, w_down, row_gate)


def _combine_kernel(dest_ref, ys_hbm, x_ref, o_ref, buf_ref, sem):
    i = pl.program_id(0)
    nsteps = pl.num_programs(0)
    tc, d = x_ref.shape
    nc = d // LANES

    def row_copy(row, slot, dst):
        return pltpu.make_async_copy(ys_hbm.at[pl.ds(pl.multiple_of(row * nc, nc), nc), :],
                                     buf_ref.at[slot, pl.ds(pl.multiple_of(dst * nc, nc), nc), :], sem.at[slot])

    def start_gather(step, slot):
        def issue(r, carry):
            row_copy(dest_ref[step * (2 * tc) + r], slot, (r % 2) * tc + r // 2).start()
            return carry
        lax.fori_loop(0, 2 * tc, issue, 0)

    def wait_gather(slot):
        def drain(r, carry):
            row_copy(0, slot, r).wait()
            return carry
        lax.fori_loop(0, 2 * tc, drain, 0)

    slot = i % 2

    @pl.when(i == 0)
    def _():
        start_gather(0, 0)

    wait_gather(slot)

    @pl.when(i + 1 < nsteps)
    def _():
        start_gather(i + 1, 1 - slot)

    for c in range(nc):
        sl = slice(c * LANES, (c + 1) * LANES)
        y0 = buf_ref[slot, pl.ds(c, tc, stride=nc), :]
        y1 = buf_ref[slot, pl.ds(tc * nc + c, tc, stride=nc), :]
        o_ref[:, sl] = x_ref[:, sl] + (y0 + y1)


def _combine(x, ys, dest):
    n, d = x.shape
    tc = min(COMBINE_BLOCK, n)
    grid_spec = pltpu.PrefetchScalarGridSpec(
        num_scalar_prefetch=1,
        grid=(n // tc,),
        in_specs=[
            pl.BlockSpec(memory_space=pl.ANY),
            pl.BlockSpec((tc, d), lambda i, dr: (i, 0)),
        ],
        out_specs=pl.BlockSpec((tc, d), lambda i, dr: (i, 0)),
        scratch_shapes=[pltpu.VMEM((2, 2 * tc * (d // LANES), LANES), F32), pltpu.SemaphoreType.DMA((2,))],
    )
    return pl.pallas_call(
        _combine_kernel,
        grid_spec=grid_spec,
        out_shape=jax.ShapeDtypeStruct((n, d), F32),
        compiler_params=_params("arbitrary"),
        name="moe_combine",
    )(dest, ys, x)


def _routing_tables(idx, gate, tmb):
    n = idx.shape[0]
    nk = n * TOP_K
    e_flat = idx[:, :TOP_K].reshape(-1)
    g_flat = gate[:, :TOP_K].reshape(-1)
    onehot = (e_flat[:, None] == jnp.arange(N_EXPERTS, dtype=jnp.int32)[None, :]).astype(jnp.int32)
    csum = jnp.cumsum(onehot, axis=0)
    rank = jnp.sum(onehot * csum, axis=1) - 1
    counts = csum[-1]
    padded = (counts + tmb - 1) // tmb * tmb
    pad_end = jnp.cumsum(padded)
    pad_start = pad_end - padded
    dest = (jnp.sum(onehot * pad_start[None, :], axis=1) + rank).astype(jnp.int32)
    n_blocks = -(-nk // tmb) + N_EXPERTS
    block_expert = jnp.minimum(
        jnp.searchsorted(pad_end, jnp.arange(n_blocks, dtype=jnp.int32) * tmb, side="right"), N_EXPERTS - 1
    ).astype(jnp.int32)
    n_used = (pad_end[-1] // tmb).astype(jnp.int32).reshape(1)
    row_tok = jnp.zeros((n_blocks * tmb,), jnp.int32).at[dest].set(jnp.arange(nk, dtype=jnp.int32) // TOP_K)
    row_gate = jnp.zeros((n_blocks * tmb,), F32).at[dest].set(g_flat).reshape(-1, 1)
    return dest, block_expert, n_used, row_tok, row_gate


def _rope_tables(seq_len):
    half = HEAD_DIM // 2
    inv_freq = ROPE_THETA ** (-jnp.arange(half, dtype=F32) / half)
    ang = jnp.arange(seq_len, dtype=F32)[:, None] * inv_freq[None, :]
    cos, sin = jnp.cos(ang), jnp.sin(ang)
    return jnp.concatenate([cos, cos], axis=-1), jnp.concatenate([-sin, sin], axis=-1)


def kernel(x_prompt, x_sample, norm_mix, norm_ffn, na_w_qkv, na_q_gain, na_k_gain, na_rpb, na_w_o,
           da_w_qkv, da_q_gain, da_k_gain, da_w_o, ffn_w_gate, ffn_w_up, ffn_w_down,
           moe_w_router, moe_w_gate, moe_w_up, moe_w_down):
    d = x_prompt.shape[-1]
    trunks = []
    off = 0
    for xin in (x_prompt, x_sample):
        b, s, _ = xin.shape
        trunks.append((off, b, s))
        off += b * s
    n = off
    x = jnp.concatenate([x_prompt.reshape(-1, d), x_sample.reshape(-1, d)], axis=0)
    depth = norm_mix.shape[0]
    tm = min(TOKEN_BLOCK, n)

    for layer in range(depth):
        lj = layer // 2
        if layer % 2 == 0:
            heads = na_w_qkv.shape[2] // (3 * HEAD_DIM)
            tn = 1024
            per = heads * HEAD_DIM // tn
            hg = jnp.concatenate([
                jnp.broadcast_to(na_q_gain[lj], (per, HEAD_DIM)),
                jnp.broadcast_to(na_k_gain[lj], (per, HEAD_DIM)),
                jnp.ones((per, HEAD_DIM), F32),
            ]).reshape(3 * per, 1, HEAD_DIM)
            qkv = _qkv_project(x, norm_mix[layer], na_w_qkv[lj].astype(BF16), hg, 3 * per, 2 * per)
            tables = _na_bias_tables(na_rpb[lj])
            att = jnp.zeros((n, heads * HEAD_DIM), BF16)
            for (toff, b, s) in trunks:
                att = _na_attention(qkv, tables, att, toff, b, s, heads)
            x = _proj_residual(att, na_w_o[lj].astype(BF16), x)
            x = _dense_ffn(x, norm_ffn[layer], ffn_w_gate[lj].astype(BF16), ffn_w_up[lj].astype(BF16),
                           ffn_w_down[lj].astype(BF16))
        else:
            groups = len(DIL_CONFIGS)
            heads = da_w_qkv.shape[2] // (groups * 3 * HEAD_DIM)
            ones = jnp.ones((HEAD_DIM,), F32)
            hg = jnp.stack([t for g in range(groups) for t in (da_q_gain[lj, g], da_k_gain[lj, g], ones)])
            hg = hg.reshape(groups * 3, 1, HEAD_DIM)
            (o0, b0, s0), (o1, b1, s1) = trunks
            assert s0 % tm == 0 and s1 % tm == 0 and o1 % tm == 0
            nb0, p0, p1 = o1 // tm, s0 // tm, s1 // tm
            pos_map = lambda i: jnp.where(i < nb0, i % p0, (i - nb0) % p1)
            qkv = _qkv_project(x, norm_mix[layer], da_w_qkv[lj].astype(BF16), hg, 3, 2,
                               rope_tabs=_rope_tables(max(s0, s1)), pos_map=pos_map)
            outs, lses = [], []
            for g in range(groups):
                o_g = jnp.zeros((n, heads * HEAD_DIM), F32)
                lse_g = jnp.zeros((n, LANES), F32)
                for (toff, b, s) in trunks:
                    o_g, lse_g = _band_attention(qkv, o_g, lse_g, toff, b, s, g, heads)
                outs.append(o_g)
                lses.append(lse_g)
            x = _merge_proj(outs, lses, da_w_o[lj].astype(BF16), x)
            h, idx, gate = _router(x, norm_ffn[layer], moe_w_router[lj])
            dest, block_expert, n_used, row_tok, row_gate = _routing_tables(idx, gate, MOE_BLOCK)
            ys = _expert_ffn(h, block_expert, n_used, row_tok, row_gate, moe_w_gate[lj].astype(BF16),
                             moe_w_up[lj].astype(BF16), moe_w_down[lj].astype(BF16))
            x = _combine(x, ys, dest)

    outs = []
    for (toff, b, s), xin in zip(trunks, (x_prompt, x_sample)):
        outs.append(x[toff:toff + b * s].reshape(xin.shape))
    return tuple(outs)
```

```python
import functools

import jax
import jax.numpy as jnp
from jax import lax
from jax.experimental import pallas as pl
from jax.experimental.pallas import tpu as pltpu

F32 = jnp.float32
BF16 = jnp.bfloat16

HEAD_DIM = 128
GRID_W = 64
NA_ROWS = 8
NA_COLS = 16
NA_GROUP = NA_ROWS // 2
NA_KEY_ROWS = NA_GROUP + NA_ROWS - 1
DIL_CONFIGS = ((128, 1), (512, 4), (2048, 16))
N_EXPERTS = 8
TOP_K = 2
ROPE_THETA = 10000.0
EPS = 1e-6
NEG_INF = -1e30
ATTN_SCALE = HEAD_DIM ** -0.5

LANES = 128
VMEM_LIMIT = 52 * 1024 * 1024

TOKEN_BLOCK = 1024
FFN_TOKEN_BLOCK = 512
FFN_F_BLOCK = 512
BAND_Q_BLOCK = 128
MOE_BLOCK = 512
MOE_F_BLOCK = 512
COMBINE_BLOCK = 256


def _params(*sem):
    return pltpu.CompilerParams(dimension_semantics=sem, vmem_limit_bytes=VMEM_LIMIT)


def _rms_rows(x, gain):
    ms = jnp.mean(x * x, axis=-1, keepdims=True)
    return x * lax.rsqrt(ms + EPS) * gain


def _qkv_kernel(x_ref, gain_ref, w_ref, hg_ref, o_ref, h_ref, *, v_every, v_from):
    j = pl.program_id(1)

    @pl.when(j == 0)
    def _():
        h_ref[...] = _rms_rows(x_ref[...], gain_ref[...]).astype(BF16)

    acc = jnp.dot(h_ref[...], w_ref[...], preferred_element_type=F32)
    is_qk = (j % v_every) < v_from

    @pl.when(is_qk)
    def _():
        for h in range(acc.shape[1] // HEAD_DIM):
            sl = slice(h * HEAD_DIM, (h + 1) * HEAD_DIM)
            o_ref[:, sl] = _rms_rows(acc[:, sl], hg_ref[0]).astype(o_ref.dtype)

    @pl.when(jnp.logical_not(is_qk))
    def _():
        o_ref[...] = acc.astype(o_ref.dtype)


def _qkv_project(x, gain, w, head_gain, v_every, v_from):
    n, d = x.shape
    n_out = w.shape[1]
    tm = min(TOKEN_BLOCK, n)
    tn = n_out // head_gain.shape[0]
    return pl.pallas_call(
        functools.partial(_qkv_kernel, v_every=v_every, v_from=v_from),
        grid=(n // tm, n_out // tn),
        in_specs=[
            pl.BlockSpec((tm, d), lambda i, j: (i, 0)),
            pl.BlockSpec((1, d), lambda i, j: (0, 0)),
            pl.BlockSpec((d, tn), lambda i, j: (0, j)),
            pl.BlockSpec((1, 1, HEAD_DIM), lambda i, j: (j, 0, 0)),
        ],
        out_specs=pl.BlockSpec((tm, tn), lambda i, j: (i, j)),
        out_shape=jax.ShapeDtypeStruct((n, n_out), BF16),
        scratch_shapes=[pltpu.VMEM((tm, d), BF16)],
        compiler_params=_params("arbitrary", "arbitrary"),
        name="qkv_project",
    )(x, gain.reshape(1, d), w, head_gain)


def _qkv_dil_kernel(x_ref, gain_ref, w_ref, hg_ref, cos_ref, sin_ref, *rest, dils, per_group, v_from):
    out_refs, (h_ref, stage_ref) = rest[:len(dils)], rest[len(dils):]
    j = pl.program_id(1)
    tm = x_ref.shape[0]

    @pl.when(j == 0)
    def _():
        h_ref[...] = _rms_rows(x_ref[...], gain_ref[...]).astype(BF16)

    acc = jnp.dot(h_ref[...], w_ref[...], preferred_element_type=F32)
    is_qk = (j % per_group) < v_from

    @pl.when(is_qk)
    def _():
        for h in range(acc.shape[1] // HEAD_DIM):
            sl = slice(h * HEAD_DIM, (h + 1) * HEAD_DIM)
            y = _rms_rows(acc[:, sl], hg_ref[0])
            stage_ref[h] = y * cos_ref[...] + pltpu.roll(y, HEAD_DIM // 2, 1) * sin_ref[...]

    @pl.when(jnp.logical_not(is_qk))
    def _():
        for h in range(acc.shape[1] // HEAD_DIM):
            stage_ref[h] = acc[:, h * HEAD_DIM:(h + 1) * HEAD_DIM]

    for g, (dil, o_ref) in enumerate(zip(dils, out_refs)):
        @pl.when(j // per_group == g)
        def _(dil=dil, o_ref=o_ref):
            for r in range(dil):
                for h in range(acc.shape[1] // HEAD_DIM):
                    rows = stage_ref[h, pl.ds(r, tm // dil, stride=dil), :]
                    o_ref[r, :, h * HEAD_DIM:(h + 1) * HEAD_DIM] = rows.astype(o_ref.dtype)


def _qkv_project_dilated(x, gain, w, head_gain, rope_tabs, pos_map, dils, heads):
    n, d = x.shape
    n_out = w.shape[1]
    tm = min(TOKEN_BLOCK, n)
    width = heads * HEAD_DIM
    tn = width // 2
    per_which = width // tn
    per_group = 3 * per_which
    assert n_out == len(dils) * 3 * width and head_gain.shape[0] == n_out // tn

    def out_map(g):
        def index(i, j):
            local = jnp.clip(j - g * per_group, 0, per_group - 1)
            return (local // per_which, 0, i, local % per_which)
        return index

    return pl.pallas_call(
        functools.partial(_qkv_dil_kernel, dils=tuple(dils), per_group=per_group, v_from=2 * per_which),
        grid=(n // tm, n_out // tn),
        in_specs=[
            pl.BlockSpec((tm, d), lambda i, j: (i, 0)),
            pl.BlockSpec((1, d), lambda i, j: (0, 0)),
            pl.BlockSpec((d, tn), lambda i, j: (0, j)),
            pl.BlockSpec((1, 1, HEAD_DIM), lambda i, j: (j, 0, 0)),
            pl.BlockSpec((tm, HEAD_DIM), lambda i, j: (pos_map(i), 0)),
            pl.BlockSpec((tm, HEAD_DIM), lambda i, j: (pos_map(i), 0)),
        ],
        out_specs=[pl.BlockSpec((None, dil, tm // dil, tn), out_map(g)) for g, dil in enumerate(dils)],
        out_shape=[jax.ShapeDtypeStruct((3, dil, n // dil, width), BF16) for dil in dils],
        scratch_shapes=[pltpu.VMEM((tm, d), BF16), pltpu.VMEM((tn // HEAD_DIM, tm, HEAD_DIM), F32)],
        compiler_params=_params("arbitrary", "arbitrary"),
        name="qkv_project_dilated",
    )(x, gain.reshape(1, d), w, head_gain, *rope_tabs)


def _na_bias_tables(rpb):
    half = NA_ROWS // 2
    qi = jnp.arange(NA_GROUP)[:, None]
    kj = jnp.arange(NA_KEY_ROWS)[None, :]
    dr = jnp.stack([kj - qi, kj - half - qi, kj - qi - (NA_KEY_ROWS - NA_GROUP)])
    rvalid = jnp.stack([
        jnp.broadcast_to(kj < NA_ROWS, (NA_GROUP, NA_KEY_ROWS)),
        (kj - qi >= 0) & (kj - qi < NA_ROWS),
        jnp.broadcast_to(kj >= NA_KEY_ROWS - NA_ROWS, (NA_GROUP, NA_KEY_ROWS)),
    ])
    c = jnp.arange(GRID_W)
    cs = jnp.clip(c - NA_COLS // 2, 0, GRID_W - NA_COLS)
    cvalid = (c[None, :] >= cs[:, None]) & (c[None, :] < cs[:, None] + NA_COLS)
    dc = jnp.clip(c[None, :] - c[:, None], 1 - NA_COLS, NA_COLS - 1)
    dri = jnp.clip(dr, 1 - NA_ROWS, NA_ROWS - 1) + NA_ROWS - 1
    sel_r = jax.nn.one_hot(dri, 2 * NA_ROWS - 1, dtype=F32)
    sel_c = jax.nn.one_hot(dc + NA_COLS - 1, 2 * NA_COLS - 1, dtype=F32)
    hp = lax.Precision.HIGHEST
    tmp = jnp.einsum("hab,qcb->haqc", rpb.astype(F32), sel_c, precision=hp)
    bias = jnp.einsum("haqc,vgka->vhgqkc", tmp, sel_r, precision=hp)
    valid = rvalid[:, None, :, None, :, None] & cvalid[None, None, None, :, None, :]
    bias = jnp.where(valid, bias, NEG_INF)
    h = rpb.shape[0]
    return bias.reshape(3, h, NA_GROUP * GRID_W, NA_KEY_ROWS * GRID_W)


def _na_kernel(q_ref, k_ref, v_ref, tab_ref, prev_ref, o_ref, *, rows):
    del prev_ref
    g = pl.program_id(2)
    ks = jnp.clip(g * NA_GROUP - NA_ROWS // 2, 0, rows - NA_KEY_ROWS)
    start = pl.multiple_of(ks * GRID_W, GRID_W)
    k = k_ref[pl.ds(start, NA_KEY_ROWS * GRID_W), :]
    v = v_ref[pl.ds(start, NA_KEY_ROWS * GRID_W), :]
    s = lax.dot_general(q_ref[...], k, (((1,), (1,)), ((), ())), preferred_element_type=F32)
    s = s * ATTN_SCALE + tab_ref[0, 0]
    m = jnp.max(s, axis=-1, keepdims=True)
    p = jnp.exp(s - m)
    den = jnp.sum(p, axis=-1, keepdims=True)
    o = jnp.dot(p.astype(BF16), v, preferred_element_type=F32) / den
    o_ref[...] = o.astype(o_ref.dtype)


def _na_attention(qkv, tables, out_prev, tok_off, batch, seq, heads):
    rows = seq // GRID_W
    groups = rows // NA_GROUP
    assert rows % NA_GROUP == 0 and rows >= NA_KEY_ROWS and tok_off % seq == 0
    gq = NA_GROUP * GRID_W
    seq0 = tok_off // seq
    blk0 = tok_off // gq

    def variant(g):
        return jnp.where(g == 0, 0, jnp.where(g == groups - 1, 2, 1))

    return pl.pallas_call(
        functools.partial(_na_kernel, rows=rows),
        grid=(batch, heads, groups),
        in_specs=[
            pl.BlockSpec((gq, HEAD_DIM), lambda b, h, g: (blk0 + b * groups + g, h)),
            pl.BlockSpec((seq, HEAD_DIM), lambda b, h, g: (seq0 + b, heads + h)),
            pl.BlockSpec((seq, HEAD_DIM), lambda b, h, g: (seq0 + b, 2 * heads + h)),
            pl.BlockSpec((1, 1, gq, NA_KEY_ROWS * GRID_W), lambda b, h, g: (variant(g), h, 0, 0)),
            pl.BlockSpec(memory_space=pl.ANY),
        ],
        out_specs=pl.BlockSpec((gq, HEAD_DIM), lambda b, h, g: (blk0 + b * groups + g, h)),
        out_shape=jax.ShapeDtypeStruct(out_prev.shape, out_prev.dtype),
        input_output_aliases={4: 0},
        compiler_params=_params("arbitrary", "arbitrary", "arbitrary"),
        name="na_attention",
    )(qkv, qkv, qkv, tables, out_prev)


def _proj_residual_kernel(a_ref, w_ref, x_ref, o_ref):
    o_ref[...] = x_ref[...] + jnp.dot(a_ref[...], w_ref[...], preferred_element_type=F32)


def _proj_residual(a, w, x):
    n, k = a.shape
    d = w.shape[1]
    tm = min(TOKEN_BLOCK, n)
    tn = min(1024, d)
    return pl.pallas_call(
        _proj_residual_kernel,
        grid=(n // tm, d // tn),
        in_specs=[
            pl.BlockSpec((tm, k), lambda i, j: (i, 0)),
            pl.BlockSpec((k, tn), lambda i, j: (0, j)),
            pl.BlockSpec((tm, tn), lambda i, j: (i, j)),
        ],
        out_specs=pl.BlockSpec((tm, tn), lambda i, j: (i, j)),
        out_shape=jax.ShapeDtypeStruct((n, d), F32),
        compiler_params=_params("arbitrary", "arbitrary"),
        name="proj_residual",
    )(a, w, x)


def _silu_mul(g, u):
    return g * (1.0 / (1.0 + jnp.exp(-g))) * u


def _ffn_kernel(x_ref, gain_ref, wg_ref, wu_ref, wd_ref, o_ref, h_ref, acc_ref):
    j = pl.program_id(1)

    @pl.when(j == 0)
    def _():
        h_ref[...] = _rms_rows(x_ref[...], gain_ref[...]).astype(BF16)
        acc_ref[...] = jnp.zeros_like(acc_ref)

    h = h_ref[...]
    g = jnp.dot(h, wg_ref[...], preferred_element_type=F32)
    u = jnp.dot(h, wu_ref[...], preferred_element_type=F32)
    acc_ref[...] += jnp.dot(_silu_mul(g, u).astype(BF16), wd_ref[...], preferred_element_type=F32)

    @pl.when(j == pl.num_programs(1) - 1)
    def _():
        o_ref[...] = x_ref[...] + acc_ref[...]


def _dense_ffn(x, gain, w_gate, w_up, w_down):
    n, d = x.shape
    f = w_gate.shape[1]
    tm = min(FFN_TOKEN_BLOCK, n)
    tf = FFN_F_BLOCK
    assert f % tf == 0
    return pl.pallas_call(
        _ffn_kernel,
        grid=(n // tm, f // tf),
        in_specs=[
            pl.BlockSpec((tm, d), lambda i, j: (i, 0)),
            pl.BlockSpec((1, d), lambda i, j: (0, 0)),
            pl.BlockSpec((d, tf), lambda i, j: (0, j)),
            pl.BlockSpec((d, tf), lambda i, j: (0, j)),
            pl.BlockSpec((tf, d), lambda i, j: (j, 0)),
        ],
        out_specs=pl.BlockSpec((tm, d), lambda i, j: (i, 0)),
        out_shape=jax.ShapeDtypeStruct((n, d), F32),
        scratch_shapes=[pltpu.VMEM((tm, d), BF16), pltpu.VMEM((tm, d), F32)],
        compiler_params=_params("arbitrary", "arbitrary"),
        name="dense_ffn",
    )(x, gain.reshape(1, d), w_gate, w_up, w_down)


def _band_kernel(q_ref, k_ref, v_ref, o_prev, lse_prev, o_ref, lse_ref, *, length, tq, kw, half):
    del o_prev, lse_prev
    h = pl.program_id(2)
    lane = lax.broadcasted_iota(jnp.int32, (tq, LANES), 1)

    @pl.when(h == 0)
    def _():
        lse_ref[...] = jnp.zeros_like(lse_ref)

    def body(t, carry):
        q0 = pl.multiple_of(t * tq, tq)
        ks = pl.multiple_of(jnp.clip(q0 - half, 0, length - kw), half)
        q = q_ref[pl.ds(q0, tq), :]
        k = k_ref[pl.ds(ks, kw), :]
        v = v_ref[pl.ds(ks, kw), :]
        s = lax.dot_general(q, k, (((1,), (1,)), ((), ())), preferred_element_type=F32) * ATTN_SCALE
        qpos = q0 + lax.broadcasted_iota(jnp.int32, (tq, kw), 0)
        kpos = ks + lax.broadcasted_iota(jnp.int32, (tq, kw), 1)
        s = jnp.where(jnp.abs(qpos - kpos) <= half, s, NEG_INF)
        m = jnp.max(s, axis=-1, keepdims=True)
        p = jnp.exp(s - m)
        den = jnp.sum(p, axis=-1, keepdims=True)
        o = jnp.dot(p.astype(BF16), v, preferred_element_type=F32) / den
        o_ref[pl.ds(q0, tq), :] = o
        lse = m + jnp.log(den)
        lse_ref[pl.ds(q0, tq), :] = jnp.where(lane == h, lse, lse_ref[pl.ds(q0, tq), :])
        return carry

    lax.fori_loop(0, length // tq, body, 0)


def _band_attention(qkv, o_prev, lse_prev, tok_off, batch, seq, group):
    window, dil = DIL_CONFIGS[group]
    half = window // (2 * dil)
    length = seq // dil
    assert seq % dil == 0 and length % half == 0 and tok_off % seq == 0 and qkv.shape[1] == dil
    tq = min(BAND_Q_BLOCK, length)
    kw = min(tq + 2 * half, length)
    heads = qkv.shape[3] // HEAD_DIM
    seq0 = tok_off // seq

    def col(which):
        return lambda b, r, h: (which, r, seq0 + b, h)

    return pl.pallas_call(
        functools.partial(_band_kernel, length=length, tq=tq, kw=kw, half=half),
        grid=(batch, dil, heads),
        in_specs=[
            pl.BlockSpec((None, None, length, HEAD_DIM), col(0)),
            pl.BlockSpec((None, None, length, HEAD_DIM), col(1)),
            pl.BlockSpec((None, None, length, HEAD_DIM), col(2)),
            pl.BlockSpec(memory_space=pl.ANY),
            pl.BlockSpec(memory_space=pl.ANY),
        ],
        out_specs=[
            pl.BlockSpec((None, length, HEAD_DIM), lambda b, r, h: (r, seq0 + b, h)),
            pl.BlockSpec((None, length, LANES), lambda b, r, h: (r, seq0 + b, 0)),
        ],
        out_shape=[jax.ShapeDtypeStruct(o_prev.shape, F32), jax.ShapeDtypeStruct(lse_prev.shape, F32)],
        input_output_aliases={3: 0, 4: 1},
        compiler_params=_params("arbitrary", "arbitrary", "arbitrary"),
        name="band_attention",
    )(qkv, qkv, qkv, o_prev, lse_prev)


def _merge_proj_kernel(*refs, dils):
    ng = len(dils)
    o_refs, l_refs = refs[:ng], refs[ng:2 * ng]
    w_ref, x_ref, out_ref, a_ref, os_ref, ls_ref = refs[2 * ng:]
    tm = x_ref.shape[0]

    heads = a_ref.shape[1] // HEAD_DIM
    for g, dil in enumerate(dils):
        for r in range(dil):
            ls_ref[g, pl.ds(r, tm // dil, stride=dil), :] = l_refs[g][r]
            for h in range(heads):
                os_ref[g, h, pl.ds(r, tm // dil, stride=dil), :] = o_refs[g][r, :, h * HEAD_DIM:(h + 1) * HEAD_DIM]

    lse = [ls_ref[g] for g in range(ng)]
    m = functools.reduce(jnp.maximum, lse)
    e = [jnp.exp(l - m) for l in lse]
    tot = functools.reduce(lambda p, q: p + q, e)
    wts = [eg / tot for eg in e]
    for h in range(heads):
        terms = [wts[g][:, h:h + 1] * os_ref[g, h] for g in range(ng)]
        a_ref[:, h * HEAD_DIM:(h + 1) * HEAD_DIM] = functools.reduce(lambda p, q: p + q, terms).astype(BF16)
    out_ref[...] = x_ref[...] + jnp.dot(a_ref[...], w_ref[...], preferred_element_type=F32)


def _merge_proj(outs, lses, w, x):
    n, d = x.shape
    width = w.shape[0]
    dils = tuple(o.shape[0] for o in outs)
    tm = min(512, n)
    row = lambda i: (i, 0)
    grp = lambda i: (0, i, 0)
    return pl.pallas_call(
        functools.partial(_merge_proj_kernel, dils=dils),
        grid=(n // tm,),
        in_specs=[pl.BlockSpec((dil, tm // dil, width), grp) for dil in dils]
        + [pl.BlockSpec((dil, tm // dil, LANES), grp) for dil in dils]
        + [pl.BlockSpec((width, d), lambda i: (0, 0)), pl.BlockSpec((tm, d), row)],
        out_specs=pl.BlockSpec((tm, d), row),
        out_shape=jax.ShapeDtypeStruct((n, d), F32),
        scratch_shapes=[
            pltpu.VMEM((tm, width), BF16),
            pltpu.VMEM((len(dils), width // HEAD_DIM, tm, HEAD_DIM), F32),
            pltpu.VMEM((len(dils), tm, LANES), F32),
        ],
        compiler_params=_params("arbitrary"),
        name="merge_proj",
    )(*outs, *lses, w, x)


def _store_chunk_rows(ref, x):
    rows, d = x.shape
    nc = d // LANES
    for c in range(nc):
        ref[pl.ds(c, rows, stride=nc), :] = x[:, c * LANES:(c + 1) * LANES]


def _router_kernel(x_ref, gain_ref, wr_ref, h_ref, idx_ref, gate_ref):
    h = _rms_rows(x_ref[...], gain_ref[...])
    _store_chunk_rows(h_ref, h)
    logits = jnp.dot(h, wr_ref[...], preferred_element_type=F32, precision=lax.Precision.HIGHEST)
    lane = lax.broadcasted_iota(jnp.int32, logits.shape, 1)
    lane_f = lane.astype(F32)
    logits = jnp.where(lane < N_EXPERTS, logits, -jnp.inf)
    m1 = jnp.max(logits, axis=-1, keepdims=True)
    i1 = jnp.min(jnp.where(logits == m1, lane_f, float(LANES)), axis=-1, keepdims=True)
    rest = jnp.where(lane_f == i1, -jnp.inf, logits)
    m2 = jnp.max(rest, axis=-1, keepdims=True)
    i2 = jnp.min(jnp.where(rest == m2, lane_f, float(LANES)), axis=-1, keepdims=True)
    e = jnp.exp(m2 - m1)
    tot = 1.0 + e
    idx_ref[...] = jnp.where(lane == 0, i1, jnp.where(lane == 1, i2, 0.0)).astype(jnp.int32)
    gate_ref[...] = jnp.where(lane == 0, 1.0 / tot, jnp.where(lane == 1, e / tot, 0.0))


def _router(x, gain, w_router):
    n, d = x.shape
    tm = min(512, n)
    wr = jnp.zeros((d, LANES), F32).at[:, :N_EXPERTS].set(w_router.astype(F32))
    row = lambda i: (i, 0)
    return pl.pallas_call(
        _router_kernel,
        grid=(n // tm,),
        in_specs=[
            pl.BlockSpec((tm, d), row),
            pl.BlockSpec((1, d), lambda i: (0, 0)),
            pl.BlockSpec((d, LANES), lambda i: (0, 0)),
        ],
        out_specs=[pl.BlockSpec((tm * (d // LANES), LANES), row), pl.BlockSpec((tm, LANES), row),
                   pl.BlockSpec((tm, LANES), row)],
        out_shape=[
            jax.ShapeDtypeStruct((n * (d // LANES), LANES), F32),
            jax.ShapeDtypeStruct((n, LANES), jnp.int32),
            jax.ShapeDtypeStruct((n, LANES), F32),
        ],
        compiler_params=_params("arbitrary"),
        name="router",
    )(x, gain.reshape(1, d), wr)


def _expert_kernel(bexp_ref, nused_ref, rtok_ref, h_hbm, wg_ref, wu_ref, wd_ref, gate_ref, y_ref,
                   xs_ref, xb_ref, acc_ref, sem):
    del bexp_ref
    i = pl.program_id(0)
    j = pl.program_id(1)
    tmb, d = xb_ref.shape
    nc = d // LANES
    n_used = nused_ref[0]

    def row_copy(tok, slot, r):
        return pltpu.make_async_copy(h_hbm.at[pl.ds(pl.multiple_of(tok * nc, nc), nc), :],
                                     xs_ref.at[slot, pl.ds(pl.multiple_of(r * nc, nc), nc), :], sem.at[slot])

    def start_gather(blk, slot):
        def issue(r, carry):
            row_copy(rtok_ref[blk * tmb + r], slot, r).start()
            return carry
        lax.fori_loop(0, tmb, issue, 0, unroll=8)

    def wait_gather(slot):
        pltpu.make_async_copy(h_hbm.at[pl.ds(0, tmb * nc), :], xs_ref.at[slot], sem.at[slot]).wait()

    @pl.when(i < n_used)
    def _():
        slot = i % 2

        @pl.when(j == 0)
        def _():
            @pl.when(i == 0)
            def _():
                start_gather(0, 0)

            wait_gather(slot)

            @pl.when(i + 1 < n_used)
            def _():
                start_gather(i + 1, 1 - slot)

            for c in range(nc):
                xb_ref[:, c * LANES:(c + 1) * LANES] = xs_ref[slot, pl.ds(c, tmb, stride=nc), :].astype(BF16)
            acc_ref[...] = jnp.zeros_like(acc_ref)

        x = xb_ref[...]
        g = jnp.dot(x, wg_ref[0], preferred_element_type=F32)
        u = jnp.dot(x, wu_ref[0], preferred_element_type=F32)
        acc_ref[...] += jnp.dot(_silu_mul(g, u).astype(BF16), wd_ref[0], preferred_element_type=F32)

    last = j == pl.num_programs(1) - 1

    @pl.when(last & (i < n_used))
    def _():
        _store_chunk_rows(y_ref, acc_ref[...] * gate_ref[...])

    @pl.when(last & (i >= n_used))
    def _():
        y_ref[...] = jnp.zeros_like(y_ref)


def _expert_ffn(h_rows, block_expert, n_used, row_tok, row_gate, w_gate, w_up, w_down):
    d = w_gate.shape[1]
    f = w_gate.shape[2]
    tmb = MOE_BLOCK
    tf = MOE_F_BLOCK
    n_blocks = block_expert.shape[0]
    nf = f // tf
    assert f % tf == 0

    def fblk(i, j, nu):
        return jnp.where(i < nu[0], j, nf - 1)

    grid_spec = pltpu.PrefetchScalarGridSpec(
        num_scalar_prefetch=3,
        grid=(n_blocks, nf),
        in_specs=[
            pl.BlockSpec(memory_space=pl.ANY),
            pl.BlockSpec((1, d, tf), lambda i, j, be, nu, rt: (be[i], 0, fblk(i, j, nu))),
            pl.BlockSpec((1, d, tf), lambda i, j, be, nu, rt: (be[i], 0, fblk(i, j, nu))),
            pl.BlockSpec((1, tf, d), lambda i, j, be, nu, rt: (be[i], fblk(i, j, nu), 0)),
            pl.BlockSpec((tmb, 1), lambda i, j, be, nu, rt: (i, 0)),
        ],
        out_specs=pl.BlockSpec((tmb * (d // LANES), LANES), lambda i, j, be, nu, rt: (i, 0)),
        scratch_shapes=[
            pltpu.VMEM((2, tmb * (d // LANES), LANES), F32),
            pltpu.VMEM((tmb, d), BF16),
            pltpu.VMEM((tmb, d), F32),
            pltpu.SemaphoreType.DMA((2,)),
        ],
    )
    return pl.pallas_call(
        _expert_kernel,
        grid_spec=grid_spec,
        out_shape=jax.ShapeDtypeStruct((n_blocks * tmb * (d // LANES), LANES), F32),
        compiler_params=_params("arbitrary", "arbitrary"),
        name="expert_ffn",
    )(block_expert, n_used, row_tok, h_rows, w_gate, w_up, w_down, row_gate)


def _combine_kernel(dest_ref, ys_hbm, x_ref, o_ref, buf_ref, sem):
    i = pl.program_id(0)
    nsteps = pl.num_programs(0)
    tc, d = x_ref.shape
    nc = d // LANES

    def row_copy(row, slot, dst):
        return pltpu.make_async_copy(ys_hbm.at[pl.ds(pl.multiple_of(row * nc, nc), nc), :],
                                     buf_ref.at[slot, pl.ds(pl.multiple_of(dst * nc, nc), nc), :], sem.at[slot])

    def start_gather(step, slot):
        def issue(r, carry):
            row_copy(dest_ref[step * (2 * tc) + r], slot, (r % 2) * tc + r // 2).start()
            return carry
        lax.fori_loop(0, 2 * tc, issue, 0, unroll=8)

    def wait_gather(slot):
        pltpu.make_async_copy(ys_hbm.at[pl.ds(0, 2 * tc * nc), :], buf_ref.at[slot], sem.at[slot]).wait()

    slot = i % 2

    @pl.when(i == 0)
    def _():
        start_gather(0, 0)

    wait_gather(slot)

    @pl.when(i + 1 < nsteps)
    def _():
        start_gather(i + 1, 1 - slot)

    for c in range(nc):
        sl = slice(c * LANES, (c + 1) * LANES)
        y0 = buf_ref[slot, pl.ds(c, tc, stride=nc), :]
        y1 = buf_ref[slot, pl.ds(tc * nc + c, tc, stride=nc), :]
        o_ref[:, sl] = x_ref[:, sl] + (y0 + y1)


def _combine(x, ys, dest):
    n, d = x.shape
    tc = min(COMBINE_BLOCK, n)
    grid_spec = pltpu.PrefetchScalarGridSpec(
        num_scalar_prefetch=1,
        grid=(n // tc,),
        in_specs=[
            pl.BlockSpec(memory_space=pl.ANY),
            pl.BlockSpec((tc, d), lambda i, dr: (i, 0)),
        ],
        out_specs=pl.BlockSpec((tc, d), lambda i, dr: (i, 0)),
        scratch_shapes=[pltpu.VMEM((2, 2 * tc * (d // LANES), LANES), F32), pltpu.SemaphoreType.DMA((2,))],
    )
    return pl.pallas_call(
        _combine_kernel,
        grid_spec=grid_spec,
        out_shape=jax.ShapeDtypeStruct((n, d), F32),
        compiler_params=_params("arbitrary"),
        name="moe_combine",
    )(dest, ys, x)


def _routing_tables(idx, gate, tmb):
    n = idx.shape[0]
    nk = n * TOP_K
    e_flat = idx[:, :TOP_K].reshape(-1)
    g_flat = gate[:, :TOP_K].reshape(-1)
    onehot = (e_flat[:, None] == jnp.arange(N_EXPERTS, dtype=jnp.int32)[None, :]).astype(jnp.int32)
    csum = jnp.cumsum(onehot, axis=0)
    rank = jnp.sum(onehot * csum, axis=1) - 1
    counts = csum[-1]
    padded = (counts + tmb - 1) // tmb * tmb
    pad_end = jnp.cumsum(padded)
    pad_start = pad_end - padded
    dest = (jnp.sum(onehot * pad_start[None, :], axis=1) + rank).astype(jnp.int32)
    n_blocks = -(-nk // tmb) + N_EXPERTS
    block_expert = jnp.minimum(
        jnp.searchsorted(pad_end, jnp.arange(n_blocks, dtype=jnp.int32) * tmb, side="right"), N_EXPERTS - 1
    ).astype(jnp.int32)
    n_used = (pad_end[-1] // tmb).astype(jnp.int32).reshape(1)
    row_tok = jnp.zeros((n_blocks * tmb,), jnp.int32).at[dest].set(jnp.arange(nk, dtype=jnp.int32) // TOP_K)
    row_gate = jnp.zeros((n_blocks * tmb,), F32).at[dest].set(g_flat).reshape(-1, 1)
    return dest, block_expert, n_used, row_tok, row_gate


def _rope_tables(seq_len):
    half = HEAD_DIM // 2
    inv_freq = ROPE_THETA ** (-jnp.arange(half, dtype=F32) / half)
    ang = jnp.arange(seq_len, dtype=F32)[:, None] * inv_freq[None, :]
    cos, sin = jnp.cos(ang), jnp.sin(ang)
    return jnp.concatenate([cos, cos], axis=-1), jnp.concatenate([-sin, sin], axis=-1)


def kernel(x_prompt, x_sample, norm_mix, norm_ffn, na_w_qkv, na_q_gain, na_k_gain, na_rpb, na_w_o,
           da_w_qkv, da_q_gain, da_k_gain, da_w_o, ffn_w_gate, ffn_w_up, ffn_w_down,
           moe_w_router, moe_w_gate, moe_w_up, moe_w_down):
    d = x_prompt.shape[-1]
    trunks = []
    off = 0
    for xin in (x_prompt, x_sample):
        b, s, _ = xin.shape
        trunks.append((off, b, s))
        off += b * s
    n = off
    x = jnp.concatenate([x_prompt.reshape(-1, d), x_sample.reshape(-1, d)], axis=0)
    depth = norm_mix.shape[0]
    tm = min(TOKEN_BLOCK, n)

    for layer in range(depth):
        lj = layer // 2
        if layer % 2 == 0:
            heads = na_w_qkv.shape[2] // (3 * HEAD_DIM)
            tn = 1024
            per = heads * HEAD_DIM // tn
            hg = jnp.concatenate([
                jnp.broadcast_to(na_q_gain[lj], (per, HEAD_DIM)),
                jnp.broadcast_to(na_k_gain[lj], (per, HEAD_DIM)),
                jnp.ones((per, HEAD_DIM), F32),
            ]).reshape(3 * per, 1, HEAD_DIM)
            qkv = _qkv_project(x, norm_mix[layer], na_w_qkv[lj].astype(BF16), hg, 3 * per, 2 * per)
            tables = _na_bias_tables(na_rpb[lj])
            att = jnp.zeros((n, heads * HEAD_DIM), BF16)
            for (toff, b, s) in trunks:
                att = _na_attention(qkv, tables, att, toff, b, s, heads)
            x = _proj_residual(att, na_w_o[lj].astype(BF16), x)
            x = _dense_ffn(x, norm_ffn[layer], ffn_w_gate[lj].astype(BF16), ffn_w_up[lj].astype(BF16),
                           ffn_w_down[lj].astype(BF16))
        else:
            groups = len(DIL_CONFIGS)
            heads = da_w_qkv.shape[2] // (groups * 3 * HEAD_DIM)
            dils = [dil for _, dil in DIL_CONFIGS]
            ones = jnp.ones((HEAD_DIM,), F32)
            hg = jnp.stack([t for g in range(groups) for t in (da_q_gain[lj, g], da_k_gain[lj, g], ones)
                            for _ in range(2)])
            hg = hg.reshape(groups * 3 * 2, 1, HEAD_DIM)
            (o0, b0, s0), (o1, b1, s1) = trunks
            assert s0 % tm == 0 and s1 % tm == 0 and o1 % tm == 0
            nb0, p0, p1 = o1 // tm, s0 // tm, s1 // tm
            pos_map = lambda i: jnp.where(i < nb0, i % p0, (i - nb0) % p1)
            qkvs = _qkv_project_dilated(x, norm_mix[layer], da_w_qkv[lj].astype(BF16), hg,
                                        _rope_tables(max(s0, s1)), pos_map, dils, heads)
            outs, lses = [], []
            for g, dil in enumerate(dils):
                o_g = jnp.zeros((dil, n // dil, heads * HEAD_DIM), F32)
                lse_g = jnp.zeros((dil, n // dil, LANES), F32)
                for (toff, b, s) in trunks:
                    o_g, lse_g = _band_attention(qkvs[g], o_g, lse_g, toff, b, s, g)
                outs.append(o_g)
                lses.append(lse_g)
            x = _merge_proj(outs, lses, da_w_o[lj].astype(BF16), x)
            h, idx, gate = _router(x, norm_ffn[layer], moe_w_router[lj])
            dest, block_expert, n_used, row_tok, row_gate = _routing_tables(idx, gate, MOE_BLOCK)
            ys = _expert_ffn(h, block_expert, n_used, row_tok, row_gate, moe_w_gate[lj].astype(BF16),
                             moe_w_up[lj].astype(BF16), moe_w_down[lj].astype(BF16))
            x = _combine(x, ys, dest)

    outs = []
    for (toff, b, s), xin in zip(trunks, (x_prompt, x_sample)):
        outs.append(x[toff:toff + b * s].reshape(xin.shape))
    return tuple(outs)
```

```python
import functools

import jax
import jax.numpy as jnp
from jax import lax
from jax.experimental import pallas as pl
from jax.experimental.pallas import tpu as pltpu

F32 = jnp.float32
BF16 = jnp.bfloat16

HEAD_DIM = 128
GRID_W = 64
NA_ROWS = 8
NA_COLS = 16
NA_GROUP = NA_ROWS // 2
NA_KEY_ROWS = 3 * NA_GROUP
NA_HEAD_BLOCK = 4
DIL_CONFIGS = ((128, 1), (512, 4), (2048, 16))
N_EXPERTS = 8
TOP_K = 2
ROPE_THETA = 10000.0
EPS = 1e-6
NEG_INF = -1e30
ATTN_SCALE = HEAD_DIM ** -0.5

LANES = 128
VMEM_LIMIT = 52 * 1024 * 1024

TOKEN_BLOCK = 1024
FFN_TOKEN_BLOCK = 512
FFN_F_BLOCK = 512
BAND_Q_BLOCK = 512
BAND_Q_TILE = 128
MOE_BLOCK = 512
MOE_F_BLOCK = 512
COMBINE_BLOCK = 256


def _params(*sem):
    return pltpu.CompilerParams(dimension_semantics=sem, vmem_limit_bytes=VMEM_LIMIT)


def _rms_rows(x, gain):
    ms = jnp.mean(x * x, axis=-1, keepdims=True)
    return x * lax.rsqrt(ms + EPS) * gain


def _qkv_kernel(x_ref, gain_ref, w_ref, hg_ref, o_ref, h_ref, *, v_every, v_from):
    j = pl.program_id(1)

    @pl.when(j == 0)
    def _():
        h_ref[...] = _rms_rows(x_ref[...], gain_ref[...]).astype(BF16)

    acc = jnp.dot(h_ref[...], w_ref[...], preferred_element_type=F32)
    is_qk = (j % v_every) < v_from

    @pl.when(is_qk)
    def _():
        for h in range(acc.shape[1] // HEAD_DIM):
            sl = slice(h * HEAD_DIM, (h + 1) * HEAD_DIM)
            o_ref[:, sl] = _rms_rows(acc[:, sl], hg_ref[0]).astype(o_ref.dtype)

    @pl.when(jnp.logical_not(is_qk))
    def _():
        o_ref[...] = acc.astype(o_ref.dtype)


def _qkv_project(x, gain, w, head_gain, v_every, v_from):
    n, d = x.shape
    n_out = w.shape[1]
    tm = min(TOKEN_BLOCK, n)
    tn = n_out // head_gain.shape[0]
    return pl.pallas_call(
        functools.partial(_qkv_kernel, v_every=v_every, v_from=v_from),
        grid=(n // tm, n_out // tn),
        in_specs=[
            pl.BlockSpec((tm, d), lambda i, j: (i, 0)),
            pl.BlockSpec((1, d), lambda i, j: (0, 0)),
            pl.BlockSpec((d, tn), lambda i, j: (0, j)),
            pl.BlockSpec((1, 1, HEAD_DIM), lambda i, j: (j, 0, 0)),
        ],
        out_specs=pl.BlockSpec((tm, tn), lambda i, j: (i, j)),
        out_shape=jax.ShapeDtypeStruct((n, n_out), BF16),
        scratch_shapes=[pltpu.VMEM((tm, d), BF16)],
        compiler_params=_params("arbitrary", "arbitrary"),
        name="qkv_project",
    )(x, gain.reshape(1, d), w, head_gain)


def _qkv_dil_kernel(x_ref, gain_ref, w_ref, hg_ref, cos_ref, sin_ref, *rest, dils, per_group, v_from):
    out_refs, (h_ref, stage_ref) = rest[:len(dils)], rest[len(dils):]
    j = pl.program_id(1)
    tm = x_ref.shape[0]

    @pl.when(j == 0)
    def _():
        h_ref[...] = _rms_rows(x_ref[...], gain_ref[...]).astype(BF16)

    acc = jnp.dot(h_ref[...], w_ref[...], preferred_element_type=F32)
    is_qk = (j % per_group) < v_from

    @pl.when(is_qk)
    def _():
        for h in range(acc.shape[1] // HEAD_DIM):
            sl = slice(h * HEAD_DIM, (h + 1) * HEAD_DIM)
            y = _rms_rows(acc[:, sl], hg_ref[0])
            stage_ref[h] = y * cos_ref[...] + pltpu.roll(y, HEAD_DIM // 2, 1) * sin_ref[...]

    @pl.when(jnp.logical_not(is_qk))
    def _():
        for h in range(acc.shape[1] // HEAD_DIM):
            stage_ref[h] = acc[:, h * HEAD_DIM:(h + 1) * HEAD_DIM]

    for g, (dil, o_ref) in enumerate(zip(dils, out_refs)):
        @pl.when(j // per_group == g)
        def _(dil=dil, o_ref=o_ref):
            for r in range(dil):
                for h in range(acc.shape[1] // HEAD_DIM):
                    rows = stage_ref[h, pl.ds(r, tm // dil, stride=dil), :]
                    o_ref[r, :, h * HEAD_DIM:(h + 1) * HEAD_DIM] = rows.astype(o_ref.dtype)


def _qkv_project_dilated(x, gain, w, head_gain, rope_tabs, pos_map, dils, heads):
    n, d = x.shape
    n_out = w.shape[1]
    tm = min(TOKEN_BLOCK, n)
    width = heads * HEAD_DIM
    tn = width // 2
    per_which = width // tn
    per_group = 3 * per_which
    assert n_out == len(dils) * 3 * width and head_gain.shape[0] == n_out // tn

    def out_map(g):
        def index(i, j):
            local = jnp.clip(j - g * per_group, 0, per_group - 1)
            return (local // per_which, 0, i, local % per_which)
        return index

    return pl.pallas_call(
        functools.partial(_qkv_dil_kernel, dils=tuple(dils), per_group=per_group, v_from=2 * per_which),
        grid=(n // tm, n_out // tn),
        in_specs=[
            pl.BlockSpec((tm, d), lambda i, j: (i, 0)),
            pl.BlockSpec((1, d), lambda i, j: (0, 0)),
            pl.BlockSpec((d, tn), lambda i, j: (0, j)),
            pl.BlockSpec((1, 1, HEAD_DIM), lambda i, j: (j, 0, 0)),
            pl.BlockSpec((tm, HEAD_DIM), lambda i, j: (pos_map(i), 0)),
            pl.BlockSpec((tm, HEAD_DIM), lambda i, j: (pos_map(i), 0)),
        ],
        out_specs=[pl.BlockSpec((None, dil, tm // dil, tn), out_map(g)) for g, dil in enumerate(dils)],
        out_shape=[jax.ShapeDtypeStruct((3, dil, n // dil, width), BF16) for dil in dils],
        scratch_shapes=[pltpu.VMEM((tm, d), BF16), pltpu.VMEM((tn // HEAD_DIM, tm, HEAD_DIM), F32)],
        compiler_params=_params("arbitrary", "arbitrary"),
        name="qkv_project_dilated",
    )(x, gain.reshape(1, d), w, head_gain, *rope_tabs)


def _na_bias_tables(rpb):
    half = NA_ROWS // 2
    assert NA_GROUP == half
    qi = jnp.arange(NA_GROUP)[:, None]
    kj = jnp.arange(NA_KEY_ROWS)[None, :]
    dr = jnp.broadcast_to(kj - NA_GROUP - qi, (3, NA_GROUP, NA_KEY_ROWS))
    rvalid = jnp.stack([
        jnp.broadcast_to(kj >= NA_GROUP, (NA_GROUP, NA_KEY_ROWS)),
        (kj - qi >= 0) & (kj - qi < NA_ROWS),
        jnp.broadcast_to(kj < NA_ROWS, (NA_GROUP, NA_KEY_ROWS)),
    ])
    c = jnp.arange(GRID_W)
    cs = jnp.clip(c - NA_COLS // 2, 0, GRID_W - NA_COLS)
    cvalid = (c[None, :] >= cs[:, None]) & (c[None, :] < cs[:, None] + NA_COLS)
    dc = jnp.clip(c[None, :] - c[:, None], 1 - NA_COLS, NA_COLS - 1)
    dri = jnp.clip(dr, 1 - NA_ROWS, NA_ROWS - 1) + NA_ROWS - 1
    sel_r = jax.nn.one_hot(dri, 2 * NA_ROWS - 1, dtype=F32)
    sel_c = jax.nn.one_hot(dc + NA_COLS - 1, 2 * NA_COLS - 1, dtype=F32)
    hp = lax.Precision.HIGHEST
    tmp = jnp.einsum("hab,qcb->haqc", rpb.astype(F32), sel_c, precision=hp)
    bias = jnp.einsum("haqc,vgka->vhgqkc", tmp, sel_r, precision=hp)
    valid = rvalid[:, None, :, None, :, None] & cvalid[None, None, None, :, None, :]
    bias = jnp.where(valid, bias, NEG_INF)
    h = rpb.shape[0]
    return bias.reshape(3, h, NA_GROUP * GRID_W, NA_KEY_ROWS * GRID_W)


def _na_kernel(q_ref, kp_ref, k_ref, kn_ref, vp_ref, v_ref, vn_ref, tab_ref, prev_ref, o_ref):
    del prev_ref
    gq = q_ref.shape[0]
    nt = (((1,), (1,)), ((), ()))
    for h in range(q_ref.shape[1] // HEAD_DIM):
        sl = slice(h * HEAD_DIM, (h + 1) * HEAD_DIM)
        q = q_ref[:, sl]
        s = jnp.concatenate(
            [lax.dot_general(q, kr[:, sl], nt, preferred_element_type=F32) for kr in (kp_ref, k_ref, kn_ref)],
            axis=1)
        s = s * ATTN_SCALE + tab_ref[0, h]
        m = jnp.max(s, axis=-1, keepdims=True)
        p = jnp.exp(s - m)
        den = jnp.sum(p, axis=-1, keepdims=True)
        pb = p.astype(BF16)
        o = (jnp.dot(pb[:, :gq], vp_ref[:, sl], preferred_element_type=F32)
             + jnp.dot(pb[:, gq:2 * gq], v_ref[:, sl], preferred_element_type=F32)
             + jnp.dot(pb[:, 2 * gq:], vn_ref[:, sl], preferred_element_type=F32))
        o_ref[:, sl] = (o / den).astype(o_ref.dtype)


def _na_attention(qkv, tables, out_prev, tok_off, batch, seq, heads):
    rows = seq // GRID_W
    groups = rows // NA_GROUP
    hb = NA_HEAD_BLOCK
    assert rows % NA_GROUP == 0 and rows >= NA_ROWS and tok_off % seq == 0 and heads % hb == 0
    gq = NA_GROUP * GRID_W
    blk0 = tok_off // gq
    hblocks = heads // hb

    def variant(g):
        return jnp.where(g == 0, 0, jnp.where(g == groups - 1, 2, 1))

    def rows_of(which, shift):
        def index(b, h, g):
            gg = jnp.clip(g + shift, 0, groups - 1)
            return (blk0 + b * groups + gg, which * hblocks + h)
        return index

    blk = lambda index: pl.BlockSpec((gq, hb * HEAD_DIM), index)
    return pl.pallas_call(
        _na_kernel,
        grid=(batch, hblocks, groups),
        in_specs=[
            blk(rows_of(0, 0)),
            blk(rows_of(1, -1)), blk(rows_of(1, 0)), blk(rows_of(1, 1)),
            blk(rows_of(2, -1)), blk(rows_of(2, 0)), blk(rows_of(2, 1)),
            pl.BlockSpec((1, hb, gq, NA_KEY_ROWS * GRID_W), lambda b, h, g: (variant(g), h, 0, 0)),
            pl.BlockSpec(memory_space=pl.ANY),
        ],
        out_specs=blk(rows_of(0, 0)),
        out_shape=jax.ShapeDtypeStruct(out_prev.shape, out_prev.dtype),
        input_output_aliases={8: 0},
        compiler_params=_params("arbitrary", "arbitrary", "arbitrary"),
        name="na_attention",
    )(qkv, qkv, qkv, qkv, qkv, qkv, qkv, tables, out_prev)


def _proj_residual_kernel(a_ref, w_ref, x_ref, o_ref):
    o_ref[...] = x_ref[...] + jnp.dot(a_ref[...], w_ref[...], preferred_element_type=F32)


def _proj_residual(a, w, x):
    n, k = a.shape
    d = w.shape[1]
    tm = min(TOKEN_BLOCK, n)
    tn = min(1024, d)
    return pl.pallas_call(
        _proj_residual_kernel,
        grid=(n // tm, d // tn),
        in_specs=[
            pl.BlockSpec((tm, k), lambda i, j: (i, 0)),
            pl.BlockSpec((k, tn), lambda i, j: (0, j)),
            pl.BlockSpec((tm, tn), lambda i, j: (i, j)),
        ],
        out_specs=pl.BlockSpec((tm, tn), lambda i, j: (i, j)),
        out_shape=jax.ShapeDtypeStruct((n, d), F32),
        compiler_params=_params("arbitrary", "arbitrary"),
        name="proj_residual",
    )(a, w, x)


def _silu_mul(g, u):
    return g * (1.0 / (1.0 + jnp.exp(-g))) * u


def _ffn_kernel(x_ref, gain_ref, wg_ref, wu_ref, wd_ref, o_ref, h_ref, acc_ref):
    j = pl.program_id(1)

    @pl.when(j == 0)
    def _():
        h_ref[...] = _rms_rows(x_ref[...], gain_ref[...]).astype(BF16)
        acc_ref[...] = jnp.zeros_like(acc_ref)

    h = h_ref[...]
    g = jnp.dot(h, wg_ref[...], preferred_element_type=F32)
    u = jnp.dot(h, wu_ref[...], preferred_element_type=F32)
    acc_ref[...] += jnp.dot(_silu_mul(g, u).astype(BF16), wd_ref[...], preferred_element_type=F32)

    @pl.when(j == pl.num_programs(1) - 1)
    def _():
        o_ref[...] = x_ref[...] + acc_ref[...]


def _dense_ffn(x, gain, w_gate, w_up, w_down):
    n, d = x.shape
    f = w_gate.shape[1]
    tm = min(FFN_TOKEN_BLOCK, n)
    tf = FFN_F_BLOCK
    assert f % tf == 0
    return pl.pallas_call(
        _ffn_kernel,
        grid=(n // tm, f // tf),
        in_specs=[
            pl.BlockSpec((tm, d), lambda i, j: (i, 0)),
            pl.BlockSpec((1, d), lambda i, j: (0, 0)),
            pl.BlockSpec((d, tf), lambda i, j: (0, j)),
            pl.BlockSpec((d, tf), lambda i, j: (0, j)),
            pl.BlockSpec((tf, d), lambda i, j: (j, 0)),
        ],
        out_specs=pl.BlockSpec((tm, d), lambda i, j: (i, 0)),
        out_shape=jax.ShapeDtypeStruct((n, d), F32),
        scratch_shapes=[pltpu.VMEM((tm, d), BF16), pltpu.VMEM((tm, d), F32)],
        compiler_params=_params("arbitrary", "arbitrary"),
        name="dense_ffn",
    )(x, gain.reshape(1, d), w_gate, w_up, w_down)


def _band_kernel(q_ref, kp_ref, k_ref, kn_ref, vp_ref, v_ref, vn_ref, o_prev, lse_prev, o_ref, lse_ref,
                 kwin_ref, vwin_ref, *, length, tq, half):
    del o_prev, lse_prev
    tqb = q_ref.shape[0]
    kw = tq + 2 * half
    q0 = pl.program_id(2) * tqb
    for win, parts in ((kwin_ref, (kp_ref, k_ref, kn_ref)), (vwin_ref, (vp_ref, v_ref, vn_ref))):
        win[:half] = parts[0][...]
        win[half:half + tqb] = parts[1][...]
        win[half + tqb:] = parts[2][...]
    lane = lax.broadcasted_iota(jnp.int32, (tq, LANES), 1)

    def body(t, carry):
        r0 = pl.multiple_of(t * tq, tq)
        qpos = q0 + r0 + lax.broadcasted_iota(jnp.int32, (tq, kw), 0)
        kpos = q0 + r0 - half + lax.broadcasted_iota(jnp.int32, (tq, kw), 1)
        ok = (jnp.abs(qpos - kpos) <= half) & (kpos >= 0) & (kpos < length)
        lse_tile = jnp.zeros((tq, LANES), F32)
        for h in range(q_ref.shape[1] // HEAD_DIM):
            sl = slice(h * HEAD_DIM, (h + 1) * HEAD_DIM)
            q = q_ref[pl.ds(r0, tq), sl]
            k = kwin_ref[pl.ds(r0, kw), sl]
            v = vwin_ref[pl.ds(r0, kw), sl]
            s = lax.dot_general(q, k, (((1,), (1,)), ((), ())), preferred_element_type=F32) * ATTN_SCALE
            s = jnp.where(ok, s, NEG_INF)
            m = jnp.max(s, axis=-1, keepdims=True)
            p = jnp.exp(s - m)
            den = jnp.sum(p, axis=-1, keepdims=True)
            o = jnp.dot(p.astype(BF16), v, preferred_element_type=F32) / den
            o_ref[pl.ds(r0, tq), sl] = o
            lse_tile = jnp.where(lane == h, m + jnp.log(den), lse_tile)
        lse_ref[pl.ds(r0, tq), :] = lse_tile
        return carry

    lax.fori_loop(0, tqb // tq, body, 0)


def _band_attention(qkv, o_prev, lse_prev, tok_off, batch, seq, group):
    window, dil = DIL_CONFIGS[group]
    half = window // (2 * dil)
    length = seq // dil
    assert seq % dil == 0 and length % half == 0 and tok_off % seq == 0 and qkv.shape[1] == dil
    tqb = min(BAND_Q_BLOCK, length)
    tq = min(BAND_Q_TILE, tqb)
    assert length % tqb == 0 and tqb % tq == 0 and tqb % half == 0
    width = qkv.shape[3]
    nqb = length // tqb
    hpb = tqb // half
    seq0 = tok_off // seq

    def main(which):
        return lambda b, r, t: (which, r, (seq0 + b) * nqb + t, 0)

    def halo(which, side):
        def index(b, r, t):
            first = (seq0 + b) * nqb * hpb
            blk = jnp.clip(t * hpb + (hpb if side else -1), 0, nqb * hpb - 1)
            return (which, r, first + blk, 0)
        return index

    mspec = lambda which: pl.BlockSpec((None, None, tqb, width), main(which))
    hspec = lambda which, side: pl.BlockSpec((None, None, half, width), halo(which, side))
    return pl.pallas_call(
        functools.partial(_band_kernel, length=length, tq=tq, half=half),
        grid=(batch, dil, nqb),
        in_specs=[
            mspec(0),
            hspec(1, 0), mspec(1), hspec(1, 1),
            hspec(2, 0), mspec(2), hspec(2, 1),
            pl.BlockSpec(memory_space=pl.ANY),
            pl.BlockSpec(memory_space=pl.ANY),
        ],
        out_specs=[
            pl.BlockSpec((None, tqb, width), lambda b, r, t: (r, (seq0 + b) * nqb + t, 0)),
            pl.BlockSpec((None, tqb, LANES), lambda b, r, t: (r, (seq0 + b) * nqb + t, 0)),
        ],
        out_shape=[jax.ShapeDtypeStruct(o_prev.shape, F32), jax.ShapeDtypeStruct(lse_prev.shape, F32)],
        input_output_aliases={7: 0, 8: 1},
        scratch_shapes=[pltpu.VMEM((tqb + 2 * half, width), BF16), pltpu.VMEM((tqb + 2 * half, width), BF16)],
        compiler_params=_params("arbitrary", "arbitrary", "arbitrary"),
        name="band_attention",
    )(qkv, qkv, qkv, qkv, qkv, qkv, qkv, o_prev, lse_prev)


def _merge_proj_kernel(*refs, dils):
    ng = len(dils)
    o_refs, l_refs = refs[:ng], refs[ng:2 * ng]
    w_ref, x_ref, out_ref, a_ref, os_ref, ls_ref = refs[2 * ng:]
    tm = x_ref.shape[0]

    heads = a_ref.shape[1] // HEAD_DIM
    for g, dil in enumerate(dils):
        for r in range(dil):
            ls_ref[g, pl.ds(r, tm // dil, stride=dil), :] = l_refs[g][r]
            for h in range(heads):
                os_ref[g, h, pl.ds(r, tm // dil, stride=dil), :] = o_refs[g][r, :, h * HEAD_DIM:(h + 1) * HEAD_DIM]

    lse = [ls_ref[g] for g in range(ng)]
    m = functools.reduce(jnp.maximum, lse)
    e = [jnp.exp(l - m) for l in lse]
    tot = functools.reduce(lambda p, q: p + q, e)
    wts = [eg / tot for eg in e]
    for h in range(heads):
        terms = [wts[g][:, h:h + 1] * os_ref[g, h] for g in range(ng)]
        a_ref[:, h * HEAD_DIM:(h + 1) * HEAD_DIM] = functools.reduce(lambda p, q: p + q, terms).astype(BF16)
    out_ref[...] = x_ref[...] + jnp.dot(a_ref[...], w_ref[...], preferred_element_type=F32)


def _merge_proj(outs, lses, w, x):
    n, d = x.shape
    width = w.shape[0]
    dils = tuple(o.shape[0] for o in outs)
    tm = min(512, n)
    row = lambda i: (i, 0)
    grp = lambda i: (0, i, 0)
    return pl.pallas_call(
        functools.partial(_merge_proj_kernel, dils=dils),
        grid=(n // tm,),
        in_specs=[pl.BlockSpec((dil, tm // dil, width), grp) for dil in dils]
        + [pl.BlockSpec((dil, tm // dil, LANES), grp) for dil in dils]
        + [pl.BlockSpec((width, d), lambda i: (0, 0)), pl.BlockSpec((tm, d), row)],
        out_specs=pl.BlockSpec((tm, d), row),
        out_shape=jax.ShapeDtypeStruct((n, d), F32),
        scratch_shapes=[
            pltpu.VMEM((tm, width), BF16),
            pltpu.VMEM((len(dils), width // HEAD_DIM, tm, HEAD_DIM), F32),
            pltpu.VMEM((len(dils), tm, LANES), F32),
        ],
        compiler_params=_params("arbitrary"),
        name="merge_proj",
    )(*outs, *lses, w, x)


def _store_chunk_rows(ref, x):
    rows, d = x.shape
    nc = d // LANES
    for c in range(nc):
        ref[pl.ds(c, rows, stride=nc), :] = x[:, c * LANES:(c + 1) * LANES]


def _router_kernel(x_ref, gain_ref, wr_ref, h_ref, idx_ref, gate_ref):
    h = _rms_rows(x_ref[...], gain_ref[...])
    _store_chunk_rows(h_ref, h)
    logits = jnp.dot(h, wr_ref[...], preferred_element_type=F32, precision=lax.Precision.HIGHEST)
    lane = lax.broadcasted_iota(jnp.int32, logits.shape, 1)
    lane_f = lane.astype(F32)
    logits = jnp.where(lane < N_EXPERTS, logits, -jnp.inf)
    m1 = jnp.max(logits, axis=-1, keepdims=True)
    i1 = jnp.min(jnp.where(logits == m1, lane_f, float(LANES)), axis=-1, keepdims=True)
    rest = jnp.where(lane_f == i1, -jnp.inf, logits)
    m2 = jnp.max(rest, axis=-1, keepdims=True)
    i2 = jnp.min(jnp.where(rest == m2, lane_f, float(LANES)), axis=-1, keepdims=True)
    e = jnp.exp(m2 - m1)
    tot = 1.0 + e
    idx_ref[...] = jnp.where(lane == 0, i1, jnp.where(lane == 1, i2, 0.0)).astype(jnp.int32)
    gate_ref[...] = jnp.where(lane == 0, 1.0 / tot, jnp.where(lane == 1, e / tot, 0.0))


def _router(x, gain, w_router):
    n, d = x.shape
    tm = min(512, n)
    wr = jnp.zeros((d, LANES), F32).at[:, :N_EXPERTS].set(w_router.astype(F32))
    row = lambda i: (i, 0)
    return pl.pallas_call(
        _router_kernel,
        grid=(n // tm,),
        in_specs=[
            pl.BlockSpec((tm, d), row),
            pl.BlockSpec((1, d), lambda i: (0, 0)),
            pl.BlockSpec((d, LANES), lambda i: (0, 0)),
        ],
        out_specs=[pl.BlockSpec((tm * (d // LANES), LANES), row), pl.BlockSpec((tm, LANES), row),
                   pl.BlockSpec((tm, LANES), row)],
        out_shape=[
            jax.ShapeDtypeStruct((n * (d // LANES), LANES), F32),
            jax.ShapeDtypeStruct((n, LANES), jnp.int32),
            jax.ShapeDtypeStruct((n, LANES), F32),
        ],
        compiler_params=_params("arbitrary"),
        name="router",
    )(x, gain.reshape(1, d), wr)


def _expert_kernel(bexp_ref, nused_ref, rtok_ref, h_hbm, wg_ref, wu_ref, wd_ref, gate_ref, y_ref,
                   xs_ref, xb_ref, acc_ref, sem):
    del bexp_ref
    i = pl.program_id(0)
    j = pl.program_id(1)
    tmb, d = xb_ref.shape
    nc = d // LANES
    n_used = nused_ref[0]

    def row_copy(tok, slot, r):
        return pltpu.make_async_copy(h_hbm.at[pl.ds(pl.multiple_of(tok * nc, nc), nc), :],
                                     xs_ref.at[slot, pl.ds(pl.multiple_of(r * nc, nc), nc), :], sem.at[slot])

    def start_gather(blk, slot):
        def issue(r, carry):
            row_copy(rtok_ref[blk * tmb + r], slot, r).start()
            return carry
        lax.fori_loop(0, tmb, issue, 0, unroll=8)

    def wait_gather(slot):
        pltpu.make_async_copy(h_hbm.at[pl.ds(0, tmb * nc), :], xs_ref.at[slot], sem.at[slot]).wait()

    @pl.when(i < n_used)
    def _():
        slot = i % 2

        @pl.when(j == 0)
        def _():
            @pl.when(i == 0)
            def _():
                start_gather(0, 0)

            wait_gather(slot)

            @pl.when(i + 1 < n_used)
            def _():
                start_gather(i + 1, 1 - slot)

            for c in range(nc):
                xb_ref[:, c * LANES:(c + 1) * LANES] = xs_ref[slot, pl.ds(c, tmb, stride=nc), :].astype(BF16)
            acc_ref[...] = jnp.zeros_like(acc_ref)

        x = xb_ref[...]
        g = jnp.dot(x, wg_ref[0], preferred_element_type=F32)
        u = jnp.dot(x, wu_ref[0], preferred_element_type=F32)
        acc_ref[...] += jnp.dot(_silu_mul(g, u).astype(BF16), wd_ref[0], preferred_element_type=F32)

    last = j == pl.num_programs(1) - 1

    @pl.when(last & (i < n_used))
    def _():
        _store_chunk_rows(y_ref, acc_ref[...] * gate_ref[...])

    @pl.when(last & (i >= n_used))
    def _():
        y_ref[...] = jnp.zeros_like(y_ref)


def _expert_ffn(h_rows, block_expert, n_used, row_tok, row_gate, w_gate, w_up, w_down):
    d = w_gate.shape[1]
    f = w_gate.shape[2]
    tmb = MOE_BLOCK
    tf = MOE_F_BLOCK
    n_blocks = block_expert.shape[0]
    nf = f // tf
    assert f % tf == 0

    def fblk(i, j, nu):
        return jnp.where(i < nu[0], j, nf - 1)

    grid_spec = pltpu.PrefetchScalarGridSpec(
        num_scalar_prefetch=3,
        grid=(n_blocks, nf),
        in_specs=[
            pl.BlockSpec(memory_space=pl.ANY),
            pl.BlockSpec((1, d, tf), lambda i, j, be, nu, rt: (be[i], 0, fblk(i, j, nu))),
            pl.BlockSpec((1, d, tf), lambda i, j, be, nu, rt: (be[i], 0, fblk(i, j, nu))),
            pl.BlockSpec((1, tf, d), lambda i, j, be, nu, rt: (be[i], fblk(i, j, nu), 0)),
            pl.BlockSpec((tmb, 1), lambda i, j, be, nu, rt: (i, 0)),
        ],
        out_specs=pl.BlockSpec((tmb * (d // LANES), LANES), lambda i, j, be, nu, rt: (i, 0)),
        scratch_shapes=[
            pltpu.VMEM((2, tmb * (d // LANES), LANES), F32),
            pltpu.VMEM((tmb, d), BF16),
            pltpu.VMEM((tmb, d), F32),
            pltpu.SemaphoreType.DMA((2,)),
        ],
    )
    return pl.pallas_call(
        _expert_kernel,
        grid_spec=grid_spec,
        out_shape=jax.ShapeDtypeStruct((n_blocks * tmb * (d // LANES), LANES), F32),
        compiler_params=_params("arbitrary", "arbitrary"),
        name="expert_ffn",
    )(block_expert, n_used, row_tok, h_rows, w_gate, w_up, w_down, row_gate)


def _combine_kernel(dest_ref, ys_hbm, x_ref, o_ref, buf_ref, sem):
    i = pl.program_id(0)
    nsteps = pl.num_programs(0)
    tc, d = x_ref.shape
    nc = d // LANES

    def row_copy(row, slot, dst):
        return pltpu.make_async_copy(ys_hbm.at[pl.ds(pl.multiple_of(row * nc, nc), nc), :],
                                     buf_ref.at[slot, pl.ds(pl.multiple_of(dst * nc, nc), nc), :], sem.at[slot])

    def start_gather(step, slot):
        def issue(r, carry):
            row_copy(dest_ref[step * (2 * tc) + r], slot, (r % 2) * tc + r // 2).start()
            return carry
        lax.fori_loop(0, 2 * tc, issue, 0, unroll=8)

    def wait_gather(slot):
        pltpu.make_async_copy(ys_hbm.at[pl.ds(0, 2 * tc * nc), :], buf_ref.at[slot], sem.at[slot]).wait()

    slot = i % 2

    @pl.when(i == 0)
    def _():
        start_gather(0, 0)

    wait_gather(slot)

    @pl.when(i + 1 < nsteps)
    def _():
        start_gather(i + 1, 1 - slot)

    for c in range(nc):
        sl = slice(c * LANES, (c + 1) * LANES)
        y0 = buf_ref[slot, pl.ds(c, tc, stride=nc), :]
        y1 = buf_ref[slot, pl.ds(tc * nc + c, tc, stride=nc), :]
        o_ref[:, sl] = x_ref[:, sl] + (y0 + y1)


def _combine(x, ys, dest):
    n, d = x.shape
    tc = min(COMBINE_BLOCK, n)
    grid_spec = pltpu.PrefetchScalarGridSpec(
        num_scalar_prefetch=1,
        grid=(n // tc,),
        in_specs=[
            pl.BlockSpec(memory_space=pl.ANY),
            pl.BlockSpec((tc, d), lambda i, dr: (i, 0)),
        ],
        out_specs=pl.BlockSpec((tc, d), lambda i, dr: (i, 0)),
        scratch_shapes=[pltpu.VMEM((2, 2 * tc * (d // LANES), LANES), F32), pltpu.SemaphoreType.DMA((2,))],
    )
    return pl.pallas_call(
        _combine_kernel,
        grid_spec=grid_spec,
        out_shape=jax.ShapeDtypeStruct((n, d), F32),
        compiler_params=_params("arbitrary"),
        name="moe_combine",
    )(dest, ys, x)


def _routing_tables(idx, gate, tmb):
    n = idx.shape[0]
    nk = n * TOP_K
    e_flat = idx[:, :TOP_K].reshape(-1)
    g_flat = gate[:, :TOP_K].reshape(-1)
    onehot = (e_flat[:, None] == jnp.arange(N_EXPERTS, dtype=jnp.int32)[None, :]).astype(jnp.int32)
    csum = jnp.cumsum(onehot, axis=0)
    rank = jnp.sum(onehot * csum, axis=1) - 1
    counts = csum[-1]
    padded = (counts + tmb - 1) // tmb * tmb
    pad_end = jnp.cumsum(padded)
    pad_start = pad_end - padded
    dest = (jnp.sum(onehot * pad_start[None, :], axis=1) + rank).astype(jnp.int32)
    n_blocks = -(-nk // tmb) + N_EXPERTS
    block_expert = jnp.minimum(
        jnp.searchsorted(pad_end, jnp.arange(n_blocks, dtype=jnp.int32) * tmb, side="right"), N_EXPERTS - 1
    ).astype(jnp.int32)
    n_used = (pad_end[-1] // tmb).astype(jnp.int32).reshape(1)
    row_tok = jnp.zeros((n_blocks * tmb,), jnp.int32).at[dest].set(jnp.arange(nk, dtype=jnp.int32) // TOP_K)
    row_gate = jnp.zeros((n_blocks * tmb,), F32).at[dest].set(g_flat).reshape(-1, 1)
    return dest, block_expert, n_used, row_tok, row_gate


def _rope_tables(seq_len):
    half = HEAD_DIM // 2
    inv_freq = ROPE_THETA ** (-jnp.arange(half, dtype=F32) / half)
    ang = jnp.arange(seq_len, dtype=F32)[:, None] * inv_freq[None, :]
    cos, sin = jnp.cos(ang), jnp.sin(ang)
    return jnp.concatenate([cos, cos], axis=-1), jnp.concatenate([-sin, sin], axis=-1)


def kernel(x_prompt, x_sample, norm_mix, norm_ffn, na_w_qkv, na_q_gain, na_k_gain, na_rpb, na_w_o,
           da_w_qkv, da_q_gain, da_k_gain, da_w_o, ffn_w_gate, ffn_w_up, ffn_w_down,
           moe_w_router, moe_w_gate, moe_w_up, moe_w_down):
    d = x_prompt.shape[-1]
    trunks = []
    off = 0
    for xin in (x_prompt, x_sample):
        b, s, _ = xin.shape
        trunks.append((off, b, s))
        off += b * s
    n = off
    x = jnp.concatenate([x_prompt.reshape(-1, d), x_sample.reshape(-1, d)], axis=0)
    depth = norm_mix.shape[0]
    tm = min(TOKEN_BLOCK, n)

    for layer in range(depth):
        lj = layer // 2
        if layer % 2 == 0:
            heads = na_w_qkv.shape[2] // (3 * HEAD_DIM)
            tn = 1024
            per = heads * HEAD_DIM // tn
            hg = jnp.concatenate([
                jnp.broadcast_to(na_q_gain[lj], (per, HEAD_DIM)),
                jnp.broadcast_to(na_k_gain[lj], (per, HEAD_DIM)),
                jnp.ones((per, HEAD_DIM), F32),
            ]).reshape(3 * per, 1, HEAD_DIM)
            qkv = _qkv_project(x, norm_mix[layer], na_w_qkv[lj].astype(BF16), hg, 3 * per, 2 * per)
            tables = _na_bias_tables(na_rpb[lj])
            att = jnp.zeros((n, heads * HEAD_DIM), BF16)
            for (toff, b, s) in trunks:
                att = _na_attention(qkv, tables, att, toff, b, s, heads)
            x = _proj_residual(att, na_w_o[lj].astype(BF16), x)
            x = _dense_ffn(x, norm_ffn[layer], ffn_w_gate[lj].astype(BF16), ffn_w_up[lj].astype(BF16),
                           ffn_w_down[lj].astype(BF16))
        else:
            groups = len(DIL_CONFIGS)
            heads = da_w_qkv.shape[2] // (groups * 3 * HEAD_DIM)
            dils = [dil for _, dil in DIL_CONFIGS]
            ones = jnp.ones((HEAD_DIM,), F32)
            hg = jnp.stack([t for g in range(groups) for t in (da_q_gain[lj, g], da_k_gain[lj, g], ones)
                            for _ in range(2)])
            hg = hg.reshape(groups * 3 * 2, 1, HEAD_DIM)
            (o0, b0, s0), (o1, b1, s1) = trunks
            assert s0 % tm == 0 and s1 % tm == 0 and o1 % tm == 0
            nb0, p0, p1 = o1 // tm, s0 // tm, s1 // tm
            pos_map = lambda i: jnp.where(i < nb0, i % p0, (i - nb0) % p1)
            qkvs = _qkv_project_dilated(x, norm_mix[layer], da_w_qkv[lj].astype(BF16), hg,
                                        _rope_tables(max(s0, s1)), pos_map, dils, heads)
            outs, lses = [], []
            for g, dil in enumerate(dils):
                o_g = jnp.zeros((dil, n // dil, heads * HEAD_DIM), F32)
                lse_g = jnp.zeros((dil, n // dil, LANES), F32)
                for (toff, b, s) in trunks:
                    o_g, lse_g = _band_attention(qkvs[g], o_g, lse_g, toff, b, s, g)
                outs.append(o_g)
                lses.append(lse_g)
            x = _merge_proj(outs, lses, da_w_o[lj].astype(BF16), x)
            h, idx, gate = _router(x, norm_ffn[layer], moe_w_router[lj])
            dest, block_expert, n_used, row_tok, row_gate = _routing_tables(idx, gate, MOE_BLOCK)
            ys = _expert_ffn(h, block_expert, n_used, row_tok, row_gate, moe_w_gate[lj].astype(BF16),
                             moe_w_up[lj].astype(BF16), moe_w_down[lj].astype(BF16))
            x = _combine(x, ys, dest)

    outs = []
    for (toff, b, s), xin in zip(trunks, (x_prompt, x_sample)):
        outs.append(x[toff:toff + b * s].reshape(xin.shape))
    return tuple(outs)
```

```python
import functools

import jax
import jax.numpy as jnp
from jax import lax
from jax.experimental import pallas as pl
from jax.experimental.pallas import tpu as pltpu

F32 = jnp.float32
BF16 = jnp.bfloat16

HEAD_DIM = 128
GRID_W = 64
NA_ROWS = 8
NA_COLS = 16
NA_GROUP = NA_ROWS // 2
NA_KEY_ROWS = 3 * NA_GROUP
NA_HEAD_BLOCK = 4
DIL_CONFIGS = ((128, 1), (512, 4), (2048, 16))
N_EXPERTS = 8
TOP_K = 2
ROPE_THETA = 10000.0
EPS = 1e-6
NEG_INF = -1e30
ATTN_SCALE = HEAD_DIM ** -0.5

LANES = 128
VMEM_LIMIT = 52 * 1024 * 1024

TOKEN_BLOCK = 1024
QKV_DIL_TOKEN_BLOCK = 512
FFN_TOKEN_BLOCK = 512
FFN_F_BLOCK = 512
BAND_Q_BLOCK = 512
BAND_Q_TILE = 128
MOE_BLOCK = 512
MOE_F_BLOCK = 512
COMBINE_BLOCK = 256


def _params(*sem):
    return pltpu.CompilerParams(dimension_semantics=sem, vmem_limit_bytes=VMEM_LIMIT)


def _rms_rows(x, gain):
    ms = jnp.mean(x * x, axis=-1, keepdims=True)
    return x * lax.rsqrt(ms + EPS) * gain


def _pick_rows(i, first_blocks, xa_ref, xb_ref, use):
    @pl.when(i < first_blocks)
    def _():
        use(xa_ref)

    @pl.when(i >= first_blocks)
    def _():
        use(xb_ref)


def _two_batch_specs(tm, d, first_blocks, **kw):
    return [pl.BlockSpec((tm, d), lambda i, *_: (jnp.minimum(i, first_blocks - 1), 0), **kw),
            pl.BlockSpec((tm, d), lambda i, *_: (jnp.maximum(i - first_blocks, 0), 0), **kw)]


def _qkv_kernel(xa_ref, xb_ref, gain_ref, w_ref, hg_ref, o_ref, h_ref, *, first_blocks, v_every, v_from):
    j = pl.program_id(1)

    @pl.when(j == 0)
    def _():
        def norm(x_ref):
            h_ref[...] = _rms_rows(x_ref[...], gain_ref[...]).astype(BF16)
        _pick_rows(pl.program_id(0), first_blocks, xa_ref, xb_ref, norm)

    acc = jnp.dot(h_ref[...], w_ref[...], preferred_element_type=F32)
    is_qk = (j % v_every) < v_from

    @pl.when(is_qk)
    def _():
        for h in range(acc.shape[1] // HEAD_DIM):
            sl = slice(h * HEAD_DIM, (h + 1) * HEAD_DIM)
            o_ref[:, sl] = _rms_rows(acc[:, sl], hg_ref[0]).astype(o_ref.dtype)

    @pl.when(jnp.logical_not(is_qk))
    def _():
        o_ref[...] = acc.astype(o_ref.dtype)


def _qkv_project(xa, xb, gain, w, head_gain, v_every, v_from):
    d = xa.shape[1]
    n = xa.shape[0] + xb.shape[0]
    n_out = w.shape[1]
    tm = min(TOKEN_BLOCK, xa.shape[0], xb.shape[0])
    assert xa.shape[0] % tm == 0 and xb.shape[0] % tm == 0
    first_blocks = xa.shape[0] // tm
    tn = n_out // head_gain.shape[0]
    return pl.pallas_call(
        functools.partial(_qkv_kernel, first_blocks=first_blocks, v_every=v_every, v_from=v_from),
        grid=(n // tm, n_out // tn),
        in_specs=_two_batch_specs(tm, d, first_blocks, pipeline_mode=pl.Buffered(1)) + [
            pl.BlockSpec((1, d), lambda i, j: (0, 0)),
            pl.BlockSpec((d, tn), lambda i, j: (0, j)),
            pl.BlockSpec((1, 1, HEAD_DIM), lambda i, j: (j, 0, 0)),
        ],
        out_specs=pl.BlockSpec((tm, tn), lambda i, j: (i, j)),
        out_shape=jax.ShapeDtypeStruct((n, n_out), BF16),
        scratch_shapes=[pltpu.VMEM((tm, d), BF16)],
        compiler_params=_params("arbitrary", "arbitrary"),
        name="qkv_project",
    )(xa, xb, gain.reshape(1, d), w, head_gain)


def _qkv_dil_kernel(x_ref, gain_ref, w_ref, hg_ref, cos_ref, sin_ref, *rest, dils):
    out_refs, (h_ref, stage_ref) = rest[:len(dils)], rest[len(dils):]
    j = pl.program_id(1)
    tm, d = x_ref.shape
    tn = w_ref.shape[1]

    @pl.when(j == 0)
    def _():
        y = _rms_rows(x_ref[...], gain_ref[...])
        for c in range(d // LANES):
            stage_ref[c] = y[:, c * LANES:(c + 1) * LANES]
        for g, dil in enumerate(dils):
            rows = tm // dil
            for r in range(dil):
                for c in range(d // LANES):
                    h_ref[g, r * rows:(r + 1) * rows, c * LANES:(c + 1) * LANES] = (
                        stage_ref[c, pl.ds(r, rows, stride=dil), :].astype(BF16))

    group = j // 3
    which = j % 3
    acc = jnp.dot(h_ref[group], w_ref[...], preferred_element_type=F32)

    for g, (dil, o_ref) in enumerate(zip(dils, out_refs)):
        rows = tm // dil

        @pl.when((group == g) & (which < 2))
        def _(dil=dil, o_ref=o_ref, rows=rows):
            for h in range(tn // HEAD_DIM):
                sl = slice(h * HEAD_DIM, (h + 1) * HEAD_DIM)
                y = _rms_rows(acc[:, sl], hg_ref[0])
                y = (y * cos_ref[...] + pltpu.roll(y, HEAD_DIM // 2, 1) * sin_ref[...]).astype(o_ref.dtype)
                for r in range(dil):
                    o_ref[r, :, sl] = y[r * rows:(r + 1) * rows]

        @pl.when((group == g) & (which == 2))
        def _(dil=dil, o_ref=o_ref, rows=rows):
            v = acc.astype(o_ref.dtype)
            for r in range(dil):
                o_ref[r] = v[r * rows:(r + 1) * rows]


def _qkv_project_dilated(x, gain, w, head_gain, rope_tabs, pos_map, dils, heads, tm):
    n, d = x.shape
    n_out = w.shape[1]
    width = heads * HEAD_DIM
    ng = len(dils)
    assert n_out == ng * 3 * width and head_gain.shape[0] == ng * 3

    def out_map(g):
        def index(i, j):
            return (jnp.clip(j - 3 * g, 0, 2), 0, i, 0)
        return index

    rope_spec = pl.BlockSpec((None, tm, HEAD_DIM), lambda i, j: (j // 3, pos_map(i), 0))
    return pl.pallas_call(
        functools.partial(_qkv_dil_kernel, dils=tuple(dils)),
        grid=(n // tm, ng * 3),
        in_specs=[
            pl.BlockSpec((tm, d), lambda i, j: (i, 0)),
            pl.BlockSpec((1, d), lambda i, j: (0, 0)),
            pl.BlockSpec((d, width), lambda i, j: (0, j)),
            pl.BlockSpec((1, 1, HEAD_DIM), lambda i, j: (j, 0, 0)),
            rope_spec,
            rope_spec,
        ],
        out_specs=[pl.BlockSpec((None, dil, tm // dil, width), out_map(g)) for g, dil in enumerate(dils)],
        out_shape=[jax.ShapeDtypeStruct((3, dil, n // dil, width), BF16) for dil in dils],
        scratch_shapes=[pltpu.VMEM((ng, tm, d), BF16), pltpu.VMEM((d // LANES, tm, LANES), F32)],
        compiler_params=_params("arbitrary", "arbitrary"),
        name="qkv_project_dilated",
    )(x, gain.reshape(1, d), w, head_gain, *rope_tabs)


def _na_bias_tables(rpb):
    half = NA_ROWS // 2
    assert NA_GROUP == half
    qi = jnp.arange(NA_GROUP)[:, None]
    kj = jnp.arange(NA_KEY_ROWS)[None, :]
    dr = jnp.broadcast_to(kj - NA_GROUP - qi, (3, NA_GROUP, NA_KEY_ROWS))
    rvalid = jnp.stack([
        jnp.broadcast_to(kj >= NA_GROUP, (NA_GROUP, NA_KEY_ROWS)),
        (kj - qi >= 0) & (kj - qi < NA_ROWS),
        jnp.broadcast_to(kj < NA_ROWS, (NA_GROUP, NA_KEY_ROWS)),
    ])
    c = jnp.arange(GRID_W)
    cs = jnp.clip(c - NA_COLS // 2, 0, GRID_W - NA_COLS)
    cvalid = (c[None, :] >= cs[:, None]) & (c[None, :] < cs[:, None] + NA_COLS)
    dc = jnp.clip(c[None, :] - c[:, None], 1 - NA_COLS, NA_COLS - 1)
    dri = jnp.clip(dr, 1 - NA_ROWS, NA_ROWS - 1) + NA_ROWS - 1
    sel_r = jax.nn.one_hot(dri, 2 * NA_ROWS - 1, dtype=F32)
    sel_c = jax.nn.one_hot(dc + NA_COLS - 1, 2 * NA_COLS - 1, dtype=F32)
    hp = lax.Precision.HIGHEST
    tmp = jnp.einsum("hab,qcb->haqc", rpb.astype(F32), sel_c, precision=hp)
    bias = jnp.einsum("haqc,vgka->vhgqkc", tmp, sel_r, precision=hp)
    valid = rvalid[:, None, :, None, :, None] & cvalid[None, None, None, :, None, :]
    bias = jnp.where(valid, bias, NEG_INF)
    h = rpb.shape[0]
    return bias.reshape(3, h, NA_GROUP * GRID_W, NA_KEY_ROWS * GRID_W)


def _na_kernel(q_ref, kp_ref, k_ref, kn_ref, vp_ref, v_ref, vn_ref, tab_ref, prev_ref, o_ref):
    del prev_ref
    gq = q_ref.shape[0]
    nt = (((1,), (1,)), ((), ()))
    for h in range(q_ref.shape[1] // HEAD_DIM):
        sl = slice(h * HEAD_DIM, (h + 1) * HEAD_DIM)
        q = q_ref[:, sl]
        s = jnp.concatenate(
            [lax.dot_general(q, kr[:, sl], nt, preferred_element_type=F32) for kr in (kp_ref, k_ref, kn_ref)],
            axis=1)
        s = s * ATTN_SCALE + tab_ref[0, h]
        m = jnp.max(s, axis=-1, keepdims=True)
        p = jnp.exp(s - m)
        den = jnp.sum(p, axis=-1, keepdims=True)
        pb = p.astype(BF16)
        o = (jnp.dot(pb[:, :gq], vp_ref[:, sl], preferred_element_type=F32)
             + jnp.dot(pb[:, gq:2 * gq], v_ref[:, sl], preferred_element_type=F32)
             + jnp.dot(pb[:, 2 * gq:], vn_ref[:, sl], preferred_element_type=F32))
        o_ref[:, sl] = (o / den).astype(o_ref.dtype)


def _na_attention(qkv, tables, out_prev, tok_off, batch, seq, heads):
    rows = seq // GRID_W
    groups = rows // NA_GROUP
    hb = NA_HEAD_BLOCK
    assert rows % NA_GROUP == 0 and rows >= NA_ROWS and tok_off % seq == 0 and heads % hb == 0
    gq = NA_GROUP * GRID_W
    blk0 = tok_off // gq
    hblocks = heads // hb

    def variant(g):
        return jnp.where(g == 0, 0, jnp.where(g == groups - 1, 2, 1))

    def rows_of(which, shift):
        def index(b, h, g):
            gg = jnp.clip(g + shift, 0, groups - 1)
            return (blk0 + b * groups + gg, which * hblocks + h)
        return index

    blk = lambda index: pl.BlockSpec((gq, hb * HEAD_DIM), index)
    return pl.pallas_call(
        _na_kernel,
        grid=(batch, hblocks, groups),
        in_specs=[
            blk(rows_of(0, 0)),
            blk(rows_of(1, -1)), blk(rows_of(1, 0)), blk(rows_of(1, 1)),
            blk(rows_of(2, -1)), blk(rows_of(2, 0)), blk(rows_of(2, 1)),
            pl.BlockSpec((1, hb, gq, NA_KEY_ROWS * GRID_W), lambda b, h, g: (variant(g), h, 0, 0)),
            pl.BlockSpec(memory_space=pl.ANY),
        ],
        out_specs=blk(rows_of(0, 0)),
        out_shape=jax.ShapeDtypeStruct(out_prev.shape, out_prev.dtype),
        input_output_aliases={8: 0},
        compiler_params=_params("arbitrary", "arbitrary", "arbitrary"),
        name="na_attention",
    )(qkv, qkv, qkv, qkv, qkv, qkv, qkv, tables, out_prev)


def _proj_residual_kernel(a_ref, w_ref, xa_ref, xb_ref, o_ref, *, first_blocks):
    y = jnp.dot(a_ref[...], w_ref[...], preferred_element_type=F32)

    def add(x_ref):
        o_ref[...] = x_ref[...] + y
    _pick_rows(pl.program_id(1), first_blocks, xa_ref, xb_ref, add)


def _proj_residual(a, w, xa, xb):
    n, k = a.shape
    d = w.shape[1]
    tm = min(TOKEN_BLOCK, xa.shape[0], xb.shape[0])
    tn = min(1024, d)
    assert xa.shape[0] % tm == 0 and xb.shape[0] % tm == 0 and xa.shape[0] + xb.shape[0] == n
    first_blocks = xa.shape[0] // tm
    return pl.pallas_call(
        functools.partial(_proj_residual_kernel, first_blocks=first_blocks),
        grid=(d // tn, n // tm),
        in_specs=[
            pl.BlockSpec((tm, k), lambda j, i: (i, 0)),
            pl.BlockSpec((k, tn), lambda j, i: (0, j)),
            pl.BlockSpec((tm, tn), lambda j, i: (jnp.minimum(i, first_blocks - 1), j)),
            pl.BlockSpec((tm, tn), lambda j, i: (jnp.maximum(i - first_blocks, 0), j)),
        ],
        out_specs=pl.BlockSpec((tm, tn), lambda j, i: (i, j)),
        out_shape=jax.ShapeDtypeStruct((n, d), F32),
        compiler_params=_params("arbitrary", "arbitrary"),
        name="proj_residual",
    )(a, w, xa, xb)


def _silu_mul(g, u):
    return g * (1.0 / (1.0 + jnp.exp(-g))) * u


def _ffn_kernel(x_ref, gain_ref, wg_ref, wu_ref, wd_ref, o_ref, h_ref, acc_ref):
    j = pl.program_id(1)

    @pl.when(j == 0)
    def _():
        h_ref[...] = _rms_rows(x_ref[...], gain_ref[...]).astype(BF16)
        acc_ref[...] = jnp.zeros_like(acc_ref)

    h = h_ref[...]
    g = jnp.dot(h, wg_ref[...], preferred_element_type=F32)
    u = jnp.dot(h, wu_ref[...], preferred_element_type=F32)
    acc_ref[...] += jnp.dot(_silu_mul(g, u).astype(BF16), wd_ref[...], preferred_element_type=F32)

    @pl.when(j == pl.num_programs(1) - 1)
    def _():
        o_ref[...] = x_ref[...] + acc_ref[...]


def _dense_ffn(x, gain, w_gate, w_up, w_down):
    n, d = x.shape
    f = w_gate.shape[1]
    tm = min(FFN_TOKEN_BLOCK, n)
    tf = FFN_F_BLOCK
    assert f % tf == 0
    return pl.pallas_call(
        _ffn_kernel,
        grid=(n // tm, f // tf),
        in_specs=[
            pl.BlockSpec((tm, d), lambda i, j: (i, 0)),
            pl.BlockSpec((1, d), lambda i, j: (0, 0)),
            pl.BlockSpec((d, tf), lambda i, j: (0, j)),
            pl.BlockSpec((d, tf), lambda i, j: (0, j)),
            pl.BlockSpec((tf, d), lambda i, j: (j, 0)),
        ],
        out_specs=pl.BlockSpec((tm, d), lambda i, j: (i, 0)),
        out_shape=jax.ShapeDtypeStruct((n, d), F32),
        scratch_shapes=[pltpu.VMEM((tm, d), BF16), pltpu.VMEM((tm, d), F32)],
        compiler_params=_params("arbitrary", "arbitrary"),
        name="dense_ffn",
    )(x, gain.reshape(1, d), w_gate, w_up, w_down)


def _band_kernel(q_ref, kp_ref, k_ref, kn_ref, vp_ref, v_ref, vn_ref, o_prev, lse_prev, o_ref, lse_ref,
                 kwin_ref, vwin_ref, *, length, tq, half):
    del o_prev, lse_prev
    tqb = q_ref.shape[0]
    kw = tq + 2 * half
    q0 = pl.program_id(2) * tqb
    for win, parts in ((kwin_ref, (kp_ref, k_ref, kn_ref)), (vwin_ref, (vp_ref, v_ref, vn_ref))):
        win[:half] = parts[0][...]
        win[half:half + tqb] = parts[1][...]
        win[half + tqb:] = parts[2][...]
    lane = lax.broadcasted_iota(jnp.int32, (tq, LANES), 1)

    def body(t, carry):
        r0 = pl.multiple_of(t * tq, tq)
        qpos = q0 + r0 + lax.broadcasted_iota(jnp.int32, (tq, kw), 0)
        kpos = q0 + r0 - half + lax.broadcasted_iota(jnp.int32, (tq, kw), 1)
        ok = (jnp.abs(qpos - kpos) <= half) & (kpos >= 0) & (kpos < length)
        lse_tile = jnp.zeros((tq, LANES), F32)
        for h in range(q_ref.shape[1] // HEAD_DIM):
            sl = slice(h * HEAD_DIM, (h + 1) * HEAD_DIM)
            q = q_ref[pl.ds(r0, tq), sl]
            k = kwin_ref[pl.ds(r0, kw), sl]
            v = vwin_ref[pl.ds(r0, kw), sl]
            s = lax.dot_general(q, k, (((1,), (1,)), ((), ())), preferred_element_type=F32) * ATTN_SCALE
            s = jnp.where(ok, s, NEG_INF)
            m = jnp.max(s, axis=-1, keepdims=True)
            p = jnp.exp(s - m)
            den = jnp.sum(p, axis=-1, keepdims=True)
            o = jnp.dot(p.astype(BF16), v, preferred_element_type=F32) / den
            o_ref[pl.ds(r0, tq), sl] = o
            lse_tile = jnp.where(lane == h, m + jnp.log(den), lse_tile)
        lse_ref[pl.ds(r0, tq), :] = lse_tile
        return carry

    lax.fori_loop(0, tqb // tq, body, 0)


def _band_attention(qkv, o_prev, lse_prev, tok_off, batch, seq, group):
    window, dil = DIL_CONFIGS[group]
    half = window // (2 * dil)
    length = seq // dil
    assert seq % dil == 0 and length % half == 0 and tok_off % seq == 0 and qkv.shape[1] == dil
    tqb = min(BAND_Q_BLOCK, length)
    tq = min(BAND_Q_TILE, tqb)
    assert length % tqb == 0 and tqb % tq == 0 and tqb % half == 0
    width = qkv.shape[3]
    nqb = length // tqb
    hpb = tqb // half
    seq0 = tok_off // seq

    def main(which):
        return lambda b, r, t: (which, r, (seq0 + b) * nqb + t, 0)

    def halo(which, side):
        def index(b, r, t):
            first = (seq0 + b) * nqb * hpb
            blk = jnp.clip(t * hpb + (hpb if side else -1), 0, nqb * hpb - 1)
            return (which, r, first + blk, 0)
        return index

    mspec = lambda which: pl.BlockSpec((None, None, tqb, width), main(which))
    hspec = lambda which, side: pl.BlockSpec((None, None, half, width), halo(which, side))
    return pl.pallas_call(
        functools.partial(_band_kernel, length=length, tq=tq, half=half),
        grid=(batch, dil, nqb),
        in_specs=[
            mspec(0),
            hspec(1, 0), mspec(1), hspec(1, 1),
            hspec(2, 0), mspec(2), hspec(2, 1),
            pl.BlockSpec(memory_space=pl.ANY),
            pl.BlockSpec(memory_space=pl.ANY),
        ],
        out_specs=[
            pl.BlockSpec((None, tqb, width), lambda b, r, t: (r, (seq0 + b) * nqb + t, 0)),
            pl.BlockSpec((None, tqb, LANES), lambda b, r, t: (r, (seq0 + b) * nqb + t, 0)),
        ],
        out_shape=[jax.ShapeDtypeStruct(o_prev.shape, F32), jax.ShapeDtypeStruct(lse_prev.shape, F32)],
        input_output_aliases={7: 0, 8: 1},
        scratch_shapes=[pltpu.VMEM((tqb + 2 * half, width), BF16), pltpu.VMEM((tqb + 2 * half, width), BF16)],
        compiler_params=_params("arbitrary", "arbitrary", "arbitrary"),
        name="band_attention",
    )(qkv, qkv, qkv, qkv, qkv, qkv, qkv, o_prev, lse_prev)


def _merge_proj_kernel(*refs, dils):
    ng = len(dils)
    o_refs, l_refs = refs[:ng], refs[ng:2 * ng]
    w_ref, x_ref, out_ref, a_ref, os_ref, ls_ref = refs[2 * ng:]
    tm = x_ref.shape[0]

    heads = a_ref.shape[1] // HEAD_DIM
    for g, dil in enumerate(dils):
        for r in range(dil):
            ls_ref[g, pl.ds(r, tm // dil, stride=dil), :] = l_refs[g][r]
            for h in range(heads):
                os_ref[g, h, pl.ds(r, tm // dil, stride=dil), :] = o_refs[g][r, :, h * HEAD_DIM:(h + 1) * HEAD_DIM]

    lse = [ls_ref[g] for g in range(ng)]
    m = functools.reduce(jnp.maximum, lse)
    e = [jnp.exp(l - m) for l in lse]
    tot = functools.reduce(lambda p, q: p + q, e)
    wts = [eg / tot for eg in e]
    for h in range(heads):
        terms = [wts[g][:, h:h + 1] * os_ref[g, h] for g in range(ng)]
        a_ref[:, h * HEAD_DIM:(h + 1) * HEAD_DIM] = functools.reduce(lambda p, q: p + q, terms).astype(BF16)
    out_ref[...] = x_ref[...] + jnp.dot(a_ref[...], w_ref[...], preferred_element_type=F32)


def _merge_proj(outs, lses, w, x):
    n, d = x.shape
    width = w.shape[0]
    dils = tuple(o.shape[0] for o in outs)
    tm = min(512, n)
    row = lambda i: (i, 0)
    grp = lambda i: (0, i, 0)
    return pl.pallas_call(
        functools.partial(_merge_proj_kernel, dils=dils),
        grid=(n // tm,),
        in_specs=[pl.BlockSpec((dil, tm // dil, width), grp) for dil in dils]
        + [pl.BlockSpec((dil, tm // dil, LANES), grp) for dil in dils]
        + [pl.BlockSpec((width, d), lambda i: (0, 0)), pl.BlockSpec((tm, d), row)],
        out_specs=pl.BlockSpec((tm, d), row),
        out_shape=jax.ShapeDtypeStruct((n, d), F32),
        scratch_shapes=[
            pltpu.VMEM((tm, width), BF16),
            pltpu.VMEM((len(dils), width // HEAD_DIM, tm, HEAD_DIM), F32),
            pltpu.VMEM((len(dils), tm, LANES), F32),
        ],
        compiler_params=_params("arbitrary"),
        name="merge_proj",
    )(*outs, *lses, w, x)


def _store_chunk_rows(ref, x):
    rows, d = x.shape
    nc = d // LANES
    for c in range(nc):
        ref[pl.ds(c, rows, stride=nc), :] = x[:, c * LANES:(c + 1) * LANES]


def _router_kernel(x_ref, gain_ref, wr_ref, h_ref, idx_ref, gate_ref):
    h = _rms_rows(x_ref[...], gain_ref[...])
    _store_chunk_rows(h_ref, h)
    logits = jnp.dot(h, wr_ref[...], preferred_element_type=F32, precision=lax.Precision.HIGHEST)
    lane = lax.broadcasted_iota(jnp.int32, logits.shape, 1)
    lane_f = lane.astype(F32)
    logits = jnp.where(lane < N_EXPERTS, logits, -jnp.inf)
    m1 = jnp.max(logits, axis=-1, keepdims=True)
    i1 = jnp.min(jnp.where(logits == m1, lane_f, float(LANES)), axis=-1, keepdims=True)
    rest = jnp.where(lane_f == i1, -jnp.inf, logits)
    m2 = jnp.max(rest, axis=-1, keepdims=True)
    i2 = jnp.min(jnp.where(rest == m2, lane_f, float(LANES)), axis=-1, keepdims=True)
    e = jnp.exp(m2 - m1)
    tot = 1.0 + e
    idx_ref[...] = jnp.where(lane == 0, i1, jnp.where(lane == 1, i2, 0.0)).astype(jnp.int32)
    gate_ref[...] = jnp.where(lane == 0, 1.0 / tot, jnp.where(lane == 1, e / tot, 0.0))


def _router(x, gain, w_router):
    n, d = x.shape
    tm = min(512, n)
    wr = jnp.zeros((d, LANES), F32).at[:, :N_EXPERTS].set(w_router.astype(F32))
    row = lambda i: (i, 0)
    return pl.pallas_call(
        _router_kernel,
        grid=(n // tm,),
        in_specs=[
            pl.BlockSpec((tm, d), row),
            pl.BlockSpec((1, d), lambda i: (0, 0)),
            pl.BlockSpec((d, LANES), lambda i: (0, 0)),
        ],
        out_specs=[pl.BlockSpec((tm * (d // LANES), LANES), row), pl.BlockSpec((tm, LANES), row),
                   pl.BlockSpec((tm, LANES), row)],
        out_shape=[
            jax.ShapeDtypeStruct((n * (d // LANES), LANES), F32),
            jax.ShapeDtypeStruct((n, LANES), jnp.int32),
            jax.ShapeDtypeStruct((n, LANES), F32),
        ],
        compiler_params=_params("arbitrary"),
        name="router",
    )(x, gain.reshape(1, d), wr)


def _expert_kernel(bexp_ref, nused_ref, rtok_ref, h_hbm, wg_ref, wu_ref, wd_ref, y_ref,
                   xs_ref, xb_ref, acc_ref, sem):
    del bexp_ref
    i = pl.program_id(0)
    j = pl.program_id(1)
    tmb, d = xb_ref.shape
    nc = d // LANES
    n_used = nused_ref[0]

    def row_copy(tok, slot, r):
        return pltpu.make_async_copy(h_hbm.at[pl.ds(pl.multiple_of(tok * nc, nc), nc), :],
                                     xs_ref.at[slot, pl.ds(pl.multiple_of(r * nc, nc), nc), :], sem.at[slot])

    def start_gather(blk, slot):
        def issue(r, carry):
            row_copy(rtok_ref[blk * tmb + r], slot, r).start()
            return carry
        lax.fori_loop(0, tmb, issue, 0, unroll=8)

    def wait_gather(slot):
        pltpu.make_async_copy(h_hbm.at[pl.ds(0, tmb * nc), :], xs_ref.at[slot], sem.at[slot]).wait()

    @pl.when(i < n_used)
    def _():
        slot = i % 2

        @pl.when(j == 0)
        def _():
            @pl.when(i == 0)
            def _():
                start_gather(0, 0)

            wait_gather(slot)

            @pl.when(i + 1 < n_used)
            def _():
                start_gather(i + 1, 1 - slot)

            for c in range(nc):
                xb_ref[:, c * LANES:(c + 1) * LANES] = xs_ref[slot, pl.ds(c, tmb, stride=nc), :].astype(BF16)
            acc_ref[...] = jnp.zeros_like(acc_ref)

        x = xb_ref[...]
        g = jnp.dot(x, wg_ref[0], preferred_element_type=F32)
        u = jnp.dot(x, wu_ref[0], preferred_element_type=F32)
        acc_ref[...] += jnp.dot(_silu_mul(g, u).astype(BF16), wd_ref[0], preferred_element_type=F32)

    last = j == pl.num_programs(1) - 1

    @pl.when(last & (i < n_used))
    def _():
        _store_chunk_rows(y_ref, acc_ref[...])

    @pl.when(last & (i >= n_used))
    def _():
        y_ref[...] = jnp.zeros_like(y_ref)


def _expert_ffn(h_rows, block_expert, n_used, row_tok, w_gate, w_up, w_down):
    d = w_gate.shape[1]
    f = w_gate.shape[2]
    tmb = MOE_BLOCK
    tf = MOE_F_BLOCK
    n_blocks = block_expert.shape[0]
    nf = f // tf
    assert f % tf == 0

    def fblk(i, j, nu):
        return jnp.where(i < nu[0], j, nf - 1)

    grid_spec = pltpu.PrefetchScalarGridSpec(
        num_scalar_prefetch=3,
        grid=(n_blocks, nf),
        in_specs=[
            pl.BlockSpec(memory_space=pl.ANY),
            pl.BlockSpec((1, d, tf), lambda i, j, be, nu, rt: (be[i], 0, fblk(i, j, nu))),
            pl.BlockSpec((1, d, tf), lambda i, j, be, nu, rt: (be[i], 0, fblk(i, j, nu))),
            pl.BlockSpec((1, tf, d), lambda i, j, be, nu, rt: (be[i], fblk(i, j, nu), 0)),
        ],
        out_specs=pl.BlockSpec((tmb * (d // LANES), LANES), lambda i, j, be, nu, rt: (i, 0)),
        scratch_shapes=[
            pltpu.VMEM((2, tmb * (d // LANES), LANES), F32),
            pltpu.VMEM((tmb, d), BF16),
            pltpu.VMEM((tmb, d), F32),
            pltpu.SemaphoreType.DMA((2,)),
        ],
    )
    return pl.pallas_call(
        _expert_kernel,
        grid_spec=grid_spec,
        out_shape=jax.ShapeDtypeStruct((n_blocks * tmb * (d // LANES), LANES), F32),
        compiler_params=_params("arbitrary", "arbitrary"),
        name="expert_ffn",
    )(block_expert, n_used, row_tok, h_rows, w_gate, w_up, w_down)


def _combine_kernel(dest_ref, ys_hbm, x_ref, gate_ref, oa_ref, ob_ref, buf_ref, sem, *, first_blocks):
    i = pl.program_id(0)
    nsteps = pl.num_programs(0)
    tc, d = x_ref.shape
    nc = d // LANES

    def row_copy(row, slot, dst):
        return pltpu.make_async_copy(ys_hbm.at[pl.ds(pl.multiple_of(row * nc, nc), nc), :],
                                     buf_ref.at[slot, pl.ds(pl.multiple_of(dst * nc, nc), nc), :], sem.at[slot])

    def start_gather(step, slot):
        def issue(r, carry):
            row_copy(dest_ref[step * (2 * tc) + r], slot, (r % 2) * tc + r // 2).start()
            return carry
        lax.fori_loop(0, 2 * tc, issue, 0, unroll=8)

    def wait_gather(slot):
        pltpu.make_async_copy(ys_hbm.at[pl.ds(0, 2 * tc * nc), :], buf_ref.at[slot], sem.at[slot]).wait()

    slot = i % 2

    @pl.when(i == 0)
    def _():
        start_gather(0, 0)

    wait_gather(slot)

    @pl.when(i + 1 < nsteps)
    def _():
        start_gather(i + 1, 1 - slot)

    g0 = gate_ref[:, 0:1]
    g1 = gate_ref[:, 1:2]

    def emit(o_ref):
        for c in range(nc):
            sl = slice(c * LANES, (c + 1) * LANES)
            y0 = buf_ref[slot, pl.ds(c, tc, stride=nc), :]
            y1 = buf_ref[slot, pl.ds(tc * nc + c, tc, stride=nc), :]
            o_ref[:, sl] = x_ref[:, sl] + (y0 * g0 + y1 * g1)
    _pick_rows(i, first_blocks, oa_ref, ob_ref, emit)


def _combine(x, ys, dest, gate, n_first):
    n, d = x.shape
    tc = min(COMBINE_BLOCK, n_first, n - n_first)
    assert n_first % tc == 0 and (n - n_first) % tc == 0
    first_blocks = n_first // tc
    grid_spec = pltpu.PrefetchScalarGridSpec(
        num_scalar_prefetch=1,
        grid=(n // tc,),
        in_specs=[
            pl.BlockSpec(memory_space=pl.ANY),
            pl.BlockSpec((tc, d), lambda i, dr: (i, 0)),
            pl.BlockSpec((tc, LANES), lambda i, dr: (i, 0)),
        ],
        out_specs=_two_batch_specs(tc, d, first_blocks),
        scratch_shapes=[pltpu.VMEM((2, 2 * tc * (d // LANES), LANES), F32), pltpu.SemaphoreType.DMA((2,))],
    )
    return pl.pallas_call(
        functools.partial(_combine_kernel, first_blocks=first_blocks),
        grid_spec=grid_spec,
        out_shape=[jax.ShapeDtypeStruct((n_first, d), F32), jax.ShapeDtypeStruct((n - n_first, d), F32)],
        compiler_params=_params("arbitrary"),
        name="moe_combine",
    )(dest, ys, x, gate)


def _routing_tables(idx, tmb):
    n = idx.shape[0]
    nk = n * TOP_K
    e_flat = idx[:, :TOP_K].reshape(-1)
    onehot = (e_flat[:, None] == jnp.arange(N_EXPERTS, dtype=jnp.int32)[None, :]).astype(jnp.int32)
    csum = jnp.cumsum(onehot, axis=0)
    rank = jnp.sum(onehot * csum, axis=1) - 1
    counts = csum[-1]
    padded = (counts + tmb - 1) // tmb * tmb
    pad_end = jnp.cumsum(padded)
    pad_start = pad_end - padded
    dest = (jnp.sum(onehot * pad_start[None, :], axis=1) + rank).astype(jnp.int32)
    n_blocks = -(-nk // tmb) + N_EXPERTS
    block_expert = jnp.minimum(
        jnp.searchsorted(pad_end, jnp.arange(n_blocks, dtype=jnp.int32) * tmb, side="right"), N_EXPERTS - 1
    ).astype(jnp.int32)
    n_used = (pad_end[-1] // tmb).astype(jnp.int32).reshape(1)
    row_tok = jnp.zeros((n_blocks * tmb,), jnp.int32).at[dest].set(jnp.arange(nk, dtype=jnp.int32) // TOP_K)
    return dest, block_expert, n_used, row_tok


def _rope_tables(seq_len):
    half = HEAD_DIM // 2
    inv_freq = ROPE_THETA ** (-jnp.arange(half, dtype=F32) / half)
    ang = jnp.arange(seq_len, dtype=F32)[:, None] * inv_freq[None, :]
    cos, sin = jnp.cos(ang), jnp.sin(ang)
    return jnp.concatenate([cos, cos], axis=-1), jnp.concatenate([-sin, sin], axis=-1)


def kernel(x_prompt, x_sample, norm_mix, norm_ffn, na_w_qkv, na_q_gain, na_k_gain, na_rpb, na_w_o,
           da_w_qkv, da_q_gain, da_k_gain, da_w_o, ffn_w_gate, ffn_w_up, ffn_w_down,
           moe_w_router, moe_w_gate, moe_w_up, moe_w_down):
    d = x_prompt.shape[-1]
    trunks = []
    off = 0
    for xin in (x_prompt, x_sample):
        b, s, _ = xin.shape
        trunks.append((off, b, s))
        off += b * s
    n = off
    xa, xb = x_prompt.reshape(-1, d), x_sample.reshape(-1, d)
    depth = norm_mix.shape[0]
    assert depth == 2, "layer 0 reads the two request batches directly and layer 1 writes them separately"
    x = None

    for layer in range(depth):
        lj = layer // 2
        if layer % 2 == 0:
            heads = na_w_qkv.shape[2] // (3 * HEAD_DIM)
            tn = 1024
            per = heads * HEAD_DIM // tn
            hg = jnp.concatenate([
                jnp.broadcast_to(na_q_gain[lj], (per, HEAD_DIM)),
                jnp.broadcast_to(na_k_gain[lj], (per, HEAD_DIM)),
                jnp.ones((per, HEAD_DIM), F32),
            ]).reshape(3 * per, 1, HEAD_DIM)
            qkv = _qkv_project(xa, xb, norm_mix[layer], na_w_qkv[lj].astype(BF16), hg, 3 * per, 2 * per)
            tables = _na_bias_tables(na_rpb[lj])
            att = jnp.zeros((n, heads * HEAD_DIM), BF16)
            for (toff, b, s) in trunks:
                att = _na_attention(qkv, tables, att, toff, b, s, heads)
            x = _proj_residual(att, na_w_o[lj].astype(BF16), xa, xb)
            x = _dense_ffn(x, norm_ffn[layer], ffn_w_gate[lj].astype(BF16), ffn_w_up[lj].astype(BF16),
                           ffn_w_down[lj].astype(BF16))
        else:
            groups = len(DIL_CONFIGS)
            heads = da_w_qkv.shape[2] // (groups * 3 * HEAD_DIM)
            dils = [dil for _, dil in DIL_CONFIGS]
            ones = jnp.ones((HEAD_DIM,), F32)
            hg = jnp.stack([t for g in range(groups) for t in (da_q_gain[lj, g], da_k_gain[lj, g], ones)])
            hg = hg.reshape(groups * 3, 1, HEAD_DIM)
            (o0, b0, s0), (o1, b1, s1) = trunks
            tq = min(QKV_DIL_TOKEN_BLOCK, n)
            assert s0 % tq == 0 and s1 % tq == 0 and o1 % tq == 0
            nb0, p0, p1 = o1 // tq, s0 // tq, s1 // tq
            pos_map = lambda i: jnp.where(i < nb0, i % p0, (i - nb0) % p1)
            smax = max(s0, s1)
            rope = [jnp.stack([t.reshape(smax // tq, tq // dil, dil, HEAD_DIM).transpose(0, 2, 1, 3)
                               .reshape(smax, HEAD_DIM) for dil in dils]) for t in _rope_tables(smax)]
            qkvs = _qkv_project_dilated(x, norm_mix[layer], da_w_qkv[lj].astype(BF16), hg, rope, pos_map,
                                        dils, heads, tq)
            outs, lses = [], []
            for g, dil in enumerate(dils):
                o_g = jnp.zeros((dil, n // dil, heads * HEAD_DIM), F32)
                lse_g = jnp.zeros((dil, n // dil, LANES), F32)
                for (toff, b, s) in trunks:
                    o_g, lse_g = _band_attention(qkvs[g], o_g, lse_g, toff, b, s, g)
                outs.append(o_g)
                lses.append(lse_g)
            x = _merge_proj(outs, lses, da_w_o[lj].astype(BF16), x)
            h, idx, gate = _router(x, norm_ffn[layer], moe_w_router[lj])
            dest, block_expert, n_used, row_tok = _routing_tables(idx, MOE_BLOCK)
            ys = _expert_ffn(h, block_expert, n_used, row_tok, moe_w_gate[lj].astype(BF16),
                             moe_w_up[lj].astype(BF16), moe_w_down[lj].astype(BF16))
            ya, yb = _combine(x, ys, dest, gate, xa.shape[0])

    return ya.reshape(x_prompt.shape), yb.reshape(x_sample.shape)
```

```python
import functools

import jax
import jax.numpy as jnp
from jax import lax
from jax.experimental import pallas as pl
from jax.experimental.pallas import tpu as pltpu

F32 = jnp.float32
BF16 = jnp.bfloat16

HEAD_DIM = 128
GRID_W = 64
NA_ROWS = 8
NA_COLS = 16
NA_GROUP = NA_ROWS // 2
NA_KEY_ROWS = 3 * NA_GROUP
NA_HEAD_BLOCK = 4
DIL_CONFIGS = ((128, 1), (512, 4), (2048, 16))
N_EXPERTS = 8
TOP_K = 2
ROPE_THETA = 10000.0
EPS = 1e-6
NEG_INF = -1e30
ATTN_SCALE = HEAD_DIM ** -0.5

LANES = 128
MXU_COLS = 256
VMEM_LIMIT = 52 * 1024 * 1024

TOKEN_BLOCK = 1024
QKV_DIL_TOKEN_BLOCK = 512
FFN_TOKEN_BLOCK = 512
FFN_F_BLOCK = 512
BAND_Q_BLOCK = 512
BAND_Q_TILE = 128
MOE_BLOCK = 512
MOE_F_BLOCK = 1024
COMBINE_BLOCK = 256


def _params(*sem):
    return pltpu.CompilerParams(dimension_semantics=sem, vmem_limit_bytes=VMEM_LIMIT)


def _rms_rows(x, gain):
    ms = jnp.mean(x * x, axis=-1, keepdims=True)
    return x * lax.rsqrt(ms + EPS) * gain


def _pick_rows(i, first_blocks, xa_ref, xb_ref, use):
    @pl.when(i < first_blocks)
    def _():
        use(xa_ref)

    @pl.when(i >= first_blocks)
    def _():
        use(xb_ref)


def _two_batch_specs(tm, d, first_blocks, **kw):
    return [pl.BlockSpec((tm, d), lambda i, *_: (jnp.minimum(i, first_blocks - 1), 0), **kw),
            pl.BlockSpec((tm, d), lambda i, *_: (jnp.maximum(i - first_blocks, 0), 0), **kw)]


def _qkv_kernel(xa_ref, xb_ref, gain_ref, w_ref, hg_ref, o_ref, h_ref, *, first_blocks, v_every, v_from):
    j = pl.program_id(1)

    @pl.when(j == 0)
    def _():
        def norm(x_ref):
            h_ref[...] = _rms_rows(x_ref[...], gain_ref[...]).astype(BF16)
        _pick_rows(pl.program_id(0), first_blocks, xa_ref, xb_ref, norm)

    is_qk = (j % v_every) < v_from
    for c in range(w_ref.shape[1] // MXU_COLS):
        acc = jnp.dot(h_ref[...], w_ref[:, c * MXU_COLS:(c + 1) * MXU_COLS], preferred_element_type=F32)
        for h in range(MXU_COLS // HEAD_DIM):
            seg = acc[:, h * HEAD_DIM:(h + 1) * HEAD_DIM]
            lo = c * MXU_COLS + h * HEAD_DIM
            o_ref[:, lo:lo + HEAD_DIM] = jnp.where(is_qk, _rms_rows(seg, hg_ref[0]), seg).astype(o_ref.dtype)


def _qkv_project(xa, xb, gain, w, head_gain, v_every, v_from):
    d = xa.shape[1]
    n = xa.shape[0] + xb.shape[0]
    n_out = w.shape[1]
    tm = min(TOKEN_BLOCK, xa.shape[0], xb.shape[0])
    assert xa.shape[0] % tm == 0 and xb.shape[0] % tm == 0
    first_blocks = xa.shape[0] // tm
    tn = n_out // head_gain.shape[0]
    return pl.pallas_call(
        functools.partial(_qkv_kernel, first_blocks=first_blocks, v_every=v_every, v_from=v_from),
        grid=(n // tm, n_out // tn),
        in_specs=_two_batch_specs(tm, d, first_blocks, pipeline_mode=pl.Buffered(1)) + [
            pl.BlockSpec((1, d), lambda i, j: (0, 0)),
            pl.BlockSpec((d, tn), lambda i, j: (0, j)),
            pl.BlockSpec((1, 1, HEAD_DIM), lambda i, j: (j, 0, 0)),
        ],
        out_specs=pl.BlockSpec((tm, tn), lambda i, j: (i, j)),
        out_shape=jax.ShapeDtypeStruct((n, n_out), BF16),
        scratch_shapes=[pltpu.VMEM((tm, d), BF16)],
        compiler_params=_params("arbitrary", "arbitrary"),
        name="qkv_project",
    )(xa, xb, gain.reshape(1, d), w, head_gain)


def _qkv_dil_kernel(x_ref, gain_ref, w_ref, hg_ref, cos_ref, sin_ref, *rest, dils):
    out_refs, (h_ref, stage_ref) = rest[:len(dils)], rest[len(dils):]
    j = pl.program_id(1)
    tm, d = x_ref.shape
    tn = w_ref.shape[1]

    @pl.when(j == 0)
    def _():
        y = _rms_rows(x_ref[...], gain_ref[...])
        for c in range(d // LANES):
            stage_ref[c] = y[:, c * LANES:(c + 1) * LANES]
        for g, dil in enumerate(dils):
            rows = tm // dil
            for r in range(dil):
                for c in range(d // LANES):
                    h_ref[g, r * rows:(r + 1) * rows, c * LANES:(c + 1) * LANES] = (
                        stage_ref[c, pl.ds(r, rows, stride=dil), :].astype(BF16))

    group = j // 3
    which = j % 3
    acc = jnp.dot(h_ref[group], w_ref[...], preferred_element_type=F32)

    for g, (dil, o_ref) in enumerate(zip(dils, out_refs)):
        rows = tm // dil

        @pl.when((group == g) & (which < 2))
        def _(dil=dil, o_ref=o_ref, rows=rows):
            for h in range(tn // HEAD_DIM):
                sl = slice(h * HEAD_DIM, (h + 1) * HEAD_DIM)
                y = _rms_rows(acc[:, sl], hg_ref[0])
                y = (y * cos_ref[...] + pltpu.roll(y, HEAD_DIM // 2, 1) * sin_ref[...]).astype(o_ref.dtype)
                for r in range(dil):
                    o_ref[r, :, sl] = y[r * rows:(r + 1) * rows]

        @pl.when((group == g) & (which == 2))
        def _(dil=dil, o_ref=o_ref, rows=rows):
            v = acc.astype(o_ref.dtype)
            for r in range(dil):
                o_ref[r] = v[r * rows:(r + 1) * rows]


def _qkv_project_dilated(x, gain, w, head_gain, rope_tabs, pos_map, dils, heads, tm):
    n, d = x.shape
    n_out = w.shape[1]
    width = heads * HEAD_DIM
    ng = len(dils)
    assert n_out == ng * 3 * width and head_gain.shape[0] == ng * 3

    def out_map(g):
        def index(i, j):
            return (jnp.clip(j - 3 * g, 0, 2), 0, i, 0)
        return index

    rope_spec = pl.BlockSpec((None, tm, HEAD_DIM), lambda i, j: (j // 3, pos_map(i), 0))
    return pl.pallas_call(
        functools.partial(_qkv_dil_kernel, dils=tuple(dils)),
        grid=(n // tm, ng * 3),
        in_specs=[
            pl.BlockSpec((tm, d), lambda i, j: (i, 0)),
            pl.BlockSpec((1, d), lambda i, j: (0, 0)),
            pl.BlockSpec((d, width), lambda i, j: (0, j)),
            pl.BlockSpec((1, 1, HEAD_DIM), lambda i, j: (j, 0, 0)),
            rope_spec,
            rope_spec,
        ],
        out_specs=[pl.BlockSpec((None, dil, tm // dil, width), out_map(g)) for g, dil in enumerate(dils)],
        out_shape=[jax.ShapeDtypeStruct((3, dil, n // dil, width), BF16) for dil in dils],
        scratch_shapes=[pltpu.VMEM((ng, tm, d), BF16), pltpu.VMEM((d // LANES, tm, LANES), F32)],
        compiler_params=_params("arbitrary", "arbitrary"),
        name="qkv_project_dilated",
    )(x, gain.reshape(1, d), w, head_gain, *rope_tabs)


def _na_bias_tables(rpb):
    half = NA_ROWS // 2
    assert NA_GROUP == half
    qi = jnp.arange(NA_GROUP)[:, None]
    kj = jnp.arange(NA_KEY_ROWS)[None, :]
    dr = jnp.broadcast_to(kj - NA_GROUP - qi, (3, NA_GROUP, NA_KEY_ROWS))
    rvalid = jnp.stack([
        jnp.broadcast_to(kj >= NA_GROUP, (NA_GROUP, NA_KEY_ROWS)),
        (kj - qi >= 0) & (kj - qi < NA_ROWS),
        jnp.broadcast_to(kj < NA_ROWS, (NA_GROUP, NA_KEY_ROWS)),
    ])
    c = jnp.arange(GRID_W)
    cs = jnp.clip(c - NA_COLS // 2, 0, GRID_W - NA_COLS)
    cvalid = (c[None, :] >= cs[:, None]) & (c[None, :] < cs[:, None] + NA_COLS)
    dc = jnp.clip(c[None, :] - c[:, None], 1 - NA_COLS, NA_COLS - 1)
    dri = jnp.clip(dr, 1 - NA_ROWS, NA_ROWS - 1) + NA_ROWS - 1
    sel_r = jax.nn.one_hot(dri, 2 * NA_ROWS - 1, dtype=F32)
    sel_c = jax.nn.one_hot(dc + NA_COLS - 1, 2 * NA_COLS - 1, dtype=F32)
    hp = lax.Precision.HIGHEST
    tmp = jnp.einsum("hab,qcb->haqc", rpb.astype(F32), sel_c, precision=hp)
    bias = jnp.einsum("haqc,vgka->vhgqkc", tmp, sel_r, precision=hp)
    valid = rvalid[:, None, :, None, :, None] & cvalid[None, None, None, :, None, :]
    bias = jnp.where(valid, bias, NEG_INF)
    h = rpb.shape[0]
    return bias.reshape(3, h, NA_GROUP * GRID_W, NA_KEY_ROWS * GRID_W)


def _na_kernel(q_ref, kp_ref, k_ref, kn_ref, vp_ref, v_ref, vn_ref, tab_ref, prev_ref, o_ref):
    del prev_ref
    gq = q_ref.shape[0]
    nt = (((1,), (1,)), ((), ()))
    for h in range(q_ref.shape[1] // HEAD_DIM):
        sl = slice(h * HEAD_DIM, (h + 1) * HEAD_DIM)
        q = q_ref[:, sl]
        s = jnp.concatenate(
            [lax.dot_general(q, kr[:, sl], nt, preferred_element_type=F32) for kr in (kp_ref, k_ref, kn_ref)],
            axis=1)
        s = s * ATTN_SCALE + tab_ref[0, h]
        m = jnp.max(s, axis=-1, keepdims=True)
        p = jnp.exp(s - m)
        den = jnp.sum(p, axis=-1, keepdims=True)
        pb = p.astype(BF16)
        o = (jnp.dot(pb[:, :gq], vp_ref[:, sl], preferred_element_type=F32)
             + jnp.dot(pb[:, gq:2 * gq], v_ref[:, sl], preferred_element_type=F32)
             + jnp.dot(pb[:, 2 * gq:], vn_ref[:, sl], preferred_element_type=F32))
        o_ref[:, sl] = (o / den).astype(o_ref.dtype)


def _na_attention(qkv, tables, out_prev, tok_off, batch, seq, heads):
    rows = seq // GRID_W
    groups = rows // NA_GROUP
    hb = NA_HEAD_BLOCK
    assert rows % NA_GROUP == 0 and rows >= NA_ROWS and tok_off % seq == 0 and heads % hb == 0
    gq = NA_GROUP * GRID_W
    blk0 = tok_off // gq
    hblocks = heads // hb

    def variant(g):
        return jnp.where(g == 0, 0, jnp.where(g == groups - 1, 2, 1))

    def rows_of(which, shift):
        def index(b, h, g):
            gg = jnp.clip(g + shift, 0, groups - 1)
            return (blk0 + b * groups + gg, which * hblocks + h)
        return index

    blk = lambda index: pl.BlockSpec((gq, hb * HEAD_DIM), index)
    return pl.pallas_call(
        _na_kernel,
        grid=(batch, hblocks, groups),
        in_specs=[
            blk(rows_of(0, 0)),
            blk(rows_of(1, -1)), blk(rows_of(1, 0)), blk(rows_of(1, 1)),
            blk(rows_of(2, -1)), blk(rows_of(2, 0)), blk(rows_of(2, 1)),
            pl.BlockSpec((1, hb, gq, NA_KEY_ROWS * GRID_W), lambda b, h, g: (variant(g), h, 0, 0)),
            pl.BlockSpec(memory_space=pl.ANY),
        ],
        out_specs=blk(rows_of(0, 0)),
        out_shape=jax.ShapeDtypeStruct(out_prev.shape, out_prev.dtype),
        input_output_aliases={8: 0},
        compiler_params=_params("arbitrary", "arbitrary", "arbitrary"),
        name="na_attention",
    )(qkv, qkv, qkv, qkv, qkv, qkv, qkv, tables, out_prev)


def _proj_residual_kernel(a_ref, w_ref, xa_ref, xb_ref, o_ref, *, first_blocks):
    y = jnp.dot(a_ref[...], w_ref[...], preferred_element_type=F32)

    def add(x_ref):
        o_ref[...] = x_ref[...] + y
    _pick_rows(pl.program_id(1), first_blocks, xa_ref, xb_ref, add)


def _proj_residual(a, w, xa, xb):
    n, k = a.shape
    d = w.shape[1]
    tm = min(TOKEN_BLOCK, xa.shape[0], xb.shape[0])
    tn = min(1024, d)
    assert xa.shape[0] % tm == 0 and xb.shape[0] % tm == 0 and xa.shape[0] + xb.shape[0] == n
    first_blocks = xa.shape[0] // tm
    return pl.pallas_call(
        functools.partial(_proj_residual_kernel, first_blocks=first_blocks),
        grid=(d // tn, n // tm),
        in_specs=[
            pl.BlockSpec((tm, k), lambda j, i: (i, 0)),
            pl.BlockSpec((k, tn), lambda j, i: (0, j)),
            pl.BlockSpec((tm, tn), lambda j, i: (jnp.minimum(i, first_blocks - 1), j)),
            pl.BlockSpec((tm, tn), lambda j, i: (jnp.maximum(i - first_blocks, 0), j)),
        ],
        out_specs=pl.BlockSpec((tm, tn), lambda j, i: (i, j)),
        out_shape=jax.ShapeDtypeStruct((n, d), F32),
        compiler_params=_params("arbitrary", "arbitrary"),
        name="proj_residual",
    )(a, w, xa, xb)


def _silu_mul(g, u):
    return g * (1.0 / (1.0 + jnp.exp(-g))) * u


def _ffn_kernel(x_ref, gain_ref, wg_ref, wu_ref, wd_ref, o_ref, h_ref, acc_ref):
    j = pl.program_id(1)

    @pl.when(j == 0)
    def _():
        h_ref[...] = _rms_rows(x_ref[...], gain_ref[...]).astype(BF16)
        acc_ref[...] = jnp.zeros_like(acc_ref)

    h = h_ref[...]
    g = jnp.dot(h, wg_ref[...], preferred_element_type=F32)
    u = jnp.dot(h, wu_ref[...], preferred_element_type=F32)
    acc_ref[...] += jnp.dot(_silu_mul(g, u).astype(BF16), wd_ref[...], preferred_element_type=F32)

    @pl.when(j == pl.num_programs(1) - 1)
    def _():
        o_ref[...] = x_ref[...] + acc_ref[...]


def _dense_ffn(x, gain, w_gate, w_up, w_down):
    n, d = x.shape
    f = w_gate.shape[1]
    tm = min(FFN_TOKEN_BLOCK, n)
    tf = FFN_F_BLOCK
    assert f % tf == 0
    return pl.pallas_call(
        _ffn_kernel,
        grid=(n // tm, f // tf),
        in_specs=[
            pl.BlockSpec((tm, d), lambda i, j: (i, 0)),
            pl.BlockSpec((1, d), lambda i, j: (0, 0)),
            pl.BlockSpec((d, tf), lambda i, j: (0, j)),
            pl.BlockSpec((d, tf), lambda i, j: (0, j)),
            pl.BlockSpec((tf, d), lambda i, j: (j, 0)),
        ],
        out_specs=pl.BlockSpec((tm, d), lambda i, j: (i, 0)),
        out_shape=jax.ShapeDtypeStruct((n, d), F32),
        scratch_shapes=[pltpu.VMEM((tm, d), BF16), pltpu.VMEM((tm, d), F32)],
        compiler_params=_params("arbitrary", "arbitrary"),
        name="dense_ffn",
    )(x, gain.reshape(1, d), w_gate, w_up, w_down)


def _band_kernel(q_ref, kp_ref, k_ref, kn_ref, vp_ref, v_ref, vn_ref, o_prev, lse_prev, o_ref, lse_ref,
                 kwin_ref, vwin_ref, *, length, tq, half):
    del o_prev, lse_prev
    tqb = q_ref.shape[0]
    kw = tq + 2 * half
    q0 = pl.program_id(2) * tqb
    for win, parts in ((kwin_ref, (kp_ref, k_ref, kn_ref)), (vwin_ref, (vp_ref, v_ref, vn_ref))):
        win[:half] = parts[0][...]
        win[half:half + tqb] = parts[1][...]
        win[half + tqb:] = parts[2][...]
    lane = lax.broadcasted_iota(jnp.int32, (tq, LANES), 1)

    def body(t, carry):
        r0 = pl.multiple_of(t * tq, tq)
        qpos = q0 + r0 + lax.broadcasted_iota(jnp.int32, (tq, kw), 0)
        kpos = q0 + r0 - half + lax.broadcasted_iota(jnp.int32, (tq, kw), 1)
        ok = (jnp.abs(qpos - kpos) <= half) & (kpos >= 0) & (kpos < length)
        lse_tile = jnp.zeros((tq, LANES), F32)
        for h in range(q_ref.shape[1] // HEAD_DIM):
            sl = slice(h * HEAD_DIM, (h + 1) * HEAD_DIM)
            q = q_ref[pl.ds(r0, tq), sl]
            k = kwin_ref[pl.ds(r0, kw), sl]
            v = vwin_ref[pl.ds(r0, kw), sl]
            s = lax.dot_general(q, k, (((1,), (1,)), ((), ())), preferred_element_type=F32) * ATTN_SCALE
            s = jnp.where(ok, s, NEG_INF)
            m = jnp.max(s, axis=-1, keepdims=True)
            p = jnp.exp(s - m)
            den = jnp.sum(p, axis=-1, keepdims=True)
            o = jnp.dot(p.astype(BF16), v, preferred_element_type=F32) / den
            o_ref[pl.ds(r0, tq), sl] = o
            lse_tile = jnp.where(lane == h, m + jnp.log(den), lse_tile)
        lse_ref[pl.ds(r0, tq), :] = lse_tile
        return carry

    lax.fori_loop(0, tqb // tq, body, 0)


def _band_attention(qkv, o_prev, lse_prev, tok_off, batch, seq, group):
    window, dil = DIL_CONFIGS[group]
    half = window // (2 * dil)
    length = seq // dil
    assert seq % dil == 0 and length % half == 0 and tok_off % seq == 0 and qkv.shape[1] == dil
    tqb = min(BAND_Q_BLOCK, length)
    tq = min(BAND_Q_TILE, tqb)
    assert length % tqb == 0 and tqb % tq == 0 and tqb % half == 0
    width = qkv.shape[3]
    nqb = length // tqb
    hpb = tqb // half
    seq0 = tok_off // seq

    def main(which):
        return lambda b, r, t: (which, r, (seq0 + b) * nqb + t, 0)

    def halo(which, side):
        def index(b, r, t):
            first = (seq0 + b) * nqb * hpb
            blk = jnp.clip(t * hpb + (hpb if side else -1), 0, nqb * hpb - 1)
            return (which, r, first + blk, 0)
        return index

    mspec = lambda which: pl.BlockSpec((None, None, tqb, width), main(which))
    hspec = lambda which, side: pl.BlockSpec((None, None, half, width), halo(which, side))
    return pl.pallas_call(
        functools.partial(_band_kernel, length=length, tq=tq, half=half),
        grid=(batch, dil, nqb),
        in_specs=[
            mspec(0),
            hspec(1, 0), mspec(1), hspec(1, 1),
            hspec(2, 0), mspec(2), hspec(2, 1),
            pl.BlockSpec(memory_space=pl.ANY),
            pl.BlockSpec(memory_space=pl.ANY),
        ],
        out_specs=[
            pl.BlockSpec((None, tqb, width), lambda b, r, t: (r, (seq0 + b) * nqb + t, 0)),
            pl.BlockSpec((None, tqb, LANES), lambda b, r, t: (r, (seq0 + b) * nqb + t, 0)),
        ],
        out_shape=[jax.ShapeDtypeStruct(o_prev.shape, F32), jax.ShapeDtypeStruct(lse_prev.shape, F32)],
        input_output_aliases={7: 0, 8: 1},
        scratch_shapes=[pltpu.VMEM((tqb + 2 * half, width), BF16), pltpu.VMEM((tqb + 2 * half, width), BF16)],
        compiler_params=_params("arbitrary", "arbitrary", "arbitrary"),
        name="band_attention",
    )(qkv, qkv, qkv, qkv, qkv, qkv, qkv, o_prev, lse_prev)


def _merge_proj_kernel(*refs, dils):
    ng = len(dils)
    o_refs, l_refs = refs[:ng], refs[ng:2 * ng]
    w_ref, x_ref, out_ref, a_ref, os_ref, ls_ref = refs[2 * ng:]
    tm = x_ref.shape[0]

    heads = a_ref.shape[1] // HEAD_DIM
    for g, dil in enumerate(dils):
        for r in range(dil):
            ls_ref[g, pl.ds(r, tm // dil, stride=dil), :] = l_refs[g][r]
            for h in range(heads):
                os_ref[g, h, pl.ds(r, tm // dil, stride=dil), :] = o_refs[g][r, :, h * HEAD_DIM:(h + 1) * HEAD_DIM]

    lse = [ls_ref[g] for g in range(ng)]
    m = functools.reduce(jnp.maximum, lse)
    e = [jnp.exp(l - m) for l in lse]
    tot = functools.reduce(lambda p, q: p + q, e)
    wts = [eg / tot for eg in e]
    for h in range(heads):
        terms = [wts[g][:, h:h + 1] * os_ref[g, h] for g in range(ng)]
        a_ref[:, h * HEAD_DIM:(h + 1) * HEAD_DIM] = functools.reduce(lambda p, q: p + q, terms).astype(BF16)
    out_ref[...] = x_ref[...] + jnp.dot(a_ref[...], w_ref[...], preferred_element_type=F32)


def _merge_proj(outs, lses, w, x):
    n, d = x.shape
    width = w.shape[0]
    dils = tuple(o.shape[0] for o in outs)
    tm = min(512, n)
    row = lambda i: (i, 0)
    grp = lambda i: (0, i, 0)
    return pl.pallas_call(
        functools.partial(_merge_proj_kernel, dils=dils),
        grid=(n // tm,),
        in_specs=[pl.BlockSpec((dil, tm // dil, width), grp) for dil in dils]
        + [pl.BlockSpec((dil, tm // dil, LANES), grp) for dil in dils]
        + [pl.BlockSpec((width, d), lambda i: (0, 0)), pl.BlockSpec((tm, d), row)],
        out_specs=pl.BlockSpec((tm, d), row),
        out_shape=jax.ShapeDtypeStruct((n, d), F32),
        scratch_shapes=[
            pltpu.VMEM((tm, width), BF16),
            pltpu.VMEM((len(dils), width // HEAD_DIM, tm, HEAD_DIM), F32),
            pltpu.VMEM((len(dils), tm, LANES), F32),
        ],
        compiler_params=_params("arbitrary"),
        name="merge_proj",
    )(*outs, *lses, w, x)


def _store_chunk_rows(ref, x):
    rows, d = x.shape
    nc = d // LANES
    for c in range(nc):
        ref[pl.ds(c, rows, stride=nc), :] = x[:, c * LANES:(c + 1) * LANES]


def _router_kernel(x_ref, gain_ref, wr_ref, h_ref, idx_ref, gate_ref):
    h = _rms_rows(x_ref[...], gain_ref[...])
    _store_chunk_rows(h_ref, h)
    logits = jnp.dot(h, wr_ref[...], preferred_element_type=F32, precision=lax.Precision.HIGHEST)
    lane = lax.broadcasted_iota(jnp.int32, logits.shape, 1)
    lane_f = lane.astype(F32)
    logits = jnp.where(lane < N_EXPERTS, logits, -jnp.inf)
    m1 = jnp.max(logits, axis=-1, keepdims=True)
    i1 = jnp.min(jnp.where(logits == m1, lane_f, float(LANES)), axis=-1, keepdims=True)
    rest = jnp.where(lane_f == i1, -jnp.inf, logits)
    m2 = jnp.max(rest, axis=-1, keepdims=True)
    i2 = jnp.min(jnp.where(rest == m2, lane_f, float(LANES)), axis=-1, keepdims=True)
    e = jnp.exp(m2 - m1)
    tot = 1.0 + e
    idx_ref[...] = jnp.where(lane == 0, i1, jnp.where(lane == 1, i2, 0.0)).astype(jnp.int32)
    gate_ref[...] = jnp.where(lane == 0, 1.0 / tot, jnp.where(lane == 1, e / tot, 0.0))


def _router(x, gain, w_router):
    n, d = x.shape
    tm = min(512, n)
    wr = jnp.zeros((d, LANES), F32).at[:, :N_EXPERTS].set(w_router.astype(F32))
    row = lambda i: (i, 0)
    return pl.pallas_call(
        _router_kernel,
        grid=(n // tm,),
        in_specs=[
            pl.BlockSpec((tm, d), row),
            pl.BlockSpec((1, d), lambda i: (0, 0)),
            pl.BlockSpec((d, LANES), lambda i: (0, 0)),
        ],
        out_specs=[pl.BlockSpec((tm * (d // LANES), LANES), row), pl.BlockSpec((tm, LANES), row),
                   pl.BlockSpec((tm, LANES), row)],
        out_shape=[
            jax.ShapeDtypeStruct((n * (d // LANES), LANES), F32),
            jax.ShapeDtypeStruct((n, LANES), jnp.int32),
            jax.ShapeDtypeStruct((n, LANES), F32),
        ],
        compiler_params=_params("arbitrary"),
        name="router",
    )(x, gain.reshape(1, d), wr)


def _expert_kernel(bexp_ref, nused_ref, rtok_ref, h_hbm, wg_ref, wu_ref, wd_ref, y_ref,
                   xs_ref, xb_ref, acc_ref, sem):
    del bexp_ref
    i = pl.program_id(0)
    j = pl.program_id(1)
    tmb, d = xb_ref.shape
    nc = d // LANES
    n_used = nused_ref[0]

    def row_copy(tok, slot, r):
        return pltpu.make_async_copy(h_hbm.at[pl.ds(pl.multiple_of(tok * nc, nc), nc), :],
                                     xs_ref.at[slot, pl.ds(pl.multiple_of(r * nc, nc), nc), :], sem.at[slot])

    def start_gather(blk, slot):
        def issue(r, carry):
            row_copy(rtok_ref[blk * tmb + r], slot, r).start()
            return carry
        lax.fori_loop(0, tmb, issue, 0, unroll=8)

    def wait_gather(slot):
        pltpu.make_async_copy(h_hbm.at[pl.ds(0, tmb * nc), :], xs_ref.at[slot], sem.at[slot]).wait()

    @pl.when(i < n_used)
    def _():
        slot = i % 2

        @pl.when(j == 0)
        def _():
            @pl.when(i == 0)
            def _():
                start_gather(0, 0)

            wait_gather(slot)

            @pl.when(i + 1 < n_used)
            def _():
                start_gather(i + 1, 1 - slot)

            for c in range(nc):
                xb_ref[:, c * LANES:(c + 1) * LANES] = xs_ref[slot, pl.ds(c, tmb, stride=nc), :].astype(BF16)
            acc_ref[...] = jnp.zeros_like(acc_ref)

        x = xb_ref[...]
        g = jnp.dot(x, wg_ref[0], preferred_element_type=F32)
        u = jnp.dot(x, wu_ref[0], preferred_element_type=F32)
        acc_ref[...] += jnp.dot(_silu_mul(g, u).astype(BF16), wd_ref[0], preferred_element_type=F32)

    last = j == pl.num_programs(1) - 1

    @pl.when(last & (i < n_used))
    def _():
        _store_chunk_rows(y_ref, acc_ref[...])

    @pl.when(last & (i >= n_used))
    def _():
        y_ref[...] = jnp.zeros_like(y_ref)


def _expert_ffn(h_rows, block_expert, n_used, row_tok, w_gate, w_up, w_down):
    d = w_gate.shape[1]
    f = w_gate.shape[2]
    tmb = MOE_BLOCK
    tf = MOE_F_BLOCK
    n_blocks = block_expert.shape[0]
    nf = f // tf
    assert f % tf == 0

    def fblk(i, j, nu):
        return jnp.where(i < nu[0], j, nf - 1)

    grid_spec = pltpu.PrefetchScalarGridSpec(
        num_scalar_prefetch=3,
        grid=(n_blocks, nf),
        in_specs=[
            pl.BlockSpec(memory_space=pl.ANY),
            pl.BlockSpec((1, d, tf), lambda i, j, be, nu, rt: (be[i], 0, fblk(i, j, nu))),
            pl.BlockSpec((1, d, tf), lambda i, j, be, nu, rt: (be[i], 0, fblk(i, j, nu))),
            pl.BlockSpec((1, tf, d), lambda i, j, be, nu, rt: (be[i], fblk(i, j, nu), 0)),
        ],
        out_specs=pl.BlockSpec((tmb * (d // LANES), LANES), lambda i, j, be, nu, rt: (i, 0)),
        scratch_shapes=[
            pltpu.VMEM((2, tmb * (d // LANES), LANES), F32),
            pltpu.VMEM((tmb, d), BF16),
            pltpu.VMEM((tmb, d), F32),
            pltpu.SemaphoreType.DMA((2,)),
        ],
    )
    return pl.pallas_call(
        _expert_kernel,
        grid_spec=grid_spec,
        out_shape=jax.ShapeDtypeStruct((n_blocks * tmb * (d // LANES), LANES), F32),
        compiler_params=_params("arbitrary", "arbitrary"),
        name="expert_ffn",
    )(block_expert, n_used, row_tok, h_rows, w_gate, w_up, w_down)


def _combine_kernel(dest_ref, ys_hbm, x_ref, gate_ref, oa_ref, ob_ref, buf_ref, sem, *, first_blocks):
    i = pl.program_id(0)
    nsteps = pl.num_programs(0)
    tc, d = x_ref.shape
    nc = d // LANES

    def row_copy(row, slot, dst):
        return pltpu.make_async_copy(ys_hbm.at[pl.ds(pl.multiple_of(row * nc, nc), nc), :],
                                     buf_ref.at[slot, pl.ds(pl.multiple_of(dst * nc, nc), nc), :], sem.at[slot])

    def start_gather(step, slot):
        def issue(r, carry):
            row_copy(dest_ref[step * (2 * tc) + r], slot, (r % 2) * tc + r // 2).start()
            return carry
        lax.fori_loop(0, 2 * tc, issue, 0, unroll=8)

    def wait_gather(slot):
        pltpu.make_async_copy(ys_hbm.at[pl.ds(0, 2 * tc * nc), :], buf_ref.at[slot], sem.at[slot]).wait()

    slot = i % 2

    @pl.when(i == 0)
    def _():
        start_gather(0, 0)

    wait_gather(slot)

    @pl.when(i + 1 < nsteps)
    def _():
        start_gather(i + 1, 1 - slot)

    g0 = gate_ref[:, 0:1]
    g1 = gate_ref[:, 1:2]

    def emit(o_ref):
        for c in range(nc):
            sl = slice(c * LANES, (c + 1) * LANES)
            y0 = buf_ref[slot, pl.ds(c, tc, stride=nc), :]
            y1 = buf_ref[slot, pl.ds(tc * nc + c, tc, stride=nc), :]
            o_ref[:, sl] = x_ref[:, sl] + (y0 * g0 + y1 * g1)
    _pick_rows(i, first_blocks, oa_ref, ob_ref, emit)


def _combine(x, ys, dest, gate, n_first):
    n, d = x.shape
    tc = min(COMBINE_BLOCK, n_first, n - n_first)
    assert n_first % tc == 0 and (n - n_first) % tc == 0
    first_blocks = n_first // tc
    grid_spec = pltpu.PrefetchScalarGridSpec(
        num_scalar_prefetch=1,
        grid=(n // tc,),
        in_specs=[
            pl.BlockSpec(memory_space=pl.ANY),
            pl.BlockSpec((tc, d), lambda i, dr: (i, 0)),
            pl.BlockSpec((tc, LANES), lambda i, dr: (i, 0)),
        ],
        out_specs=_two_batch_specs(tc, d, first_blocks),
        scratch_shapes=[pltpu.VMEM((2, 2 * tc * (d // LANES), LANES), F32), pltpu.SemaphoreType.DMA((2,))],
    )
    return pl.pallas_call(
        functools.partial(_combine_kernel, first_blocks=first_blocks),
        grid_spec=grid_spec,
        out_shape=[jax.ShapeDtypeStruct((n_first, d), F32), jax.ShapeDtypeStruct((n - n_first, d), F32)],
        compiler_params=_params("arbitrary"),
        name="moe_combine",
    )(dest, ys, x, gate)


def _routing_tables(idx, tmb):
    n = idx.shape[0]
    nk = n * TOP_K
    e_flat = idx[:, :TOP_K].reshape(-1)
    onehot = (e_flat[:, None] == jnp.arange(N_EXPERTS, dtype=jnp.int32)[None, :]).astype(jnp.int32)
    csum = jnp.cumsum(onehot, axis=0)
    rank = jnp.sum(onehot * csum, axis=1) - 1
    counts = csum[-1]
    padded = (counts + tmb - 1) // tmb * tmb
    pad_end = jnp.cumsum(padded)
    pad_start = pad_end - padded
    dest = (jnp.sum(onehot * pad_start[None, :], axis=1) + rank).astype(jnp.int32)
    n_blocks = -(-nk // tmb) + N_EXPERTS
    block_expert = jnp.minimum(
        jnp.searchsorted(pad_end, jnp.arange(n_blocks, dtype=jnp.int32) * tmb, side="right"), N_EXPERTS - 1
    ).astype(jnp.int32)
    n_used = (pad_end[-1] // tmb).astype(jnp.int32).reshape(1)
    row_tok = jnp.zeros((n_blocks * tmb,), jnp.int32).at[dest].set(jnp.arange(nk, dtype=jnp.int32) // TOP_K)
    return dest, block_expert, n_used, row_tok


def _rope_tables(seq_len):
    half = HEAD_DIM // 2
    inv_freq = ROPE_THETA ** (-jnp.arange(half, dtype=F32) / half)
    ang = jnp.arange(seq_len, dtype=F32)[:, None] * inv_freq[None, :]
    cos, sin = jnp.cos(ang), jnp.sin(ang)
    return jnp.concatenate([cos, cos], axis=-1), jnp.concatenate([-sin, sin], axis=-1)


def kernel(x_prompt, x_sample, norm_mix, norm_ffn, na_w_qkv, na_q_gain, na_k_gain, na_rpb, na_w_o,
           da_w_qkv, da_q_gain, da_k_gain, da_w_o, ffn_w_gate, ffn_w_up, ffn_w_down,
           moe_w_router, moe_w_gate, moe_w_up, moe_w_down):
    d = x_prompt.shape[-1]
    trunks = []
    off = 0
    for xin in (x_prompt, x_sample):
        b, s, _ = xin.shape
        trunks.append((off, b, s))
        off += b * s
    n = off
    xa, xb = x_prompt.reshape(-1, d), x_sample.reshape(-1, d)
    depth = norm_mix.shape[0]
    assert depth == 2, "layer 0 reads the two request batches directly and layer 1 writes them separately"
    x = None

    for layer in range(depth):
        lj = layer // 2
        if layer % 2 == 0:
            heads = na_w_qkv.shape[2] // (3 * HEAD_DIM)
            tn = 1024
            per = heads * HEAD_DIM // tn
            hg = jnp.concatenate([
                jnp.broadcast_to(na_q_gain[lj], (per, HEAD_DIM)),
                jnp.broadcast_to(na_k_gain[lj], (per, HEAD_DIM)),
                jnp.ones((per, HEAD_DIM), F32),
            ]).reshape(3 * per, 1, HEAD_DIM)
            qkv = _qkv_project(xa, xb, norm_mix[layer], na_w_qkv[lj].astype(BF16), hg, 3 * per, 2 * per)
            tables = _na_bias_tables(na_rpb[lj])
            att = jnp.zeros((n, heads * HEAD_DIM), BF16)
            for (toff, b, s) in trunks:
                att = _na_attention(qkv, tables, att, toff, b, s, heads)
            x = _proj_residual(att, na_w_o[lj].astype(BF16), xa, xb)
            x = _dense_ffn(x, norm_ffn[layer], ffn_w_gate[lj].astype(BF16), ffn_w_up[lj].astype(BF16),
                           ffn_w_down[lj].astype(BF16))
        else:
            groups = len(DIL_CONFIGS)
            heads = da_w_qkv.shape[2] // (groups * 3 * HEAD_DIM)
            dils = [dil for _, dil in DIL_CONFIGS]
            ones = jnp.ones((HEAD_DIM,), F32)
            hg = jnp.stack([t for g in range(groups) for t in (da_q_gain[lj, g], da_k_gain[lj, g], ones)])
            hg = hg.reshape(groups * 3, 1, HEAD_DIM)
            (o0, b0, s0), (o1, b1, s1) = trunks
            tq = min(QKV_DIL_TOKEN_BLOCK, n)
            assert s0 % tq == 0 and s1 % tq == 0 and o1 % tq == 0
            nb0, p0, p1 = o1 // tq, s0 // tq, s1 // tq
            pos_map = lambda i: jnp.where(i < nb0, i % p0, (i - nb0) % p1)
            smax = max(s0, s1)
            rope = [jnp.stack([t.reshape(smax // tq, tq // dil, dil, HEAD_DIM).transpose(0, 2, 1, 3)
                               .reshape(smax, HEAD_DIM) for dil in dils]) for t in _rope_tables(smax)]
            qkvs = _qkv_project_dilated(x, norm_mix[layer], da_w_qkv[lj].astype(BF16), hg, rope, pos_map,
                                        dils, heads, tq)
            outs, lses = [], []
            for g, dil in enumerate(dils):
                o_g = jnp.zeros((dil, n // dil, heads * HEAD_DIM), F32)
                lse_g = jnp.zeros((dil, n // dil, LANES), F32)
                for (toff, b, s) in trunks:
                    o_g, lse_g = _band_attention(qkvs[g], o_g, lse_g, toff, b, s, g)
                outs.append(o_g)
                lses.append(lse_g)
            x = _merge_proj(outs, lses, da_w_o[lj].astype(BF16), x)
            h, idx, gate = _router(x, norm_ffn[layer], moe_w_router[lj])
            dest, block_expert, n_used, row_tok = _routing_tables(idx, MOE_BLOCK)
            ys = _expert_ffn(h, block_expert, n_used, row_tok, moe_w_gate[lj].astype(BF16),
                             moe_w_up[lj].astype(BF16), moe_w_down[lj].astype(BF16))
            ya, yb = _combine(x, ys, dest, gate, xa.shape[0])

    return ya.reshape(x_prompt.shape), yb.reshape(x_sample.shape)
```

```python
import functools

import jax
import jax.numpy as jnp
from jax import lax
from jax.experimental import pallas as pl
from jax.experimental.pallas import tpu as pltpu

F32 = jnp.float32
BF16 = jnp.bfloat16

HEAD_DIM = 128
GRID_W = 64
NA_ROWS = 8
NA_COLS = 16
NA_GROUP = NA_ROWS // 2
NA_KEY_ROWS = 3 * NA_GROUP
NA_HEAD_BLOCK = 4
DIL_CONFIGS = ((128, 1), (512, 4), (2048, 16))
N_EXPERTS = 8
TOP_K = 2
ROPE_THETA = 10000.0
EPS = 1e-6
NEG_INF = -1e30
ATTN_SCALE = HEAD_DIM ** -0.5

LANES = 128
MXU_COLS = 256
VMEM_LIMIT = 52 * 1024 * 1024

TOKEN_BLOCK = 1024
QKV_DIL_TOKEN_BLOCK = 512
FFN_TOKEN_BLOCK = 512
FFN_F_BLOCK = 512
BAND_Q_BLOCK = 512
BAND_Q_TILE = 128
MOE_BLOCK = 512
MOE_F_BLOCK = 1024
COMBINE_BLOCK = 256


def _params(*sem):
    return pltpu.CompilerParams(dimension_semantics=sem, vmem_limit_bytes=VMEM_LIMIT)


def _rms_rows(x, gain):
    ms = jnp.mean(x * x, axis=-1, keepdims=True)
    return x * lax.rsqrt(ms + EPS) * gain


def _pick_rows(i, first_blocks, xa_ref, xb_ref, use):
    @pl.when(i < first_blocks)
    def _():
        use(xa_ref)

    @pl.when(i >= first_blocks)
    def _():
        use(xb_ref)


def _two_batch_specs(tm, d, first_blocks, **kw):
    return [pl.BlockSpec((tm, d), lambda i, *_: (jnp.minimum(i, first_blocks - 1), 0), **kw),
            pl.BlockSpec((tm, d), lambda i, *_: (jnp.maximum(i - first_blocks, 0), 0), **kw)]


def _qkv_kernel(xa_ref, xb_ref, gain_ref, w_ref, hg_ref, o_ref, h_ref, *, first_blocks, v_every, v_from):
    j = pl.program_id(1)

    @pl.when(j == 0)
    def _():
        def norm(x_ref):
            h_ref[...] = _rms_rows(x_ref[...], gain_ref[...]).astype(BF16)
        _pick_rows(pl.program_id(0), first_blocks, xa_ref, xb_ref, norm)

    is_qk = (j % v_every) < v_from
    for c in range(w_ref.shape[1] // MXU_COLS):
        acc = jnp.dot(h_ref[...], w_ref[:, c * MXU_COLS:(c + 1) * MXU_COLS], preferred_element_type=F32)
        for h in range(MXU_COLS // HEAD_DIM):
            seg = acc[:, h * HEAD_DIM:(h + 1) * HEAD_DIM]
            lo = c * MXU_COLS + h * HEAD_DIM
            o_ref[:, lo:lo + HEAD_DIM] = jnp.where(is_qk, _rms_rows(seg, hg_ref[0]), seg).astype(o_ref.dtype)


def _qkv_project(xa, xb, gain, w, head_gain, v_every, v_from):
    d = xa.shape[1]
    n = xa.shape[0] + xb.shape[0]
    n_out = w.shape[1]
    tm = min(TOKEN_BLOCK, xa.shape[0], xb.shape[0])
    assert xa.shape[0] % tm == 0 and xb.shape[0] % tm == 0
    first_blocks = xa.shape[0] // tm
    tn = n_out // head_gain.shape[0]
    return pl.pallas_call(
        functools.partial(_qkv_kernel, first_blocks=first_blocks, v_every=v_every, v_from=v_from),
        grid=(n // tm, n_out // tn),
        in_specs=_two_batch_specs(tm, d, first_blocks, pipeline_mode=pl.Buffered(1)) + [
            pl.BlockSpec((1, d), lambda i, j: (0, 0)),
            pl.BlockSpec((d, tn), lambda i, j: (0, j)),
            pl.BlockSpec((1, 1, HEAD_DIM), lambda i, j: (j, 0, 0)),
        ],
        out_specs=pl.BlockSpec((tm, tn), lambda i, j: (i, j)),
        out_shape=jax.ShapeDtypeStruct((n, n_out), BF16),
        scratch_shapes=[pltpu.VMEM((tm, d), BF16)],
        compiler_params=_params("arbitrary", "arbitrary"),
        name="qkv_project",
    )(xa, xb, gain.reshape(1, d), w, head_gain)


def _qkv_dil_kernel(x_ref, gain_ref, w_ref, hg_ref, cos_ref, sin_ref, *rest, dils):
    out_refs, (h_ref, stage_ref) = rest[:len(dils)], rest[len(dils):]
    j = pl.program_id(1)
    tm, d = x_ref.shape
    tn = w_ref.shape[1]

    @pl.when(j == 0)
    def _():
        y = _rms_rows(x_ref[...], gain_ref[...])
        for c in range(d // LANES):
            stage_ref[c] = y[:, c * LANES:(c + 1) * LANES]
        for g, dil in enumerate(dils):
            rows = tm // dil
            for r in range(dil):
                for c in range(d // LANES):
                    h_ref[g, r * rows:(r + 1) * rows, c * LANES:(c + 1) * LANES] = (
                        stage_ref[c, pl.ds(r, rows, stride=dil), :].astype(BF16))

    group = j // 3
    which = j % 3
    acc = jnp.dot(h_ref[group], w_ref[...], preferred_element_type=F32)

    for g, (dil, o_ref) in enumerate(zip(dils, out_refs)):
        rows = tm // dil

        @pl.when((group == g) & (which < 2))
        def _(dil=dil, o_ref=o_ref, rows=rows):
            for h in range(tn // HEAD_DIM):
                sl = slice(h * HEAD_DIM, (h + 1) * HEAD_DIM)
                y = _rms_rows(acc[:, sl], hg_ref[0])
                y = (y * cos_ref[...] + pltpu.roll(y, HEAD_DIM // 2, 1) * sin_ref[...]).astype(o_ref.dtype)
                for r in range(dil):
                    o_ref[r, :, sl] = y[r * rows:(r + 1) * rows]

        @pl.when((group == g) & (which == 2))
        def _(dil=dil, o_ref=o_ref, rows=rows):
            v = acc.astype(o_ref.dtype)
            for r in range(dil):
                o_ref[r] = v[r * rows:(r + 1) * rows]


def _qkv_project_dilated(x, gain, w, head_gain, rope_tabs, pos_map, dils, heads, tm):
    n, d = x.shape
    n_out = w.shape[1]
    width = heads * HEAD_DIM
    ng = len(dils)
    assert n_out == ng * 3 * width and head_gain.shape[0] == ng * 3

    def out_map(g):
        def index(i, j):
            return (jnp.clip(j - 3 * g, 0, 2), 0, i, 0)
        return index

    rope_spec = pl.BlockSpec((None, tm, HEAD_DIM), lambda i, j: (j // 3, pos_map(i), 0))
    return pl.pallas_call(
        functools.partial(_qkv_dil_kernel, dils=tuple(dils)),
        grid=(n // tm, ng * 3),
        in_specs=[
            pl.BlockSpec((tm, d), lambda i, j: (i, 0)),
            pl.BlockSpec((1, d), lambda i, j: (0, 0)),
            pl.BlockSpec((d, width), lambda i, j: (0, j)),
            pl.BlockSpec((1, 1, HEAD_DIM), lambda i, j: (j, 0, 0)),
            rope_spec,
            rope_spec,
        ],
        out_specs=[pl.BlockSpec((None, dil, tm // dil, width), out_map(g)) for g, dil in enumerate(dils)],
        out_shape=[jax.ShapeDtypeStruct((3, dil, n // dil, width), BF16) for dil in dils],
        scratch_shapes=[pltpu.VMEM((ng, tm, d), BF16), pltpu.VMEM((d // LANES, tm, LANES), F32)],
        compiler_params=_params("arbitrary", "arbitrary"),
        name="qkv_project_dilated",
    )(x, gain.reshape(1, d), w, head_gain, *rope_tabs)


def _na_bias_tables(rpb):
    half = NA_ROWS // 2
    assert NA_GROUP == half
    qi = jnp.arange(NA_GROUP)[:, None]
    kj = jnp.arange(NA_KEY_ROWS)[None, :]
    dr = jnp.broadcast_to(kj - NA_GROUP - qi, (3, NA_GROUP, NA_KEY_ROWS))
    rvalid = jnp.stack([
        jnp.broadcast_to(kj >= NA_GROUP, (NA_GROUP, NA_KEY_ROWS)),
        (kj - qi >= 0) & (kj - qi < NA_ROWS),
        jnp.broadcast_to(kj < NA_ROWS, (NA_GROUP, NA_KEY_ROWS)),
    ])
    c = jnp.arange(GRID_W)
    cs = jnp.clip(c - NA_COLS // 2, 0, GRID_W - NA_COLS)
    cvalid = (c[None, :] >= cs[:, None]) & (c[None, :] < cs[:, None] + NA_COLS)
    dc = jnp.clip(c[None, :] - c[:, None], 1 - NA_COLS, NA_COLS - 1)
    dri = jnp.clip(dr, 1 - NA_ROWS, NA_ROWS - 1) + NA_ROWS - 1
    sel_r = jax.nn.one_hot(dri, 2 * NA_ROWS - 1, dtype=F32)
    sel_c = jax.nn.one_hot(dc + NA_COLS - 1, 2 * NA_COLS - 1, dtype=F32)
    hp = lax.Precision.HIGHEST
    tmp = jnp.einsum("hab,qcb->haqc", rpb.astype(F32), sel_c, precision=hp)
    bias = jnp.einsum("haqc,vgka->vhgqkc", tmp, sel_r, precision=hp)
    valid = rvalid[:, None, :, None, :, None] & cvalid[None, None, None, :, None, :]
    bias = jnp.where(valid, bias, NEG_INF)
    h = rpb.shape[0]
    return bias.reshape(3, h, NA_GROUP * GRID_W, NA_KEY_ROWS * GRID_W)


def _na_kernel(q_ref, kp_ref, k_ref, kn_ref, vp_ref, v_ref, vn_ref, tab_ref, prev_ref, o_ref):
    del prev_ref
    gq = q_ref.shape[0]
    nt = (((1,), (1,)), ((), ()))
    for h in range(q_ref.shape[1] // HEAD_DIM):
        sl = slice(h * HEAD_DIM, (h + 1) * HEAD_DIM)
        q = q_ref[:, sl]
        s = jnp.concatenate(
            [lax.dot_general(q, kr[:, sl], nt, preferred_element_type=F32) for kr in (kp_ref, k_ref, kn_ref)],
            axis=1)
        s = s * ATTN_SCALE + tab_ref[0, h]
        m = jnp.max(s, axis=-1, keepdims=True)
        p = jnp.exp(s - m)
        den = jnp.sum(p, axis=-1, keepdims=True)
        pb = p.astype(BF16)
        o = (jnp.dot(pb[:, :gq], vp_ref[:, sl], preferred_element_type=F32)
             + jnp.dot(pb[:, gq:2 * gq], v_ref[:, sl], preferred_element_type=F32)
             + jnp.dot(pb[:, 2 * gq:], vn_ref[:, sl], preferred_element_type=F32))
        o_ref[:, sl] = (o / den).astype(o_ref.dtype)


def _na_attention(qkv, tables, out_prev, tok_off, batch, seq, heads):
    rows = seq // GRID_W
    groups = rows // NA_GROUP
    hb = NA_HEAD_BLOCK
    assert rows % NA_GROUP == 0 and rows >= NA_ROWS and tok_off % seq == 0 and heads % hb == 0
    gq = NA_GROUP * GRID_W
    blk0 = tok_off // gq
    hblocks = heads // hb

    def variant(g):
        return jnp.where(g == 0, 0, jnp.where(g == groups - 1, 2, 1))

    def rows_of(which, shift):
        def index(b, h, g):
            gg = jnp.clip(g + shift, 0, groups - 1)
            return (blk0 + b * groups + gg, which * hblocks + h)
        return index

    blk = lambda index: pl.BlockSpec((gq, hb * HEAD_DIM), index)
    return pl.pallas_call(
        _na_kernel,
        grid=(batch, hblocks, groups),
        in_specs=[
            blk(rows_of(0, 0)),
            blk(rows_of(1, -1)), blk(rows_of(1, 0)), blk(rows_of(1, 1)),
            blk(rows_of(2, -1)), blk(rows_of(2, 0)), blk(rows_of(2, 1)),
            pl.BlockSpec((1, hb, gq, NA_KEY_ROWS * GRID_W), lambda b, h, g: (variant(g), h, 0, 0)),
            pl.BlockSpec(memory_space=pl.ANY),
        ],
        out_specs=blk(rows_of(0, 0)),
        out_shape=jax.ShapeDtypeStruct(out_prev.shape, out_prev.dtype),
        input_output_aliases={8: 0},
        compiler_params=_params("arbitrary", "arbitrary", "arbitrary"),
        name="na_attention",
    )(qkv, qkv, qkv, qkv, qkv, qkv, qkv, tables, out_prev)


def _proj_residual_kernel(a_ref, w_ref, xa_ref, xb_ref, o_ref, *, first_blocks):
    y = jnp.dot(a_ref[...], w_ref[...], preferred_element_type=F32)

    def add(x_ref):
        o_ref[...] = x_ref[...] + y
    _pick_rows(pl.program_id(1), first_blocks, xa_ref, xb_ref, add)


def _proj_residual(a, w, xa, xb):
    n, k = a.shape
    d = w.shape[1]
    tm = min(TOKEN_BLOCK, xa.shape[0], xb.shape[0])
    tn = min(1024, d)
    assert xa.shape[0] % tm == 0 and xb.shape[0] % tm == 0 and xa.shape[0] + xb.shape[0] == n
    first_blocks = xa.shape[0] // tm
    return pl.pallas_call(
        functools.partial(_proj_residual_kernel, first_blocks=first_blocks),
        grid=(d // tn, n // tm),
        in_specs=[
            pl.BlockSpec((tm, k), lambda j, i: (i, 0)),
            pl.BlockSpec((k, tn), lambda j, i: (0, j)),
            pl.BlockSpec((tm, tn), lambda j, i: (jnp.minimum(i, first_blocks - 1), j)),
            pl.BlockSpec((tm, tn), lambda j, i: (jnp.maximum(i - first_blocks, 0), j)),
        ],
        out_specs=pl.BlockSpec((tm, tn), lambda j, i: (i, j)),
        out_shape=jax.ShapeDtypeStruct((n, d), F32),
        compiler_params=_params("arbitrary", "arbitrary"),
        name="proj_residual",
    )(a, w, xa, xb)


def _silu_mul(g, u):
    return g * (1.0 / (1.0 + jnp.exp(-g))) * u


def _ffn_kernel(x_ref, gain_ref, wg_ref, wu_ref, wd_ref, o_ref, h_ref, acc_ref):
    j = pl.program_id(1)

    @pl.when(j == 0)
    def _():
        h_ref[...] = _rms_rows(x_ref[...], gain_ref[...]).astype(BF16)
        acc_ref[...] = jnp.zeros_like(acc_ref)

    h = h_ref[...]
    g = jnp.dot(h, wg_ref[...], preferred_element_type=F32)
    u = jnp.dot(h, wu_ref[...], preferred_element_type=F32)
    acc_ref[...] += jnp.dot(_silu_mul(g, u).astype(BF16), wd_ref[...], preferred_element_type=F32)

    @pl.when(j == pl.num_programs(1) - 1)
    def _():
        o_ref[...] = x_ref[...] + acc_ref[...]


def _dense_ffn(x, gain, w_gate, w_up, w_down):
    n, d = x.shape
    f = w_gate.shape[1]
    tm = min(FFN_TOKEN_BLOCK, n)
    tf = FFN_F_BLOCK
    assert f % tf == 0
    return pl.pallas_call(
        _ffn_kernel,
        grid=(n // tm, f // tf),
        in_specs=[
            pl.BlockSpec((tm, d), lambda i, j: (i, 0)),
            pl.BlockSpec((1, d), lambda i, j: (0, 0)),
            pl.BlockSpec((d, tf), lambda i, j: (0, j)),
            pl.BlockSpec((d, tf), lambda i, j: (0, j)),
            pl.BlockSpec((tf, d), lambda i, j: (j, 0)),
        ],
        out_specs=pl.BlockSpec((tm, d), lambda i, j: (i, 0)),
        out_shape=jax.ShapeDtypeStruct((n, d), F32),
        scratch_shapes=[pltpu.VMEM((tm, d), BF16), pltpu.VMEM((tm, d), F32)],
        compiler_params=_params("arbitrary", "arbitrary"),
        name="dense_ffn",
    )(x, gain.reshape(1, d), w_gate, w_up, w_down)


def _band_kernel(q_ref, kp_ref, k_ref, kn_ref, vp_ref, v_ref, vn_ref, o_prev, lse_prev, o_ref, lse_ref,
                 kwin_ref, vwin_ref, *, length, tq, half):
    del o_prev, lse_prev
    tqb = q_ref.shape[0]
    kw = tq + 2 * half
    q0 = pl.program_id(2) * tqb
    for win, parts in ((kwin_ref, (kp_ref, k_ref, kn_ref)), (vwin_ref, (vp_ref, v_ref, vn_ref))):
        win[:half] = parts[0][...]
        win[half:half + tqb] = parts[1][...]
        win[half + tqb:] = parts[2][...]
    lane = lax.broadcasted_iota(jnp.int32, (tq, LANES), 1)

    def body(t, carry):
        r0 = pl.multiple_of(t * tq, tq)
        qpos = q0 + r0 + lax.broadcasted_iota(jnp.int32, (tq, kw), 0)
        kpos = q0 + r0 - half + lax.broadcasted_iota(jnp.int32, (tq, kw), 1)
        ok = (jnp.abs(qpos - kpos) <= half) & (kpos >= 0) & (kpos < length)
        lse_tile = jnp.zeros((tq, LANES), F32)
        for h in range(q_ref.shape[1] // HEAD_DIM):
            sl = slice(h * HEAD_DIM, (h + 1) * HEAD_DIM)
            q = q_ref[pl.ds(r0, tq), sl]
            k = kwin_ref[pl.ds(r0, kw), sl]
            v = vwin_ref[pl.ds(r0, kw), sl]
            s = lax.dot_general(q, k, (((1,), (1,)), ((), ())), preferred_element_type=F32) * ATTN_SCALE
            s = jnp.where(ok, s, NEG_INF)
            m = jnp.max(s, axis=-1, keepdims=True)
            p = jnp.exp(s - m)
            den = jnp.sum(p, axis=-1, keepdims=True)
            o = jnp.dot(p.astype(BF16), v, preferred_element_type=F32) / den
            o_ref[pl.ds(r0, tq), sl] = o
            lse_tile = jnp.where(lane == h, m + jnp.log(den), lse_tile)
        lse_ref[pl.ds(r0, tq), :] = lse_tile
        return carry

    lax.fori_loop(0, tqb // tq, body, 0)


def _band_attention(qkv, o_prev, lse_prev, tok_off, batch, seq, group):
    window, dil = DIL_CONFIGS[group]
    half = window // (2 * dil)
    length = seq // dil
    assert seq % dil == 0 and length % half == 0 and tok_off % seq == 0 and qkv.shape[1] == dil
    tqb = min(BAND_Q_BLOCK, length)
    tq = min(BAND_Q_TILE, tqb)
    assert length % tqb == 0 and tqb % tq == 0 and tqb % half == 0
    width = qkv.shape[3]
    nqb = length // tqb
    hpb = tqb // half
    seq0 = tok_off // seq

    def main(which):
        return lambda b, r, t: (which, r, (seq0 + b) * nqb + t, 0)

    def halo(which, side):
        def index(b, r, t):
            first = (seq0 + b) * nqb * hpb
            blk = jnp.clip(t * hpb + (hpb if side else -1), 0, nqb * hpb - 1)
            return (which, r, first + blk, 0)
        return index

    mspec = lambda which: pl.BlockSpec((None, None, tqb, width), main(which))
    hspec = lambda which, side: pl.BlockSpec((None, None, half, width), halo(which, side))
    return pl.pallas_call(
        functools.partial(_band_kernel, length=length, tq=tq, half=half),
        grid=(batch, dil, nqb),
        in_specs=[
            mspec(0),
            hspec(1, 0), mspec(1), hspec(1, 1),
            hspec(2, 0), mspec(2), hspec(2, 1),
            pl.BlockSpec(memory_space=pl.ANY),
            pl.BlockSpec(memory_space=pl.ANY),
        ],
        out_specs=[
            pl.BlockSpec((None, tqb, width), lambda b, r, t: (r, (seq0 + b) * nqb + t, 0)),
            pl.BlockSpec((None, tqb, LANES), lambda b, r, t: (r, (seq0 + b) * nqb + t, 0)),
        ],
        out_shape=[jax.ShapeDtypeStruct(o_prev.shape, F32), jax.ShapeDtypeStruct(lse_prev.shape, F32)],
        input_output_aliases={7: 0, 8: 1},
        scratch_shapes=[pltpu.VMEM((tqb + 2 * half, width), BF16), pltpu.VMEM((tqb + 2 * half, width), BF16)],
        compiler_params=_params("arbitrary", "arbitrary", "arbitrary"),
        name="band_attention",
    )(qkv, qkv, qkv, qkv, qkv, qkv, qkv, o_prev, lse_prev)


def _merge_proj_kernel(*refs, dils):
    ng = len(dils)
    o_refs, l_refs = refs[:ng], refs[ng:2 * ng]
    w_ref, x_ref, out_ref, a_ref, os_ref, ls_ref = refs[2 * ng:]
    tm = x_ref.shape[0]

    heads = a_ref.shape[1] // HEAD_DIM
    for g, dil in enumerate(dils):
        for r in range(dil):
            ls_ref[g, pl.ds(r, tm // dil, stride=dil), :] = l_refs[g][r]
            for h in range(heads):
                os_ref[g, h, pl.ds(r, tm // dil, stride=dil), :] = o_refs[g][r, :, h * HEAD_DIM:(h + 1) * HEAD_DIM]

    lse = [ls_ref[g] for g in range(ng)]
    m = functools.reduce(jnp.maximum, lse)
    e = [jnp.exp(l - m) for l in lse]
    tot = functools.reduce(lambda p, q: p + q, e)
    wts = [eg / tot for eg in e]
    for h in range(heads):
        terms = [wts[g][:, h:h + 1] * os_ref[g, h] for g in range(ng)]
        a_ref[:, h * HEAD_DIM:(h + 1) * HEAD_DIM] = functools.reduce(lambda p, q: p + q, terms).astype(BF16)
    out_ref[...] = x_ref[...] + jnp.dot(a_ref[...], w_ref[...], preferred_element_type=F32)


def _merge_proj(outs, lses, w, x):
    n, d = x.shape
    width = w.shape[0]
    dils = tuple(o.shape[0] for o in outs)
    tm = min(512, n)
    row = lambda i: (i, 0)
    grp = lambda i: (0, i, 0)
    return pl.pallas_call(
        functools.partial(_merge_proj_kernel, dils=dils),
        grid=(n // tm,),
        in_specs=[pl.BlockSpec((dil, tm // dil, width), grp) for dil in dils]
        + [pl.BlockSpec((dil, tm // dil, LANES), grp) for dil in dils]
        + [pl.BlockSpec((width, d), lambda i: (0, 0)), pl.BlockSpec((tm, d), row)],
        out_specs=pl.BlockSpec((tm, d), row),
        out_shape=jax.ShapeDtypeStruct((n, d), F32),
        scratch_shapes=[
            pltpu.VMEM((tm, width), BF16),
            pltpu.VMEM((len(dils), width // HEAD_DIM, tm, HEAD_DIM), F32),
            pltpu.VMEM((len(dils), tm, LANES), F32),
        ],
        compiler_params=_params("arbitrary"),
        name="merge_proj",
    )(*outs, *lses, w, x)


def _store_chunk_rows(ref, x):
    rows, d = x.shape
    nc = d // LANES
    for c in range(nc):
        ref[pl.ds(c, rows, stride=nc), :] = x[:, c * LANES:(c + 1) * LANES]


def _router_kernel(x_ref, gain_ref, wr_ref, h_ref, idx_ref, gate_ref):
    h = _rms_rows(x_ref[...], gain_ref[...])
    _store_chunk_rows(h_ref, h)
    logits = jnp.dot(h, wr_ref[...], preferred_element_type=F32, precision=lax.Precision.HIGHEST)
    lane = lax.broadcasted_iota(jnp.int32, logits.shape, 1)
    lane_f = lane.astype(F32)
    logits = jnp.where(lane < N_EXPERTS, logits, -jnp.inf)
    m1 = jnp.max(logits, axis=-1, keepdims=True)
    i1 = jnp.min(jnp.where(logits == m1, lane_f, float(LANES)), axis=-1, keepdims=True)
    rest = jnp.where(lane_f == i1, -jnp.inf, logits)
    m2 = jnp.max(rest, axis=-1, keepdims=True)
    i2 = jnp.min(jnp.where(rest == m2, lane_f, float(LANES)), axis=-1, keepdims=True)
    e = jnp.exp(m2 - m1)
    tot = 1.0 + e
    idx_ref[...] = jnp.where(lane == 0, i1, jnp.where(lane == 1, i2, 0.0)).astype(jnp.int32)
    gate_ref[...] = jnp.where(lane == 0, 1.0 / tot, jnp.where(lane == 1, e / tot, 0.0))


def _router(x, gain, w_router):
    n, d = x.shape
    tm = min(512, n)
    wr = jnp.zeros((d, LANES), F32).at[:, :N_EXPERTS].set(w_router.astype(F32))
    row = lambda i: (i, 0)
    return pl.pallas_call(
        _router_kernel,
        grid=(n // tm,),
        in_specs=[
            pl.BlockSpec((tm, d), row),
            pl.BlockSpec((1, d), lambda i: (0, 0)),
            pl.BlockSpec((d, LANES), lambda i: (0, 0)),
        ],
        out_specs=[pl.BlockSpec((tm * (d // LANES), LANES), row), pl.BlockSpec((tm, LANES), row),
                   pl.BlockSpec((tm, LANES), row)],
        out_shape=[
            jax.ShapeDtypeStruct((n * (d // LANES), LANES), F32),
            jax.ShapeDtypeStruct((n, LANES), jnp.int32),
            jax.ShapeDtypeStruct((n, LANES), F32),
        ],
        compiler_params=_params("arbitrary"),
        name="router",
    )(x, gain.reshape(1, d), wr)


def _expert_kernel(bexp_ref, nused_ref, rtok_ref, h_hbm, wg_ref, wu_ref, wd_ref, y_ref,
                   xs_ref, xb_ref, acc_ref, sem):
    del bexp_ref
    i = pl.program_id(0)
    j = pl.program_id(1)
    tmb, d = xb_ref.shape
    nc = d // LANES
    n_used = nused_ref[0]

    def row_copy(tok, slot, r):
        return pltpu.make_async_copy(h_hbm.at[pl.ds(pl.multiple_of(tok * nc, nc), nc), :],
                                     xs_ref.at[slot, pl.ds(pl.multiple_of(r * nc, nc), nc), :], sem.at[slot])

    def start_gather(blk, slot):
        def issue(r, carry):
            row_copy(rtok_ref[blk * tmb + r], slot, r).start()
            return carry
        lax.fori_loop(0, tmb, issue, 0, unroll=8)

    def wait_gather(slot):
        pltpu.make_async_copy(h_hbm.at[pl.ds(0, tmb * nc), :], xs_ref.at[slot], sem.at[slot]).wait()

    @pl.when(i < n_used)
    def _():
        slot = i % 2

        @pl.when(j == 0)
        def _():
            @pl.when(i == 0)
            def _():
                start_gather(0, 0)

            wait_gather(slot)

            @pl.when(i + 1 < n_used)
            def _():
                start_gather(i + 1, 1 - slot)

            for c in range(nc):
                xb_ref[:, c * LANES:(c + 1) * LANES] = xs_ref[slot, pl.ds(c, tmb, stride=nc), :].astype(BF16)
            acc_ref[...] = jnp.zeros_like(acc_ref)

        x = xb_ref[...]
        g = jnp.dot(x, wg_ref[0], preferred_element_type=F32)
        u = jnp.dot(x, wu_ref[0], preferred_element_type=F32)
        acc_ref[...] += jnp.dot(_silu_mul(g, u).astype(BF16), wd_ref[0], preferred_element_type=F32)

    last = j == pl.num_programs(1) - 1

    @pl.when(last & (i < n_used))
    def _():
        _store_chunk_rows(y_ref, acc_ref[...])

    @pl.when(last & (i >= n_used))
    def _():
        y_ref[...] = jnp.zeros_like(y_ref)


def _expert_ffn(h_rows, block_expert, n_used, row_tok, w_gate, w_up, w_down):
    d = w_gate.shape[1]
    f = w_gate.shape[2]
    tmb = MOE_BLOCK
    tf = MOE_F_BLOCK
    n_blocks = block_expert.shape[0]
    nf = f // tf
    assert f % tf == 0

    def fblk(i, j, nu):
        return jnp.where(i < nu[0], j, nf - 1)

    grid_spec = pltpu.PrefetchScalarGridSpec(
        num_scalar_prefetch=3,
        grid=(n_blocks, nf),
        in_specs=[
            pl.BlockSpec(memory_space=pl.ANY),
            pl.BlockSpec((1, d, tf), lambda i, j, be, nu, rt: (be[i], 0, fblk(i, j, nu))),
            pl.BlockSpec((1, d, tf), lambda i, j, be, nu, rt: (be[i], 0, fblk(i, j, nu))),
            pl.BlockSpec((1, tf, d), lambda i, j, be, nu, rt: (be[i], fblk(i, j, nu), 0)),
        ],
        out_specs=pl.BlockSpec((tmb * (d // LANES), LANES), lambda i, j, be, nu, rt: (i, 0)),
        scratch_shapes=[
            pltpu.VMEM((2, tmb * (d // LANES), LANES), F32),
            pltpu.VMEM((tmb, d), BF16),
            pltpu.VMEM((tmb, d), F32),
            pltpu.SemaphoreType.DMA((2,)),
        ],
    )
    return pl.pallas_call(
        _expert_kernel,
        grid_spec=grid_spec,
        out_shape=jax.ShapeDtypeStruct((n_blocks * tmb * (d // LANES), LANES), F32),
        compiler_params=_params("arbitrary", "arbitrary"),
        name="expert_ffn",
    )(block_expert, n_used, row_tok, h_rows, w_gate, w_up, w_down)


def _combine_kernel(dest_ref, ys_hbm, x_ref, gate_ref, oa_ref, ob_ref, buf_ref, sem, *, first_blocks):
    i = pl.program_id(0)
    nsteps = pl.num_programs(0)
    tc, d = x_ref.shape
    nc = d // LANES
    pitch = buf_ref.shape[1] // (2 * tc)

    def row_copy(row, slot, dst):
        return pltpu.make_async_copy(ys_hbm.at[pl.ds(pl.multiple_of(row * nc, nc), nc), :],
                                     buf_ref.at[slot, pl.ds(pl.multiple_of(dst * pitch, 8), nc), :], sem.at[slot])

    def start_gather(step, slot):
        def issue(r, carry):
            row_copy(dest_ref[step * (2 * tc) + r], slot, r).start()
            return carry
        lax.fori_loop(0, 2 * tc, issue, 0, unroll=8)

    def wait_gather(slot):
        pltpu.make_async_copy(ys_hbm.at[pl.ds(0, 2 * tc * nc), :], buf_ref.at[slot, pl.ds(0, 2 * tc * nc), :],
                              sem.at[slot]).wait()

    slot = i % 2

    @pl.when(i == 0)
    def _():
        start_gather(0, 0)

    wait_gather(slot)

    @pl.when(i + 1 < nsteps)
    def _():
        start_gather(i + 1, 1 - slot)

    g0 = gate_ref[:, 0:1]
    g1 = gate_ref[:, 1:2]

    def emit(o_ref):
        for c in range(nc):
            sl = slice(c * LANES, (c + 1) * LANES)
            y0 = buf_ref[slot, pl.ds(c, tc, stride=pitch), :]
            y1 = buf_ref[slot, pl.ds(tc * pitch + c, tc, stride=pitch), :]
            o_ref[:, sl] = x_ref[:, sl] + (y0 * g0 + y1 * g1)
    _pick_rows(i, first_blocks, oa_ref, ob_ref, emit)


def _combine(x, ys, dest, gate, n_first):
    n, d = x.shape
    tc = min(COMBINE_BLOCK, n_first, n - n_first)
    assert n_first % tc == 0 and (n - n_first) % tc == 0
    first_blocks = n_first // tc
    nc = d // LANES
    pitch = nc if (nc // 8) % 2 else nc + 8
    dest = dest.reshape(n // tc, tc, TOP_K).transpose(0, 2, 1).reshape(-1)
    grid_spec = pltpu.PrefetchScalarGridSpec(
        num_scalar_prefetch=1,
        grid=(n // tc,),
        in_specs=[
            pl.BlockSpec(memory_space=pl.ANY),
            pl.BlockSpec((tc, d), lambda i, dr: (i, 0)),
            pl.BlockSpec((tc, LANES), lambda i, dr: (i, 0)),
        ],
        out_specs=_two_batch_specs(tc, d, first_blocks),
        scratch_shapes=[pltpu.VMEM((2, 2 * tc * pitch, LANES), F32), pltpu.SemaphoreType.DMA((2,))],
    )
    return pl.pallas_call(
        functools.partial(_combine_kernel, first_blocks=first_blocks),
        grid_spec=grid_spec,
        out_shape=[jax.ShapeDtypeStruct((n_first, d), F32), jax.ShapeDtypeStruct((n - n_first, d), F32)],
        compiler_params=_params("arbitrary"),
        name="moe_combine",
    )(dest, ys, x, gate)


def _routing_tables(idx, tmb):
    n = idx.shape[0]
    nk = n * TOP_K
    e_flat = idx[:, :TOP_K].reshape(-1)
    onehot = (e_flat[:, None] == jnp.arange(N_EXPERTS, dtype=jnp.int32)[None, :]).astype(jnp.int32)
    csum = jnp.cumsum(onehot, axis=0)
    rank = jnp.sum(onehot * csum, axis=1) - 1
    counts = csum[-1]
    padded = (counts + tmb - 1) // tmb * tmb
    pad_end = jnp.cumsum(padded)
    pad_start = pad_end - padded
    dest = (jnp.sum(onehot * pad_start[None, :], axis=1) + rank).astype(jnp.int32)
    n_blocks = -(-nk // tmb) + N_EXPERTS
    block_expert = jnp.minimum(
        jnp.searchsorted(pad_end, jnp.arange(n_blocks, dtype=jnp.int32) * tmb, side="right"), N_EXPERTS - 1
    ).astype(jnp.int32)
    n_used = (pad_end[-1] // tmb).astype(jnp.int32).reshape(1)
    row_tok = jnp.zeros((n_blocks * tmb,), jnp.int32).at[dest].set(jnp.arange(nk, dtype=jnp.int32) // TOP_K)
    return dest, block_expert, n_used, row_tok


def _rope_tables(seq_len):
    half = HEAD_DIM // 2
    inv_freq = ROPE_THETA ** (-jnp.arange(half, dtype=F32) / half)
    ang = jnp.arange(seq_len, dtype=F32)[:, None] * inv_freq[None, :]
    cos, sin = jnp.cos(ang), jnp.sin(ang)
    return jnp.concatenate([cos, cos], axis=-1), jnp.concatenate([-sin, sin], axis=-1)


def kernel(x_prompt, x_sample, norm_mix, norm_ffn, na_w_qkv, na_q_gain, na_k_gain, na_rpb, na_w_o,
           da_w_qkv, da_q_gain, da_k_gain, da_w_o, ffn_w_gate, ffn_w_up, ffn_w_down,
           moe_w_router, moe_w_gate, moe_w_up, moe_w_down):
    d = x_prompt.shape[-1]
    trunks = []
    off = 0
    for xin in (x_prompt, x_sample):
        b, s, _ = xin.shape
        trunks.append((off, b, s))
        off += b * s
    n = off
    xa, xb = x_prompt.reshape(-1, d), x_sample.reshape(-1, d)
    depth = norm_mix.shape[0]
    assert depth == 2, "layer 0 reads the two request batches directly and layer 1 writes them separately"
    x = None

    for layer in range(depth):
        lj = layer // 2
        if layer % 2 == 0:
            heads = na_w_qkv.shape[2] // (3 * HEAD_DIM)
            tn = 1024
            per = heads * HEAD_DIM // tn
            hg = jnp.concatenate([
                jnp.broadcast_to(na_q_gain[lj], (per, HEAD_DIM)),
                jnp.broadcast_to(na_k_gain[lj], (per, HEAD_DIM)),
                jnp.ones((per, HEAD_DIM), F32),
            ]).reshape(3 * per, 1, HEAD_DIM)
            qkv = _qkv_project(xa, xb, norm_mix[layer], na_w_qkv[lj].astype(BF16), hg, 3 * per, 2 * per)
            tables = _na_bias_tables(na_rpb[lj])
            att = jnp.zeros((n, heads * HEAD_DIM), BF16)
            for (toff, b, s) in trunks:
                att = _na_attention(qkv, tables, att, toff, b, s, heads)
            x = _proj_residual(att, na_w_o[lj].astype(BF16), xa, xb)
            x = _dense_ffn(x, norm_ffn[layer], ffn_w_gate[lj].astype(BF16), ffn_w_up[lj].astype(BF16),
                           ffn_w_down[lj].astype(BF16))
        else:
            groups = len(DIL_CONFIGS)
            heads = da_w_qkv.shape[2] // (groups * 3 * HEAD_DIM)
            dils = [dil for _, dil in DIL_CONFIGS]
            ones = jnp.ones((HEAD_DIM,), F32)
            hg = jnp.stack([t for g in range(groups) for t in (da_q_gain[lj, g], da_k_gain[lj, g], ones)])
            hg = hg.reshape(groups * 3, 1, HEAD_DIM)
            (o0, b0, s0), (o1, b1, s1) = trunks
            tq = min(QKV_DIL_TOKEN_BLOCK, n)
            assert s0 % tq == 0 and s1 % tq == 0 and o1 % tq == 0
            nb0, p0, p1 = o1 // tq, s0 // tq, s1 // tq
            pos_map = lambda i: jnp.where(i < nb0, i % p0, (i - nb0) % p1)
            smax = max(s0, s1)
            rope = [jnp.stack([t.reshape(smax // tq, tq // dil, dil, HEAD_DIM).transpose(0, 2, 1, 3)
                               .reshape(smax, HEAD_DIM) for dil in dils]) for t in _rope_tables(smax)]
            qkvs = _qkv_project_dilated(x, norm_mix[layer], da_w_qkv[lj].astype(BF16), hg, rope, pos_map,
                                        dils, heads, tq)
            outs, lses = [], []
            for g, dil in enumerate(dils):
                o_g = jnp.zeros((dil, n // dil, heads * HEAD_DIM), F32)
                lse_g = jnp.zeros((dil, n // dil, LANES), F32)
                for (toff, b, s) in trunks:
                    o_g, lse_g = _band_attention(qkvs[g], o_g, lse_g, toff, b, s, g)
                outs.append(o_g)
                lses.append(lse_g)
            x = _merge_proj(outs, lses, da_w_o[lj].astype(BF16), x)
            h, idx, gate = _router(x, norm_ffn[layer], moe_w_router[lj])
            dest, block_expert, n_used, row_tok = _routing_tables(idx, MOE_BLOCK)
            ys = _expert_ffn(h, block_expert, n_used, row_tok, moe_w_gate[lj].astype(BF16),
                             moe_w_up[lj].astype(BF16), moe_w_down[lj].astype(BF16))
            ya, yb = _combine(x, ys, dest, gate, xa.shape[0])

    return ya.reshape(x_prompt.shape), yb.reshape(x_sample.shape)
```

```python
import functools

import jax
import jax.numpy as jnp
from jax import lax
from jax.experimental import pallas as pl
from jax.experimental.pallas import tpu as pltpu

F32 = jnp.float32
BF16 = jnp.bfloat16

HEAD_DIM = 128
GRID_W = 64
NA_ROWS = 8
NA_COLS = 16
NA_GROUP = NA_ROWS // 2
NA_KEY_ROWS = 3 * NA_GROUP
NA_HEAD_BLOCK = 8
DIL_CONFIGS = ((128, 1), (512, 4), (2048, 16))
N_EXPERTS = 8
TOP_K = 2
ROPE_THETA = 10000.0
EPS = 1e-6
NEG_INF = -1e30
ATTN_SCALE = HEAD_DIM ** -0.5

LANES = 128
MXU_COLS = 256
VMEM_LIMIT = 52 * 1024 * 1024

TOKEN_BLOCK = 1024
QKV_DIL_TOKEN_BLOCK = 512
FFN_TOKEN_BLOCK = 512
FFN_F_BLOCK = 512
BAND_Q_BLOCK = 512
BAND_Q_TILE = 128
MOE_BLOCK = 512
MOE_F_BLOCK = 1024
COMBINE_BLOCK = 256


def _params(*sem):
    return pltpu.CompilerParams(dimension_semantics=sem, vmem_limit_bytes=VMEM_LIMIT)


def _rms_rows(x, gain):
    ms = jnp.mean(x * x, axis=-1, keepdims=True)
    return x * lax.rsqrt(ms + EPS) * gain


def _pick_rows(i, first_blocks, xa_ref, xb_ref, use):
    @pl.when(i < first_blocks)
    def _():
        use(xa_ref)

    @pl.when(i >= first_blocks)
    def _():
        use(xb_ref)


def _two_batch_specs(tm, d, first_blocks, **kw):
    return [pl.BlockSpec((tm, d), lambda i, *_: (jnp.minimum(i, first_blocks - 1), 0), **kw),
            pl.BlockSpec((tm, d), lambda i, *_: (jnp.maximum(i - first_blocks, 0), 0), **kw)]


def _qkv_kernel(xa_ref, xb_ref, gain_ref, w_ref, hg_ref, o_ref, h_ref, *, first_blocks, v_every, v_from):
    j = pl.program_id(1)

    @pl.when(j == 0)
    def _():
        def norm(x_ref):
            h_ref[...] = _rms_rows(x_ref[...], gain_ref[...]).astype(BF16)
        _pick_rows(pl.program_id(0), first_blocks, xa_ref, xb_ref, norm)

    is_qk = (j % v_every) < v_from
    for c in range(w_ref.shape[1] // MXU_COLS):
        acc = jnp.dot(h_ref[...], w_ref[:, c * MXU_COLS:(c + 1) * MXU_COLS], preferred_element_type=F32)
        for h in range(MXU_COLS // HEAD_DIM):
            seg = acc[:, h * HEAD_DIM:(h + 1) * HEAD_DIM]
            lo = c * MXU_COLS + h * HEAD_DIM
            o_ref[:, lo:lo + HEAD_DIM] = jnp.where(is_qk, _rms_rows(seg, hg_ref[0]), seg).astype(o_ref.dtype)


def _qkv_project(xa, xb, gain, w, head_gain, v_every, v_from):
    d = xa.shape[1]
    n = xa.shape[0] + xb.shape[0]
    n_out = w.shape[1]
    tm = min(TOKEN_BLOCK, xa.shape[0], xb.shape[0])
    assert xa.shape[0] % tm == 0 and xb.shape[0] % tm == 0
    first_blocks = xa.shape[0] // tm
    tn = n_out // head_gain.shape[0]
    return pl.pallas_call(
        functools.partial(_qkv_kernel, first_blocks=first_blocks, v_every=v_every, v_from=v_from),
        grid=(n // tm, n_out // tn),
        in_specs=_two_batch_specs(tm, d, first_blocks, pipeline_mode=pl.Buffered(1)) + [
            pl.BlockSpec((1, d), lambda i, j: (0, 0)),
            pl.BlockSpec((d, tn), lambda i, j: (0, j)),
            pl.BlockSpec((1, 1, HEAD_DIM), lambda i, j: (j, 0, 0)),
        ],
        out_specs=pl.BlockSpec((tm, tn), lambda i, j: (i, j)),
        out_shape=jax.ShapeDtypeStruct((n, n_out), BF16),
        scratch_shapes=[pltpu.VMEM((tm, d), BF16)],
        compiler_params=_params("arbitrary", "arbitrary"),
        name="qkv_project",
    )(xa, xb, gain.reshape(1, d), w, head_gain)


def _qkv_dil_kernel(x_ref, gain_ref, w_ref, hg_ref, cos_ref, sin_ref, *rest, dils):
    out_refs, (h_ref, stage_ref) = rest[:len(dils)], rest[len(dils):]
    j = pl.program_id(1)
    tm, d = x_ref.shape
    tn = w_ref.shape[1]

    @pl.when(j == 0)
    def _():
        y = _rms_rows(x_ref[...], gain_ref[...])
        for c in range(d // LANES):
            stage_ref[c] = y[:, c * LANES:(c + 1) * LANES]
        for g, dil in enumerate(dils):
            rows = tm // dil
            for r in range(dil):
                for c in range(d // LANES):
                    h_ref[g, r * rows:(r + 1) * rows, c * LANES:(c + 1) * LANES] = (
                        stage_ref[c, pl.ds(r, rows, stride=dil), :].astype(BF16))

    group = j // 3
    which = j % 3
    acc = jnp.dot(h_ref[group], w_ref[...], preferred_element_type=F32)

    for g, (dil, o_ref) in enumerate(zip(dils, out_refs)):
        rows = tm // dil

        @pl.when((group == g) & (which < 2))
        def _(dil=dil, o_ref=o_ref, rows=rows):
            for h in range(tn // HEAD_DIM):
                sl = slice(h * HEAD_DIM, (h + 1) * HEAD_DIM)
                y = _rms_rows(acc[:, sl], hg_ref[0])
                y = (y * cos_ref[...] + pltpu.roll(y, HEAD_DIM // 2, 1) * sin_ref[...]).astype(o_ref.dtype)
                for r in range(dil):
                    o_ref[r, :, sl] = y[r * rows:(r + 1) * rows]

        @pl.when((group == g) & (which == 2))
        def _(dil=dil, o_ref=o_ref, rows=rows):
            v = acc.astype(o_ref.dtype)
            for r in range(dil):
                o_ref[r] = v[r * rows:(r + 1) * rows]


def _qkv_project_dilated(x, gain, w, head_gain, rope_tabs, pos_map, dils, heads, tm):
    n, d = x.shape
    n_out = w.shape[1]
    width = heads * HEAD_DIM
    ng = len(dils)
    assert n_out == ng * 3 * width and head_gain.shape[0] == ng * 3

    def out_map(g):
        def index(i, j):
            return (jnp.clip(j - 3 * g, 0, 2), 0, i, 0)
        return index

    rope_spec = pl.BlockSpec((None, tm, HEAD_DIM), lambda i, j: (j // 3, pos_map(i), 0))
    return pl.pallas_call(
        functools.partial(_qkv_dil_kernel, dils=tuple(dils)),
        grid=(n // tm, ng * 3),
        in_specs=[
            pl.BlockSpec((tm, d), lambda i, j: (i, 0)),
            pl.BlockSpec((1, d), lambda i, j: (0, 0)),
            pl.BlockSpec((d, width), lambda i, j: (0, j)),
            pl.BlockSpec((1, 1, HEAD_DIM), lambda i, j: (j, 0, 0)),
            rope_spec,
            rope_spec,
        ],
        out_specs=[pl.BlockSpec((None, dil, tm // dil, width), out_map(g)) for g, dil in enumerate(dils)],
        out_shape=[jax.ShapeDtypeStruct((3, dil, n // dil, width), BF16) for dil in dils],
        scratch_shapes=[pltpu.VMEM((ng, tm, d), BF16), pltpu.VMEM((d // LANES, tm, LANES), F32)],
        compiler_params=_params("arbitrary", "arbitrary"),
        name="qkv_project_dilated",
    )(x, gain.reshape(1, d), w, head_gain, *rope_tabs)


def _na_bias_tables(rpb):
    half = NA_ROWS // 2
    assert NA_GROUP == half
    qi = jnp.arange(NA_GROUP)[:, None]
    kj = jnp.arange(NA_KEY_ROWS)[None, :]
    dr = jnp.broadcast_to(kj - NA_GROUP - qi, (3, NA_GROUP, NA_KEY_ROWS))
    rvalid = jnp.stack([
        jnp.broadcast_to(kj >= NA_GROUP, (NA_GROUP, NA_KEY_ROWS)),
        (kj - qi >= 0) & (kj - qi < NA_ROWS),
        jnp.broadcast_to(kj < NA_ROWS, (NA_GROUP, NA_KEY_ROWS)),
    ])
    c = jnp.arange(GRID_W)
    cs = jnp.clip(c - NA_COLS // 2, 0, GRID_W - NA_COLS)
    cvalid = (c[None, :] >= cs[:, None]) & (c[None, :] < cs[:, None] + NA_COLS)
    dc = jnp.clip(c[None, :] - c[:, None], 1 - NA_COLS, NA_COLS - 1)
    dri = jnp.clip(dr, 1 - NA_ROWS, NA_ROWS - 1) + NA_ROWS - 1
    sel_r = jax.nn.one_hot(dri, 2 * NA_ROWS - 1, dtype=F32)
    sel_c = jax.nn.one_hot(dc + NA_COLS - 1, 2 * NA_COLS - 1, dtype=F32)
    hp = lax.Precision.HIGHEST
    tmp = jnp.einsum("hab,qcb->haqc", rpb.astype(F32), sel_c, precision=hp)
    bias = jnp.einsum("haqc,vgka->vhgqkc", tmp, sel_r, precision=hp)
    valid = rvalid[:, None, :, None, :, None] & cvalid[None, None, None, :, None, :]
    bias = jnp.where(valid, bias, NEG_INF)
    h = rpb.shape[0]
    return bias.reshape(3, h, NA_GROUP * GRID_W, NA_KEY_ROWS * GRID_W)


def _na_kernel(q_ref, kp_ref, k_ref, kn_ref, vp_ref, v_ref, vn_ref, tab_ref, prev_ref, o_ref):
    del prev_ref
    gq = q_ref.shape[0]
    nt = (((1,), (1,)), ((), ()))
    for h in range(q_ref.shape[1] // HEAD_DIM):
        sl = slice(h * HEAD_DIM, (h + 1) * HEAD_DIM)
        q = q_ref[:, sl]
        s = jnp.concatenate(
            [lax.dot_general(q, kr[:, sl], nt, preferred_element_type=F32) for kr in (kp_ref, k_ref, kn_ref)],
            axis=1)
        s = s * ATTN_SCALE + tab_ref[0, h]
        m = jnp.max(s, axis=-1, keepdims=True)
        p = jnp.exp(s - m)
        den = jnp.sum(p, axis=-1, keepdims=True)
        pb = p.astype(BF16)
        o = (jnp.dot(pb[:, :gq], vp_ref[:, sl], preferred_element_type=F32)
             + jnp.dot(pb[:, gq:2 * gq], v_ref[:, sl], preferred_element_type=F32)
             + jnp.dot(pb[:, 2 * gq:], vn_ref[:, sl], preferred_element_type=F32))
        o_ref[:, sl] = (o / den).astype(o_ref.dtype)


def _na_attention(qkv, tables, out_prev, tok_off, batch, seq, heads):
    rows = seq // GRID_W
    groups = rows // NA_GROUP
    hb = NA_HEAD_BLOCK
    assert rows % NA_GROUP == 0 and rows >= NA_ROWS and tok_off % seq == 0 and heads % hb == 0
    gq = NA_GROUP * GRID_W
    blk0 = tok_off // gq
    hblocks = heads // hb

    def variant(g):
        return jnp.where(g == 0, 0, jnp.where(g == groups - 1, 2, 1))

    def rows_of(which, shift):
        def index(b, h, g):
            gg = jnp.clip(g + shift, 0, groups - 1)
            return (blk0 + b * groups + gg, which * hblocks + h)
        return index

    blk = lambda index: pl.BlockSpec((gq, hb * HEAD_DIM), index)
    return pl.pallas_call(
        _na_kernel,
        grid=(batch, hblocks, groups),
        in_specs=[
            blk(rows_of(0, 0)),
            blk(rows_of(1, -1)), blk(rows_of(1, 0)), blk(rows_of(1, 1)),
            blk(rows_of(2, -1)), blk(rows_of(2, 0)), blk(rows_of(2, 1)),
            pl.BlockSpec((1, hb, gq, NA_KEY_ROWS * GRID_W), lambda b, h, g: (variant(g), h, 0, 0)),
            pl.BlockSpec(memory_space=pl.ANY),
        ],
        out_specs=blk(rows_of(0, 0)),
        out_shape=jax.ShapeDtypeStruct(out_prev.shape, out_prev.dtype),
        input_output_aliases={8: 0},
        compiler_params=_params("arbitrary", "arbitrary", "arbitrary"),
        name="na_attention",
    )(qkv, qkv, qkv, qkv, qkv, qkv, qkv, tables, out_prev)


def _proj_residual_kernel(a_ref, w_ref, xa_ref, xb_ref, o_ref, *, first_blocks):
    y = jnp.dot(a_ref[...], w_ref[...], preferred_element_type=F32)

    def add(x_ref):
        o_ref[...] = x_ref[...] + y
    _pick_rows(pl.program_id(1), first_blocks, xa_ref, xb_ref, add)


def _proj_residual(a, w, xa, xb):
    n, k = a.shape
    d = w.shape[1]
    tm = min(TOKEN_BLOCK, xa.shape[0], xb.shape[0])
    tn = min(1024, d)
    assert xa.shape[0] % tm == 0 and xb.shape[0] % tm == 0 and xa.shape[0] + xb.shape[0] == n
    first_blocks = xa.shape[0] // tm
    return pl.pallas_call(
        functools.partial(_proj_residual_kernel, first_blocks=first_blocks),
        grid=(d // tn, n // tm),
        in_specs=[
            pl.BlockSpec((tm, k), lambda j, i: (i, 0)),
            pl.BlockSpec((k, tn), lambda j, i: (0, j)),
            pl.BlockSpec((tm, tn), lambda j, i: (jnp.minimum(i, first_blocks - 1), j)),
            pl.BlockSpec((tm, tn), lambda j, i: (jnp.maximum(i - first_blocks, 0), j)),
        ],
        out_specs=pl.BlockSpec((tm, tn), lambda j, i: (i, j)),
        out_shape=jax.ShapeDtypeStruct((n, d), F32),
        compiler_params=_params("arbitrary", "arbitrary"),
        name="proj_residual",
    )(a, w, xa, xb)


def _silu_mul(g, u):
    return g * (1.0 / (1.0 + jnp.exp(-g))) * u


def _ffn_kernel(x_ref, gain_ref, wg_ref, wu_ref, wd_ref, o_ref, h_ref, acc_ref):
    j = pl.program_id(1)

    @pl.when(j == 0)
    def _():
        h_ref[...] = _rms_rows(x_ref[...], gain_ref[...]).astype(BF16)
        acc_ref[...] = jnp.zeros_like(acc_ref)

    h = h_ref[...]
    g = jnp.dot(h, wg_ref[...], preferred_element_type=F32)
    u = jnp.dot(h, wu_ref[...], preferred_element_type=F32)
    acc_ref[...] += jnp.dot(_silu_mul(g, u).astype(BF16), wd_ref[...], preferred_element_type=F32)

    @pl.when(j == pl.num_programs(1) - 1)
    def _():
        o_ref[...] = x_ref[...] + acc_ref[...]


def _dense_ffn(x, gain, w_gate, w_up, w_down):
    n, d = x.shape
    f = w_gate.shape[1]
    tm = min(FFN_TOKEN_BLOCK, n)
    tf = FFN_F_BLOCK
    assert f % tf == 0
    return pl.pallas_call(
        _ffn_kernel,
        grid=(n // tm, f // tf),
        in_specs=[
            pl.BlockSpec((tm, d), lambda i, j: (i, 0)),
            pl.BlockSpec((1, d), lambda i, j: (0, 0)),
            pl.BlockSpec((d, tf), lambda i, j: (0, j)),
            pl.BlockSpec((d, tf), lambda i, j: (0, j)),
            pl.BlockSpec((tf, d), lambda i, j: (j, 0)),
        ],
        out_specs=pl.BlockSpec((tm, d), lambda i, j: (i, 0)),
        out_shape=jax.ShapeDtypeStruct((n, d), F32),
        scratch_shapes=[pltpu.VMEM((tm, d), BF16), pltpu.VMEM((tm, d), F32)],
        compiler_params=_params("arbitrary", "arbitrary"),
        name="dense_ffn",
    )(x, gain.reshape(1, d), w_gate, w_up, w_down)


def _band_kernel(q_ref, kp_ref, k_ref, kn_ref, vp_ref, v_ref, vn_ref, o_prev, lse_prev, o_ref, lse_ref,
                 kwin_ref, vwin_ref, *, length, tq, half):
    del o_prev, lse_prev
    tqb = q_ref.shape[0]
    kw = tq + 2 * half
    q0 = pl.program_id(2) * tqb
    for win, parts in ((kwin_ref, (kp_ref, k_ref, kn_ref)), (vwin_ref, (vp_ref, v_ref, vn_ref))):
        win[:half] = parts[0][...]
        win[half:half + tqb] = parts[1][...]
        win[half + tqb:] = parts[2][...]
    lane = lax.broadcasted_iota(jnp.int32, (tq, LANES), 1)

    def body(t, carry):
        r0 = pl.multiple_of(t * tq, tq)
        qpos = q0 + r0 + lax.broadcasted_iota(jnp.int32, (tq, kw), 0)
        kpos = q0 + r0 - half + lax.broadcasted_iota(jnp.int32, (tq, kw), 1)
        ok = (jnp.abs(qpos - kpos) <= half) & (kpos >= 0) & (kpos < length)
        lse_tile = jnp.zeros((tq, LANES), F32)
        for h in range(q_ref.shape[1] // HEAD_DIM):
            sl = slice(h * HEAD_DIM, (h + 1) * HEAD_DIM)
            q = q_ref[pl.ds(r0, tq), sl]
            k = kwin_ref[pl.ds(r0, kw), sl]
            v = vwin_ref[pl.ds(r0, kw), sl]
            s = lax.dot_general(q, k, (((1,), (1,)), ((), ())), preferred_element_type=F32) * ATTN_SCALE
            s = jnp.where(ok, s, NEG_INF)
            m = jnp.max(s, axis=-1, keepdims=True)
            p = jnp.exp(s - m)
            den = jnp.sum(p, axis=-1, keepdims=True)
            o = jnp.dot(p.astype(BF16), v, preferred_element_type=F32) / den
            o_ref[pl.ds(r0, tq), sl] = o
            lse_tile = jnp.where(lane == h, m + jnp.log(den), lse_tile)
        lse_ref[pl.ds(r0, tq), :] = lse_tile
        return carry

    lax.fori_loop(0, tqb // tq, body, 0)


def _band_attention(qkv, o_prev, lse_prev, tok_off, batch, seq, group):
    window, dil = DIL_CONFIGS[group]
    half = window // (2 * dil)
    length = seq // dil
    assert seq % dil == 0 and length % half == 0 and tok_off % seq == 0 and qkv.shape[1] == dil
    tqb = min(BAND_Q_BLOCK, length)
    tq = min(BAND_Q_TILE, tqb)
    assert length % tqb == 0 and tqb % tq == 0 and tqb % half == 0
    width = qkv.shape[3]
    nqb = length // tqb
    hpb = tqb // half
    seq0 = tok_off // seq

    def main(which):
        return lambda b, r, t: (which, r, (seq0 + b) * nqb + t, 0)

    def halo(which, side):
        def index(b, r, t):
            first = (seq0 + b) * nqb * hpb
            blk = jnp.clip(t * hpb + (hpb if side else -1), 0, nqb * hpb - 1)
            return (which, r, first + blk, 0)
        return index

    mspec = lambda which: pl.BlockSpec((None, None, tqb, width), main(which))
    hspec = lambda which, side: pl.BlockSpec((None, None, half, width), halo(which, side))
    return pl.pallas_call(
        functools.partial(_band_kernel, length=length, tq=tq, half=half),
        grid=(batch, dil, nqb),
        in_specs=[
            mspec(0),
            hspec(1, 0), mspec(1), hspec(1, 1),
            hspec(2, 0), mspec(2), hspec(2, 1),
            pl.BlockSpec(memory_space=pl.ANY),
            pl.BlockSpec(memory_space=pl.ANY),
        ],
        out_specs=[
            pl.BlockSpec((None, tqb, width), lambda b, r, t: (r, (seq0 + b) * nqb + t, 0)),
            pl.BlockSpec((None, tqb, LANES), lambda b, r, t: (r, (seq0 + b) * nqb + t, 0)),
        ],
        out_shape=[jax.ShapeDtypeStruct(o_prev.shape, F32), jax.ShapeDtypeStruct(lse_prev.shape, F32)],
        input_output_aliases={7: 0, 8: 1},
        scratch_shapes=[pltpu.VMEM((tqb + 2 * half, width), BF16), pltpu.VMEM((tqb + 2 * half, width), BF16)],
        compiler_params=_params("arbitrary", "arbitrary", "arbitrary"),
        name="band_attention",
    )(qkv, qkv, qkv, qkv, qkv, qkv, qkv, o_prev, lse_prev)


def _merge_proj_kernel(*refs, dils):
    ng = len(dils)
    o_refs, l_refs = refs[:ng], refs[ng:2 * ng]
    w_ref, x_ref, out_ref, a_ref, os_ref, ls_ref = refs[2 * ng:]
    tm = x_ref.shape[0]

    heads = a_ref.shape[1] // HEAD_DIM
    for g, dil in enumerate(dils):
        for r in range(dil):
            ls_ref[g, pl.ds(r, tm // dil, stride=dil), :] = l_refs[g][r]
            for h in range(heads):
                os_ref[g, h, pl.ds(r, tm // dil, stride=dil), :] = o_refs[g][r, :, h * HEAD_DIM:(h + 1) * HEAD_DIM]

    lse = [ls_ref[g] for g in range(ng)]
    m = functools.reduce(jnp.maximum, lse)
    e = [jnp.exp(l - m) for l in lse]
    tot = functools.reduce(lambda p, q: p + q, e)
    wts = [eg / tot for eg in e]
    for h in range(heads):
        terms = [wts[g][:, h:h + 1] * os_ref[g, h] for g in range(ng)]
        a_ref[:, h * HEAD_DIM:(h + 1) * HEAD_DIM] = functools.reduce(lambda p, q: p + q, terms).astype(BF16)
    out_ref[...] = x_ref[...] + jnp.dot(a_ref[...], w_ref[...], preferred_element_type=F32)


def _merge_proj(outs, lses, w, x):
    n, d = x.shape
    width = w.shape[0]
    dils = tuple(o.shape[0] for o in outs)
    tm = min(512, n)
    row = lambda i: (i, 0)
    grp = lambda i: (0, i, 0)
    return pl.pallas_call(
        functools.partial(_merge_proj_kernel, dils=dils),
        grid=(n // tm,),
        in_specs=[pl.BlockSpec((dil, tm // dil, width), grp) for dil in dils]
        + [pl.BlockSpec((dil, tm // dil, LANES), grp) for dil in dils]
        + [pl.BlockSpec((width, d), lambda i: (0, 0)), pl.BlockSpec((tm, d), row)],
        out_specs=pl.BlockSpec((tm, d), row),
        out_shape=jax.ShapeDtypeStruct((n, d), F32),
        scratch_shapes=[
            pltpu.VMEM((tm, width), BF16),
            pltpu.VMEM((len(dils), width // HEAD_DIM, tm, HEAD_DIM), F32),
            pltpu.VMEM((len(dils), tm, LANES), F32),
        ],
        compiler_params=_params("arbitrary"),
        name="merge_proj",
    )(*outs, *lses, w, x)


def _store_chunk_rows(ref, x):
    rows, d = x.shape
    nc = d // LANES
    for c in range(nc):
        ref[pl.ds(c, rows, stride=nc), :] = x[:, c * LANES:(c + 1) * LANES]


def _store_packed_rows(ref, x):
    rows, d = x.shape
    nw = d // (2 * LANES)
    for c in range(nw):
        lo = x[:, (2 * c) * LANES:(2 * c + 1) * LANES].astype(BF16).astype(F32)
        hi = x[:, (2 * c + 1) * LANES:(2 * c + 2) * LANES].astype(BF16).astype(F32)
        word = (pltpu.bitcast(lo, jnp.uint32) >> 16) | (pltpu.bitcast(hi, jnp.uint32) & jnp.uint32(0xFFFF0000))
        ref[pl.ds(c, rows, stride=nw), :] = word


def _load_packed_rows(ref, lead, rows, nw, c):
    word = ref[lead, pl.ds(c, rows, stride=nw), :]
    lo = pltpu.bitcast(word << 16, F32)
    hi = pltpu.bitcast(word & jnp.uint32(0xFFFF0000), F32)
    return lo.astype(BF16), hi.astype(BF16)


def _router_kernel(x_ref, gain_ref, wr_ref, h_ref, idx_ref, gate_ref):
    h = _rms_rows(x_ref[...], gain_ref[...])
    _store_packed_rows(h_ref, h)
    logits = jnp.dot(h, wr_ref[...], preferred_element_type=F32, precision=lax.Precision.HIGHEST)
    lane = lax.broadcasted_iota(jnp.int32, logits.shape, 1)
    lane_f = lane.astype(F32)
    logits = jnp.where(lane < N_EXPERTS, logits, -jnp.inf)
    m1 = jnp.max(logits, axis=-1, keepdims=True)
    i1 = jnp.min(jnp.where(logits == m1, lane_f, float(LANES)), axis=-1, keepdims=True)
    rest = jnp.where(lane_f == i1, -jnp.inf, logits)
    m2 = jnp.max(rest, axis=-1, keepdims=True)
    i2 = jnp.min(jnp.where(rest == m2, lane_f, float(LANES)), axis=-1, keepdims=True)
    e = jnp.exp(m2 - m1)
    tot = 1.0 + e
    idx_ref[...] = jnp.where(lane == 0, i1, jnp.where(lane == 1, i2, 0.0)).astype(jnp.int32)
    gate_ref[...] = jnp.where(lane == 0, 1.0 / tot, jnp.where(lane == 1, e / tot, 0.0))


def _router(x, gain, w_router):
    n, d = x.shape
    tm = min(512, n)
    wr = jnp.zeros((d, LANES), F32).at[:, :N_EXPERTS].set(w_router.astype(F32))
    row = lambda i: (i, 0)
    return pl.pallas_call(
        _router_kernel,
        grid=(n // tm,),
        in_specs=[
            pl.BlockSpec((tm, d), row),
            pl.BlockSpec((1, d), lambda i: (0, 0)),
            pl.BlockSpec((d, LANES), lambda i: (0, 0)),
        ],
        out_specs=[pl.BlockSpec((tm * (d // (2 * LANES)), LANES), row), pl.BlockSpec((tm, LANES), row),
                   pl.BlockSpec((tm, LANES), row)],
        out_shape=[
            jax.ShapeDtypeStruct((n * (d // (2 * LANES)), LANES), jnp.uint32),
            jax.ShapeDtypeStruct((n, LANES), jnp.int32),
            jax.ShapeDtypeStruct((n, LANES), F32),
        ],
        compiler_params=_params("arbitrary"),
        name="router",
    )(x, gain.reshape(1, d), wr)


def _expert_kernel(bexp_ref, nused_ref, rtok_ref, h_hbm, wg_ref, wu_ref, wd_ref, y_ref,
                   xs_ref, xb_ref, acc_ref, sem):
    del bexp_ref
    i = pl.program_id(0)
    j = pl.program_id(1)
    tmb, d = xb_ref.shape
    nw = d // (2 * LANES)
    n_used = nused_ref[0]

    def row_copy(tok, slot, r):
        return pltpu.make_async_copy(h_hbm.at[pl.ds(pl.multiple_of(tok * nw, nw), nw), :],
                                     xs_ref.at[slot, pl.ds(pl.multiple_of(r * nw, nw), nw), :], sem.at[slot])

    def start_gather(blk, slot):
        def issue(r, carry):
            row_copy(rtok_ref[blk * tmb + r], slot, r).start()
            return carry
        lax.fori_loop(0, tmb, issue, 0, unroll=8)

    def wait_gather(slot):
        pltpu.make_async_copy(h_hbm.at[pl.ds(0, tmb * nw), :], xs_ref.at[slot], sem.at[slot]).wait()

    @pl.when(i < n_used)
    def _():
        slot = i % 2

        @pl.when(j == 0)
        def _():
            @pl.when(i == 0)
            def _():
                start_gather(0, 0)

            wait_gather(slot)

            @pl.when(i + 1 < n_used)
            def _():
                start_gather(i + 1, 1 - slot)

            for c in range(nw):
                lo, hi = _load_packed_rows(xs_ref, slot, tmb, nw, c)
                xb_ref[:, (2 * c) * LANES:(2 * c + 1) * LANES] = lo
                xb_ref[:, (2 * c + 1) * LANES:(2 * c + 2) * LANES] = hi
            acc_ref[...] = jnp.zeros_like(acc_ref)

        x = xb_ref[...]
        g = jnp.dot(x, wg_ref[0], preferred_element_type=F32)
        u = jnp.dot(x, wu_ref[0], preferred_element_type=F32)
        acc_ref[...] += jnp.dot(_silu_mul(g, u).astype(BF16), wd_ref[0], preferred_element_type=F32)

    last = j == pl.num_programs(1) - 1

    @pl.when(last & (i < n_used))
    def _():
        _store_chunk_rows(y_ref, acc_ref[...])

    @pl.when(last & (i >= n_used))
    def _():
        y_ref[...] = jnp.zeros_like(y_ref)


def _expert_ffn(h_rows, block_expert, n_used, row_tok, w_gate, w_up, w_down):
    d = w_gate.shape[1]
    f = w_gate.shape[2]
    tmb = MOE_BLOCK
    tf = MOE_F_BLOCK
    n_blocks = block_expert.shape[0]
    nf = f // tf
    assert f % tf == 0

    def fblk(i, j, nu):
        return jnp.where(i < nu[0], j, nf - 1)

    grid_spec = pltpu.PrefetchScalarGridSpec(
        num_scalar_prefetch=3,
        grid=(n_blocks, nf),
        in_specs=[
            pl.BlockSpec(memory_space=pl.ANY),
            pl.BlockSpec((1, d, tf), lambda i, j, be, nu, rt: (be[i], 0, fblk(i, j, nu))),
            pl.BlockSpec((1, d, tf), lambda i, j, be, nu, rt: (be[i], 0, fblk(i, j, nu))),
            pl.BlockSpec((1, tf, d), lambda i, j, be, nu, rt: (be[i], fblk(i, j, nu), 0)),
        ],
        out_specs=pl.BlockSpec((tmb * (d // LANES), LANES), lambda i, j, be, nu, rt: (i, 0)),
        scratch_shapes=[
            pltpu.VMEM((2, tmb * (d // (2 * LANES)), LANES), jnp.uint32),
            pltpu.VMEM((tmb, d), BF16),
            pltpu.VMEM((tmb, d), F32),
            pltpu.SemaphoreType.DMA((2,)),
        ],
    )
    return pl.pallas_call(
        _expert_kernel,
        grid_spec=grid_spec,
        out_shape=jax.ShapeDtypeStruct((n_blocks * tmb * (d // LANES), LANES), F32),
        compiler_params=_params("arbitrary", "arbitrary"),
        name="expert_ffn",
    )(block_expert, n_used, row_tok, h_rows, w_gate, w_up, w_down)


def _combine_kernel(dest_ref, ys_hbm, x_ref, gate_ref, oa_ref, ob_ref, buf_ref, sem, *, first_blocks):
    i = pl.program_id(0)
    nsteps = pl.num_programs(0)
    tc, d = x_ref.shape
    nc = d // LANES
    pitch = buf_ref.shape[1] // (2 * tc)

    def row_copy(row, slot, dst):
        return pltpu.make_async_copy(ys_hbm.at[pl.ds(pl.multiple_of(row * nc, nc), nc), :],
                                     buf_ref.at[slot, pl.ds(pl.multiple_of(dst * pitch, 8), nc), :], sem.at[slot])

    def start_gather(step, slot):
        def issue(r, carry):
            row_copy(dest_ref[step * (2 * tc) + r], slot, r).start()
            return carry
        lax.fori_loop(0, 2 * tc, issue, 0, unroll=8)

    def wait_gather(slot):
        pltpu.make_async_copy(ys_hbm.at[pl.ds(0, 2 * tc * nc), :], buf_ref.at[slot, pl.ds(0, 2 * tc * nc), :],
                              sem.at[slot]).wait()

    slot = i % 2

    @pl.when(i == 0)
    def _():
        start_gather(0, 0)

    wait_gather(slot)

    @pl.when(i + 1 < nsteps)
    def _():
        start_gather(i + 1, 1 - slot)

    g0 = gate_ref[:, 0:1]
    g1 = gate_ref[:, 1:2]

    def emit(o_ref):
        for c in range(nc):
            sl = slice(c * LANES, (c + 1) * LANES)
            y0 = buf_ref[slot, pl.ds(c, tc, stride=pitch), :]
            y1 = buf_ref[slot, pl.ds(tc * pitch + c, tc, stride=pitch), :]
            o_ref[:, sl] = x_ref[:, sl] + (y0 * g0 + y1 * g1)
    _pick_rows(i, first_blocks, oa_ref, ob_ref, emit)


def _combine(x, ys, dest, gate, n_first):
    n, d = x.shape
    tc = min(COMBINE_BLOCK, n_first, n - n_first)
    assert n_first % tc == 0 and (n - n_first) % tc == 0
    first_blocks = n_first // tc
    nc = d // LANES
    pitch = nc if (nc // 8) % 2 else nc + 8
    dest = dest.reshape(n // tc, tc, TOP_K).transpose(0, 2, 1).reshape(-1)
    grid_spec = pltpu.PrefetchScalarGridSpec(
        num_scalar_prefetch=1,
        grid=(n // tc,),
        in_specs=[
            pl.BlockSpec(memory_space=pl.ANY),
            pl.BlockSpec((tc, d), lambda i, dr: (i, 0)),
            pl.BlockSpec((tc, LANES), lambda i, dr: (i, 0)),
        ],
        out_specs=_two_batch_specs(tc, d, first_blocks),
        scratch_shapes=[pltpu.VMEM((2, 2 * tc * pitch, LANES), F32), pltpu.SemaphoreType.DMA((2,))],
    )
    return pl.pallas_call(
        functools.partial(_combine_kernel, first_blocks=first_blocks),
        grid_spec=grid_spec,
        out_shape=[jax.ShapeDtypeStruct((n_first, d), F32), jax.ShapeDtypeStruct((n - n_first, d), F32)],
        compiler_params=_params("arbitrary"),
        name="moe_combine",
    )(dest, ys, x, gate)


def _routing_tables(idx, tmb):
    n = idx.shape[0]
    nk = n * TOP_K
    e_flat = idx[:, :TOP_K].reshape(-1)
    onehot = (e_flat[:, None] == jnp.arange(N_EXPERTS, dtype=jnp.int32)[None, :]).astype(jnp.int32)
    csum = jnp.cumsum(onehot, axis=0)
    rank = jnp.sum(onehot * csum, axis=1) - 1
    counts = csum[-1]
    padded = (counts + tmb - 1) // tmb * tmb
    pad_end = jnp.cumsum(padded)
    pad_start = pad_end - padded
    dest = (jnp.sum(onehot * pad_start[None, :], axis=1) + rank).astype(jnp.int32)
    n_blocks = -(-nk // tmb) + N_EXPERTS
    block_expert = jnp.minimum(
        jnp.searchsorted(pad_end, jnp.arange(n_blocks, dtype=jnp.int32) * tmb, side="right"), N_EXPERTS - 1
    ).astype(jnp.int32)
    n_used = (pad_end[-1] // tmb).astype(jnp.int32).reshape(1)
    row_tok = jnp.zeros((n_blocks * tmb,), jnp.int32).at[dest].set(jnp.arange(nk, dtype=jnp.int32) // TOP_K)
    return dest, block_expert, n_used, row_tok


def _rope_tables(seq_len):
    half = HEAD_DIM // 2
    inv_freq = ROPE_THETA ** (-jnp.arange(half, dtype=F32) / half)
    ang = jnp.arange(seq_len, dtype=F32)[:, None] * inv_freq[None, :]
    cos, sin = jnp.cos(ang), jnp.sin(ang)
    return jnp.concatenate([cos, cos], axis=-1), jnp.concatenate([-sin, sin], axis=-1)


def kernel(x_prompt, x_sample, norm_mix, norm_ffn, na_w_qkv, na_q_gain, na_k_gain, na_rpb, na_w_o,
           da_w_qkv, da_q_gain, da_k_gain, da_w_o, ffn_w_gate, ffn_w_up, ffn_w_down,
           moe_w_router, moe_w_gate, moe_w_up, moe_w_down):
    d = x_prompt.shape[-1]
    trunks = []
    off = 0
    for xin in (x_prompt, x_sample):
        b, s, _ = xin.shape
        trunks.append((off, b, s))
        off += b * s
    n = off
    xa, xb = x_prompt.reshape(-1, d), x_sample.reshape(-1, d)
    depth = norm_mix.shape[0]
    assert depth == 2, "layer 0 reads the two request batches directly and layer 1 writes them separately"
    x = None

    for layer in range(depth):
        lj = layer // 2
        if layer % 2 == 0:
            heads = na_w_qkv.shape[2] // (3 * HEAD_DIM)
            tn = 1024
            per = heads * HEAD_DIM // tn
            hg = jnp.concatenate([
                jnp.broadcast_to(na_q_gain[lj], (per, HEAD_DIM)),
                jnp.broadcast_to(na_k_gain[lj], (per, HEAD_DIM)),
                jnp.ones((per, HEAD_DIM), F32),
            ]).reshape(3 * per, 1, HEAD_DIM)
            qkv = _qkv_project(xa, xb, norm_mix[layer], na_w_qkv[lj].astype(BF16), hg, 3 * per, 2 * per)
            tables = _na_bias_tables(na_rpb[lj])
            att = jnp.zeros((n, heads * HEAD_DIM), BF16)
            for (toff, b, s) in trunks:
                att = _na_attention(qkv, tables, att, toff, b, s, heads)
            x = _proj_residual(att, na_w_o[lj].astype(BF16), xa, xb)
            x = _dense_ffn(x, norm_ffn[layer], ffn_w_gate[lj].astype(BF16), ffn_w_up[lj].astype(BF16),
                           ffn_w_down[lj].astype(BF16))
        else:
            groups = len(DIL_CONFIGS)
            heads = da_w_qkv.shape[2] // (groups * 3 * HEAD_DIM)
            dils = [dil for _, dil in DIL_CONFIGS]
            ones = jnp.ones((HEAD_DIM,), F32)
            hg = jnp.stack([t for g in range(groups) for t in (da_q_gain[lj, g], da_k_gain[lj, g], ones)])
            hg = hg.reshape(groups * 3, 1, HEAD_DIM)
            (o0, b0, s0), (o1, b1, s1) = trunks
            tq = min(QKV_DIL_TOKEN_BLOCK, n)
            assert s0 % tq == 0 and s1 % tq == 0 and o1 % tq == 0
            nb0, p0, p1 = o1 // tq, s0 // tq, s1 // tq
            pos_map = lambda i: jnp.where(i < nb0, i % p0, (i - nb0) % p1)
            smax = max(s0, s1)
            rope = [jnp.stack([t.reshape(smax // tq, tq // dil, dil, HEAD_DIM).transpose(0, 2, 1, 3)
                               .reshape(smax, HEAD_DIM) for dil in dils]) for t in _rope_tables(smax)]
            qkvs = _qkv_project_dilated(x, norm_mix[layer], da_w_qkv[lj].astype(BF16), hg, rope, pos_map,
                                        dils, heads, tq)
            outs, lses = [], []
            for g, dil in enumerate(dils):
                o_g = jnp.zeros((dil, n // dil, heads * HEAD_DIM), F32)
                lse_g = jnp.zeros((dil, n // dil, LANES), F32)
                for (toff, b, s) in trunks:
                    o_g, lse_g = _band_attention(qkvs[g], o_g, lse_g, toff, b, s, g)
                outs.append(o_g)
                lses.append(lse_g)
            x = _merge_proj(outs, lses, da_w_o[lj].astype(BF16), x)
            h, idx, gate = _router(x, norm_ffn[layer], moe_w_router[lj])
            dest, block_expert, n_used, row_tok = _routing_tables(idx, MOE_BLOCK)
            ys = _expert_ffn(h, block_expert, n_used, row_tok, moe_w_gate[lj].astype(BF16),
                             moe_w_up[lj].astype(BF16), moe_w_down[lj].astype(BF16))
            ya, yb = _combine(x, ys, dest, gate, xa.shape[0])

    return ya.reshape(x_prompt.shape), yb.reshape(x_sample.shape)
```

```python
import functools

import jax
import jax.numpy as jnp
from jax import lax
from jax.experimental import pallas as pl
from jax.experimental.pallas import tpu as pltpu

F32 = jnp.float32
BF16 = jnp.bfloat16

HEAD_DIM = 128
GRID_W = 64
NA_ROWS = 8
NA_COLS = 16
NA_GROUP = NA_ROWS // 2
NA_KEY_ROWS = 3 * NA_GROUP
NA_HEAD_BLOCK = 8
DIL_CONFIGS = ((128, 1), (512, 4), (2048, 16))
N_EXPERTS = 8
TOP_K = 2
ROPE_THETA = 10000.0
EPS = 1e-6
NEG_INF = -1e30
ATTN_SCALE = HEAD_DIM ** -0.5

LANES = 128
MXU_COLS = 256
VMEM_LIMIT = 52 * 1024 * 1024

TOKEN_BLOCK = 1024
QKV_DIL_TOKEN_BLOCK = 512
FFN_TOKEN_BLOCK = 512
FFN_F_BLOCK = 512
BAND_Q_BLOCK = 512
BAND_Q_TILE = 128
MOE_BLOCK = 512
MOE_F_BLOCK = 1024
COMBINE_BLOCK = 256


def _params(*sem):
    return pltpu.CompilerParams(dimension_semantics=sem, vmem_limit_bytes=VMEM_LIMIT)


def _rms_rows(x, gain):
    ms = jnp.mean(x * x, axis=-1, keepdims=True)
    return x * lax.rsqrt(ms + EPS) * gain


def _pick_rows(i, first_blocks, xa_ref, xb_ref, use):
    @pl.when(i < first_blocks)
    def _():
        use(xa_ref)

    @pl.when(i >= first_blocks)
    def _():
        use(xb_ref)


def _two_batch_specs(tm, d, first_blocks, **kw):
    return [pl.BlockSpec((tm, d), lambda i, *_: (jnp.minimum(i, first_blocks - 1), 0), **kw),
            pl.BlockSpec((tm, d), lambda i, *_: (jnp.maximum(i - first_blocks, 0), 0), **kw)]


def _qkv_kernel(xa_ref, xb_ref, gain_ref, w_ref, hg_ref, o_ref, h_ref, *, first_blocks, v_every, v_from):
    j = pl.program_id(1)

    @pl.when(j == 0)
    def _():
        def norm(x_ref):
            h_ref[...] = _rms_rows(x_ref[...], gain_ref[...]).astype(BF16)
        _pick_rows(pl.program_id(0), first_blocks, xa_ref, xb_ref, norm)

    is_qk = (j % v_every) < v_from
    for c in range(w_ref.shape[1] // MXU_COLS):
        acc = jnp.dot(h_ref[...], w_ref[:, c * MXU_COLS:(c + 1) * MXU_COLS], preferred_element_type=F32)
        for h in range(MXU_COLS // HEAD_DIM):
            seg = acc[:, h * HEAD_DIM:(h + 1) * HEAD_DIM]
            lo = c * MXU_COLS + h * HEAD_DIM
            o_ref[:, lo:lo + HEAD_DIM] = jnp.where(is_qk, _rms_rows(seg, hg_ref[0]), seg).astype(o_ref.dtype)


def _qkv_project(xa, xb, gain, w, head_gain, v_every, v_from):
    d = xa.shape[1]
    n = xa.shape[0] + xb.shape[0]
    n_out = w.shape[1]
    tm = min(TOKEN_BLOCK, xa.shape[0], xb.shape[0])
    assert xa.shape[0] % tm == 0 and xb.shape[0] % tm == 0
    first_blocks = xa.shape[0] // tm
    tn = n_out // head_gain.shape[0]
    return pl.pallas_call(
        functools.partial(_qkv_kernel, first_blocks=first_blocks, v_every=v_every, v_from=v_from),
        grid=(n // tm, n_out // tn),
        in_specs=_two_batch_specs(tm, d, first_blocks, pipeline_mode=pl.Buffered(1)) + [
            pl.BlockSpec((1, d), lambda i, j: (0, 0)),
            pl.BlockSpec((d, tn), lambda i, j: (0, j)),
            pl.BlockSpec((1, 1, HEAD_DIM), lambda i, j: (j, 0, 0)),
        ],
        out_specs=pl.BlockSpec((tm, tn), lambda i, j: (i, j)),
        out_shape=jax.ShapeDtypeStruct((n, n_out), BF16),
        scratch_shapes=[pltpu.VMEM((tm, d), BF16)],
        compiler_params=_params("arbitrary", "arbitrary"),
        name="qkv_project",
    )(xa, xb, gain.reshape(1, d), w, head_gain)


def _qkv_dil_kernel(x_ref, gain_ref, w_ref, hg_ref, cos_ref, sin_ref, *rest, dils):
    out_refs, (h_ref, stage_ref) = rest[:len(dils)], rest[len(dils):]
    j = pl.program_id(1)
    tm, d = x_ref.shape
    tn = w_ref.shape[1]

    @pl.when(j == 0)
    def _():
        y = _rms_rows(x_ref[...], gain_ref[...])
        for c in range(d // LANES):
            stage_ref[c] = y[:, c * LANES:(c + 1) * LANES]
        for g, dil in enumerate(dils):
            rows = tm // dil
            for r in range(dil):
                for c in range(d // LANES):
                    h_ref[g, r * rows:(r + 1) * rows, c * LANES:(c + 1) * LANES] = (
                        stage_ref[c, pl.ds(r, rows, stride=dil), :].astype(BF16))

    group = j // 3
    which = j % 3
    acc = jnp.dot(h_ref[group], w_ref[...], preferred_element_type=F32)

    for g, (dil, o_ref) in enumerate(zip(dils, out_refs)):
        rows = tm // dil

        @pl.when((group == g) & (which < 2))
        def _(dil=dil, o_ref=o_ref, rows=rows):
            for h in range(tn // HEAD_DIM):
                sl = slice(h * HEAD_DIM, (h + 1) * HEAD_DIM)
                y = _rms_rows(acc[:, sl], hg_ref[0])
                y = (y * cos_ref[...] + pltpu.roll(y, HEAD_DIM // 2, 1) * sin_ref[...]).astype(o_ref.dtype)
                for r in range(dil):
                    o_ref[r, :, sl] = y[r * rows:(r + 1) * rows]

        @pl.when((group == g) & (which == 2))
        def _(dil=dil, o_ref=o_ref, rows=rows):
            v = acc.astype(o_ref.dtype)
            for r in range(dil):
                o_ref[r] = v[r * rows:(r + 1) * rows]


def _qkv_project_dilated(x, gain, w, head_gain, rope_tabs, pos_map, dils, heads, tm):
    n, d = x.shape
    n_out = w.shape[1]
    width = heads * HEAD_DIM
    ng = len(dils)
    assert n_out == ng * 3 * width and head_gain.shape[0] == ng * 3

    def out_map(g):
        def index(i, j):
            return (jnp.clip(j - 3 * g, 0, 2), 0, i, 0)
        return index

    rope_spec = pl.BlockSpec((None, tm, HEAD_DIM), lambda i, j: (j // 3, pos_map(i), 0))
    return pl.pallas_call(
        functools.partial(_qkv_dil_kernel, dils=tuple(dils)),
        grid=(n // tm, ng * 3),
        in_specs=[
            pl.BlockSpec((tm, d), lambda i, j: (i, 0)),
            pl.BlockSpec((1, d), lambda i, j: (0, 0)),
            pl.BlockSpec((d, width), lambda i, j: (0, j)),
            pl.BlockSpec((1, 1, HEAD_DIM), lambda i, j: (j, 0, 0)),
            rope_spec,
            rope_spec,
        ],
        out_specs=[pl.BlockSpec((None, dil, tm // dil, width), out_map(g)) for g, dil in enumerate(dils)],
        out_shape=[jax.ShapeDtypeStruct((3, dil, n // dil, width), BF16) for dil in dils],
        scratch_shapes=[pltpu.VMEM((ng, tm, d), BF16), pltpu.VMEM((d // LANES, tm, LANES), F32)],
        compiler_params=_params("arbitrary", "arbitrary"),
        name="qkv_project_dilated",
    )(x, gain.reshape(1, d), w, head_gain, *rope_tabs)


def _na_bias_tables(rpb):
    half = NA_ROWS // 2
    assert NA_GROUP == half
    qi = jnp.arange(NA_GROUP)[:, None]
    kj = jnp.arange(NA_KEY_ROWS)[None, :]
    dr = jnp.broadcast_to(kj - NA_GROUP - qi, (3, NA_GROUP, NA_KEY_ROWS))
    rvalid = jnp.stack([
        jnp.broadcast_to(kj >= NA_GROUP, (NA_GROUP, NA_KEY_ROWS)),
        (kj - qi >= 0) & (kj - qi < NA_ROWS),
        jnp.broadcast_to(kj < NA_ROWS, (NA_GROUP, NA_KEY_ROWS)),
    ])
    c = jnp.arange(GRID_W)
    cs = jnp.clip(c - NA_COLS // 2, 0, GRID_W - NA_COLS)
    cvalid = (c[None, :] >= cs[:, None]) & (c[None, :] < cs[:, None] + NA_COLS)
    dc = jnp.clip(c[None, :] - c[:, None], 1 - NA_COLS, NA_COLS - 1)
    dri = jnp.clip(dr, 1 - NA_ROWS, NA_ROWS - 1) + NA_ROWS - 1
    sel_r = jax.nn.one_hot(dri, 2 * NA_ROWS - 1, dtype=F32)
    sel_c = jax.nn.one_hot(dc + NA_COLS - 1, 2 * NA_COLS - 1, dtype=F32)
    hp = lax.Precision.HIGHEST
    tmp = jnp.einsum("hab,qcb->haqc", rpb.astype(F32), sel_c, precision=hp)
    bias = jnp.einsum("haqc,vgka->vhgqkc", tmp, sel_r, precision=hp)
    valid = rvalid[:, None, :, None, :, None] & cvalid[None, None, None, :, None, :]
    bias = jnp.where(valid, bias, NEG_INF)
    h = rpb.shape[0]
    return bias.reshape(3, h, NA_GROUP * GRID_W, NA_KEY_ROWS * GRID_W)


def _na_kernel(q_ref, kp_ref, k_ref, kn_ref, vp_ref, v_ref, vn_ref, tab_ref, o_ref):
    gq = q_ref.shape[0]
    nt = (((1,), (1,)), ((), ()))
    for h in range(q_ref.shape[1] // HEAD_DIM):
        sl = slice(h * HEAD_DIM, (h + 1) * HEAD_DIM)
        q = q_ref[:, sl]
        s = jnp.concatenate(
            [lax.dot_general(q, kr[:, sl], nt, preferred_element_type=F32) for kr in (kp_ref, k_ref, kn_ref)],
            axis=1)
        s = s * ATTN_SCALE + tab_ref[0, h]
        m = jnp.max(s, axis=-1, keepdims=True)
        p = jnp.exp(s - m)
        den = jnp.sum(p, axis=-1, keepdims=True)
        pb = p.astype(BF16)
        o = (jnp.dot(pb[:, :gq], vp_ref[:, sl], preferred_element_type=F32)
             + jnp.dot(pb[:, gq:2 * gq], v_ref[:, sl], preferred_element_type=F32)
             + jnp.dot(pb[:, 2 * gq:], vn_ref[:, sl], preferred_element_type=F32))
        o_ref[:, sl] = (o / den).astype(o_ref.dtype)


def _two_batch_position(t, first_blocks, per_seq_a, per_seq_b):
    in_a = t < first_blocks
    per_seq = jnp.where(in_a, per_seq_a, per_seq_b)
    return jnp.where(in_a, t, t - first_blocks) % per_seq, per_seq


def _na_attention(qkv, tables, trunks, heads):
    n = qkv.shape[0]
    gq = NA_GROUP * GRID_W
    hb = NA_HEAD_BLOCK
    hblocks = heads // hb
    (off_a, b_a, s_a), (off_b, b_b, s_b) = trunks
    for (_, _, seq) in trunks:
        assert seq % gq == 0 and seq // GRID_W >= NA_ROWS
    assert heads % hb == 0 and off_a == 0 and off_b == b_a * s_a
    first_groups, gps_a, gps_b = off_b // gq, s_a // gq, s_b // gq

    def place(g):
        pos, per_seq = _two_batch_position(g, first_groups, gps_a, gps_b)
        return pos == 0, pos == per_seq - 1

    def variant(g):
        first, last = place(g)
        return jnp.where(first, 0, jnp.where(last, 2, 1))

    def rows_of(which, shift):
        def index(h, g):
            first, last = place(g)
            gg = g + jnp.where((shift < 0) & first, 0, jnp.where((shift > 0) & last, 0, shift))
            return (gg, which * hblocks + h)
        return index

    blk = lambda index: pl.BlockSpec((gq, hb * HEAD_DIM), index)
    return pl.pallas_call(
        _na_kernel,
        grid=(hblocks, n // gq),
        in_specs=[
            blk(rows_of(0, 0)),
            blk(rows_of(1, -1)), blk(rows_of(1, 0)), blk(rows_of(1, 1)),
            blk(rows_of(2, -1)), blk(rows_of(2, 0)), blk(rows_of(2, 1)),
            pl.BlockSpec((1, hb, gq, NA_KEY_ROWS * GRID_W), lambda h, g: (variant(g), h, 0, 0)),
        ],
        out_specs=blk(rows_of(0, 0)),
        out_shape=jax.ShapeDtypeStruct((n, heads * HEAD_DIM), BF16),
        compiler_params=_params("arbitrary", "arbitrary"),
        name="na_attention",
    )(qkv, qkv, qkv, qkv, qkv, qkv, qkv, tables)


def _proj_residual_kernel(a_ref, w_ref, xa_ref, xb_ref, o_ref, *, first_blocks):
    y = jnp.dot(a_ref[...], w_ref[...], preferred_element_type=F32)

    def add(x_ref):
        o_ref[...] = x_ref[...] + y
    _pick_rows(pl.program_id(1), first_blocks, xa_ref, xb_ref, add)


def _proj_residual(a, w, xa, xb):
    n, k = a.shape
    d = w.shape[1]
    tm = min(TOKEN_BLOCK, xa.shape[0], xb.shape[0])
    tn = min(1024, d)
    assert xa.shape[0] % tm == 0 and xb.shape[0] % tm == 0 and xa.shape[0] + xb.shape[0] == n
    first_blocks = xa.shape[0] // tm
    return pl.pallas_call(
        functools.partial(_proj_residual_kernel, first_blocks=first_blocks),
        grid=(d // tn, n // tm),
        in_specs=[
            pl.BlockSpec((tm, k), lambda j, i: (i, 0)),
            pl.BlockSpec((k, tn), lambda j, i: (0, j)),
            pl.BlockSpec((tm, tn), lambda j, i: (jnp.minimum(i, first_blocks - 1), j)),
            pl.BlockSpec((tm, tn), lambda j, i: (jnp.maximum(i - first_blocks, 0), j)),
        ],
        out_specs=pl.BlockSpec((tm, tn), lambda j, i: (i, j)),
        out_shape=jax.ShapeDtypeStruct((n, d), F32),
        compiler_params=_params("arbitrary", "arbitrary"),
        name="proj_residual",
    )(a, w, xa, xb)


def _silu_mul(g, u):
    return g * (1.0 / (1.0 + jnp.exp(-g))) * u


def _ffn_kernel(x_ref, gain_ref, wg_ref, wu_ref, wd_ref, o_ref, h_ref, acc_ref):
    j = pl.program_id(1)

    @pl.when(j == 0)
    def _():
        h_ref[...] = _rms_rows(x_ref[...], gain_ref[...]).astype(BF16)
        acc_ref[...] = jnp.zeros_like(acc_ref)

    h = h_ref[...]
    g = jnp.dot(h, wg_ref[...], preferred_element_type=F32)
    u = jnp.dot(h, wu_ref[...], preferred_element_type=F32)
    acc_ref[...] += jnp.dot(_silu_mul(g, u).astype(BF16), wd_ref[...], preferred_element_type=F32)

    @pl.when(j == pl.num_programs(1) - 1)
    def _():
        o_ref[...] = x_ref[...] + acc_ref[...]


def _dense_ffn(x, gain, w_gate, w_up, w_down):
    n, d = x.shape
    f = w_gate.shape[1]
    tm = min(FFN_TOKEN_BLOCK, n)
    tf = FFN_F_BLOCK
    assert f % tf == 0
    return pl.pallas_call(
        _ffn_kernel,
        grid=(n // tm, f // tf),
        in_specs=[
            pl.BlockSpec((tm, d), lambda i, j: (i, 0)),
            pl.BlockSpec((1, d), lambda i, j: (0, 0)),
            pl.BlockSpec((d, tf), lambda i, j: (0, j)),
            pl.BlockSpec((d, tf), lambda i, j: (0, j)),
            pl.BlockSpec((tf, d), lambda i, j: (j, 0)),
        ],
        out_specs=pl.BlockSpec((tm, d), lambda i, j: (i, 0)),
        out_shape=jax.ShapeDtypeStruct((n, d), F32),
        scratch_shapes=[pltpu.VMEM((tm, d), BF16), pltpu.VMEM((tm, d), F32)],
        compiler_params=_params("arbitrary", "arbitrary"),
        name="dense_ffn",
    )(x, gain.reshape(1, d), w_gate, w_up, w_down)


def _band_kernel(q_ref, kp_ref, k_ref, kn_ref, vp_ref, v_ref, vn_ref, o_ref, lse_ref, kwin_ref, vwin_ref, *,
                 first_blocks, per_seq_a, per_seq_b, tq, half):
    tqb = q_ref.shape[0]
    kw = tq + 2 * half
    pos, per_seq = _two_batch_position(pl.program_id(1), first_blocks, per_seq_a, per_seq_b)
    q0 = pos * tqb
    length = per_seq * tqb
    for win, parts in ((kwin_ref, (kp_ref, k_ref, kn_ref)), (vwin_ref, (vp_ref, v_ref, vn_ref))):
        win[:half] = parts[0][...]
        win[half:half + tqb] = parts[1][...]
        win[half + tqb:] = parts[2][...]
    lane = lax.broadcasted_iota(jnp.int32, (tq, LANES), 1)

    def body(t, carry):
        r0 = pl.multiple_of(t * tq, tq)
        qpos = q0 + r0 + lax.broadcasted_iota(jnp.int32, (tq, kw), 0)
        kpos = q0 + r0 - half + lax.broadcasted_iota(jnp.int32, (tq, kw), 1)
        ok = (jnp.abs(qpos - kpos) <= half) & (kpos >= 0) & (kpos < length)
        lse_tile = jnp.zeros((tq, LANES), F32)
        for h in range(q_ref.shape[1] // HEAD_DIM):
            sl = slice(h * HEAD_DIM, (h + 1) * HEAD_DIM)
            q = q_ref[pl.ds(r0, tq), sl]
            k = kwin_ref[pl.ds(r0, kw), sl]
            v = vwin_ref[pl.ds(r0, kw), sl]
            s = lax.dot_general(q, k, (((1,), (1,)), ((), ())), preferred_element_type=F32) * ATTN_SCALE
            s = jnp.where(ok, s, NEG_INF)
            m = jnp.max(s, axis=-1, keepdims=True)
            p = jnp.exp(s - m)
            den = jnp.sum(p, axis=-1, keepdims=True)
            o = jnp.dot(p.astype(BF16), v, preferred_element_type=F32) / den
            o_ref[pl.ds(r0, tq), sl] = o
            lse_tile = jnp.where(lane == h, m + jnp.log(den), lse_tile)
        lse_ref[pl.ds(r0, tq), :] = lse_tile
        return carry

    lax.fori_loop(0, tqb // tq, body, 0)


def _band_attention(qkv, trunks, group):
    window, dil = DIL_CONFIGS[group]
    half = window // (2 * dil)
    _, _, rows, width = qkv.shape
    (off_a, b_a, s_a), (off_b, b_b, s_b) = trunks
    len_a, len_b = s_a // dil, s_b // dil
    assert s_a % dil == 0 and s_b % dil == 0 and qkv.shape[1] == dil and off_a == 0 and off_b == b_a * s_a
    tqb = min(BAND_Q_BLOCK, len_a, len_b)
    tq = min(BAND_Q_TILE, tqb)
    assert len_a % tqb == 0 and len_b % tqb == 0 and tqb % tq == 0 and tqb % half == 0
    hpb = tqb // half
    first_blocks, per_seq_a, per_seq_b = off_b // dil // tqb, len_a // tqb, len_b // tqb

    def halo(which, side):
        def index(r, t):
            pos, per_seq = _two_batch_position(t, first_blocks, per_seq_a, per_seq_b)
            edge = (pos == per_seq - 1) if side else (pos == 0)
            blk = t * hpb + (jnp.where(edge, hpb - 1, hpb) if side else jnp.where(edge, 0, -1))
            return (which, r, blk, 0)
        return index

    mspec = lambda which: pl.BlockSpec((None, None, tqb, width), lambda r, t: (which, r, t, 0))
    hspec = lambda which, side: pl.BlockSpec((None, None, half, width), halo(which, side))
    return pl.pallas_call(
        functools.partial(_band_kernel, first_blocks=first_blocks, per_seq_a=per_seq_a, per_seq_b=per_seq_b,
                          tq=tq, half=half),
        grid=(dil, rows // tqb),
        in_specs=[
            mspec(0),
            hspec(1, 0), mspec(1), hspec(1, 1),
            hspec(2, 0), mspec(2), hspec(2, 1),
        ],
        out_specs=[
            pl.BlockSpec((None, tqb, width), lambda r, t: (r, t, 0)),
            pl.BlockSpec((None, tqb, LANES), lambda r, t: (r, t, 0)),
        ],
        out_shape=[jax.ShapeDtypeStruct((dil, rows, width), F32), jax.ShapeDtypeStruct((dil, rows, LANES), F32)],
        scratch_shapes=[pltpu.VMEM((tqb + 2 * half, width), BF16), pltpu.VMEM((tqb + 2 * half, width), BF16)],
        compiler_params=_params("arbitrary", "arbitrary"),
        name="band_attention",
    )(qkv, qkv, qkv, qkv, qkv, qkv, qkv)


def _merge_proj_kernel(*refs, dils):
    ng = len(dils)
    o_refs, l_refs = refs[:ng], refs[ng:2 * ng]
    w_ref, x_ref, out_ref, a_ref, os_ref, ls_ref = refs[2 * ng:]
    tm = x_ref.shape[0]

    heads = a_ref.shape[1] // HEAD_DIM
    for g, dil in enumerate(dils):
        for r in range(dil):
            ls_ref[g, pl.ds(r, tm // dil, stride=dil), :] = l_refs[g][r]
            for h in range(heads):
                os_ref[g, h, pl.ds(r, tm // dil, stride=dil), :] = o_refs[g][r, :, h * HEAD_DIM:(h + 1) * HEAD_DIM]

    lse = [ls_ref[g] for g in range(ng)]
    m = functools.reduce(jnp.maximum, lse)
    e = [jnp.exp(l - m) for l in lse]
    tot = functools.reduce(lambda p, q: p + q, e)
    wts = [eg / tot for eg in e]
    for h in range(heads):
        terms = [wts[g][:, h:h + 1] * os_ref[g, h] for g in range(ng)]
        a_ref[:, h * HEAD_DIM:(h + 1) * HEAD_DIM] = functools.reduce(lambda p, q: p + q, terms).astype(BF16)
    out_ref[...] = x_ref[...] + jnp.dot(a_ref[...], w_ref[...], preferred_element_type=F32)


def _merge_proj(outs, lses, w, x):
    n, d = x.shape
    width = w.shape[0]
    dils = tuple(o.shape[0] for o in outs)
    tm = min(512, n)
    row = lambda i: (i, 0)
    grp = lambda i: (0, i, 0)
    return pl.pallas_call(
        functools.partial(_merge_proj_kernel, dils=dils),
        grid=(n // tm,),
        in_specs=[pl.BlockSpec((dil, tm // dil, width), grp) for dil in dils]
        + [pl.BlockSpec((dil, tm // dil, LANES), grp) for dil in dils]
        + [pl.BlockSpec((width, d), lambda i: (0, 0)), pl.BlockSpec((tm, d), row)],
        out_specs=pl.BlockSpec((tm, d), row),
        out_shape=jax.ShapeDtypeStruct((n, d), F32),
        scratch_shapes=[
            pltpu.VMEM((tm, width), BF16),
            pltpu.VMEM((len(dils), width // HEAD_DIM, tm, HEAD_DIM), F32),
            pltpu.VMEM((len(dils), tm, LANES), F32),
        ],
        compiler_params=_params("arbitrary"),
        name="merge_proj",
    )(*outs, *lses, w, x)


def _store_chunk_rows(ref, x):
    rows, d = x.shape
    nc = d // LANES
    for c in range(nc):
        ref[pl.ds(c, rows, stride=nc), :] = x[:, c * LANES:(c + 1) * LANES]


def _store_packed_rows(ref, x):
    rows, d = x.shape
    nw = d // (2 * LANES)
    for c in range(nw):
        lo = x[:, (2 * c) * LANES:(2 * c + 1) * LANES].astype(BF16).astype(F32)
        hi = x[:, (2 * c + 1) * LANES:(2 * c + 2) * LANES].astype(BF16).astype(F32)
        word = (pltpu.bitcast(lo, jnp.uint32) >> 16) | (pltpu.bitcast(hi, jnp.uint32) & jnp.uint32(0xFFFF0000))
        ref[pl.ds(c, rows, stride=nw), :] = word


def _load_packed_rows(ref, lead, rows, nw, c):
    word = ref[lead, pl.ds(c, rows, stride=nw), :]
    lo = pltpu.bitcast(word << 16, F32)
    hi = pltpu.bitcast(word & jnp.uint32(0xFFFF0000), F32)
    return lo.astype(BF16), hi.astype(BF16)


def _router_kernel(x_ref, gain_ref, wr_ref, h_ref, idx_ref, gate_ref):
    h = _rms_rows(x_ref[...], gain_ref[...])
    _store_packed_rows(h_ref, h)
    logits = jnp.dot(h, wr_ref[...], preferred_element_type=F32, precision=lax.Precision.HIGHEST)
    lane = lax.broadcasted_iota(jnp.int32, logits.shape, 1)
    lane_f = lane.astype(F32)
    logits = jnp.where(lane < N_EXPERTS, logits, -jnp.inf)
    m1 = jnp.max(logits, axis=-1, keepdims=True)
    i1 = jnp.min(jnp.where(logits == m1, lane_f, float(LANES)), axis=-1, keepdims=True)
    rest = jnp.where(lane_f == i1, -jnp.inf, logits)
    m2 = jnp.max(rest, axis=-1, keepdims=True)
    i2 = jnp.min(jnp.where(rest == m2, lane_f, float(LANES)), axis=-1, keepdims=True)
    e = jnp.exp(m2 - m1)
    tot = 1.0 + e
    idx_ref[...] = jnp.where(lane == 0, i1, jnp.where(lane == 1, i2, 0.0)).astype(jnp.int32)
    gate_ref[...] = jnp.where(lane == 0, 1.0 / tot, jnp.where(lane == 1, e / tot, 0.0))


def _router(x, gain, w_router):
    n, d = x.shape
    tm = min(512, n)
    wr = jnp.zeros((d, LANES), F32).at[:, :N_EXPERTS].set(w_router.astype(F32))
    row = lambda i: (i, 0)
    return pl.pallas_call(
        _router_kernel,
        grid=(n // tm,),
        in_specs=[
            pl.BlockSpec((tm, d), row),
            pl.BlockSpec((1, d), lambda i: (0, 0)),
            pl.BlockSpec((d, LANES), lambda i: (0, 0)),
        ],
        out_specs=[pl.BlockSpec((tm * (d // (2 * LANES)), LANES), row), pl.BlockSpec((tm, LANES), row),
                   pl.BlockSpec((tm, LANES), row)],
        out_shape=[
            jax.ShapeDtypeStruct((n * (d // (2 * LANES)), LANES), jnp.uint32),
            jax.ShapeDtypeStruct((n, LANES), jnp.int32),
            jax.ShapeDtypeStruct((n, LANES), F32),
        ],
        compiler_params=_params("arbitrary"),
        name="router",
    )(x, gain.reshape(1, d), wr)


def _expert_kernel(bexp_ref, nused_ref, rtok_ref, h_hbm, wg_ref, wu_ref, wd_ref, y_ref,
                   xs_ref, xb_ref, acc_ref, sem):
    del bexp_ref
    i = pl.program_id(0)
    j = pl.program_id(1)
    tmb, d = xb_ref.shape
    nw = d // (2 * LANES)
    n_used = nused_ref[0]

    def row_copy(tok, slot, r):
        return pltpu.make_async_copy(h_hbm.at[pl.ds(pl.multiple_of(tok * nw, nw), nw), :],
                                     xs_ref.at[slot, pl.ds(pl.multiple_of(r * nw, nw), nw), :], sem.at[slot])

    def start_gather(blk, slot):
        def issue(r, carry):
            row_copy(rtok_ref[blk * tmb + r], slot, r).start()
            return carry
        lax.fori_loop(0, tmb, issue, 0, unroll=8)

    def wait_gather(slot):
        pltpu.make_async_copy(h_hbm.at[pl.ds(0, tmb * nw), :], xs_ref.at[slot], sem.at[slot]).wait()

    @pl.when(i < n_used)
    def _():
        slot = i % 2

        @pl.when(j == 0)
        def _():
            @pl.when(i == 0)
            def _():
                start_gather(0, 0)

            wait_gather(slot)

            @pl.when(i + 1 < n_used)
            def _():
                start_gather(i + 1, 1 - slot)

            for c in range(nw):
                lo, hi = _load_packed_rows(xs_ref, slot, tmb, nw, c)
                xb_ref[:, (2 * c) * LANES:(2 * c + 1) * LANES] = lo
                xb_ref[:, (2 * c + 1) * LANES:(2 * c + 2) * LANES] = hi
            acc_ref[...] = jnp.zeros_like(acc_ref)

        x = xb_ref[...]
        g = jnp.dot(x, wg_ref[0], preferred_element_type=F32)
        u = jnp.dot(x, wu_ref[0], preferred_element_type=F32)
        acc_ref[...] += jnp.dot(_silu_mul(g, u).astype(BF16), wd_ref[0], preferred_element_type=F32)

    last = j == pl.num_programs(1) - 1

    @pl.when(last & (i < n_used))
    def _():
        _store_chunk_rows(y_ref, acc_ref[...])

    @pl.when(last & (i >= n_used))
    def _():
        y_ref[...] = jnp.zeros_like(y_ref)


def _expert_ffn(h_rows, block_expert, n_used, row_tok, w_gate, w_up, w_down):
    d = w_gate.shape[1]
    f = w_gate.shape[2]
    tmb = MOE_BLOCK
    tf = MOE_F_BLOCK
    n_blocks = block_expert.shape[0]
    nf = f // tf
    assert f % tf == 0

    def fblk(i, j, nu):
        return jnp.where(i < nu[0], j, nf - 1)

    grid_spec = pltpu.PrefetchScalarGridSpec(
        num_scalar_prefetch=3,
        grid=(n_blocks, nf),
        in_specs=[
            pl.BlockSpec(memory_space=pl.ANY),
            pl.BlockSpec((1, d, tf), lambda i, j, be, nu, rt: (be[i], 0, fblk(i, j, nu))),
            pl.BlockSpec((1, d, tf), lambda i, j, be, nu, rt: (be[i], 0, fblk(i, j, nu))),
            pl.BlockSpec((1, tf, d), lambda i, j, be, nu, rt: (be[i], fblk(i, j, nu), 0)),
        ],
        out_specs=pl.BlockSpec((tmb * (d // LANES), LANES), lambda i, j, be, nu, rt: (i, 0)),
        scratch_shapes=[
            pltpu.VMEM((2, tmb * (d // (2 * LANES)), LANES), jnp.uint32),
            pltpu.VMEM((tmb, d), BF16),
            pltpu.VMEM((tmb, d), F32),
            pltpu.SemaphoreType.DMA((2,)),
        ],
    )
    return pl.pallas_call(
        _expert_kernel,
        grid_spec=grid_spec,
        out_shape=jax.ShapeDtypeStruct((n_blocks * tmb * (d // LANES), LANES), F32),
        compiler_params=_params("arbitrary", "arbitrary"),
        name="expert_ffn",
    )(block_expert, n_used, row_tok, h_rows, w_gate, w_up, w_down)


def _combine_kernel(dest_ref, ys_hbm, x_ref, gate_ref, oa_ref, ob_ref, buf_ref, sem, *, first_blocks):
    i = pl.program_id(0)
    nsteps = pl.num_programs(0)
    tc, d = x_ref.shape
    nc = d // LANES
    pitch = buf_ref.shape[1] // (2 * tc)

    def row_copy(row, slot, dst):
        return pltpu.make_async_copy(ys_hbm.at[pl.ds(pl.multiple_of(row * nc, nc), nc), :],
                                     buf_ref.at[slot, pl.ds(pl.multiple_of(dst * pitch, 8), nc), :], sem.at[slot])

    def start_gather(step, slot):
        def issue(r, carry):
            row_copy(dest_ref[step * (2 * tc) + r], slot, r).start()
            return carry
        lax.fori_loop(0, 2 * tc, issue, 0, unroll=8)

    def wait_gather(slot):
        pltpu.make_async_copy(ys_hbm.at[pl.ds(0, 2 * tc * nc), :], buf_ref.at[slot, pl.ds(0, 2 * tc * nc), :],
                              sem.at[slot]).wait()

    slot = i % 2

    @pl.when(i == 0)
    def _():
        start_gather(0, 0)

    wait_gather(slot)

    @pl.when(i + 1 < nsteps)
    def _():
        start_gather(i + 1, 1 - slot)

    g0 = gate_ref[:, 0:1]
    g1 = gate_ref[:, 1:2]

    def emit(o_ref):
        for c in range(nc):
            sl = slice(c * LANES, (c + 1) * LANES)
            y0 = buf_ref[slot, pl.ds(c, tc, stride=pitch), :]
            y1 = buf_ref[slot, pl.ds(tc * pitch + c, tc, stride=pitch), :]
            o_ref[:, sl] = x_ref[:, sl] + (y0 * g0 + y1 * g1)
    _pick_rows(i, first_blocks, oa_ref, ob_ref, emit)


def _combine(x, ys, dest, gate, n_first):
    n, d = x.shape
    tc = min(COMBINE_BLOCK, n_first, n - n_first)
    assert n_first % tc == 0 and (n - n_first) % tc == 0
    first_blocks = n_first // tc
    nc = d // LANES
    pitch = nc if (nc // 8) % 2 else nc + 8
    dest = dest.reshape(n // tc, tc, TOP_K).transpose(0, 2, 1).reshape(-1)
    grid_spec = pltpu.PrefetchScalarGridSpec(
        num_scalar_prefetch=1,
        grid=(n // tc,),
        in_specs=[
            pl.BlockSpec(memory_space=pl.ANY),
            pl.BlockSpec((tc, d), lambda i, dr: (i, 0)),
            pl.BlockSpec((tc, LANES), lambda i, dr: (i, 0)),
        ],
        out_specs=_two_batch_specs(tc, d, first_blocks),
        scratch_shapes=[pltpu.VMEM((2, 2 * tc * pitch, LANES), F32), pltpu.SemaphoreType.DMA((2,))],
    )
    return pl.pallas_call(
        functools.partial(_combine_kernel, first_blocks=first_blocks),
        grid_spec=grid_spec,
        out_shape=[jax.ShapeDtypeStruct((n_first, d), F32), jax.ShapeDtypeStruct((n - n_first, d), F32)],
        compiler_params=_params("arbitrary"),
        name="moe_combine",
    )(dest, ys, x, gate)


def _routing_tables(idx, tmb):
    n = idx.shape[0]
    nk = n * TOP_K
    e_flat = idx[:, :TOP_K].reshape(-1)
    onehot = (e_flat[:, None] == jnp.arange(N_EXPERTS, dtype=jnp.int32)[None, :]).astype(jnp.int32)
    csum = jnp.cumsum(onehot, axis=0)
    rank = jnp.sum(onehot * csum, axis=1) - 1
    counts = csum[-1]
    padded = (counts + tmb - 1) // tmb * tmb
    pad_end = jnp.cumsum(padded)
    pad_start = pad_end - padded
    dest = (jnp.sum(onehot * pad_start[None, :], axis=1) + rank).astype(jnp.int32)
    n_blocks = -(-nk // tmb) + N_EXPERTS
    block_expert = jnp.minimum(
        jnp.searchsorted(pad_end, jnp.arange(n_blocks, dtype=jnp.int32) * tmb, side="right"), N_EXPERTS - 1
    ).astype(jnp.int32)
    n_used = (pad_end[-1] // tmb).astype(jnp.int32).reshape(1)
    row_tok = jnp.zeros((n_blocks * tmb,), jnp.int32).at[dest].set(jnp.arange(nk, dtype=jnp.int32) // TOP_K)
    return dest, block_expert, n_used, row_tok


def _rope_tables(seq_len):
    half = HEAD_DIM // 2
    inv_freq = ROPE_THETA ** (-jnp.arange(half, dtype=F32) / half)
    ang = jnp.arange(seq_len, dtype=F32)[:, None] * inv_freq[None, :]
    cos, sin = jnp.cos(ang), jnp.sin(ang)
    return jnp.concatenate([cos, cos], axis=-1), jnp.concatenate([-sin, sin], axis=-1)


def kernel(x_prompt, x_sample, norm_mix, norm_ffn, na_w_qkv, na_q_gain, na_k_gain, na_rpb, na_w_o,
           da_w_qkv, da_q_gain, da_k_gain, da_w_o, ffn_w_gate, ffn_w_up, ffn_w_down,
           moe_w_router, moe_w_gate, moe_w_up, moe_w_down):
    d = x_prompt.shape[-1]
    trunks = []
    off = 0
    for xin in (x_prompt, x_sample):
        b, s, _ = xin.shape
        trunks.append((off, b, s))
        off += b * s
    n = off
    xa, xb = x_prompt.reshape(-1, d), x_sample.reshape(-1, d)
    depth = norm_mix.shape[0]
    assert depth == 2, "layer 0 reads the two request batches directly and layer 1 writes them separately"
    x = None

    for layer in range(depth):
        lj = layer // 2
        if layer % 2 == 0:
            heads = na_w_qkv.shape[2] // (3 * HEAD_DIM)
            tn = 1024
            per = heads * HEAD_DIM // tn
            hg = jnp.concatenate([
                jnp.broadcast_to(na_q_gain[lj], (per, HEAD_DIM)),
                jnp.broadcast_to(na_k_gain[lj], (per, HEAD_DIM)),
                jnp.ones((per, HEAD_DIM), F32),
            ]).reshape(3 * per, 1, HEAD_DIM)
            qkv = _qkv_project(xa, xb, norm_mix[layer], na_w_qkv[lj].astype(BF16), hg, 3 * per, 2 * per)
            tables = _na_bias_tables(na_rpb[lj])
            att = _na_attention(qkv, tables, trunks, heads)
            x = _proj_residual(att, na_w_o[lj].astype(BF16), xa, xb)
            x = _dense_ffn(x, norm_ffn[layer], ffn_w_gate[lj].astype(BF16), ffn_w_up[lj].astype(BF16),
                           ffn_w_down[lj].astype(BF16))
        else:
            groups = len(DIL_CONFIGS)
            heads = da_w_qkv.shape[2] // (groups * 3 * HEAD_DIM)
            dils = [dil for _, dil in DIL_CONFIGS]
            ones = jnp.ones((HEAD_DIM,), F32)
            hg = jnp.stack([t for g in range(groups) for t in (da_q_gain[lj, g], da_k_gain[lj, g], ones)])
            hg = hg.reshape(groups * 3, 1, HEAD_DIM)
            (o0, b0, s0), (o1, b1, s1) = trunks
            tq = min(QKV_DIL_TOKEN_BLOCK, n)
            assert s0 % tq == 0 and s1 % tq == 0 and o1 % tq == 0
            nb0, p0, p1 = o1 // tq, s0 // tq, s1 // tq
            pos_map = lambda i: jnp.where(i < nb0, i % p0, (i - nb0) % p1)
            smax = max(s0, s1)
            rope = [jnp.stack([t.reshape(smax // tq, tq // dil, dil, HEAD_DIM).transpose(0, 2, 1, 3)
                               .reshape(smax, HEAD_DIM) for dil in dils]) for t in _rope_tables(smax)]
            qkvs = _qkv_project_dilated(x, norm_mix[layer], da_w_qkv[lj].astype(BF16), hg, rope, pos_map,
                                        dils, heads, tq)
            outs, lses = [], []
            for g in range(groups):
                o_g, lse_g = _band_attention(qkvs[g], trunks, g)
                outs.append(o_g)
                lses.append(lse_g)
            x = _merge_proj(outs, lses, da_w_o[lj].astype(BF16), x)
            h, idx, gate = _router(x, norm_ffn[layer], moe_w_router[lj])
            dest, block_expert, n_used, row_tok = _routing_tables(idx, MOE_BLOCK)
            ys = _expert_ffn(h, block_expert, n_used, row_tok, moe_w_gate[lj].astype(BF16),
                             moe_w_up[lj].astype(BF16), moe_w_down[lj].astype(BF16))
            ya, yb = _combine(x, ys, dest, gate, xa.shape[0])

    return ya.reshape(x_prompt.shape), yb.reshape(x_sample.shape)
```

```python
import functools

import jax
import jax.numpy as jnp
from jax import lax
from jax.experimental import pallas as pl
from jax.experimental.pallas import tpu as pltpu

F32 = jnp.float32
BF16 = jnp.bfloat16

HEAD_DIM = 128
GRID_W = 64
NA_ROWS = 8
NA_COLS = 16
NA_GROUP = NA_ROWS // 2
NA_KEY_ROWS = 3 * NA_GROUP
NA_HEAD_BLOCK = 8
DIL_CONFIGS = ((128, 1), (512, 4), (2048, 16))
N_EXPERTS = 8
TOP_K = 2
ROPE_THETA = 10000.0
EPS = 1e-6
NEG_INF = -1e30
ATTN_SCALE = HEAD_DIM ** -0.5

LANES = 128
MXU_COLS = 256
VMEM_LIMIT = 52 * 1024 * 1024

TOKEN_BLOCK = 1024
QKV_DIL_TOKEN_BLOCK = 512
FFN_TOKEN_BLOCK = 512
FFN_F_BLOCK = 512
BAND_Q_BLOCK = 512
BAND_Q_TILE = 128
MOE_BLOCK = 512
MOE_F_BLOCK = 1024
COMBINE_BLOCK = 256
SIDE_CAST_ROWS_WIDE = 128
SIDE_CAST_ROWS_NARROW = 512


def _params(*sem):
    return pltpu.CompilerParams(dimension_semantics=sem, vmem_limit_bytes=VMEM_LIMIT)


def _rms_rows(x, gain):
    ms = jnp.mean(x * x, axis=-1, keepdims=True)
    return x * lax.rsqrt(ms + EPS) * gain


def _pick_rows(i, first_blocks, xa_ref, xb_ref, use):
    @pl.when(i < first_blocks)
    def _():
        use(xa_ref)

    @pl.when(i >= first_blocks)
    def _():
        use(xb_ref)


def _two_batch_specs(tm, d, first_blocks, **kw):
    return [pl.BlockSpec((tm, d), lambda i, *_: (jnp.minimum(i, first_blocks - 1), 0), **kw),
            pl.BlockSpec((tm, d), lambda i, *_: (jnp.maximum(i - first_blocks, 0), 0), **kw)]


def _qkv_kernel(xa_ref, xb_ref, gain_ref, w_ref, hg_ref, o_ref, h_ref, *, first_blocks, v_every, v_from):
    j = pl.program_id(1)

    @pl.when(j == 0)
    def _():
        def norm(x_ref):
            h_ref[...] = _rms_rows(x_ref[...], gain_ref[...]).astype(BF16)
        _pick_rows(pl.program_id(0), first_blocks, xa_ref, xb_ref, norm)

    is_qk = (j % v_every) < v_from
    for c in range(w_ref.shape[1] // MXU_COLS):
        acc = jnp.dot(h_ref[...], w_ref[:, c * MXU_COLS:(c + 1) * MXU_COLS], preferred_element_type=F32)
        for h in range(MXU_COLS // HEAD_DIM):
            seg = acc[:, h * HEAD_DIM:(h + 1) * HEAD_DIM]
            lo = c * MXU_COLS + h * HEAD_DIM
            o_ref[:, lo:lo + HEAD_DIM] = jnp.where(is_qk, _rms_rows(seg, hg_ref[0]), seg).astype(o_ref.dtype)


def _qkv_project(xa, xb, gain, w, head_gain, v_every, v_from):
    d = xa.shape[1]
    n = xa.shape[0] + xb.shape[0]
    n_out = w.shape[1]
    tm = min(TOKEN_BLOCK, xa.shape[0], xb.shape[0])
    assert xa.shape[0] % tm == 0 and xb.shape[0] % tm == 0
    first_blocks = xa.shape[0] // tm
    tn = n_out // head_gain.shape[0]
    return pl.pallas_call(
        functools.partial(_qkv_kernel, first_blocks=first_blocks, v_every=v_every, v_from=v_from),
        grid=(n // tm, n_out // tn),
        in_specs=_two_batch_specs(tm, d, first_blocks, pipeline_mode=pl.Buffered(1)) + [
            pl.BlockSpec((1, d), lambda i, j: (0, 0)),
            pl.BlockSpec((d, tn), lambda i, j: (0, j)),
            pl.BlockSpec((1, 1, HEAD_DIM), lambda i, j: (j, 0, 0)),
        ],
        out_specs=pl.BlockSpec((tm, tn), lambda i, j: (i, j)),
        out_shape=jax.ShapeDtypeStruct((n, n_out), BF16),
        scratch_shapes=[pltpu.VMEM((tm, d), BF16)],
        compiler_params=_params("arbitrary", "arbitrary"),
        name="qkv_project",
    )(xa, xb, gain.reshape(1, d), w, head_gain)


def _qkv_dil_kernel(x_ref, gain_ref, w_ref, hg_ref, cos_ref, sin_ref, *rest, dils, n_side):
    ng = len(dils)
    srcs, out_refs, dsts = rest[:n_side], rest[n_side:n_side + ng], rest[n_side + ng:2 * n_side + ng]
    h_ref, stage_ref, *side = rest[2 * n_side + ng:]
    j = pl.program_id(1)
    if n_side:
        _side_cast_step(pl.program_id(0) * pl.num_programs(1) + j, srcs, dsts, *side)
    tm, d = x_ref.shape
    tn = w_ref.shape[1]

    @pl.when(j == 0)
    def _():
        y = _rms_rows(x_ref[...], gain_ref[...])
        for c in range(d // LANES):
            stage_ref[c] = y[:, c * LANES:(c + 1) * LANES]
        for g, dil in enumerate(dils):
            rows = tm // dil
            for r in range(dil):
                for c in range(d // LANES):
                    h_ref[g, r * rows:(r + 1) * rows, c * LANES:(c + 1) * LANES] = (
                        stage_ref[c, pl.ds(r, rows, stride=dil), :].astype(BF16))

    group = j // 3
    which = j % 3
    acc = jnp.dot(h_ref[group], w_ref[...], preferred_element_type=F32)

    for g, (dil, o_ref) in enumerate(zip(dils, out_refs)):
        rows = tm // dil

        @pl.when((group == g) & (which < 2))
        def _(dil=dil, o_ref=o_ref, rows=rows):
            for h in range(tn // HEAD_DIM):
                sl = slice(h * HEAD_DIM, (h + 1) * HEAD_DIM)
                y = _rms_rows(acc[:, sl], hg_ref[0])
                y = (y * cos_ref[...] + pltpu.roll(y, HEAD_DIM // 2, 1) * sin_ref[...]).astype(o_ref.dtype)
                for r in range(dil):
                    o_ref[r, :, sl] = y[r * rows:(r + 1) * rows]

        @pl.when((group == g) & (which == 2))
        def _(dil=dil, o_ref=o_ref, rows=rows):
            v = acc.astype(o_ref.dtype)
            for r in range(dil):
                o_ref[r] = v[r * rows:(r + 1) * rows]


def _qkv_project_dilated(x, gain, w, head_gain, rope_tabs, pos_map, dils, heads, tm, side_srcs=(), side_rows=0):
    n, d = x.shape
    n_out = w.shape[1]
    width = heads * HEAD_DIM
    ng = len(dils)
    assert n_out == ng * 3 * width and head_gain.shape[0] == ng * 3
    side_in, side_out, side_shapes, side_scratch = (
        _side_cast_specs(side_srcs, side_rows, (n // tm) * ng * 3) if side_srcs else ([], [], [], []))

    def out_map(g):
        def index(i, j):
            return (jnp.clip(j - 3 * g, 0, 2), 0, i, 0)
        return index

    rope_spec = pl.BlockSpec((None, tm, HEAD_DIM), lambda i, j: (j // 3, pos_map(i), 0))
    return pl.pallas_call(
        functools.partial(_qkv_dil_kernel, dils=tuple(dils), n_side=len(side_srcs)),
        grid=(n // tm, ng * 3),
        in_specs=[
            pl.BlockSpec((tm, d), lambda i, j: (i, 0)),
            pl.BlockSpec((1, d), lambda i, j: (0, 0)),
            pl.BlockSpec((d, width), lambda i, j: (0, j)),
            pl.BlockSpec((1, 1, HEAD_DIM), lambda i, j: (j, 0, 0)),
            rope_spec,
            rope_spec,
        ] + side_in,
        out_specs=[pl.BlockSpec((None, dil, tm // dil, width), out_map(g)) for g, dil in enumerate(dils)] + side_out,
        out_shape=[jax.ShapeDtypeStruct((3, dil, n // dil, width), BF16) for dil in dils] + side_shapes,
        scratch_shapes=[pltpu.VMEM((ng, tm, d), BF16), pltpu.VMEM((d // LANES, tm, LANES), F32)] + side_scratch,
        compiler_params=_params("arbitrary", "arbitrary"),
        name="qkv_project_dilated",
    )(x, gain.reshape(1, d), w, head_gain, *rope_tabs, *side_srcs)


def _na_bias_tables(rpb):
    half = NA_ROWS // 2
    assert NA_GROUP == half
    qi = jnp.arange(NA_GROUP)[:, None]
    kj = jnp.arange(NA_KEY_ROWS)[None, :]
    dr = jnp.broadcast_to(kj - NA_GROUP - qi, (3, NA_GROUP, NA_KEY_ROWS))
    rvalid = jnp.stack([
        jnp.broadcast_to(kj >= NA_GROUP, (NA_GROUP, NA_KEY_ROWS)),
        (kj - qi >= 0) & (kj - qi < NA_ROWS),
        jnp.broadcast_to(kj < NA_ROWS, (NA_GROUP, NA_KEY_ROWS)),
    ])
    c = jnp.arange(GRID_W)
    cs = jnp.clip(c - NA_COLS // 2, 0, GRID_W - NA_COLS)
    cvalid = (c[None, :] >= cs[:, None]) & (c[None, :] < cs[:, None] + NA_COLS)
    dc = jnp.clip(c[None, :] - c[:, None], 1 - NA_COLS, NA_COLS - 1)
    dri = jnp.clip(dr, 1 - NA_ROWS, NA_ROWS - 1) + NA_ROWS - 1
    sel_r = jax.nn.one_hot(dri, 2 * NA_ROWS - 1, dtype=F32)
    sel_c = jax.nn.one_hot(dc + NA_COLS - 1, 2 * NA_COLS - 1, dtype=F32)
    hp = lax.Precision.HIGHEST
    tmp = jnp.einsum("hab,qcb->haqc", rpb.astype(F32), sel_c, precision=hp)
    bias = jnp.einsum("haqc,vgka->vhgqkc", tmp, sel_r, precision=hp)
    valid = rvalid[:, None, :, None, :, None] & cvalid[None, None, None, :, None, :]
    bias = jnp.where(valid, bias, NEG_INF)
    h = rpb.shape[0]
    return bias.reshape(3, h, NA_GROUP * GRID_W, NA_KEY_ROWS * GRID_W)


def _na_kernel(q_ref, kp_ref, k_ref, kn_ref, vp_ref, v_ref, vn_ref, tab_ref, o_ref):
    gq = q_ref.shape[0]
    nt = (((1,), (1,)), ((), ()))
    for h in range(q_ref.shape[1] // HEAD_DIM):
        sl = slice(h * HEAD_DIM, (h + 1) * HEAD_DIM)
        q = q_ref[:, sl]
        s = jnp.concatenate(
            [lax.dot_general(q, kr[:, sl], nt, preferred_element_type=F32) for kr in (kp_ref, k_ref, kn_ref)],
            axis=1)
        s = s * ATTN_SCALE + tab_ref[0, h]
        m = jnp.max(s, axis=-1, keepdims=True)
        p = jnp.exp(s - m)
        den = jnp.sum(p, axis=-1, keepdims=True)
        pb = p.astype(BF16)
        o = (jnp.dot(pb[:, :gq], vp_ref[:, sl], preferred_element_type=F32)
             + jnp.dot(pb[:, gq:2 * gq], v_ref[:, sl], preferred_element_type=F32)
             + jnp.dot(pb[:, 2 * gq:], vn_ref[:, sl], preferred_element_type=F32))
        o_ref[:, sl] = (o / den).astype(o_ref.dtype)


def _two_batch_position(t, first_blocks, per_seq_a, per_seq_b):
    in_a = t < first_blocks
    per_seq = jnp.where(in_a, per_seq_a, per_seq_b)
    return jnp.where(in_a, t, t - first_blocks) % per_seq, per_seq


def _na_attention(qkv, tables, trunks, heads):
    n = qkv.shape[0]
    gq = NA_GROUP * GRID_W
    hb = NA_HEAD_BLOCK
    hblocks = heads // hb
    (off_a, b_a, s_a), (off_b, b_b, s_b) = trunks
    for (_, _, seq) in trunks:
        assert seq % gq == 0 and seq // GRID_W >= NA_ROWS
    assert heads % hb == 0 and off_a == 0 and off_b == b_a * s_a
    first_groups, gps_a, gps_b = off_b // gq, s_a // gq, s_b // gq

    def place(g):
        pos, per_seq = _two_batch_position(g, first_groups, gps_a, gps_b)
        return pos == 0, pos == per_seq - 1

    def variant(g):
        first, last = place(g)
        return jnp.where(first, 0, jnp.where(last, 2, 1))

    def rows_of(which, shift):
        def index(h, g):
            first, last = place(g)
            gg = g + jnp.where((shift < 0) & first, 0, jnp.where((shift > 0) & last, 0, shift))
            return (gg, which * hblocks + h)
        return index

    blk = lambda index: pl.BlockSpec((gq, hb * HEAD_DIM), index)
    return pl.pallas_call(
        _na_kernel,
        grid=(hblocks, n // gq),
        in_specs=[
            blk(rows_of(0, 0)),
            blk(rows_of(1, -1)), blk(rows_of(1, 0)), blk(rows_of(1, 1)),
            blk(rows_of(2, -1)), blk(rows_of(2, 0)), blk(rows_of(2, 1)),
            pl.BlockSpec((1, hb, gq, NA_KEY_ROWS * GRID_W), lambda h, g: (variant(g), h, 0, 0)),
        ],
        out_specs=blk(rows_of(0, 0)),
        out_shape=jax.ShapeDtypeStruct((n, heads * HEAD_DIM), BF16),
        compiler_params=_params("arbitrary", "arbitrary"),
        name="na_attention",
    )(qkv, qkv, qkv, qkv, qkv, qkv, qkv, tables)


def _proj_residual_kernel(a_ref, w_ref, xa_ref, xb_ref, o_ref, *, first_blocks):
    y = jnp.dot(a_ref[...], w_ref[...], preferred_element_type=F32)

    def add(x_ref):
        o_ref[...] = x_ref[...] + y
    _pick_rows(pl.program_id(1), first_blocks, xa_ref, xb_ref, add)


def _proj_residual(a, w, xa, xb):
    n, k = a.shape
    d = w.shape[1]
    tm = min(TOKEN_BLOCK, xa.shape[0], xb.shape[0])
    tn = min(1024, d)
    assert xa.shape[0] % tm == 0 and xb.shape[0] % tm == 0 and xa.shape[0] + xb.shape[0] == n
    first_blocks = xa.shape[0] // tm
    return pl.pallas_call(
        functools.partial(_proj_residual_kernel, first_blocks=first_blocks),
        grid=(d // tn, n // tm),
        in_specs=[
            pl.BlockSpec((tm, k), lambda j, i: (i, 0)),
            pl.BlockSpec((k, tn), lambda j, i: (0, j)),
            pl.BlockSpec((tm, tn), lambda j, i: (jnp.minimum(i, first_blocks - 1), j)),
            pl.BlockSpec((tm, tn), lambda j, i: (jnp.maximum(i - first_blocks, 0), j)),
        ],
        out_specs=pl.BlockSpec((tm, tn), lambda j, i: (i, j)),
        out_shape=jax.ShapeDtypeStruct((n, d), F32),
        compiler_params=_params("arbitrary", "arbitrary"),
        name="proj_residual",
    )(a, w, xa, xb)


def _side_cast_step(step, srcs, dsts, in_buf, out_buf, in_sem, out_sem):
    rows = in_buf.shape[1]
    per = srcs[0].shape[0] // rows
    nb = per * len(srcs)
    slot = step % 2

    def block_copy(refs, b, make):
        for k, ref in enumerate(refs):
            @pl.when(b // per == k)
            def _(k=k, ref=ref):
                make(ref.at[pl.ds(pl.multiple_of((b - k * per) * rows, rows), rows), :]).start()

    def in_copy(s):
        return lambda hbm: pltpu.make_async_copy(hbm, in_buf.at[s], in_sem.at[s])

    def out_copy(s):
        return lambda hbm: pltpu.make_async_copy(out_buf.at[s], hbm, out_sem.at[s])

    @pl.when(step == 0)
    def _():
        block_copy(srcs, 0, in_copy(0))

    @pl.when((step >= 2) & (step - 2 < nb))
    def _():
        out_copy(slot)(dsts[0].at[pl.ds(0, rows), :]).wait()

    @pl.when(step < nb)
    def _():
        in_copy(slot)(srcs[0].at[pl.ds(0, rows), :]).wait()

        @pl.when(step + 1 < nb)
        def _():
            block_copy(srcs, step + 1, in_copy(1 - slot))

        out_buf[slot] = in_buf[slot].astype(BF16)
        block_copy(dsts, step, out_copy(slot))


def _side_cast_fits(srcs, rows, n_steps):
    r = srcs[0].shape[0]
    return r % rows == 0 and (r // rows) * len(srcs) + 2 <= n_steps


def _side_cast_specs(srcs, rows, n_steps):
    r, c = srcs[0].shape
    assert all(a.shape == (r, c) for a in srcs) and _side_cast_fits(srcs, rows, n_steps)
    any_spec = pl.BlockSpec(memory_space=pl.ANY)
    return ([any_spec] * len(srcs), [any_spec] * len(srcs),
            [jax.ShapeDtypeStruct((r, c), BF16) for _ in srcs],
            [pltpu.VMEM((2, rows, c), F32), pltpu.VMEM((2, rows, c), BF16),
             pltpu.SemaphoreType.DMA((2,)), pltpu.SemaphoreType.DMA((2,))])


def _silu_mul(g, u):
    return g * (1.0 / (1.0 + jnp.exp(-g))) * u


def _ffn_kernel(x_ref, gain_ref, wg_ref, wu_ref, wd_ref, *rest, n_side):
    srcs, (o_ref, *dsts), (h_ref, acc_ref, *side) = rest[:n_side], rest[n_side:2 * n_side + 1], rest[2 * n_side + 1:]
    j = pl.program_id(1)
    if n_side:
        _side_cast_step(pl.program_id(0) * pl.num_programs(1) + j, srcs, dsts, *side)

    @pl.when(j == 0)
    def _():
        h_ref[...] = _rms_rows(x_ref[...], gain_ref[...]).astype(BF16)
        acc_ref[...] = jnp.zeros_like(acc_ref)

    h = h_ref[...]
    g = jnp.dot(h, wg_ref[...], preferred_element_type=F32)
    u = jnp.dot(h, wu_ref[...], preferred_element_type=F32)
    acc_ref[...] += jnp.dot(_silu_mul(g, u).astype(BF16), wd_ref[...], preferred_element_type=F32)

    @pl.when(j == pl.num_programs(1) - 1)
    def _():
        o_ref[...] = x_ref[...] + acc_ref[...]


def _dense_ffn(x, gain, w_gate, w_up, w_down, side_srcs=(), side_rows=0):
    n, d = x.shape
    f = w_gate.shape[1]
    tm = min(FFN_TOKEN_BLOCK, n)
    tf = FFN_F_BLOCK
    assert f % tf == 0
    grid = (n // tm, f // tf)
    side_in, side_out, side_shapes, side_scratch = (
        _side_cast_specs(side_srcs, side_rows, grid[0] * grid[1]) if side_srcs else ([], [], [], []))
    return pl.pallas_call(
        functools.partial(_ffn_kernel, n_side=len(side_srcs)),
        grid=grid,
        in_specs=[
            pl.BlockSpec((tm, d), lambda i, j: (i, 0)),
            pl.BlockSpec((1, d), lambda i, j: (0, 0)),
            pl.BlockSpec((d, tf), lambda i, j: (0, j)),
            pl.BlockSpec((d, tf), lambda i, j: (0, j)),
            pl.BlockSpec((tf, d), lambda i, j: (j, 0)),
        ] + side_in,
        out_specs=[pl.BlockSpec((tm, d), lambda i, j: (i, 0))] + side_out,
        out_shape=[jax.ShapeDtypeStruct((n, d), F32)] + side_shapes,
        scratch_shapes=[pltpu.VMEM((tm, d), BF16), pltpu.VMEM((tm, d), F32)] + side_scratch,
        compiler_params=_params("arbitrary", "arbitrary"),
        name="dense_ffn",
    )(x, gain.reshape(1, d), w_gate, w_up, w_down, *side_srcs)


def _band_kernel(q_ref, kp_ref, k_ref, kn_ref, vp_ref, v_ref, vn_ref, o_ref, lse_ref, kwin_ref, vwin_ref, *,
                 first_blocks, per_seq_a, per_seq_b, tq, half):
    tqb = q_ref.shape[0]
    kw = tq + 2 * half
    pos, per_seq = _two_batch_position(pl.program_id(1), first_blocks, per_seq_a, per_seq_b)
    q0 = pos * tqb
    length = per_seq * tqb
    for win, parts in ((kwin_ref, (kp_ref, k_ref, kn_ref)), (vwin_ref, (vp_ref, v_ref, vn_ref))):
        win[:half] = parts[0][...]
        win[half:half + tqb] = parts[1][...]
        win[half + tqb:] = parts[2][...]
    lane = lax.broadcasted_iota(jnp.int32, (tq, LANES), 1)

    def body(t, carry):
        r0 = pl.multiple_of(t * tq, tq)
        qpos = q0 + r0 + lax.broadcasted_iota(jnp.int32, (tq, kw), 0)
        kpos = q0 + r0 - half + lax.broadcasted_iota(jnp.int32, (tq, kw), 1)
        ok = (jnp.abs(qpos - kpos) <= half) & (kpos >= 0) & (kpos < length)
        lse_tile = jnp.zeros((tq, LANES), F32)
        for h in range(q_ref.shape[1] // HEAD_DIM):
            sl = slice(h * HEAD_DIM, (h + 1) * HEAD_DIM)
            q = q_ref[pl.ds(r0, tq), sl]
            k = kwin_ref[pl.ds(r0, kw), sl]
            v = vwin_ref[pl.ds(r0, kw), sl]
            s = lax.dot_general(q, k, (((1,), (1,)), ((), ())), preferred_element_type=F32) * ATTN_SCALE
            s = jnp.where(ok, s, NEG_INF)
            m = jnp.max(s, axis=-1, keepdims=True)
            p = jnp.exp(s - m)
            den = jnp.sum(p, axis=-1, keepdims=True)
            o = jnp.dot(p.astype(BF16), v, preferred_element_type=F32) / den
            o_ref[pl.ds(r0, tq), sl] = o
            lse_tile = jnp.where(lane == h, m + jnp.log(den), lse_tile)
        lse_ref[pl.ds(r0, tq), :] = lse_tile
        return carry

    lax.fori_loop(0, tqb // tq, body, 0)


def _band_attention(qkv, trunks, group):
    window, dil = DIL_CONFIGS[group]
    half = window // (2 * dil)
    _, _, rows, width = qkv.shape
    (off_a, b_a, s_a), (off_b, b_b, s_b) = trunks
    len_a, len_b = s_a // dil, s_b // dil
    assert s_a % dil == 0 and s_b % dil == 0 and qkv.shape[1] == dil and off_a == 0 and off_b == b_a * s_a
    tqb = min(BAND_Q_BLOCK, len_a, len_b)
    tq = min(BAND_Q_TILE, tqb)
    assert len_a % tqb == 0 and len_b % tqb == 0 and tqb % tq == 0 and tqb % half == 0
    hpb = tqb // half
    first_blocks, per_seq_a, per_seq_b = off_b // dil // tqb, len_a // tqb, len_b // tqb

    def halo(which, side):
        def index(r, t):
            pos, per_seq = _two_batch_position(t, first_blocks, per_seq_a, per_seq_b)
            edge = (pos == per_seq - 1) if side else (pos == 0)
            blk = t * hpb + (jnp.where(edge, hpb - 1, hpb) if side else jnp.where(edge, 0, -1))
            return (which, r, blk, 0)
        return index

    mspec = lambda which: pl.BlockSpec((None, None, tqb, width), lambda r, t: (which, r, t, 0))
    hspec = lambda which, side: pl.BlockSpec((None, None, half, width), halo(which, side))
    return pl.pallas_call(
        functools.partial(_band_kernel, first_blocks=first_blocks, per_seq_a=per_seq_a, per_seq_b=per_seq_b,
                          tq=tq, half=half),
        grid=(dil, rows // tqb),
        in_specs=[
            mspec(0),
            hspec(1, 0), mspec(1), hspec(1, 1),
            hspec(2, 0), mspec(2), hspec(2, 1),
        ],
        out_specs=[
            pl.BlockSpec((None, tqb, width), lambda r, t: (r, t, 0)),
            pl.BlockSpec((None, tqb, LANES), lambda r, t: (r, t, 0)),
        ],
        out_shape=[jax.ShapeDtypeStruct((dil, rows, width), F32), jax.ShapeDtypeStruct((dil, rows, LANES), F32)],
        scratch_shapes=[pltpu.VMEM((tqb + 2 * half, width), BF16), pltpu.VMEM((tqb + 2 * half, width), BF16)],
        compiler_params=_params("arbitrary", "arbitrary"),
        name="band_attention",
    )(qkv, qkv, qkv, qkv, qkv, qkv, qkv)


def _merge_proj_kernel(*refs, dils):
    ng = len(dils)
    o_refs, l_refs = refs[:ng], refs[ng:2 * ng]
    w_ref, x_ref, out_ref, a_ref, os_ref, ls_ref = refs[2 * ng:]
    tm = x_ref.shape[0]

    heads = a_ref.shape[1] // HEAD_DIM
    for g, dil in enumerate(dils):
        for r in range(dil):
            ls_ref[g, pl.ds(r, tm // dil, stride=dil), :] = l_refs[g][r]
            for h in range(heads):
                os_ref[g, h, pl.ds(r, tm // dil, stride=dil), :] = o_refs[g][r, :, h * HEAD_DIM:(h + 1) * HEAD_DIM]

    lse = [ls_ref[g] for g in range(ng)]
    m = functools.reduce(jnp.maximum, lse)
    e = [jnp.exp(l - m) for l in lse]
    tot = functools.reduce(lambda p, q: p + q, e)
    wts = [eg / tot for eg in e]
    for h in range(heads):
        terms = [wts[g][:, h:h + 1] * os_ref[g, h] for g in range(ng)]
        a_ref[:, h * HEAD_DIM:(h + 1) * HEAD_DIM] = functools.reduce(lambda p, q: p + q, terms).astype(BF16)
    out_ref[...] = x_ref[...] + jnp.dot(a_ref[...], w_ref[...], preferred_element_type=F32)


def _merge_proj(outs, lses, w, x):
    n, d = x.shape
    width = w.shape[0]
    dils = tuple(o.shape[0] for o in outs)
    tm = min(512, n)
    row = lambda i: (i, 0)
    grp = lambda i: (0, i, 0)
    return pl.pallas_call(
        functools.partial(_merge_proj_kernel, dils=dils),
        grid=(n // tm,),
        in_specs=[pl.BlockSpec((dil, tm // dil, width), grp) for dil in dils]
        + [pl.BlockSpec((dil, tm // dil, LANES), grp) for dil in dils]
        + [pl.BlockSpec((width, d), lambda i: (0, 0)), pl.BlockSpec((tm, d), row)],
        out_specs=pl.BlockSpec((tm, d), row),
        out_shape=jax.ShapeDtypeStruct((n, d), F32),
        scratch_shapes=[
            pltpu.VMEM((tm, width), BF16),
            pltpu.VMEM((len(dils), width // HEAD_DIM, tm, HEAD_DIM), F32),
            pltpu.VMEM((len(dils), tm, LANES), F32),
        ],
        compiler_params=_params("arbitrary"),
        name="merge_proj",
    )(*outs, *lses, w, x)


def _store_chunk_rows(ref, x):
    rows, d = x.shape
    nc = d // LANES
    for c in range(nc):
        ref[pl.ds(c, rows, stride=nc), :] = x[:, c * LANES:(c + 1) * LANES]


def _store_packed_rows(ref, x):
    rows, d = x.shape
    nw = d // (2 * LANES)
    for c in range(nw):
        lo = x[:, (2 * c) * LANES:(2 * c + 1) * LANES].astype(BF16).astype(F32)
        hi = x[:, (2 * c + 1) * LANES:(2 * c + 2) * LANES].astype(BF16).astype(F32)
        word = (pltpu.bitcast(lo, jnp.uint32) >> 16) | (pltpu.bitcast(hi, jnp.uint32) & jnp.uint32(0xFFFF0000))
        ref[pl.ds(c, rows, stride=nw), :] = word


def _load_packed_rows(ref, lead, rows, nw, c):
    word = ref[lead, pl.ds(c, rows, stride=nw), :]
    lo = pltpu.bitcast(word << 16, F32)
    hi = pltpu.bitcast(word & jnp.uint32(0xFFFF0000), F32)
    return lo.astype(BF16), hi.astype(BF16)


def _router_kernel(x_ref, gain_ref, wr_ref, h_ref, idx_ref, gate_ref):
    h = _rms_rows(x_ref[...], gain_ref[...])
    _store_packed_rows(h_ref, h)
    logits = jnp.dot(h, wr_ref[...], preferred_element_type=F32, precision=lax.Precision.HIGHEST)
    lane = lax.broadcasted_iota(jnp.int32, logits.shape, 1)
    lane_f = lane.astype(F32)
    logits = jnp.where(lane < N_EXPERTS, logits, -jnp.inf)
    m1 = jnp.max(logits, axis=-1, keepdims=True)
    i1 = jnp.min(jnp.where(logits == m1, lane_f, float(LANES)), axis=-1, keepdims=True)
    rest = jnp.where(lane_f == i1, -jnp.inf, logits)
    m2 = jnp.max(rest, axis=-1, keepdims=True)
    i2 = jnp.min(jnp.where(rest == m2, lane_f, float(LANES)), axis=-1, keepdims=True)
    e = jnp.exp(m2 - m1)
    tot = 1.0 + e
    idx_ref[...] = jnp.where(lane == 0, i1, jnp.where(lane == 1, i2, 0.0)).astype(jnp.int32)
    gate_ref[...] = jnp.where(lane == 0, 1.0 / tot, jnp.where(lane == 1, e / tot, 0.0))


def _router(x, gain, w_router):
    n, d = x.shape
    tm = min(512, n)
    wr = jnp.zeros((d, LANES), F32).at[:, :N_EXPERTS].set(w_router.astype(F32))
    row = lambda i: (i, 0)
    return pl.pallas_call(
        _router_kernel,
        grid=(n // tm,),
        in_specs=[
            pl.BlockSpec((tm, d), row),
            pl.BlockSpec((1, d), lambda i: (0, 0)),
            pl.BlockSpec((d, LANES), lambda i: (0, 0)),
        ],
        out_specs=[pl.BlockSpec((tm * (d // (2 * LANES)), LANES), row), pl.BlockSpec((tm, LANES), row),
                   pl.BlockSpec((tm, LANES), row)],
        out_shape=[
            jax.ShapeDtypeStruct((n * (d // (2 * LANES)), LANES), jnp.uint32),
            jax.ShapeDtypeStruct((n, LANES), jnp.int32),
            jax.ShapeDtypeStruct((n, LANES), F32),
        ],
        compiler_params=_params("arbitrary"),
        name="router",
    )(x, gain.reshape(1, d), wr)


def _expert_kernel(bexp_ref, nused_ref, rtok_ref, h_hbm, wg_ref, wu_ref, wd_ref, y_ref,
                   xs_ref, xb_ref, acc_ref, sem):
    del bexp_ref
    i = pl.program_id(0)
    j = pl.program_id(1)
    tmb, d = xb_ref.shape
    nw = d // (2 * LANES)
    n_used = nused_ref[0]

    def row_copy(tok, slot, r):
        return pltpu.make_async_copy(h_hbm.at[pl.ds(pl.multiple_of(tok * nw, nw), nw), :],
                                     xs_ref.at[slot, pl.ds(pl.multiple_of(r * nw, nw), nw), :], sem.at[slot])

    def start_gather(blk, slot):
        def issue(r, carry):
            row_copy(rtok_ref[blk * tmb + r], slot, r).start()
            return carry
        lax.fori_loop(0, tmb, issue, 0, unroll=8)

    def wait_gather(slot):
        pltpu.make_async_copy(h_hbm.at[pl.ds(0, tmb * nw), :], xs_ref.at[slot], sem.at[slot]).wait()

    @pl.when(i < n_used)
    def _():
        slot = i % 2

        @pl.when(j == 0)
        def _():
            @pl.when(i == 0)
            def _():
                start_gather(0, 0)

            wait_gather(slot)

            @pl.when(i + 1 < n_used)
            def _():
                start_gather(i + 1, 1 - slot)

            for c in range(nw):
                lo, hi = _load_packed_rows(xs_ref, slot, tmb, nw, c)
                xb_ref[:, (2 * c) * LANES:(2 * c + 1) * LANES] = lo
                xb_ref[:, (2 * c + 1) * LANES:(2 * c + 2) * LANES] = hi
            acc_ref[...] = jnp.zeros_like(acc_ref)

        x = xb_ref[...]
        g = jnp.dot(x, wg_ref[0], preferred_element_type=F32)
        u = jnp.dot(x, wu_ref[0], preferred_element_type=F32)
        acc_ref[...] += jnp.dot(_silu_mul(g, u).astype(BF16), wd_ref[0], preferred_element_type=F32)

    last = j == pl.num_programs(1) - 1

    @pl.when(last & (i < n_used))
    def _():
        _store_chunk_rows(y_ref, acc_ref[...])

    @pl.when(last & (i >= n_used))
    def _():
        y_ref[...] = jnp.zeros_like(y_ref)


def _expert_ffn(h_rows, block_expert, n_used, row_tok, w_gate, w_up, w_down):
    d = w_gate.shape[1]
    f = w_gate.shape[2]
    tmb = MOE_BLOCK
    tf = MOE_F_BLOCK
    n_blocks = block_expert.shape[0]
    nf = f // tf
    assert f % tf == 0

    def fblk(i, j, nu):
        return jnp.where(i < nu[0], j, nf - 1)

    grid_spec = pltpu.PrefetchScalarGridSpec(
        num_scalar_prefetch=3,
        grid=(n_blocks, nf),
        in_specs=[
            pl.BlockSpec(memory_space=pl.ANY),
            pl.BlockSpec((1, d, tf), lambda i, j, be, nu, rt: (be[i], 0, fblk(i, j, nu))),
            pl.BlockSpec((1, d, tf), lambda i, j, be, nu, rt: (be[i], 0, fblk(i, j, nu))),
            pl.BlockSpec((1, tf, d), lambda i, j, be, nu, rt: (be[i], fblk(i, j, nu), 0)),
        ],
        out_specs=pl.BlockSpec((tmb * (d // LANES), LANES), lambda i, j, be, nu, rt: (i, 0)),
        scratch_shapes=[
            pltpu.VMEM((2, tmb * (d // (2 * LANES)), LANES), jnp.uint32),
            pltpu.VMEM((tmb, d), BF16),
            pltpu.VMEM((tmb, d), F32),
            pltpu.SemaphoreType.DMA((2,)),
        ],
    )
    return pl.pallas_call(
        _expert_kernel,
        grid_spec=grid_spec,
        out_shape=jax.ShapeDtypeStruct((n_blocks * tmb * (d // LANES), LANES), F32),
        compiler_params=_params("arbitrary", "arbitrary"),
        name="expert_ffn",
    )(block_expert, n_used, row_tok, h_rows, w_gate, w_up, w_down)


def _combine_kernel(dest_ref, ys_hbm, x_ref, gate_ref, oa_ref, ob_ref, buf_ref, sem, *, first_blocks):
    i = pl.program_id(0)
    nsteps = pl.num_programs(0)
    tc, d = x_ref.shape
    nc = d // LANES
    pitch = buf_ref.shape[1] // (2 * tc)

    def row_copy(row, slot, dst):
        return pltpu.make_async_copy(ys_hbm.at[pl.ds(pl.multiple_of(row * nc, nc), nc), :],
                                     buf_ref.at[slot, pl.ds(pl.multiple_of(dst * pitch, 8), nc), :], sem.at[slot])

    def start_gather(step, slot):
        def issue(r, carry):
            row_copy(dest_ref[step * (2 * tc) + r], slot, r).start()
            return carry
        lax.fori_loop(0, 2 * tc, issue, 0, unroll=8)

    def wait_gather(slot):
        pltpu.make_async_copy(ys_hbm.at[pl.ds(0, 2 * tc * nc), :], buf_ref.at[slot, pl.ds(0, 2 * tc * nc), :],
                              sem.at[slot]).wait()

    slot = i % 2

    @pl.when(i == 0)
    def _():
        start_gather(0, 0)

    wait_gather(slot)

    @pl.when(i + 1 < nsteps)
    def _():
        start_gather(i + 1, 1 - slot)

    g0 = gate_ref[:, 0:1]
    g1 = gate_ref[:, 1:2]

    def emit(o_ref):
        for c in range(nc):
            sl = slice(c * LANES, (c + 1) * LANES)
            y0 = buf_ref[slot, pl.ds(c, tc, stride=pitch), :]
            y1 = buf_ref[slot, pl.ds(tc * pitch + c, tc, stride=pitch), :]
            o_ref[:, sl] = x_ref[:, sl] + (y0 * g0 + y1 * g1)
    _pick_rows(i, first_blocks, oa_ref, ob_ref, emit)


def _combine(x, ys, dest, gate, n_first):
    n, d = x.shape
    tc = min(COMBINE_BLOCK, n_first, n - n_first)
    assert n_first % tc == 0 and (n - n_first) % tc == 0
    first_blocks = n_first // tc
    nc = d // LANES
    pitch = nc if (nc // 8) % 2 else nc + 8
    dest = dest.reshape(n // tc, tc, TOP_K).transpose(0, 2, 1).reshape(-1)
    grid_spec = pltpu.PrefetchScalarGridSpec(
        num_scalar_prefetch=1,
        grid=(n // tc,),
        in_specs=[
            pl.BlockSpec(memory_space=pl.ANY),
            pl.BlockSpec((tc, d), lambda i, dr: (i, 0)),
            pl.BlockSpec((tc, LANES), lambda i, dr: (i, 0)),
        ],
        out_specs=_two_batch_specs(tc, d, first_blocks),
        scratch_shapes=[pltpu.VMEM((2, 2 * tc * pitch, LANES), F32), pltpu.SemaphoreType.DMA((2,))],
    )
    return pl.pallas_call(
        functools.partial(_combine_kernel, first_blocks=first_blocks),
        grid_spec=grid_spec,
        out_shape=[jax.ShapeDtypeStruct((n_first, d), F32), jax.ShapeDtypeStruct((n - n_first, d), F32)],
        compiler_params=_params("arbitrary"),
        name="moe_combine",
    )(dest, ys, x, gate)


def _routing_tables(idx, tmb):
    n = idx.shape[0]
    nk = n * TOP_K
    e_flat = idx[:, :TOP_K].reshape(-1)
    onehot = (e_flat[:, None] == jnp.arange(N_EXPERTS, dtype=jnp.int32)[None, :]).astype(jnp.int32)
    csum = jnp.cumsum(onehot, axis=0)
    rank = jnp.sum(onehot * csum, axis=1) - 1
    counts = csum[-1]
    padded = (counts + tmb - 1) // tmb * tmb
    pad_end = jnp.cumsum(padded)
    pad_start = pad_end - padded
    dest = (jnp.sum(onehot * pad_start[None, :], axis=1) + rank).astype(jnp.int32)
    n_blocks = -(-nk // tmb) + N_EXPERTS
    block_expert = jnp.minimum(
        jnp.searchsorted(pad_end, jnp.arange(n_blocks, dtype=jnp.int32) * tmb, side="right"), N_EXPERTS - 1
    ).astype(jnp.int32)
    n_used = (pad_end[-1] // tmb).astype(jnp.int32).reshape(1)
    row_tok = jnp.zeros((n_blocks * tmb,), jnp.int32).at[dest].set(jnp.arange(nk, dtype=jnp.int32) // TOP_K)
    return dest, block_expert, n_used, row_tok


def _rope_tables(seq_len):
    half = HEAD_DIM // 2
    inv_freq = ROPE_THETA ** (-jnp.arange(half, dtype=F32) / half)
    ang = jnp.arange(seq_len, dtype=F32)[:, None] * inv_freq[None, :]
    cos, sin = jnp.cos(ang), jnp.sin(ang)
    return jnp.concatenate([cos, cos], axis=-1), jnp.concatenate([-sin, sin], axis=-1)


def kernel(x_prompt, x_sample, norm_mix, norm_ffn, na_w_qkv, na_q_gain, na_k_gain, na_rpb, na_w_o,
           da_w_qkv, da_q_gain, da_k_gain, da_w_o, ffn_w_gate, ffn_w_up, ffn_w_down,
           moe_w_router, moe_w_gate, moe_w_up, moe_w_down):
    d = x_prompt.shape[-1]
    trunks = []
    off = 0
    for xin in (x_prompt, x_sample):
        b, s, _ = xin.shape
        trunks.append((off, b, s))
        off += b * s
    n = off
    xa, xb = x_prompt.reshape(-1, d), x_sample.reshape(-1, d)
    depth = norm_mix.shape[0]
    assert depth == 2, "layer 0 reads the two request batches directly and layer 1 writes them separately"
    x = None

    for layer in range(depth):
        lj = layer // 2
        if layer % 2 == 0:
            heads = na_w_qkv.shape[2] // (3 * HEAD_DIM)
            tn = 1024
            per = heads * HEAD_DIM // tn
            hg = jnp.concatenate([
                jnp.broadcast_to(na_q_gain[lj], (per, HEAD_DIM)),
                jnp.broadcast_to(na_k_gain[lj], (per, HEAD_DIM)),
                jnp.ones((per, HEAD_DIM), F32),
            ]).reshape(3 * per, 1, HEAD_DIM)
            qkv = _qkv_project(xa, xb, norm_mix[layer], na_w_qkv[lj].astype(BF16), hg, 3 * per, 2 * per)
            tables = _na_bias_tables(na_rpb[lj])
            att = _na_attention(qkv, tables, trunks, heads)
            x = _proj_residual(att, na_w_o[lj].astype(BF16), xa, xb)
            ne, dm, fe = moe_w_gate.shape[1:]
            wide = (moe_w_gate[lj].reshape(ne * dm, fe), moe_w_up[lj].reshape(ne * dm, fe))
            steps = (n // min(FFN_TOKEN_BLOCK, n)) * (ffn_w_gate.shape[2] // FFN_F_BLOCK)
            shadow = _side_cast_fits(wide, SIDE_CAST_ROWS_WIDE, steps)
            x, *cast = _dense_ffn(x, norm_ffn[layer], ffn_w_gate[lj].astype(BF16), ffn_w_up[lj].astype(BF16),
                                  ffn_w_down[lj].astype(BF16), side_srcs=wide if shadow else (),
                                  side_rows=SIDE_CAST_ROWS_WIDE)
            moe_gate_bf, moe_up_bf = cast if shadow else [w.astype(BF16) for w in wide]
        else:
            groups = len(DIL_CONFIGS)
            heads = da_w_qkv.shape[2] // (groups * 3 * HEAD_DIM)
            dils = [dil for _, dil in DIL_CONFIGS]
            ones = jnp.ones((HEAD_DIM,), F32)
            hg = jnp.stack([t for g in range(groups) for t in (da_q_gain[lj, g], da_k_gain[lj, g], ones)])
            hg = hg.reshape(groups * 3, 1, HEAD_DIM)
            (o0, b0, s0), (o1, b1, s1) = trunks
            tq = min(QKV_DIL_TOKEN_BLOCK, n)
            assert s0 % tq == 0 and s1 % tq == 0 and o1 % tq == 0
            nb0, p0, p1 = o1 // tq, s0 // tq, s1 // tq
            pos_map = lambda i: jnp.where(i < nb0, i % p0, (i - nb0) % p1)
            smax = max(s0, s1)
            rope = [jnp.stack([t.reshape(smax // tq, tq // dil, dil, HEAD_DIM).transpose(0, 2, 1, 3)
                               .reshape(smax, HEAD_DIM) for dil in dils]) for t in _rope_tables(smax)]
            narrow = (moe_w_down[lj].reshape(ne * fe, dm),)
            shadow = _side_cast_fits(narrow, SIDE_CAST_ROWS_NARROW, (n // tq) * groups * 3)
            qkvs = list(_qkv_project_dilated(
                x, norm_mix[layer], da_w_qkv[lj].astype(BF16), hg, rope, pos_map, dils, heads, tq,
                side_srcs=narrow if shadow else (), side_rows=SIDE_CAST_ROWS_NARROW))
            moe_down_bf = qkvs.pop() if shadow else narrow[0].astype(BF16)
            outs, lses = [], []
            for g in range(groups):
                o_g, lse_g = _band_attention(qkvs[g], trunks, g)
                outs.append(o_g)
                lses.append(lse_g)
            x = _merge_proj(outs, lses, da_w_o[lj].astype(BF16), x)
            h, idx, gate = _router(x, norm_ffn[layer], moe_w_router[lj])
            dest, block_expert, n_used, row_tok = _routing_tables(idx, MOE_BLOCK)
            ys = _expert_ffn(h, block_expert, n_used, row_tok, moe_gate_bf.reshape(ne, dm, fe),
                             moe_up_bf.reshape(ne, dm, fe), moe_down_bf.reshape(ne, fe, dm))
            ya, yb = _combine(x, ys, dest, gate, xa.shape[0])

    return ya.reshape(x_prompt.shape), yb.reshape(x_sample.shape)
```

```python
import functools

import jax
import jax.numpy as jnp
from jax import lax
from jax.experimental import pallas as pl
from jax.experimental.pallas import tpu as pltpu

F32 = jnp.float32
BF16 = jnp.bfloat16

HEAD_DIM = 128
GRID_W = 64
NA_ROWS = 8
NA_COLS = 16
NA_GROUP = NA_ROWS // 2
NA_KEY_ROWS = 3 * NA_GROUP
NA_HEAD_BLOCK = 8
DIL_CONFIGS = ((128, 1), (512, 4), (2048, 16))
N_EXPERTS = 8
TOP_K = 2
ROPE_THETA = 10000.0
EPS = 1e-6
NEG_INF = -1e30
ATTN_SCALE = HEAD_DIM ** -0.5

LANES = 128
MXU_COLS = 256
VMEM_LIMIT = 52 * 1024 * 1024
FFN_VMEM_LIMIT = 57 * 1024 * 1024

TOKEN_BLOCK = 1024
QKV_DIL_TOKEN_BLOCK = 512
FFN_TOKEN_BLOCK = 1024
FFN_F_BLOCK = 256
BAND_Q_BLOCK = 512
BAND_Q_TILE = 128
MOE_BLOCK = 512
MOE_F_BLOCK = 1024
COMBINE_BLOCK = 256
SIDE_CAST_ROWS_WIDE = 128
SIDE_CAST_ROWS_NARROW = 512


def _params(*sem, vmem_limit=VMEM_LIMIT):
    return pltpu.CompilerParams(dimension_semantics=sem, vmem_limit_bytes=vmem_limit)


def _rms_rows(x, gain):
    ms = jnp.mean(x * x, axis=-1, keepdims=True)
    return x * lax.rsqrt(ms + EPS) * gain


def _pick_rows(i, first_blocks, xa_ref, xb_ref, use):
    @pl.when(i < first_blocks)
    def _():
        use(xa_ref)

    @pl.when(i >= first_blocks)
    def _():
        use(xb_ref)


def _two_batch_specs(tm, d, first_blocks, **kw):
    return [pl.BlockSpec((tm, d), lambda i, *_: (jnp.minimum(i, first_blocks - 1), 0), **kw),
            pl.BlockSpec((tm, d), lambda i, *_: (jnp.maximum(i - first_blocks, 0), 0), **kw)]


def _qkv_kernel(xa_ref, xb_ref, gain_ref, w_ref, hg_ref, o_ref, h_ref, *, first_blocks, v_every, v_from):
    j = pl.program_id(1)

    @pl.when(j == 0)
    def _():
        def norm(x_ref):
            h_ref[...] = _rms_rows(x_ref[...], gain_ref[...]).astype(BF16)
        _pick_rows(pl.program_id(0), first_blocks, xa_ref, xb_ref, norm)

    is_qk = (j % v_every) < v_from
    for c in range(w_ref.shape[1] // MXU_COLS):
        acc = jnp.dot(h_ref[...], w_ref[:, c * MXU_COLS:(c + 1) * MXU_COLS], preferred_element_type=F32)
        for h in range(MXU_COLS // HEAD_DIM):
            seg = acc[:, h * HEAD_DIM:(h + 1) * HEAD_DIM]
            lo = c * MXU_COLS + h * HEAD_DIM
            o_ref[:, lo:lo + HEAD_DIM] = jnp.where(is_qk, _rms_rows(seg, hg_ref[0]), seg).astype(o_ref.dtype)


def _qkv_project(xa, xb, gain, w, head_gain, v_every, v_from):
    d = xa.shape[1]
    n = xa.shape[0] + xb.shape[0]
    n_out = w.shape[1]
    tm = min(TOKEN_BLOCK, xa.shape[0], xb.shape[0])
    assert xa.shape[0] % tm == 0 and xb.shape[0] % tm == 0
    first_blocks = xa.shape[0] // tm
    tn = n_out // head_gain.shape[0]
    return pl.pallas_call(
        functools.partial(_qkv_kernel, first_blocks=first_blocks, v_every=v_every, v_from=v_from),
        grid=(n // tm, n_out // tn),
        in_specs=_two_batch_specs(tm, d, first_blocks, pipeline_mode=pl.Buffered(1)) + [
            pl.BlockSpec((1, d), lambda i, j: (0, 0)),
            pl.BlockSpec((d, tn), lambda i, j: (0, j)),
            pl.BlockSpec((1, 1, HEAD_DIM), lambda i, j: (j, 0, 0)),
        ],
        out_specs=pl.BlockSpec((tm, tn), lambda i, j: (i, j)),
        out_shape=jax.ShapeDtypeStruct((n, n_out), BF16),
        scratch_shapes=[pltpu.VMEM((tm, d), BF16)],
        compiler_params=_params("arbitrary", "arbitrary"),
        name="qkv_project",
    )(xa, xb, gain.reshape(1, d), w, head_gain)


def _qkv_dil_kernel(x_ref, gain_ref, w_ref, hg_ref, cos_ref, sin_ref, *rest, dils, n_side):
    ng = len(dils)
    srcs, out_refs, dsts = rest[:n_side], rest[n_side:n_side + ng], rest[n_side + ng:2 * n_side + ng]
    h_ref, stage_ref, *side = rest[2 * n_side + ng:]
    j = pl.program_id(1)
    if n_side:
        _side_cast_step(pl.program_id(0) * pl.num_programs(1) + j, srcs, dsts, *side)
    tm, d = x_ref.shape
    tn = w_ref.shape[1]

    @pl.when(j == 0)
    def _():
        y = _rms_rows(x_ref[...], gain_ref[...])
        for c in range(d // LANES):
            stage_ref[c] = y[:, c * LANES:(c + 1) * LANES]
        for g, dil in enumerate(dils):
            rows = tm // dil
            for r in range(dil):
                for c in range(d // LANES):
                    h_ref[g, r * rows:(r + 1) * rows, c * LANES:(c + 1) * LANES] = (
                        stage_ref[c, pl.ds(r, rows, stride=dil), :].astype(BF16))

    group = j // 3
    which = j % 3
    acc = jnp.dot(h_ref[group], w_ref[...], preferred_element_type=F32)

    for g, (dil, o_ref) in enumerate(zip(dils, out_refs)):
        rows = tm // dil

        @pl.when((group == g) & (which < 2))
        def _(dil=dil, o_ref=o_ref, rows=rows):
            for h in range(tn // HEAD_DIM):
                sl = slice(h * HEAD_DIM, (h + 1) * HEAD_DIM)
                y = _rms_rows(acc[:, sl], hg_ref[0])
                y = (y * cos_ref[...] + pltpu.roll(y, HEAD_DIM // 2, 1) * sin_ref[...]).astype(o_ref.dtype)
                for r in range(dil):
                    o_ref[r, :, sl] = y[r * rows:(r + 1) * rows]

        @pl.when((group == g) & (which == 2))
        def _(dil=dil, o_ref=o_ref, rows=rows):
            v = acc.astype(o_ref.dtype)
            for r in range(dil):
                o_ref[r] = v[r * rows:(r + 1) * rows]


def _qkv_project_dilated(x, gain, w, head_gain, rope_tabs, pos_map, dils, heads, tm, side_srcs=(), side_rows=0):
    n, d = x.shape
    n_out = w.shape[1]
    width = heads * HEAD_DIM
    ng = len(dils)
    assert n_out == ng * 3 * width and head_gain.shape[0] == ng * 3
    side_in, side_out, side_shapes, side_scratch = (
        _side_cast_specs(side_srcs, side_rows, (n // tm) * ng * 3) if side_srcs else ([], [], [], []))

    def out_map(g):
        def index(i, j):
            return (jnp.clip(j - 3 * g, 0, 2), 0, i, 0)
        return index

    rope_spec = pl.BlockSpec((None, tm, HEAD_DIM), lambda i, j: (j // 3, pos_map(i), 0))
    return pl.pallas_call(
        functools.partial(_qkv_dil_kernel, dils=tuple(dils), n_side=len(side_srcs)),
        grid=(n // tm, ng * 3),
        in_specs=[
            pl.BlockSpec((tm, d), lambda i, j: (i, 0)),
            pl.BlockSpec((1, d), lambda i, j: (0, 0)),
            pl.BlockSpec((d, width), lambda i, j: (0, j)),
            pl.BlockSpec((1, 1, HEAD_DIM), lambda i, j: (j, 0, 0)),
            rope_spec,
            rope_spec,
        ] + side_in,
        out_specs=[pl.BlockSpec((None, dil, tm // dil, width), out_map(g)) for g, dil in enumerate(dils)] + side_out,
        out_shape=[jax.ShapeDtypeStruct((3, dil, n // dil, width), BF16) for dil in dils] + side_shapes,
        scratch_shapes=[pltpu.VMEM((ng, tm, d), BF16), pltpu.VMEM((d // LANES, tm, LANES), F32)] + side_scratch,
        compiler_params=_params("arbitrary", "arbitrary"),
        name="qkv_project_dilated",
    )(x, gain.reshape(1, d), w, head_gain, *rope_tabs, *side_srcs)


def _na_bias_tables(rpb):
    half = NA_ROWS // 2
    assert NA_GROUP == half
    qi = jnp.arange(NA_GROUP)[:, None]
    kj = jnp.arange(NA_KEY_ROWS)[None, :]
    dr = jnp.broadcast_to(kj - NA_GROUP - qi, (3, NA_GROUP, NA_KEY_ROWS))
    rvalid = jnp.stack([
        jnp.broadcast_to(kj >= NA_GROUP, (NA_GROUP, NA_KEY_ROWS)),
        (kj - qi >= 0) & (kj - qi < NA_ROWS),
        jnp.broadcast_to(kj < NA_ROWS, (NA_GROUP, NA_KEY_ROWS)),
    ])
    c = jnp.arange(GRID_W)
    cs = jnp.clip(c - NA_COLS // 2, 0, GRID_W - NA_COLS)
    cvalid = (c[None, :] >= cs[:, None]) & (c[None, :] < cs[:, None] + NA_COLS)
    dc = jnp.clip(c[None, :] - c[:, None], 1 - NA_COLS, NA_COLS - 1)
    dri = jnp.clip(dr, 1 - NA_ROWS, NA_ROWS - 1) + NA_ROWS - 1
    sel_r = jax.nn.one_hot(dri, 2 * NA_ROWS - 1, dtype=F32)
    sel_c = jax.nn.one_hot(dc + NA_COLS - 1, 2 * NA_COLS - 1, dtype=F32)
    hp = lax.Precision.HIGHEST
    tmp = jnp.einsum("hab,qcb->haqc", rpb.astype(F32), sel_c, precision=hp)
    bias = jnp.einsum("haqc,vgka->vhgqkc", tmp, sel_r, precision=hp)
    valid = rvalid[:, None, :, None, :, None] & cvalid[None, None, None, :, None, :]
    bias = jnp.where(valid, bias, NEG_INF)
    h = rpb.shape[0]
    return bias.reshape(3, h, NA_GROUP * GRID_W, NA_KEY_ROWS * GRID_W)


def _na_kernel(q_ref, kp_ref, k_ref, kn_ref, vp_ref, v_ref, vn_ref, tab_ref, o_ref):
    gq = q_ref.shape[0]
    nt = (((1,), (1,)), ((), ()))
    for h in range(q_ref.shape[1] // HEAD_DIM):
        sl = slice(h * HEAD_DIM, (h + 1) * HEAD_DIM)
        q = q_ref[:, sl]
        s = jnp.concatenate(
            [lax.dot_general(q, kr[:, sl], nt, preferred_element_type=F32) for kr in (kp_ref, k_ref, kn_ref)],
            axis=1)
        s = s * ATTN_SCALE + tab_ref[0, h]
        m = jnp.max(s, axis=-1, keepdims=True)
        p = jnp.exp(s - m)
        den = jnp.sum(p, axis=-1, keepdims=True)
        pb = p.astype(BF16)
        o = (jnp.dot(pb[:, :gq], vp_ref[:, sl], preferred_element_type=F32)
             + jnp.dot(pb[:, gq:2 * gq], v_ref[:, sl], preferred_element_type=F32)
             + jnp.dot(pb[:, 2 * gq:], vn_ref[:, sl], preferred_element_type=F32))
        o_ref[:, sl] = (o / den).astype(o_ref.dtype)


def _two_batch_position(t, first_blocks, per_seq_a, per_seq_b):
    in_a = t < first_blocks
    per_seq = jnp.where(in_a, per_seq_a, per_seq_b)
    return jnp.where(in_a, t, t - first_blocks) % per_seq, per_seq


def _na_attention(qkv, tables, trunks, heads):
    n = qkv.shape[0]
    gq = NA_GROUP * GRID_W
    hb = NA_HEAD_BLOCK
    hblocks = heads // hb
    (off_a, b_a, s_a), (off_b, b_b, s_b) = trunks
    for (_, _, seq) in trunks:
        assert seq % gq == 0 and seq // GRID_W >= NA_ROWS
    assert heads % hb == 0 and off_a == 0 and off_b == b_a * s_a
    first_groups, gps_a, gps_b = off_b // gq, s_a // gq, s_b // gq

    def place(g):
        pos, per_seq = _two_batch_position(g, first_groups, gps_a, gps_b)
        return pos == 0, pos == per_seq - 1

    def variant(g):
        first, last = place(g)
        return jnp.where(first, 0, jnp.where(last, 2, 1))

    def rows_of(which, shift):
        def index(h, g):
            first, last = place(g)
            gg = g + jnp.where((shift < 0) & first, 0, jnp.where((shift > 0) & last, 0, shift))
            return (gg, which * hblocks + h)
        return index

    blk = lambda index: pl.BlockSpec((gq, hb * HEAD_DIM), index)
    return pl.pallas_call(
        _na_kernel,
        grid=(hblocks, n // gq),
        in_specs=[
            blk(rows_of(0, 0)),
            blk(rows_of(1, -1)), blk(rows_of(1, 0)), blk(rows_of(1, 1)),
            blk(rows_of(2, -1)), blk(rows_of(2, 0)), blk(rows_of(2, 1)),
            pl.BlockSpec((1, hb, gq, NA_KEY_ROWS * GRID_W), lambda h, g: (variant(g), h, 0, 0)),
        ],
        out_specs=blk(rows_of(0, 0)),
        out_shape=jax.ShapeDtypeStruct((n, heads * HEAD_DIM), BF16),
        compiler_params=_params("arbitrary", "arbitrary"),
        name="na_attention",
    )(qkv, qkv, qkv, qkv, qkv, qkv, qkv, tables)


def _proj_residual_kernel(a_ref, w_ref, xa_ref, xb_ref, o_ref, *, first_blocks):
    y = jnp.dot(a_ref[...], w_ref[...], preferred_element_type=F32)

    def add(x_ref):
        o_ref[...] = x_ref[...] + y
    _pick_rows(pl.program_id(1), first_blocks, xa_ref, xb_ref, add)


def _proj_residual(a, w, xa, xb):
    n, k = a.shape
    d = w.shape[1]
    tm = min(TOKEN_BLOCK, xa.shape[0], xb.shape[0])
    tn = min(1024, d)
    assert xa.shape[0] % tm == 0 and xb.shape[0] % tm == 0 and xa.shape[0] + xb.shape[0] == n
    first_blocks = xa.shape[0] // tm
    return pl.pallas_call(
        functools.partial(_proj_residual_kernel, first_blocks=first_blocks),
        grid=(d // tn, n // tm),
        in_specs=[
            pl.BlockSpec((tm, k), lambda j, i: (i, 0)),
            pl.BlockSpec((k, tn), lambda j, i: (0, j)),
            pl.BlockSpec((tm, tn), lambda j, i: (jnp.minimum(i, first_blocks - 1), j)),
            pl.BlockSpec((tm, tn), lambda j, i: (jnp.maximum(i - first_blocks, 0), j)),
        ],
        out_specs=pl.BlockSpec((tm, tn), lambda j, i: (i, j)),
        out_shape=jax.ShapeDtypeStruct((n, d), F32),
        compiler_params=_params("arbitrary", "arbitrary"),
        name="proj_residual",
    )(a, w, xa, xb)


def _side_cast_step(step, srcs, dsts, in_buf, out_buf, in_sem, out_sem):
    rows = in_buf.shape[1]
    per = srcs[0].shape[0] // rows
    nb = per * len(srcs)
    slot = step % 2

    def block_copy(refs, b, make):
        for k, ref in enumerate(refs):
            @pl.when(b // per == k)
            def _(k=k, ref=ref):
                make(ref.at[pl.ds(pl.multiple_of((b - k * per) * rows, rows), rows), :]).start()

    def in_copy(s):
        return lambda hbm: pltpu.make_async_copy(hbm, in_buf.at[s], in_sem.at[s])

    def out_copy(s):
        return lambda hbm: pltpu.make_async_copy(out_buf.at[s], hbm, out_sem.at[s])

    @pl.when(step == 0)
    def _():
        block_copy(srcs, 0, in_copy(0))

    @pl.when((step >= 2) & (step - 2 < nb))
    def _():
        out_copy(slot)(dsts[0].at[pl.ds(0, rows), :]).wait()

    @pl.when(step < nb)
    def _():
        in_copy(slot)(srcs[0].at[pl.ds(0, rows), :]).wait()

        @pl.when(step + 1 < nb)
        def _():
            block_copy(srcs, step + 1, in_copy(1 - slot))

        out_buf[slot] = in_buf[slot].astype(BF16)
        block_copy(dsts, step, out_copy(slot))


def _side_cast_fits(srcs, rows, n_steps):
    r = srcs[0].shape[0]
    return r % rows == 0 and (r // rows) * len(srcs) + 2 <= n_steps


def _side_cast_specs(srcs, rows, n_steps):
    r, c = srcs[0].shape
    assert all(a.shape == (r, c) for a in srcs) and _side_cast_fits(srcs, rows, n_steps)
    any_spec = pl.BlockSpec(memory_space=pl.ANY)
    return ([any_spec] * len(srcs), [any_spec] * len(srcs),
            [jax.ShapeDtypeStruct((r, c), BF16) for _ in srcs],
            [pltpu.VMEM((2, rows, c), F32), pltpu.VMEM((2, rows, c), BF16),
             pltpu.SemaphoreType.DMA((2,)), pltpu.SemaphoreType.DMA((2,))])


def _silu_mul(g, u):
    return g * (1.0 / (1.0 + jnp.exp(-g))) * u


def _ffn_kernel(x_ref, gain_ref, wg_ref, wu_ref, wd_ref, *rest, n_side):
    srcs, (o_ref, *dsts), (h_ref, acc_ref, *side) = rest[:n_side], rest[n_side:2 * n_side + 1], rest[2 * n_side + 1:]
    j = pl.program_id(1)
    if n_side:
        _side_cast_step(pl.program_id(0) * pl.num_programs(1) + j, srcs, dsts, *side)

    @pl.when(j == 0)
    def _():
        h_ref[...] = _rms_rows(x_ref[...], gain_ref[...]).astype(BF16)
        acc_ref[...] = jnp.zeros_like(acc_ref)

    h = h_ref[...]
    g = jnp.dot(h, wg_ref[...], preferred_element_type=F32)
    u = jnp.dot(h, wu_ref[...], preferred_element_type=F32)
    acc_ref[...] += jnp.dot(_silu_mul(g, u).astype(BF16), wd_ref[...], preferred_element_type=F32)

    @pl.when(j == pl.num_programs(1) - 1)
    def _():
        o_ref[...] = x_ref[...] + acc_ref[...]


def _dense_ffn(x, gain, w_gate, w_up, w_down, side_srcs=(), side_rows=0):
    n, d = x.shape
    f = w_gate.shape[1]
    tm = min(FFN_TOKEN_BLOCK, n)
    tf = FFN_F_BLOCK
    assert f % tf == 0
    grid = (n // tm, f // tf)
    side_in, side_out, side_shapes, side_scratch = (
        _side_cast_specs(side_srcs, side_rows, grid[0] * grid[1]) if side_srcs else ([], [], [], []))
    return pl.pallas_call(
        functools.partial(_ffn_kernel, n_side=len(side_srcs)),
        grid=grid,
        in_specs=[
            pl.BlockSpec((tm, d), lambda i, j: (i, 0), pipeline_mode=pl.Buffered(1)),
            pl.BlockSpec((1, d), lambda i, j: (0, 0)),
            pl.BlockSpec((d, tf), lambda i, j: (0, j)),
            pl.BlockSpec((d, tf), lambda i, j: (0, j)),
            pl.BlockSpec((tf, d), lambda i, j: (j, 0)),
        ] + side_in,
        out_specs=[pl.BlockSpec((tm, d), lambda i, j: (i, 0))] + side_out,
        out_shape=[jax.ShapeDtypeStruct((n, d), F32)] + side_shapes,
        scratch_shapes=[pltpu.VMEM((tm, d), BF16), pltpu.VMEM((tm, d), F32)] + side_scratch,
        compiler_params=_params("arbitrary", "arbitrary", vmem_limit=FFN_VMEM_LIMIT),
        name="dense_ffn",
    )(x, gain.reshape(1, d), w_gate, w_up, w_down, *side_srcs)


def _band_kernel(q_ref, kp_ref, k_ref, kn_ref, vp_ref, v_ref, vn_ref, o_ref, lse_ref, kwin_ref, vwin_ref, *,
                 first_blocks, per_seq_a, per_seq_b, tq, half):
    tqb = q_ref.shape[0]
    kw = tq + 2 * half
    pos, per_seq = _two_batch_position(pl.program_id(1), first_blocks, per_seq_a, per_seq_b)
    q0 = pos * tqb
    length = per_seq * tqb
    for win, parts in ((kwin_ref, (kp_ref, k_ref, kn_ref)), (vwin_ref, (vp_ref, v_ref, vn_ref))):
        win[:half] = parts[0][...]
        win[half:half + tqb] = parts[1][...]
        win[half + tqb:] = parts[2][...]
    lane = lax.broadcasted_iota(jnp.int32, (tq, LANES), 1)

    def body(t, carry):
        r0 = pl.multiple_of(t * tq, tq)
        qpos = q0 + r0 + lax.broadcasted_iota(jnp.int32, (tq, kw), 0)
        kpos = q0 + r0 - half + lax.broadcasted_iota(jnp.int32, (tq, kw), 1)
        ok = (jnp.abs(qpos - kpos) <= half) & (kpos >= 0) & (kpos < length)
        lse_tile = jnp.zeros((tq, LANES), F32)
        for h in range(q_ref.shape[1] // HEAD_DIM):
            sl = slice(h * HEAD_DIM, (h + 1) * HEAD_DIM)
            q = q_ref[pl.ds(r0, tq), sl]
            k = kwin_ref[pl.ds(r0, kw), sl]
            v = vwin_ref[pl.ds(r0, kw), sl]
            s = lax.dot_general(q, k, (((1,), (1,)), ((), ())), preferred_element_type=F32) * ATTN_SCALE
            s = jnp.where(ok, s, NEG_INF)
            m = jnp.max(s, axis=-1, keepdims=True)
            p = jnp.exp(s - m)
            den = jnp.sum(p, axis=-1, keepdims=True)
            o = jnp.dot(p.astype(BF16), v, preferred_element_type=F32) / den
            o_ref[pl.ds(r0, tq), sl] = o
            lse_tile = jnp.where(lane == h, m + jnp.log(den), lse_tile)
        lse_ref[pl.ds(r0, tq), :] = lse_tile
        return carry

    lax.fori_loop(0, tqb // tq, body, 0)


def _band_attention(qkv, trunks, group):
    window, dil = DIL_CONFIGS[group]
    half = window // (2 * dil)
    _, _, rows, width = qkv.shape
    (off_a, b_a, s_a), (off_b, b_b, s_b) = trunks
    len_a, len_b = s_a // dil, s_b // dil
    assert s_a % dil == 0 and s_b % dil == 0 and qkv.shape[1] == dil and off_a == 0 and off_b == b_a * s_a
    tqb = min(BAND_Q_BLOCK, len_a, len_b)
    tq = min(BAND_Q_TILE, tqb)
    assert len_a % tqb == 0 and len_b % tqb == 0 and tqb % tq == 0 and tqb % half == 0
    hpb = tqb // half
    first_blocks, per_seq_a, per_seq_b = off_b // dil // tqb, len_a // tqb, len_b // tqb

    def halo(which, side):
        def index(r, t):
            pos, per_seq = _two_batch_position(t, first_blocks, per_seq_a, per_seq_b)
            edge = (pos == per_seq - 1) if side else (pos == 0)
            blk = t * hpb + (jnp.where(edge, hpb - 1, hpb) if side else jnp.where(edge, 0, -1))
            return (which, r, blk, 0)
        return index

    mspec = lambda which: pl.BlockSpec((None, None, tqb, width), lambda r, t: (which, r, t, 0))
    hspec = lambda which, side: pl.BlockSpec((None, None, half, width), halo(which, side))
    return pl.pallas_call(
        functools.partial(_band_kernel, first_blocks=first_blocks, per_seq_a=per_seq_a, per_seq_b=per_seq_b,
                          tq=tq, half=half),
        grid=(dil, rows // tqb),
        in_specs=[
            mspec(0),
            hspec(1, 0), mspec(1), hspec(1, 1),
            hspec(2, 0), mspec(2), hspec(2, 1),
        ],
        out_specs=[
            pl.BlockSpec((None, tqb, width), lambda r, t: (r, t, 0)),
            pl.BlockSpec((None, tqb, LANES), lambda r, t: (r, t, 0)),
        ],
        out_shape=[jax.ShapeDtypeStruct((dil, rows, width), F32), jax.ShapeDtypeStruct((dil, rows, LANES), F32)],
        scratch_shapes=[pltpu.VMEM((tqb + 2 * half, width), BF16), pltpu.VMEM((tqb + 2 * half, width), BF16)],
        compiler_params=_params("arbitrary", "arbitrary"),
        name="band_attention",
    )(qkv, qkv, qkv, qkv, qkv, qkv, qkv)


def _merge_proj_kernel(*refs, dils):
    ng = len(dils)
    o_refs, l_refs = refs[:ng], refs[ng:2 * ng]
    w_ref, x_ref, out_ref, a_ref, os_ref, ls_ref = refs[2 * ng:]
    tm = x_ref.shape[0]

    heads = a_ref.shape[1] // HEAD_DIM
    for g, dil in enumerate(dils):
        for r in range(dil):
            ls_ref[g, pl.ds(r, tm // dil, stride=dil), :] = l_refs[g][r]
            for h in range(heads):
                os_ref[g, h, pl.ds(r, tm // dil, stride=dil), :] = o_refs[g][r, :, h * HEAD_DIM:(h + 1) * HEAD_DIM]

    lse = [ls_ref[g] for g in range(ng)]
    m = functools.reduce(jnp.maximum, lse)
    e = [jnp.exp(l - m) for l in lse]
    tot = functools.reduce(lambda p, q: p + q, e)
    wts = [eg / tot for eg in e]
    for h in range(heads):
        terms = [wts[g][:, h:h + 1] * os_ref[g, h] for g in range(ng)]
        a_ref[:, h * HEAD_DIM:(h + 1) * HEAD_DIM] = functools.reduce(lambda p, q: p + q, terms).astype(BF16)
    out_ref[...] = x_ref[...] + jnp.dot(a_ref[...], w_ref[...], preferred_element_type=F32)


def _merge_proj(outs, lses, w, x):
    n, d = x.shape
    width = w.shape[0]
    dils = tuple(o.shape[0] for o in outs)
    tm = min(512, n)
    row = lambda i: (i, 0)
    grp = lambda i: (0, i, 0)
    return pl.pallas_call(
        functools.partial(_merge_proj_kernel, dils=dils),
        grid=(n // tm,),
        in_specs=[pl.BlockSpec((dil, tm // dil, width), grp) for dil in dils]
        + [pl.BlockSpec((dil, tm // dil, LANES), grp) for dil in dils]
        + [pl.BlockSpec((width, d), lambda i: (0, 0)), pl.BlockSpec((tm, d), row)],
        out_specs=pl.BlockSpec((tm, d), row),
        out_shape=jax.ShapeDtypeStruct((n, d), F32),
        scratch_shapes=[
            pltpu.VMEM((tm, width), BF16),
            pltpu.VMEM((len(dils), width // HEAD_DIM, tm, HEAD_DIM), F32),
            pltpu.VMEM((len(dils), tm, LANES), F32),
        ],
        compiler_params=_params("arbitrary"),
        name="merge_proj",
    )(*outs, *lses, w, x)


def _store_chunk_rows(ref, x):
    rows, d = x.shape
    nc = d // LANES
    for c in range(nc):
        ref[pl.ds(c, rows, stride=nc), :] = x[:, c * LANES:(c + 1) * LANES]


def _store_packed_rows(ref, x):
    rows, d = x.shape
    nw = d // (2 * LANES)
    for c in range(nw):
        lo = x[:, (2 * c) * LANES:(2 * c + 1) * LANES].astype(BF16).astype(F32)
        hi = x[:, (2 * c + 1) * LANES:(2 * c + 2) * LANES].astype(BF16).astype(F32)
        word = (pltpu.bitcast(lo, jnp.uint32) >> 16) | (pltpu.bitcast(hi, jnp.uint32) & jnp.uint32(0xFFFF0000))
        ref[pl.ds(c, rows, stride=nw), :] = word


def _load_packed_rows(ref, lead, rows, nw, c):
    word = ref[lead, pl.ds(c, rows, stride=nw), :]
    lo = pltpu.bitcast(word << 16, F32)
    hi = pltpu.bitcast(word & jnp.uint32(0xFFFF0000), F32)
    return lo.astype(BF16), hi.astype(BF16)


def _router_kernel(x_ref, gain_ref, wr_ref, h_ref, idx_ref, gate_ref):
    h = _rms_rows(x_ref[...], gain_ref[...])
    _store_packed_rows(h_ref, h)
    logits = jnp.dot(h, wr_ref[...], preferred_element_type=F32, precision=lax.Precision.HIGHEST)
    lane = lax.broadcasted_iota(jnp.int32, logits.shape, 1)
    lane_f = lane.astype(F32)
    logits = jnp.where(lane < N_EXPERTS, logits, -jnp.inf)
    m1 = jnp.max(logits, axis=-1, keepdims=True)
    i1 = jnp.min(jnp.where(logits == m1, lane_f, float(LANES)), axis=-1, keepdims=True)
    rest = jnp.where(lane_f == i1, -jnp.inf, logits)
    m2 = jnp.max(rest, axis=-1, keepdims=True)
    i2 = jnp.min(jnp.where(rest == m2, lane_f, float(LANES)), axis=-1, keepdims=True)
    e = jnp.exp(m2 - m1)
    tot = 1.0 + e
    idx_ref[...] = jnp.where(lane == 0, i1, jnp.where(lane == 1, i2, 0.0)).astype(jnp.int32)
    gate_ref[...] = jnp.where(lane == 0, 1.0 / tot, jnp.where(lane == 1, e / tot, 0.0))


def _router(x, gain, w_router):
    n, d = x.shape
    tm = min(512, n)
    wr = jnp.zeros((d, LANES), F32).at[:, :N_EXPERTS].set(w_router.astype(F32))
    row = lambda i: (i, 0)
    return pl.pallas_call(
        _router_kernel,
        grid=(n // tm,),
        in_specs=[
            pl.BlockSpec((tm, d), row),
            pl.BlockSpec((1, d), lambda i: (0, 0)),
            pl.BlockSpec((d, LANES), lambda i: (0, 0)),
        ],
        out_specs=[pl.BlockSpec((tm * (d // (2 * LANES)), LANES), row), pl.BlockSpec((tm, LANES), row),
                   pl.BlockSpec((tm, LANES), row)],
        out_shape=[
            jax.ShapeDtypeStruct((n * (d // (2 * LANES)), LANES), jnp.uint32),
            jax.ShapeDtypeStruct((n, LANES), jnp.int32),
            jax.ShapeDtypeStruct((n, LANES), F32),
        ],
        compiler_params=_params("arbitrary"),
        name="router",
    )(x, gain.reshape(1, d), wr)


def _expert_kernel(bexp_ref, nused_ref, rtok_ref, h_hbm, wg_ref, wu_ref, wd_ref, y_ref,
                   xs_ref, xb_ref, acc_ref, sem):
    del bexp_ref
    i = pl.program_id(0)
    j = pl.program_id(1)
    tmb, d = xb_ref.shape
    nw = d // (2 * LANES)
    n_used = nused_ref[0]

    def row_copy(tok, slot, r):
        return pltpu.make_async_copy(h_hbm.at[pl.ds(pl.multiple_of(tok * nw, nw), nw), :],
                                     xs_ref.at[slot, pl.ds(pl.multiple_of(r * nw, nw), nw), :], sem.at[slot])

    def start_gather(blk, slot):
        def issue(r, carry):
            row_copy(rtok_ref[blk * tmb + r], slot, r).start()
            return carry
        lax.fori_loop(0, tmb, issue, 0, unroll=8)

    def wait_gather(slot):
        pltpu.make_async_copy(h_hbm.at[pl.ds(0, tmb * nw), :], xs_ref.at[slot], sem.at[slot]).wait()

    @pl.when(i < n_used)
    def _():
        slot = i % 2

        @pl.when(j == 0)
        def _():
            @pl.when(i == 0)
            def _():
                start_gather(0, 0)

            wait_gather(slot)

            @pl.when(i + 1 < n_used)
            def _():
                start_gather(i + 1, 1 - slot)

            for c in range(nw):
                lo, hi = _load_packed_rows(xs_ref, slot, tmb, nw, c)
                xb_ref[:, (2 * c) * LANES:(2 * c + 1) * LANES] = lo
                xb_ref[:, (2 * c + 1) * LANES:(2 * c + 2) * LANES] = hi
            acc_ref[...] = jnp.zeros_like(acc_ref)

        x = xb_ref[...]
        g = jnp.dot(x, wg_ref[0], preferred_element_type=F32)
        u = jnp.dot(x, wu_ref[0], preferred_element_type=F32)
        acc_ref[...] += jnp.dot(_silu_mul(g, u).astype(BF16), wd_ref[0], preferred_element_type=F32)

    last = j == pl.num_programs(1) - 1

    @pl.when(last & (i < n_used))
    def _():
        _store_chunk_rows(y_ref, acc_ref[...])

    @pl.when(last & (i >= n_used))
    def _():
        y_ref[...] = jnp.zeros_like(y_ref)


def _expert_ffn(h_rows, block_expert, n_used, row_tok, w_gate, w_up, w_down):
    d = w_gate.shape[1]
    f = w_gate.shape[2]
    tmb = MOE_BLOCK
    tf = MOE_F_BLOCK
    n_blocks = block_expert.shape[0]
    nf = f // tf
    assert f % tf == 0

    def fblk(i, j, nu):
        return jnp.where(i < nu[0], j, nf - 1)

    grid_spec = pltpu.PrefetchScalarGridSpec(
        num_scalar_prefetch=3,
        grid=(n_blocks, nf),
        in_specs=[
            pl.BlockSpec(memory_space=pl.ANY),
            pl.BlockSpec((1, d, tf), lambda i, j, be, nu, rt: (be[i], 0, fblk(i, j, nu))),
            pl.BlockSpec((1, d, tf), lambda i, j, be, nu, rt: (be[i], 0, fblk(i, j, nu))),
            pl.BlockSpec((1, tf, d), lambda i, j, be, nu, rt: (be[i], fblk(i, j, nu), 0)),
        ],
        out_specs=pl.BlockSpec((tmb * (d // LANES), LANES), lambda i, j, be, nu, rt: (i, 0)),
        scratch_shapes=[
            pltpu.VMEM((2, tmb * (d // (2 * LANES)), LANES), jnp.uint32),
            pltpu.VMEM((tmb, d), BF16),
            pltpu.VMEM((tmb, d), F32),
            pltpu.SemaphoreType.DMA((2,)),
        ],
    )
    return pl.pallas_call(
        _expert_kernel,
        grid_spec=grid_spec,
        out_shape=jax.ShapeDtypeStruct((n_blocks * tmb * (d // LANES), LANES), F32),
        compiler_params=_params("arbitrary", "arbitrary"),
        name="expert_ffn",
    )(block_expert, n_used, row_tok, h_rows, w_gate, w_up, w_down)


def _combine_kernel(dest_ref, ys_hbm, x_ref, gate_ref, oa_ref, ob_ref, buf_ref, sem, *, first_blocks):
    i = pl.program_id(0)
    nsteps = pl.num_programs(0)
    tc, d = x_ref.shape
    nc = d // LANES
    pitch = buf_ref.shape[1] // (2 * tc)

    def row_copy(row, slot, dst):
        return pltpu.make_async_copy(ys_hbm.at[pl.ds(pl.multiple_of(row * nc, nc), nc), :],
                                     buf_ref.at[slot, pl.ds(pl.multiple_of(dst * pitch, 8), nc), :], sem.at[slot])

    def start_gather(step, slot):
        def issue(r, carry):
            row_copy(dest_ref[step * (2 * tc) + r], slot, r).start()
            return carry
        lax.fori_loop(0, 2 * tc, issue, 0, unroll=8)

    def wait_gather(slot):
        pltpu.make_async_copy(ys_hbm.at[pl.ds(0, 2 * tc * nc), :], buf_ref.at[slot, pl.ds(0, 2 * tc * nc), :],
                              sem.at[slot]).wait()

    slot = i % 2

    @pl.when(i == 0)
    def _():
        start_gather(0, 0)

    wait_gather(slot)

    @pl.when(i + 1 < nsteps)
    def _():
        start_gather(i + 1, 1 - slot)

    g0 = gate_ref[:, 0:1]
    g1 = gate_ref[:, 1:2]

    def emit(o_ref):
        for c in range(nc):
            sl = slice(c * LANES, (c + 1) * LANES)
            y0 = buf_ref[slot, pl.ds(c, tc, stride=pitch), :]
            y1 = buf_ref[slot, pl.ds(tc * pitch + c, tc, stride=pitch), :]
            o_ref[:, sl] = x_ref[:, sl] + (y0 * g0 + y1 * g1)
    _pick_rows(i, first_blocks, oa_ref, ob_ref, emit)


def _combine(x, ys, dest, gate, n_first):
    n, d = x.shape
    tc = min(COMBINE_BLOCK, n_first, n - n_first)
    assert n_first % tc == 0 and (n - n_first) % tc == 0
    first_blocks = n_first // tc
    nc = d // LANES
    pitch = nc if (nc // 8) % 2 else nc + 8
    dest = dest.reshape(n // tc, tc, TOP_K).transpose(0, 2, 1).reshape(-1)
    grid_spec = pltpu.PrefetchScalarGridSpec(
        num_scalar_prefetch=1,
        grid=(n // tc,),
        in_specs=[
            pl.BlockSpec(memory_space=pl.ANY),
            pl.BlockSpec((tc, d), lambda i, dr: (i, 0)),
            pl.BlockSpec((tc, LANES), lambda i, dr: (i, 0)),
        ],
        out_specs=_two_batch_specs(tc, d, first_blocks),
        scratch_shapes=[pltpu.VMEM((2, 2 * tc * pitch, LANES), F32), pltpu.SemaphoreType.DMA((2,))],
    )
    return pl.pallas_call(
        functools.partial(_combine_kernel, first_blocks=first_blocks),
        grid_spec=grid_spec,
        out_shape=[jax.ShapeDtypeStruct((n_first, d), F32), jax.ShapeDtypeStruct((n - n_first, d), F32)],
        compiler_params=_params("arbitrary"),
        name="moe_combine",
    )(dest, ys, x, gate)


def _routing_tables(idx, tmb):
    n = idx.shape[0]
    nk = n * TOP_K
    e_flat = idx[:, :TOP_K].reshape(-1)
    onehot = (e_flat[:, None] == jnp.arange(N_EXPERTS, dtype=jnp.int32)[None, :]).astype(jnp.int32)
    csum = jnp.cumsum(onehot, axis=0)
    rank = jnp.sum(onehot * csum, axis=1) - 1
    counts = csum[-1]
    padded = (counts + tmb - 1) // tmb * tmb
    pad_end = jnp.cumsum(padded)
    pad_start = pad_end - padded
    dest = (jnp.sum(onehot * pad_start[None, :], axis=1) + rank).astype(jnp.int32)
    n_blocks = -(-nk // tmb) + N_EXPERTS
    block_expert = jnp.minimum(
        jnp.searchsorted(pad_end, jnp.arange(n_blocks, dtype=jnp.int32) * tmb, side="right"), N_EXPERTS - 1
    ).astype(jnp.int32)
    n_used = (pad_end[-1] // tmb).astype(jnp.int32).reshape(1)
    row_tok = jnp.zeros((n_blocks * tmb,), jnp.int32).at[dest].set(jnp.arange(nk, dtype=jnp.int32) // TOP_K)
    return dest, block_expert, n_used, row_tok


def _rope_tables(seq_len):
    half = HEAD_DIM // 2
    inv_freq = ROPE_THETA ** (-jnp.arange(half, dtype=F32) / half)
    ang = jnp.arange(seq_len, dtype=F32)[:, None] * inv_freq[None, :]
    cos, sin = jnp.cos(ang), jnp.sin(ang)
    return jnp.concatenate([cos, cos], axis=-1), jnp.concatenate([-sin, sin], axis=-1)


def kernel(x_prompt, x_sample, norm_mix, norm_ffn, na_w_qkv, na_q_gain, na_k_gain, na_rpb, na_w_o,
           da_w_qkv, da_q_gain, da_k_gain, da_w_o, ffn_w_gate, ffn_w_up, ffn_w_down,
           moe_w_router, moe_w_gate, moe_w_up, moe_w_down):
    d = x_prompt.shape[-1]
    trunks = []
    off = 0
    for xin in (x_prompt, x_sample):
        b, s, _ = xin.shape
        trunks.append((off, b, s))
        off += b * s
    n = off
    xa, xb = x_prompt.reshape(-1, d), x_sample.reshape(-1, d)
    depth = norm_mix.shape[0]
    assert depth == 2, "layer 0 reads the two request batches directly and layer 1 writes them separately"
    x = None

    for layer in range(depth):
        lj = layer // 2
        if layer % 2 == 0:
            heads = na_w_qkv.shape[2] // (3 * HEAD_DIM)
            tn = 1024
            per = heads * HEAD_DIM // tn
            hg = jnp.concatenate([
                jnp.broadcast_to(na_q_gain[lj], (per, HEAD_DIM)),
                jnp.broadcast_to(na_k_gain[lj], (per, HEAD_DIM)),
                jnp.ones((per, HEAD_DIM), F32),
            ]).reshape(3 * per, 1, HEAD_DIM)
            qkv = _qkv_project(xa, xb, norm_mix[layer], na_w_qkv[lj].astype(BF16), hg, 3 * per, 2 * per)
            tables = _na_bias_tables(na_rpb[lj])
            att = _na_attention(qkv, tables, trunks, heads)
            x = _proj_residual(att, na_w_o[lj].astype(BF16), xa, xb)
            ne, dm, fe = moe_w_gate.shape[1:]
            wide = (moe_w_gate[lj].reshape(ne * dm, fe), moe_w_up[lj].reshape(ne * dm, fe))
            steps = (n // min(FFN_TOKEN_BLOCK, n)) * (ffn_w_gate.shape[2] // FFN_F_BLOCK)
            shadow = _side_cast_fits(wide, SIDE_CAST_ROWS_WIDE, steps)
            x, *cast = _dense_ffn(x, norm_ffn[layer], ffn_w_gate[lj].astype(BF16), ffn_w_up[lj].astype(BF16),
                                  ffn_w_down[lj].astype(BF16), side_srcs=wide if shadow else (),
                                  side_rows=SIDE_CAST_ROWS_WIDE)
            moe_gate_bf, moe_up_bf = cast if shadow else [w.astype(BF16) for w in wide]
        else:
            groups = len(DIL_CONFIGS)
            heads = da_w_qkv.shape[2] // (groups * 3 * HEAD_DIM)
            dils = [dil for _, dil in DIL_CONFIGS]
            ones = jnp.ones((HEAD_DIM,), F32)
            hg = jnp.stack([t for g in range(groups) for t in (da_q_gain[lj, g], da_k_gain[lj, g], ones)])
            hg = hg.reshape(groups * 3, 1, HEAD_DIM)
            (o0, b0, s0), (o1, b1, s1) = trunks
            tq = min(QKV_DIL_TOKEN_BLOCK, n)
            assert s0 % tq == 0 and s1 % tq == 0 and o1 % tq == 0
            nb0, p0, p1 = o1 // tq, s0 // tq, s1 // tq
            pos_map = lambda i: jnp.where(i < nb0, i % p0, (i - nb0) % p1)
            smax = max(s0, s1)
            rope = [jnp.stack([t.reshape(smax // tq, tq // dil, dil, HEAD_DIM).transpose(0, 2, 1, 3)
                               .reshape(smax, HEAD_DIM) for dil in dils]) for t in _rope_tables(smax)]
            narrow = (moe_w_down[lj].reshape(ne * fe, dm),)
            shadow = _side_cast_fits(narrow, SIDE_CAST_ROWS_NARROW, (n // tq) * groups * 3)
            qkvs = list(_qkv_project_dilated(
                x, norm_mix[layer], da_w_qkv[lj].astype(BF16), hg, rope, pos_map, dils, heads, tq,
                side_srcs=narrow if shadow else (), side_rows=SIDE_CAST_ROWS_NARROW))
            moe_down_bf = qkvs.pop() if shadow else narrow[0].astype(BF16)
            outs, lses = [], []
            for g in range(groups):
                o_g, lse_g = _band_attention(qkvs[g], trunks, g)
                outs.append(o_g)
                lses.append(lse_g)
            x = _merge_proj(outs, lses, da_w_o[lj].astype(BF16), x)
            h, idx, gate = _router(x, norm_ffn[layer], moe_w_router[lj])
            dest, block_expert, n_used, row_tok = _routing_tables(idx, MOE_BLOCK)
            ys = _expert_ffn(h, block_expert, n_used, row_tok, moe_gate_bf.reshape(ne, dm, fe),
                             moe_up_bf.reshape(ne, dm, fe), moe_down_bf.reshape(ne, fe, dm))
            ya, yb = _combine(x, ys, dest, gate, xa.shape[0])

    return ya.reshape(x_prompt.shape), yb.reshape(x_sample.shape)
```

```python
import functools

import jax
import jax.numpy as jnp
from jax import lax
from jax.experimental import pallas as pl
from jax.experimental.pallas import tpu as pltpu

F32 = jnp.float32
BF16 = jnp.bfloat16

HEAD_DIM = 128
GRID_W = 64
NA_ROWS = 8
NA_COLS = 16
NA_GROUP = NA_ROWS // 2
NA_KEY_ROWS = 3 * NA_GROUP
NA_HEAD_BLOCK = 8
DIL_CONFIGS = ((128, 1), (512, 4), (2048, 16))
N_EXPERTS = 8
TOP_K = 2
ROPE_THETA = 10000.0
EPS = 1e-6
NEG_INF = -1e30
ATTN_SCALE = HEAD_DIM ** -0.5

LANES = 128
MXU_COLS = 256
VMEM_LIMIT = 52 * 1024 * 1024
FFN_VMEM_LIMIT = 57 * 1024 * 1024

TOKEN_BLOCK = 1024
PROJ_COL_BLOCK = 1024
SMALL_TOKEN_BLOCK = 512
QKV_DIL_TOKEN_BLOCK = 512
FFN_TOKEN_BLOCK = 1024
FFN_F_BLOCK = 256
BAND_Q_BLOCK = 512
BAND_Q_TILE = 128
MOE_BLOCK = 512
MOE_F_BLOCK = 1024
COMBINE_BLOCK = 256
SIDE_CAST_ROWS_WIDE = 128
SIDE_CAST_ROWS_NARROW = 512


def _params(*sem, vmem_limit=VMEM_LIMIT):
    return pltpu.CompilerParams(dimension_semantics=sem, vmem_limit_bytes=vmem_limit)


def _rms_rows(x, gain):
    ms = jnp.mean(x * x, axis=-1, keepdims=True)
    return x * lax.rsqrt(ms + EPS) * gain


def _pick_rows(i, first_blocks, xa_ref, xb_ref, use):
    @pl.when(i < first_blocks)
    def _():
        use(xa_ref)

    @pl.when(i >= first_blocks)
    def _():
        use(xb_ref)


def _two_batch_specs(tm, d, first_blocks, **kw):
    return [pl.BlockSpec((tm, d), lambda i, *_: (jnp.minimum(i, first_blocks - 1), 0), **kw),
            pl.BlockSpec((tm, d), lambda i, *_: (jnp.maximum(i - first_blocks, 0), 0), **kw)]


def _qkv_kernel(xa_ref, xb_ref, gain_ref, w_ref, hg_ref, o_ref, h_ref, *, first_blocks, v_every, v_from):
    j = pl.program_id(1)

    @pl.when(j == 0)
    def _():
        def norm(x_ref):
            h_ref[...] = _rms_rows(x_ref[...], gain_ref[...]).astype(BF16)
        _pick_rows(pl.program_id(0), first_blocks, xa_ref, xb_ref, norm)

    is_qk = (j % v_every) < v_from
    for c in range(w_ref.shape[1] // MXU_COLS):
        acc = jnp.dot(h_ref[...], w_ref[:, c * MXU_COLS:(c + 1) * MXU_COLS], preferred_element_type=F32)
        for h in range(MXU_COLS // HEAD_DIM):
            seg = acc[:, h * HEAD_DIM:(h + 1) * HEAD_DIM]
            lo = c * MXU_COLS + h * HEAD_DIM
            o_ref[:, lo:lo + HEAD_DIM] = jnp.where(is_qk, _rms_rows(seg, hg_ref[0]), seg).astype(o_ref.dtype)


def _qkv_project(xa, xb, gain, w, head_gain, v_every, v_from):
    d = xa.shape[1]
    n = xa.shape[0] + xb.shape[0]
    n_out = w.shape[1]
    tm = min(TOKEN_BLOCK, xa.shape[0], xb.shape[0])
    assert xa.shape[0] % tm == 0 and xb.shape[0] % tm == 0
    first_blocks = xa.shape[0] // tm
    tn = n_out // head_gain.shape[0]
    return pl.pallas_call(
        functools.partial(_qkv_kernel, first_blocks=first_blocks, v_every=v_every, v_from=v_from),
        grid=(n // tm, n_out // tn),
        in_specs=_two_batch_specs(tm, d, first_blocks, pipeline_mode=pl.Buffered(1)) + [
            pl.BlockSpec((1, d), lambda i, j: (0, 0)),
            pl.BlockSpec((d, tn), lambda i, j: (0, j)),
            pl.BlockSpec((1, 1, HEAD_DIM), lambda i, j: (j, 0, 0)),
        ],
        out_specs=pl.BlockSpec((tm, tn), lambda i, j: (i, j)),
        out_shape=jax.ShapeDtypeStruct((n, n_out), BF16),
        scratch_shapes=[pltpu.VMEM((tm, d), BF16)],
        compiler_params=_params("arbitrary", "arbitrary"),
        name="qkv_project",
    )(xa, xb, gain.reshape(1, d), w, head_gain)


def _qkv_dil_kernel(x_ref, gain_ref, w_ref, hg_ref, cos_ref, sin_ref, *rest, dils, n_side):
    ng = len(dils)
    srcs, out_refs, dsts = rest[:n_side], rest[n_side:n_side + ng], rest[n_side + ng:2 * n_side + ng]
    h_ref, stage_ref, *side = rest[2 * n_side + ng:]
    j = pl.program_id(1)
    if n_side:
        _side_cast_step(pl.program_id(0) * pl.num_programs(1) + j, srcs, dsts, *side)
    tm, d = x_ref.shape
    tn = w_ref.shape[1]

    @pl.when(j == 0)
    def _():
        y = _rms_rows(x_ref[...], gain_ref[...])
        for c in range(d // LANES):
            stage_ref[c] = y[:, c * LANES:(c + 1) * LANES]
        for g, dil in enumerate(dils):
            rows = tm // dil
            for r in range(dil):
                for c in range(d // LANES):
                    h_ref[g, r * rows:(r + 1) * rows, c * LANES:(c + 1) * LANES] = (
                        stage_ref[c, pl.ds(r, rows, stride=dil), :].astype(BF16))

    group = j // 3
    which = j % 3
    acc = jnp.dot(h_ref[group], w_ref[...], preferred_element_type=F32)

    for g, (dil, o_ref) in enumerate(zip(dils, out_refs)):
        rows = tm // dil

        @pl.when((group == g) & (which < 2))
        def _(dil=dil, o_ref=o_ref, rows=rows):
            for h in range(tn // HEAD_DIM):
                sl = slice(h * HEAD_DIM, (h + 1) * HEAD_DIM)
                y = _rms_rows(acc[:, sl], hg_ref[0])
                y = (y * cos_ref[...] + pltpu.roll(y, HEAD_DIM // 2, 1) * sin_ref[...]).astype(o_ref.dtype)
                for r in range(dil):
                    o_ref[r, :, sl] = y[r * rows:(r + 1) * rows]

        @pl.when((group == g) & (which == 2))
        def _(dil=dil, o_ref=o_ref, rows=rows):
            v = acc.astype(o_ref.dtype)
            for r in range(dil):
                o_ref[r] = v[r * rows:(r + 1) * rows]


def _qkv_project_dilated(x, gain, w, head_gain, rope_tabs, pos_map, dils, heads, tm, side_srcs=(), side_rows=0):
    n, d = x.shape
    n_out = w.shape[1]
    width = heads * HEAD_DIM
    ng = len(dils)
    assert n_out == ng * 3 * width and head_gain.shape[0] == ng * 3
    side_in, side_out, side_shapes, side_scratch = (
        _side_cast_specs(side_srcs, side_rows, (n // tm) * ng * 3) if side_srcs else ([], [], [], []))

    def out_map(g):
        def index(i, j):
            return (jnp.clip(j - 3 * g, 0, 2), 0, i, 0)
        return index

    rope_spec = pl.BlockSpec((None, tm, HEAD_DIM), lambda i, j: (j // 3, pos_map(i), 0))
    return pl.pallas_call(
        functools.partial(_qkv_dil_kernel, dils=tuple(dils), n_side=len(side_srcs)),
        grid=(n // tm, ng * 3),
        in_specs=[
            pl.BlockSpec((tm, d), lambda i, j: (i, 0)),
            pl.BlockSpec((1, d), lambda i, j: (0, 0)),
            pl.BlockSpec((d, width), lambda i, j: (0, j)),
            pl.BlockSpec((1, 1, HEAD_DIM), lambda i, j: (j, 0, 0)),
            rope_spec,
            rope_spec,
        ] + side_in,
        out_specs=[pl.BlockSpec((None, dil, tm // dil, width), out_map(g)) for g, dil in enumerate(dils)] + side_out,
        out_shape=[jax.ShapeDtypeStruct((3, dil, n // dil, width), BF16) for dil in dils] + side_shapes,
        scratch_shapes=[pltpu.VMEM((ng, tm, d), BF16), pltpu.VMEM((d // LANES, tm, LANES), F32)] + side_scratch,
        compiler_params=_params("arbitrary", "arbitrary"),
        name="qkv_project_dilated",
    )(x, gain.reshape(1, d), w, head_gain, *rope_tabs, *side_srcs)


def _na_bias_tables(rpb):
    half = NA_ROWS // 2
    assert NA_GROUP == half
    qi = jnp.arange(NA_GROUP)[:, None]
    kj = jnp.arange(NA_KEY_ROWS)[None, :]
    dr = jnp.broadcast_to(kj - NA_GROUP - qi, (3, NA_GROUP, NA_KEY_ROWS))
    rvalid = jnp.stack([
        jnp.broadcast_to(kj >= NA_GROUP, (NA_GROUP, NA_KEY_ROWS)),
        (kj - qi >= 0) & (kj - qi < NA_ROWS),
        jnp.broadcast_to(kj < NA_ROWS, (NA_GROUP, NA_KEY_ROWS)),
    ])
    c = jnp.arange(GRID_W)
    cs = jnp.clip(c - NA_COLS // 2, 0, GRID_W - NA_COLS)
    cvalid = (c[None, :] >= cs[:, None]) & (c[None, :] < cs[:, None] + NA_COLS)
    dc = jnp.clip(c[None, :] - c[:, None], 1 - NA_COLS, NA_COLS - 1)
    dri = jnp.clip(dr, 1 - NA_ROWS, NA_ROWS - 1) + NA_ROWS - 1
    sel_r = jax.nn.one_hot(dri, 2 * NA_ROWS - 1, dtype=F32)
    sel_c = jax.nn.one_hot(dc + NA_COLS - 1, 2 * NA_COLS - 1, dtype=F32)
    hp = lax.Precision.HIGHEST
    tmp = jnp.einsum("hab,qcb->haqc", rpb.astype(F32), sel_c, precision=hp)
    bias = jnp.einsum("haqc,vgka->vhgqkc", tmp, sel_r, precision=hp)
    valid = rvalid[:, None, :, None, :, None] & cvalid[None, None, None, :, None, :]
    bias = jnp.where(valid, bias, NEG_INF)
    h = rpb.shape[0]
    return bias.reshape(3, h, NA_GROUP * GRID_W, NA_KEY_ROWS * GRID_W)


def _na_kernel(q_ref, kp_ref, k_ref, kn_ref, vp_ref, v_ref, vn_ref, tab_ref, o_ref):
    gq = q_ref.shape[0]
    nt = (((1,), (1,)), ((), ()))
    for h in range(q_ref.shape[1] // HEAD_DIM):
        sl = slice(h * HEAD_DIM, (h + 1) * HEAD_DIM)
        q = q_ref[:, sl]
        s = jnp.concatenate(
            [lax.dot_general(q, kr[:, sl], nt, preferred_element_type=F32) for kr in (kp_ref, k_ref, kn_ref)],
            axis=1)
        s = s * ATTN_SCALE + tab_ref[0, h]
        m = jnp.max(s, axis=-1, keepdims=True)
        p = jnp.exp(s - m)
        den = jnp.sum(p, axis=-1, keepdims=True)
        pb = p.astype(BF16)
        o = (jnp.dot(pb[:, :gq], vp_ref[:, sl], preferred_element_type=F32)
             + jnp.dot(pb[:, gq:2 * gq], v_ref[:, sl], preferred_element_type=F32)
             + jnp.dot(pb[:, 2 * gq:], vn_ref[:, sl], preferred_element_type=F32))
        o_ref[:, sl] = (o / den).astype(o_ref.dtype)


def _two_batch_position(t, first_blocks, per_seq_a, per_seq_b):
    in_a = t < first_blocks
    per_seq = jnp.where(in_a, per_seq_a, per_seq_b)
    return jnp.where(in_a, t, t - first_blocks) % per_seq, per_seq


def _na_attention(qkv, tables, trunks, heads):
    n = qkv.shape[0]
    gq = NA_GROUP * GRID_W
    hb = NA_HEAD_BLOCK
    hblocks = heads // hb
    (off_a, b_a, s_a), (off_b, b_b, s_b) = trunks
    for (_, _, seq) in trunks:
        assert seq % gq == 0 and seq // GRID_W >= NA_ROWS
    assert heads % hb == 0 and off_a == 0 and off_b == b_a * s_a
    first_groups, gps_a, gps_b = off_b // gq, s_a // gq, s_b // gq

    def place(g):
        pos, per_seq = _two_batch_position(g, first_groups, gps_a, gps_b)
        return pos == 0, pos == per_seq - 1

    def variant(g):
        first, last = place(g)
        return jnp.where(first, 0, jnp.where(last, 2, 1))

    def rows_of(which, shift):
        def index(h, g):
            first, last = place(g)
            gg = g + jnp.where((shift < 0) & first, 0, jnp.where((shift > 0) & last, 0, shift))
            return (gg, which * hblocks + h)
        return index

    blk = lambda index: pl.BlockSpec((gq, hb * HEAD_DIM), index)
    return pl.pallas_call(
        _na_kernel,
        grid=(hblocks, n // gq),
        in_specs=[
            blk(rows_of(0, 0)),
            blk(rows_of(1, -1)), blk(rows_of(1, 0)), blk(rows_of(1, 1)),
            blk(rows_of(2, -1)), blk(rows_of(2, 0)), blk(rows_of(2, 1)),
            pl.BlockSpec((1, hb, gq, NA_KEY_ROWS * GRID_W), lambda h, g: (variant(g), h, 0, 0)),
        ],
        out_specs=blk(rows_of(0, 0)),
        out_shape=jax.ShapeDtypeStruct((n, heads * HEAD_DIM), BF16),
        compiler_params=_params("arbitrary", "arbitrary"),
        name="na_attention",
    )(qkv, qkv, qkv, qkv, qkv, qkv, qkv, tables)


def _proj_residual_kernel(a_ref, w_ref, xa_ref, xb_ref, o_ref, *, first_blocks):
    y = jnp.dot(a_ref[...], w_ref[...], preferred_element_type=F32)

    def add(x_ref):
        o_ref[...] = x_ref[...] + y
    _pick_rows(pl.program_id(1), first_blocks, xa_ref, xb_ref, add)


def _proj_residual(a, w, xa, xb):
    n, k = a.shape
    d = w.shape[1]
    tm = min(TOKEN_BLOCK, xa.shape[0], xb.shape[0])
    tn = min(PROJ_COL_BLOCK, d)
    assert xa.shape[0] % tm == 0 and xb.shape[0] % tm == 0 and xa.shape[0] + xb.shape[0] == n
    first_blocks = xa.shape[0] // tm
    return pl.pallas_call(
        functools.partial(_proj_residual_kernel, first_blocks=first_blocks),
        grid=(d // tn, n // tm),
        in_specs=[
            pl.BlockSpec((tm, k), lambda j, i: (i, 0)),
            pl.BlockSpec((k, tn), lambda j, i: (0, j)),
            pl.BlockSpec((tm, tn), lambda j, i: (jnp.minimum(i, first_blocks - 1), j)),
            pl.BlockSpec((tm, tn), lambda j, i: (jnp.maximum(i - first_blocks, 0), j)),
        ],
        out_specs=pl.BlockSpec((tm, tn), lambda j, i: (i, j)),
        out_shape=jax.ShapeDtypeStruct((n, d), F32),
        compiler_params=_params("arbitrary", "arbitrary"),
        name="proj_residual",
    )(a, w, xa, xb)


def _side_cast_step(step, srcs, dsts, in_buf, out_buf, in_sem, out_sem):
    rows = in_buf.shape[1]
    per = srcs[0].shape[0] // rows
    nb = per * len(srcs)
    slot = step % 2

    def block_copy(refs, b, make):
        for k, ref in enumerate(refs):
            @pl.when(b // per == k)
            def _(k=k, ref=ref):
                make(ref.at[pl.ds(pl.multiple_of((b - k * per) * rows, rows), rows), :]).start()

    def in_copy(s):
        return lambda hbm: pltpu.make_async_copy(hbm, in_buf.at[s], in_sem.at[s])

    def out_copy(s):
        return lambda hbm: pltpu.make_async_copy(out_buf.at[s], hbm, out_sem.at[s])

    @pl.when(step == 0)
    def _():
        block_copy(srcs, 0, in_copy(0))

    @pl.when((step >= 2) & (step - 2 < nb))
    def _():
        out_copy(slot)(dsts[0].at[pl.ds(0, rows), :]).wait()

    @pl.when(step < nb)
    def _():
        in_copy(slot)(srcs[0].at[pl.ds(0, rows), :]).wait()

        @pl.when(step + 1 < nb)
        def _():
            block_copy(srcs, step + 1, in_copy(1 - slot))

        out_buf[slot] = in_buf[slot].astype(BF16)
        block_copy(dsts, step, out_copy(slot))


def _side_cast_fits(srcs, rows, n_steps):
    r = srcs[0].shape[0]
    return r % rows == 0 and (r // rows) * len(srcs) + 2 <= n_steps


def _side_cast_specs(srcs, rows, n_steps):
    r, c = srcs[0].shape
    assert all(a.shape == (r, c) for a in srcs) and _side_cast_fits(srcs, rows, n_steps)
    any_spec = pl.BlockSpec(memory_space=pl.ANY)
    return ([any_spec] * len(srcs), [any_spec] * len(srcs),
            [jax.ShapeDtypeStruct((r, c), BF16) for _ in srcs],
            [pltpu.VMEM((2, rows, c), F32), pltpu.VMEM((2, rows, c), BF16),
             pltpu.SemaphoreType.DMA((2,)), pltpu.SemaphoreType.DMA((2,))])


def _silu_mul(g, u):
    return g * (1.0 / (1.0 + jnp.exp(-g))) * u


def _ffn_kernel(x_ref, gain_ref, wg_ref, wu_ref, wd_ref, *rest, n_side):
    srcs, (o_ref, *dsts), (h_ref, acc_ref, *side) = rest[:n_side], rest[n_side:2 * n_side + 1], rest[2 * n_side + 1:]
    j = pl.program_id(1)
    if n_side:
        _side_cast_step(pl.program_id(0) * pl.num_programs(1) + j, srcs, dsts, *side)

    @pl.when(j == 0)
    def _():
        h_ref[...] = _rms_rows(x_ref[...], gain_ref[...]).astype(BF16)
        acc_ref[...] = jnp.zeros_like(acc_ref)

    h = h_ref[...]
    g = jnp.dot(h, wg_ref[...], preferred_element_type=F32)
    u = jnp.dot(h, wu_ref[...], preferred_element_type=F32)
    acc_ref[...] += jnp.dot(_silu_mul(g, u).astype(BF16), wd_ref[...], preferred_element_type=F32)

    @pl.when(j == pl.num_programs(1) - 1)
    def _():
        o_ref[...] = x_ref[...] + acc_ref[...]


def _dense_ffn(x, gain, w_gate, w_up, w_down, side_srcs=(), side_rows=0):
    n, d = x.shape
    f = w_gate.shape[1]
    tm = min(FFN_TOKEN_BLOCK, n)
    tf = FFN_F_BLOCK
    assert f % tf == 0
    grid = (n // tm, f // tf)
    side_in, side_out, side_shapes, side_scratch = (
        _side_cast_specs(side_srcs, side_rows, grid[0] * grid[1]) if side_srcs else ([], [], [], []))
    return pl.pallas_call(
        functools.partial(_ffn_kernel, n_side=len(side_srcs)),
        grid=grid,
        in_specs=[
            pl.BlockSpec((tm, d), lambda i, j: (i, 0), pipeline_mode=pl.Buffered(1)),
            pl.BlockSpec((1, d), lambda i, j: (0, 0)),
            pl.BlockSpec((d, tf), lambda i, j: (0, j)),
            pl.BlockSpec((d, tf), lambda i, j: (0, j)),
            pl.BlockSpec((tf, d), lambda i, j: (j, 0)),
        ] + side_in,
        out_specs=[pl.BlockSpec((tm, d), lambda i, j: (i, 0))] + side_out,
        out_shape=[jax.ShapeDtypeStruct((n, d), F32)] + side_shapes,
        scratch_shapes=[pltpu.VMEM((tm, d), BF16), pltpu.VMEM((tm, d), F32)] + side_scratch,
        compiler_params=_params("arbitrary", "arbitrary", vmem_limit=FFN_VMEM_LIMIT),
        name="dense_ffn",
    )(x, gain.reshape(1, d), w_gate, w_up, w_down, *side_srcs)


def _band_kernel(q_ref, kp_ref, k_ref, kn_ref, vp_ref, v_ref, vn_ref, o_ref, lse_ref, kwin_ref, vwin_ref, *,
                 first_blocks, per_seq_a, per_seq_b, tq, half):
    tqb = q_ref.shape[0]
    kw = tq + 2 * half
    pos, per_seq = _two_batch_position(pl.program_id(1), first_blocks, per_seq_a, per_seq_b)
    q0 = pos * tqb
    length = per_seq * tqb
    for win, parts in ((kwin_ref, (kp_ref, k_ref, kn_ref)), (vwin_ref, (vp_ref, v_ref, vn_ref))):
        win[:half] = parts[0][...]
        win[half:half + tqb] = parts[1][...]
        win[half + tqb:] = parts[2][...]
    lane = lax.broadcasted_iota(jnp.int32, (tq, LANES), 1)

    def body(t, carry):
        r0 = pl.multiple_of(t * tq, tq)
        qpos = q0 + r0 + lax.broadcasted_iota(jnp.int32, (tq, kw), 0)
        kpos = q0 + r0 - half + lax.broadcasted_iota(jnp.int32, (tq, kw), 1)
        ok = (jnp.abs(qpos - kpos) <= half) & (kpos >= 0) & (kpos < length)
        lse_tile = jnp.zeros((tq, LANES), F32)
        for h in range(q_ref.shape[1] // HEAD_DIM):
            sl = slice(h * HEAD_DIM, (h + 1) * HEAD_DIM)
            q = q_ref[pl.ds(r0, tq), sl]
            k = kwin_ref[pl.ds(r0, kw), sl]
            v = vwin_ref[pl.ds(r0, kw), sl]
            s = lax.dot_general(q, k, (((1,), (1,)), ((), ())), preferred_element_type=F32) * ATTN_SCALE
            s = jnp.where(ok, s, NEG_INF)
            m = jnp.max(s, axis=-1, keepdims=True)
            p = jnp.exp(s - m)
            den = jnp.sum(p, axis=-1, keepdims=True)
            o = jnp.dot(p.astype(BF16), v, preferred_element_type=F32) / den
            o_ref[pl.ds(r0, tq), sl] = o
            lse_tile = jnp.where(lane == h, m + jnp.log(den), lse_tile)
        lse_ref[pl.ds(r0, tq), :] = lse_tile
        return carry

    lax.fori_loop(0, tqb // tq, body, 0)


def _band_attention(qkv, trunks, group):
    window, dil = DIL_CONFIGS[group]
    half = window // (2 * dil)
    _, _, rows, width = qkv.shape
    (off_a, b_a, s_a), (off_b, b_b, s_b) = trunks
    len_a, len_b = s_a // dil, s_b // dil
    assert s_a % dil == 0 and s_b % dil == 0 and qkv.shape[1] == dil and off_a == 0 and off_b == b_a * s_a
    tqb = min(BAND_Q_BLOCK, len_a, len_b)
    tq = min(BAND_Q_TILE, tqb)
    assert len_a % tqb == 0 and len_b % tqb == 0 and tqb % tq == 0 and tqb % half == 0
    hpb = tqb // half
    first_blocks, per_seq_a, per_seq_b = off_b // dil // tqb, len_a // tqb, len_b // tqb

    def halo(which, side):
        def index(r, t):
            pos, per_seq = _two_batch_position(t, first_blocks, per_seq_a, per_seq_b)
            edge = (pos == per_seq - 1) if side else (pos == 0)
            blk = t * hpb + (jnp.where(edge, hpb - 1, hpb) if side else jnp.where(edge, 0, -1))
            return (which, r, blk, 0)
        return index

    mspec = lambda which: pl.BlockSpec((None, None, tqb, width), lambda r, t: (which, r, t, 0))
    hspec = lambda which, side: pl.BlockSpec((None, None, half, width), halo(which, side))
    return pl.pallas_call(
        functools.partial(_band_kernel, first_blocks=first_blocks, per_seq_a=per_seq_a, per_seq_b=per_seq_b,
                          tq=tq, half=half),
        grid=(dil, rows // tqb),
        in_specs=[
            mspec(0),
            hspec(1, 0), mspec(1), hspec(1, 1),
            hspec(2, 0), mspec(2), hspec(2, 1),
        ],
        out_specs=[
            pl.BlockSpec((None, tqb, width), lambda r, t: (r, t, 0)),
            pl.BlockSpec((None, tqb, LANES), lambda r, t: (r, t, 0)),
        ],
        out_shape=[jax.ShapeDtypeStruct((dil, rows, width), F32), jax.ShapeDtypeStruct((dil, rows, LANES), F32)],
        scratch_shapes=[pltpu.VMEM((tqb + 2 * half, width), BF16), pltpu.VMEM((tqb + 2 * half, width), BF16)],
        compiler_params=_params("arbitrary", "arbitrary"),
        name="band_attention",
    )(qkv, qkv, qkv, qkv, qkv, qkv, qkv)


def _merge_proj_kernel(*refs, dils):
    ng = len(dils)
    o_refs, l_refs = refs[:ng], refs[ng:2 * ng]
    w_ref, x_ref, out_ref, a_ref, os_ref, ls_ref = refs[2 * ng:]
    tm = x_ref.shape[0]

    heads = a_ref.shape[1] // HEAD_DIM
    for g, dil in enumerate(dils):
        for r in range(dil):
            ls_ref[g, pl.ds(r, tm // dil, stride=dil), :] = l_refs[g][r]
            for h in range(heads):
                os_ref[g, h, pl.ds(r, tm // dil, stride=dil), :] = o_refs[g][r, :, h * HEAD_DIM:(h + 1) * HEAD_DIM]

    lse = [ls_ref[g] for g in range(ng)]
    m = functools.reduce(jnp.maximum, lse)
    e = [jnp.exp(l - m) for l in lse]
    tot = functools.reduce(lambda p, q: p + q, e)
    wts = [eg / tot for eg in e]
    for h in range(heads):
        terms = [wts[g][:, h:h + 1] * os_ref[g, h] for g in range(ng)]
        a_ref[:, h * HEAD_DIM:(h + 1) * HEAD_DIM] = functools.reduce(lambda p, q: p + q, terms).astype(BF16)
    out_ref[...] = x_ref[...] + jnp.dot(a_ref[...], w_ref[...], preferred_element_type=F32)


def _merge_proj(outs, lses, w, x):
    n, d = x.shape
    width = w.shape[0]
    dils = tuple(o.shape[0] for o in outs)
    tm = min(SMALL_TOKEN_BLOCK, n)
    row = lambda i: (i, 0)
    grp = lambda i: (0, i, 0)
    return pl.pallas_call(
        functools.partial(_merge_proj_kernel, dils=dils),
        grid=(n // tm,),
        in_specs=[pl.BlockSpec((dil, tm // dil, width), grp) for dil in dils]
        + [pl.BlockSpec((dil, tm // dil, LANES), grp) for dil in dils]
        + [pl.BlockSpec((width, d), lambda i: (0, 0)), pl.BlockSpec((tm, d), row)],
        out_specs=pl.BlockSpec((tm, d), row),
        out_shape=jax.ShapeDtypeStruct((n, d), F32),
        scratch_shapes=[
            pltpu.VMEM((tm, width), BF16),
            pltpu.VMEM((len(dils), width // HEAD_DIM, tm, HEAD_DIM), F32),
            pltpu.VMEM((len(dils), tm, LANES), F32),
        ],
        compiler_params=_params("arbitrary"),
        name="merge_proj",
    )(*outs, *lses, w, x)


def _store_chunk_rows(ref, x):
    rows, d = x.shape
    nc = d // LANES
    for c in range(nc):
        ref[pl.ds(c, rows, stride=nc), :] = x[:, c * LANES:(c + 1) * LANES]


def _store_packed_rows(ref, x):
    rows, d = x.shape
    nw = d // (2 * LANES)
    for c in range(nw):
        lo = x[:, (2 * c) * LANES:(2 * c + 1) * LANES].astype(BF16).astype(F32)
        hi = x[:, (2 * c + 1) * LANES:(2 * c + 2) * LANES].astype(BF16).astype(F32)
        word = (pltpu.bitcast(lo, jnp.uint32) >> 16) | (pltpu.bitcast(hi, jnp.uint32) & jnp.uint32(0xFFFF0000))
        ref[pl.ds(c, rows, stride=nw), :] = word


def _load_packed_rows(ref, lead, rows, nw, c):
    word = ref[lead, pl.ds(c, rows, stride=nw), :]
    lo = pltpu.bitcast(word << 16, F32)
    hi = pltpu.bitcast(word & jnp.uint32(0xFFFF0000), F32)
    return lo.astype(BF16), hi.astype(BF16)


def _router_kernel(x_ref, gain_ref, wr_ref, h_ref, idx_ref, gate_ref):
    h = _rms_rows(x_ref[...], gain_ref[...])
    _store_packed_rows(h_ref, h)
    logits = jnp.dot(h, wr_ref[...], preferred_element_type=F32, precision=lax.Precision.HIGHEST)
    lane = lax.broadcasted_iota(jnp.int32, logits.shape, 1)
    lane_f = lane.astype(F32)
    logits = jnp.where(lane < N_EXPERTS, logits, -jnp.inf)
    m1 = jnp.max(logits, axis=-1, keepdims=True)
    i1 = jnp.min(jnp.where(logits == m1, lane_f, float(LANES)), axis=-1, keepdims=True)
    rest = jnp.where(lane_f == i1, -jnp.inf, logits)
    m2 = jnp.max(rest, axis=-1, keepdims=True)
    i2 = jnp.min(jnp.where(rest == m2, lane_f, float(LANES)), axis=-1, keepdims=True)
    e = jnp.exp(m2 - m1)
    tot = 1.0 + e
    idx_ref[...] = jnp.where(lane == 0, i1, jnp.where(lane == 1, i2, 0.0)).astype(jnp.int32)
    gate_ref[...] = jnp.where(lane == 0, 1.0 / tot, jnp.where(lane == 1, e / tot, 0.0))


def _router(x, gain, w_router):
    n, d = x.shape
    tm = min(SMALL_TOKEN_BLOCK, n)
    wr = jnp.zeros((d, LANES), F32).at[:, :N_EXPERTS].set(w_router.astype(F32))
    row = lambda i: (i, 0)
    return pl.pallas_call(
        _router_kernel,
        grid=(n // tm,),
        in_specs=[
            pl.BlockSpec((tm, d), row),
            pl.BlockSpec((1, d), lambda i: (0, 0)),
            pl.BlockSpec((d, LANES), lambda i: (0, 0)),
        ],
        out_specs=[pl.BlockSpec((tm * (d // (2 * LANES)), LANES), row), pl.BlockSpec((tm, LANES), row),
                   pl.BlockSpec((tm, LANES), row)],
        out_shape=[
            jax.ShapeDtypeStruct((n * (d // (2 * LANES)), LANES), jnp.uint32),
            jax.ShapeDtypeStruct((n, LANES), jnp.int32),
            jax.ShapeDtypeStruct((n, LANES), F32),
        ],
        compiler_params=_params("arbitrary"),
        name="router",
    )(x, gain.reshape(1, d), wr)


def _expert_kernel(bexp_ref, brows_ref, nused_ref, rtok_ref, h_hbm, wg_ref, wu_ref, wd_ref, y_ref,
                   xs_ref, xb_ref, acc_ref, sem):
    del bexp_ref
    i = pl.program_id(0)
    j = pl.program_id(1)
    tmb, d = xb_ref.shape
    nw = d // (2 * LANES)
    n_used = nused_ref[0]

    def row_copy(tok, slot, r):
        return pltpu.make_async_copy(h_hbm.at[pl.ds(pl.multiple_of(tok * nw, nw), nw), :],
                                     xs_ref.at[slot, pl.ds(pl.multiple_of(r * nw, nw), nw), :], sem.at[slot])

    def start_gather(blk, slot):
        def issue(r, carry):
            row_copy(rtok_ref[blk * tmb + r], slot, r).start()
            return carry
        lax.fori_loop(0, tmb, issue, 0, unroll=8)

    def wait_gather(slot):
        pltpu.make_async_copy(h_hbm.at[pl.ds(0, tmb * nw), :], xs_ref.at[slot], sem.at[slot]).wait()

    @pl.when(i < n_used)
    def _():
        slot = i % 2

        @pl.when(j == 0)
        def _():
            @pl.when(i == 0)
            def _():
                start_gather(0, 0)

            wait_gather(slot)

            @pl.when(i + 1 < n_used)
            def _():
                start_gather(i + 1, 1 - slot)

            for c in range(nw):
                lo, hi = _load_packed_rows(xs_ref, slot, tmb, nw, c)
                xb_ref[:, (2 * c) * LANES:(2 * c + 1) * LANES] = lo
                xb_ref[:, (2 * c + 1) * LANES:(2 * c + 2) * LANES] = hi
            acc_ref[...] = jnp.zeros_like(acc_ref)

        def swiglu_rows(rows):
            x = xb_ref[:rows]
            g = jnp.dot(x, wg_ref[0], preferred_element_type=F32)
            u = jnp.dot(x, wu_ref[0], preferred_element_type=F32)
            acc_ref[:rows] += jnp.dot(_silu_mul(g, u).astype(BF16), wd_ref[0], preferred_element_type=F32)

        half_full = brows_ref[i] <= tmb // 2

        @pl.when(half_full)
        def _():
            swiglu_rows(tmb // 2)

        @pl.when(jnp.logical_not(half_full))
        def _():
            swiglu_rows(tmb)

    last = j == pl.num_programs(1) - 1

    @pl.when(last & (i < n_used))
    def _():
        _store_chunk_rows(y_ref, acc_ref[...])

    @pl.when(last & (i >= n_used))
    def _():
        y_ref[...] = jnp.zeros_like(y_ref)


def _expert_ffn(h_rows, block_expert, block_rows, n_used, row_tok, w_gate, w_up, w_down):
    d = w_gate.shape[1]
    f = w_gate.shape[2]
    tmb = MOE_BLOCK
    tf = MOE_F_BLOCK
    n_blocks = block_expert.shape[0]
    nf = f // tf
    assert f % tf == 0

    def fblk(i, j, nu):
        return jnp.where(i < nu[0], j, nf - 1)

    grid_spec = pltpu.PrefetchScalarGridSpec(
        num_scalar_prefetch=4,
        grid=(n_blocks, nf),
        in_specs=[
            pl.BlockSpec(memory_space=pl.ANY),
            pl.BlockSpec((1, d, tf), lambda i, j, be, br, nu, rt: (be[i], 0, fblk(i, j, nu))),
            pl.BlockSpec((1, d, tf), lambda i, j, be, br, nu, rt: (be[i], 0, fblk(i, j, nu))),
            pl.BlockSpec((1, tf, d), lambda i, j, be, br, nu, rt: (be[i], fblk(i, j, nu), 0)),
        ],
        out_specs=pl.BlockSpec((tmb * (d // LANES), LANES), lambda i, j, be, br, nu, rt: (i, 0)),
        scratch_shapes=[
            pltpu.VMEM((2, tmb * (d // (2 * LANES)), LANES), jnp.uint32),
            pltpu.VMEM((tmb, d), BF16),
            pltpu.VMEM((tmb, d), F32),
            pltpu.SemaphoreType.DMA((2,)),
        ],
    )
    return pl.pallas_call(
        _expert_kernel,
        grid_spec=grid_spec,
        out_shape=jax.ShapeDtypeStruct((n_blocks * tmb * (d // LANES), LANES), F32),
        compiler_params=_params("arbitrary", "arbitrary"),
        name="expert_ffn",
    )(block_expert, block_rows, n_used, row_tok, h_rows, w_gate, w_up, w_down)


def _combine_kernel(dest_ref, ys_hbm, x_ref, gate_ref, oa_ref, ob_ref, buf_ref, sem, *, first_blocks):
    i = pl.program_id(0)
    nsteps = pl.num_programs(0)
    tc, d = x_ref.shape
    nc = d // LANES
    pitch = buf_ref.shape[1] // (2 * tc)

    def row_copy(row, slot, dst):
        return pltpu.make_async_copy(ys_hbm.at[pl.ds(pl.multiple_of(row * nc, nc), nc), :],
                                     buf_ref.at[slot, pl.ds(pl.multiple_of(dst * pitch, 8), nc), :], sem.at[slot])

    def start_gather(step, slot):
        def issue(r, carry):
            row_copy(dest_ref[step * (2 * tc) + r], slot, r).start()
            return carry
        lax.fori_loop(0, 2 * tc, issue, 0, unroll=8)

    def wait_gather(slot):
        pltpu.make_async_copy(ys_hbm.at[pl.ds(0, 2 * tc * nc), :], buf_ref.at[slot, pl.ds(0, 2 * tc * nc), :],
                              sem.at[slot]).wait()

    slot = i % 2

    @pl.when(i == 0)
    def _():
        start_gather(0, 0)

    wait_gather(slot)

    @pl.when(i + 1 < nsteps)
    def _():
        start_gather(i + 1, 1 - slot)

    g0 = gate_ref[:, 0:1]
    g1 = gate_ref[:, 1:2]

    def emit(o_ref):
        for c in range(nc):
            sl = slice(c * LANES, (c + 1) * LANES)
            y0 = buf_ref[slot, pl.ds(c, tc, stride=pitch), :]
            y1 = buf_ref[slot, pl.ds(tc * pitch + c, tc, stride=pitch), :]
            o_ref[:, sl] = x_ref[:, sl] + (y0 * g0 + y1 * g1)
    _pick_rows(i, first_blocks, oa_ref, ob_ref, emit)


def _combine(x, ys, dest, gate, n_first):
    n, d = x.shape
    tc = min(COMBINE_BLOCK, n_first, n - n_first)
    assert n_first % tc == 0 and (n - n_first) % tc == 0
    first_blocks = n_first // tc
    nc = d // LANES
    pitch = nc if (nc // 8) % 2 else nc + 8
    dest = dest.reshape(n // tc, tc, TOP_K).transpose(0, 2, 1).reshape(-1)
    grid_spec = pltpu.PrefetchScalarGridSpec(
        num_scalar_prefetch=1,
        grid=(n // tc,),
        in_specs=[
            pl.BlockSpec(memory_space=pl.ANY),
            pl.BlockSpec((tc, d), lambda i, dr: (i, 0)),
            pl.BlockSpec((tc, LANES), lambda i, dr: (i, 0)),
        ],
        out_specs=_two_batch_specs(tc, d, first_blocks),
        scratch_shapes=[pltpu.VMEM((2, 2 * tc * pitch, LANES), F32), pltpu.SemaphoreType.DMA((2,))],
    )
    return pl.pallas_call(
        functools.partial(_combine_kernel, first_blocks=first_blocks),
        grid_spec=grid_spec,
        out_shape=[jax.ShapeDtypeStruct((n_first, d), F32), jax.ShapeDtypeStruct((n - n_first, d), F32)],
        compiler_params=_params("arbitrary"),
        name="moe_combine",
    )(dest, ys, x, gate)


def _routing_tables(idx, tmb):
    n = idx.shape[0]
    nk = n * TOP_K
    e_flat = idx[:, :TOP_K].reshape(-1)
    onehot = (e_flat[:, None] == jnp.arange(N_EXPERTS, dtype=jnp.int32)[None, :]).astype(jnp.int32)
    csum = jnp.cumsum(onehot, axis=0)
    rank = jnp.sum(onehot * csum, axis=1) - 1
    counts = csum[-1]
    padded = (counts + tmb - 1) // tmb * tmb
    pad_end = jnp.cumsum(padded)
    pad_start = pad_end - padded
    dest = (jnp.sum(onehot * pad_start[None, :], axis=1) + rank).astype(jnp.int32)
    n_blocks = -(-nk // tmb) + N_EXPERTS
    block_expert = jnp.minimum(
        jnp.searchsorted(pad_end, jnp.arange(n_blocks, dtype=jnp.int32) * tmb, side="right"), N_EXPERTS - 1
    ).astype(jnp.int32)
    n_used = (pad_end[-1] // tmb).astype(jnp.int32).reshape(1)
    block_lo = jnp.arange(n_blocks, dtype=jnp.int32) * tmb - pad_start[block_expert]
    block_rows = jnp.clip(counts[block_expert] - block_lo, 0, tmb).astype(jnp.int32)
    row_tok = jnp.zeros((n_blocks * tmb,), jnp.int32).at[dest].set(jnp.arange(nk, dtype=jnp.int32) // TOP_K)
    return dest, block_expert, block_rows, n_used, row_tok


def _rope_tables(seq_len):
    half = HEAD_DIM // 2
    inv_freq = ROPE_THETA ** (-jnp.arange(half, dtype=F32) / half)
    ang = jnp.arange(seq_len, dtype=F32)[:, None] * inv_freq[None, :]
    cos, sin = jnp.cos(ang), jnp.sin(ang)
    return jnp.concatenate([cos, cos], axis=-1), jnp.concatenate([-sin, sin], axis=-1)


def kernel(x_prompt, x_sample, norm_mix, norm_ffn, na_w_qkv, na_q_gain, na_k_gain, na_rpb, na_w_o,
           da_w_qkv, da_q_gain, da_k_gain, da_w_o, ffn_w_gate, ffn_w_up, ffn_w_down,
           moe_w_router, moe_w_gate, moe_w_up, moe_w_down):
    d = x_prompt.shape[-1]
    trunks = []
    off = 0
    for xin in (x_prompt, x_sample):
        b, s, _ = xin.shape
        trunks.append((off, b, s))
        off += b * s
    n = off
    xa, xb = x_prompt.reshape(-1, d), x_sample.reshape(-1, d)
    depth = norm_mix.shape[0]
    assert depth == 2, "layer 0 reads the two request batches directly and layer 1 writes them separately"
    x = None

    for layer in range(depth):
        lj = layer // 2
        if layer % 2 == 0:
            heads = na_w_qkv.shape[2] // (3 * HEAD_DIM)
            per = heads * HEAD_DIM // PROJ_COL_BLOCK
            hg = jnp.concatenate([
                jnp.broadcast_to(na_q_gain[lj], (per, HEAD_DIM)),
                jnp.broadcast_to(na_k_gain[lj], (per, HEAD_DIM)),
                jnp.ones((per, HEAD_DIM), F32),
            ]).reshape(3 * per, 1, HEAD_DIM)
            qkv = _qkv_project(xa, xb, norm_mix[layer], na_w_qkv[lj].astype(BF16), hg, 3 * per, 2 * per)
            tables = _na_bias_tables(na_rpb[lj])
            att = _na_attention(qkv, tables, trunks, heads)
            x = _proj_residual(att, na_w_o[lj].astype(BF16), xa, xb)
            ne, dm, fe = moe_w_gate.shape[1:]
            wide = (moe_w_gate[lj].reshape(ne * dm, fe), moe_w_up[lj].reshape(ne * dm, fe))
            steps = (n // min(FFN_TOKEN_BLOCK, n)) * (ffn_w_gate.shape[2] // FFN_F_BLOCK)
            shadow = _side_cast_fits(wide, SIDE_CAST_ROWS_WIDE, steps)
            x, *cast = _dense_ffn(x, norm_ffn[layer], ffn_w_gate[lj].astype(BF16), ffn_w_up[lj].astype(BF16),
                                  ffn_w_down[lj].astype(BF16), side_srcs=wide if shadow else (),
                                  side_rows=SIDE_CAST_ROWS_WIDE)
            moe_gate_bf, moe_up_bf = cast if shadow else [w.astype(BF16) for w in wide]
        else:
            groups = len(DIL_CONFIGS)
            heads = da_w_qkv.shape[2] // (groups * 3 * HEAD_DIM)
            dils = [dil for _, dil in DIL_CONFIGS]
            ones = jnp.ones((HEAD_DIM,), F32)
            hg = jnp.stack([t for g in range(groups) for t in (da_q_gain[lj, g], da_k_gain[lj, g], ones)])
            hg = hg.reshape(groups * 3, 1, HEAD_DIM)
            (o0, b0, s0), (o1, b1, s1) = trunks
            tq = min(QKV_DIL_TOKEN_BLOCK, n)
            assert s0 % tq == 0 and s1 % tq == 0 and o1 % tq == 0
            nb0, p0, p1 = o1 // tq, s0 // tq, s1 // tq
            pos_map = lambda i: jnp.where(i < nb0, i % p0, (i - nb0) % p1)
            smax = max(s0, s1)
            rope = [jnp.stack([t.reshape(smax // tq, tq // dil, dil, HEAD_DIM).transpose(0, 2, 1, 3)
                               .reshape(smax, HEAD_DIM) for dil in dils]) for t in _rope_tables(smax)]
            narrow = (moe_w_down[lj].reshape(ne * fe, dm),)
            shadow = _side_cast_fits(narrow, SIDE_CAST_ROWS_NARROW, (n // tq) * groups * 3)
            qkvs = list(_qkv_project_dilated(
                x, norm_mix[layer], da_w_qkv[lj].astype(BF16), hg, rope, pos_map, dils, heads, tq,
                side_srcs=narrow if shadow else (), side_rows=SIDE_CAST_ROWS_NARROW))
            moe_down_bf = qkvs.pop() if shadow else narrow[0].astype(BF16)
            outs, lses = [], []
            for g in range(groups):
                o_g, lse_g = _band_attention(qkvs[g], trunks, g)
                outs.append(o_g)
                lses.append(lse_g)
            x = _merge_proj(outs, lses, da_w_o[lj].astype(BF16), x)
            h, idx, gate = _router(x, norm_ffn[layer], moe_w_router[lj])
            dest, block_expert, block_rows, n_used, row_tok = _routing_tables(idx, MOE_BLOCK)
            ys = _expert_ffn(h, block_expert, block_rows, n_used, row_tok, moe_gate_bf.reshape(ne, dm, fe),
                             moe_up_bf.reshape(ne, dm, fe), moe_down_bf.reshape(ne, fe, dm))
            ya, yb = _combine(x, ys, dest, gate, xa.shape[0])

    return ya.reshape(x_prompt.shape), yb.reshape(x_sample.shape)
```

```python
import functools

import jax
import jax.numpy as jnp
from jax import lax
from jax.experimental import pallas as pl
from jax.experimental.pallas import tpu as pltpu

F32 = jnp.float32
BF16 = jnp.bfloat16

HEAD_DIM = 128
GRID_W = 64
NA_ROWS = 8
NA_COLS = 16
NA_GROUP = NA_ROWS // 2
NA_KEY_ROWS = 3 * NA_GROUP
NA_HEAD_BLOCK = 8
DIL_CONFIGS = ((128, 1), (512, 4), (2048, 16))
N_EXPERTS = 8
TOP_K = 2
ROPE_THETA = 10000.0
EPS = 1e-6
NEG_INF = -1e30
ATTN_SCALE = HEAD_DIM ** -0.5

LANES = 128
MXU_COLS = 256
VMEM_LIMIT = 52 * 1024 * 1024
FFN_VMEM_LIMIT = 57 * 1024 * 1024

TOKEN_BLOCK = 1024
PROJ_COL_BLOCK = 1024
SMALL_TOKEN_BLOCK = 512
QKV_DIL_TOKEN_BLOCK = 512
FFN_TOKEN_BLOCK = 1024
FFN_F_BLOCK = 256
BAND_Q_BLOCK = 512
BAND_Q_TILE = 128
MOE_BLOCK = 512
MOE_F_BLOCK = 1024
COMBINE_BLOCK = 256
SIDE_CAST_ROWS_WIDE = 128
SIDE_CAST_ROWS_NARROW = 512


def _params(*sem, vmem_limit=VMEM_LIMIT):
    return pltpu.CompilerParams(dimension_semantics=sem, vmem_limit_bytes=vmem_limit)


def _rms_rows(x, gain):
    ms = jnp.mean(x * x, axis=-1, keepdims=True)
    return x * lax.rsqrt(ms + EPS) * gain


def _pick_rows(i, first_blocks, xa_ref, xb_ref, use):
    @pl.when(i < first_blocks)
    def _():
        use(xa_ref)

    @pl.when(i >= first_blocks)
    def _():
        use(xb_ref)


def _two_batch_specs(tm, d, first_blocks, **kw):
    return [pl.BlockSpec((tm, d), lambda i, *_: (jnp.minimum(i, first_blocks - 1), 0), **kw),
            pl.BlockSpec((tm, d), lambda i, *_: (jnp.maximum(i - first_blocks, 0), 0), **kw)]


def _qkv_kernel(xa_ref, xb_ref, gain_ref, w_ref, hg_ref, o_ref, h_ref, *, first_blocks, v_every, v_from):
    j = pl.program_id(1)

    @pl.when(j == 0)
    def _():
        def norm(x_ref):
            h_ref[...] = _rms_rows(x_ref[...], gain_ref[...]).astype(BF16)
        _pick_rows(pl.program_id(0), first_blocks, xa_ref, xb_ref, norm)

    is_qk = (j % v_every) < v_from
    for c in range(w_ref.shape[1] // MXU_COLS):
        acc = jnp.dot(h_ref[...], w_ref[:, c * MXU_COLS:(c + 1) * MXU_COLS], preferred_element_type=F32)
        for h in range(MXU_COLS // HEAD_DIM):
            seg = acc[:, h * HEAD_DIM:(h + 1) * HEAD_DIM]
            lo = c * MXU_COLS + h * HEAD_DIM
            o_ref[:, lo:lo + HEAD_DIM] = jnp.where(is_qk, _rms_rows(seg, hg_ref[0]), seg).astype(o_ref.dtype)


def _qkv_project(xa, xb, gain, w, head_gain, v_every, v_from):
    d = xa.shape[1]
    n = xa.shape[0] + xb.shape[0]
    n_out = w.shape[1]
    tm = min(TOKEN_BLOCK, xa.shape[0], xb.shape[0])
    assert xa.shape[0] % tm == 0 and xb.shape[0] % tm == 0
    first_blocks = xa.shape[0] // tm
    tn = n_out // head_gain.shape[0]
    return pl.pallas_call(
        functools.partial(_qkv_kernel, first_blocks=first_blocks, v_every=v_every, v_from=v_from),
        grid=(n // tm, n_out // tn),
        in_specs=_two_batch_specs(tm, d, first_blocks, pipeline_mode=pl.Buffered(1)) + [
            pl.BlockSpec((1, d), lambda i, j: (0, 0)),
            pl.BlockSpec((d, tn), lambda i, j: (0, j)),
            pl.BlockSpec((1, 1, HEAD_DIM), lambda i, j: (j, 0, 0)),
        ],
        out_specs=pl.BlockSpec((tm, tn), lambda i, j: (i, j)),
        out_shape=jax.ShapeDtypeStruct((n, n_out), BF16),
        scratch_shapes=[pltpu.VMEM((tm, d), BF16)],
        compiler_params=_params("arbitrary", "arbitrary"),
        name="qkv_project",
    )(xa, xb, gain.reshape(1, d), w, head_gain)


def _qkv_dil_kernel(x_ref, gain_ref, w_ref, hg_ref, cos_ref, sin_ref, *rest, dils, n_side):
    ng = len(dils)
    srcs, out_refs, dsts = rest[:n_side], rest[n_side:n_side + ng], rest[n_side + ng:2 * n_side + ng]
    h_ref, stage_ref, *side = rest[2 * n_side + ng:]
    j = pl.program_id(1)
    if n_side:
        _side_cast_step(pl.program_id(0) * pl.num_programs(1) + j, srcs, dsts, *side)
    tm, d = x_ref.shape
    tn = w_ref.shape[1]

    @pl.when(j == 0)
    def _():
        y = _rms_rows(x_ref[...], gain_ref[...])
        for c in range(d // LANES):
            stage_ref[c] = y[:, c * LANES:(c + 1) * LANES]
        for g, dil in enumerate(dils):
            rows = tm // dil
            for r in range(dil):
                for c in range(d // LANES):
                    h_ref[g, r * rows:(r + 1) * rows, c * LANES:(c + 1) * LANES] = (
                        stage_ref[c, pl.ds(r, rows, stride=dil), :].astype(BF16))

    group = j // 3
    which = j % 3
    acc = jnp.dot(h_ref[group], w_ref[...], preferred_element_type=F32)

    for g, (dil, o_ref) in enumerate(zip(dils, out_refs)):
        rows = tm // dil

        @pl.when((group == g) & (which < 2))
        def _(dil=dil, o_ref=o_ref, rows=rows):
            for h in range(tn // HEAD_DIM):
                sl = slice(h * HEAD_DIM, (h + 1) * HEAD_DIM)
                y = _rms_rows(acc[:, sl], hg_ref[0])
                y = (y * cos_ref[...] + pltpu.roll(y, HEAD_DIM // 2, 1) * sin_ref[...]).astype(o_ref.dtype)
                for r in range(dil):
                    o_ref[r, :, sl] = y[r * rows:(r + 1) * rows]

        @pl.when((group == g) & (which == 2))
        def _(dil=dil, o_ref=o_ref, rows=rows):
            v = acc.astype(o_ref.dtype)
            for r in range(dil):
                o_ref[r] = v[r * rows:(r + 1) * rows]


def _qkv_project_dilated(x, gain, w, head_gain, rope_tabs, pos_map, dils, heads, tm, side_srcs=(), side_rows=0):
    n, d = x.shape
    n_out = w.shape[1]
    width = heads * HEAD_DIM
    ng = len(dils)
    assert n_out == ng * 3 * width and head_gain.shape[0] == ng * 3
    side_in, side_out, side_shapes, side_scratch = (
        _side_cast_specs(side_srcs, side_rows, (n // tm) * ng * 3) if side_srcs else ([], [], [], []))

    def out_map(g):
        def index(i, j):
            return (jnp.clip(j - 3 * g, 0, 2), 0, i, 0)
        return index

    rope_spec = pl.BlockSpec((None, tm, HEAD_DIM), lambda i, j: (j // 3, pos_map(i), 0))
    return pl.pallas_call(
        functools.partial(_qkv_dil_kernel, dils=tuple(dils), n_side=len(side_srcs)),
        grid=(n // tm, ng * 3),
        in_specs=[
            pl.BlockSpec((tm, d), lambda i, j: (i, 0)),
            pl.BlockSpec((1, d), lambda i, j: (0, 0)),
            pl.BlockSpec((d, width), lambda i, j: (0, j)),
            pl.BlockSpec((1, 1, HEAD_DIM), lambda i, j: (j, 0, 0)),
            rope_spec,
            rope_spec,
        ] + side_in,
        out_specs=[pl.BlockSpec((None, dil, tm // dil, width), out_map(g)) for g, dil in enumerate(dils)] + side_out,
        out_shape=[jax.ShapeDtypeStruct((3, dil, n // dil, width), BF16) for dil in dils] + side_shapes,
        scratch_shapes=[pltpu.VMEM((ng, tm, d), BF16), pltpu.VMEM((d // LANES, tm, LANES), F32)] + side_scratch,
        compiler_params=_params("arbitrary", "arbitrary"),
        name="qkv_project_dilated",
    )(x, gain.reshape(1, d), w, head_gain, *rope_tabs, *side_srcs)


def _na_bias_tables(rpb):
    half = NA_ROWS // 2
    assert NA_GROUP == half
    qi = jnp.arange(NA_GROUP)[:, None]
    kj = jnp.arange(NA_KEY_ROWS)[None, :]
    dr = jnp.broadcast_to(kj - NA_GROUP - qi, (3, NA_GROUP, NA_KEY_ROWS))
    rvalid = jnp.stack([
        jnp.broadcast_to(kj >= NA_GROUP, (NA_GROUP, NA_KEY_ROWS)),
        (kj - qi >= 0) & (kj - qi < NA_ROWS),
        jnp.broadcast_to(kj < NA_ROWS, (NA_GROUP, NA_KEY_ROWS)),
    ])
    c = jnp.arange(GRID_W)
    cs = jnp.clip(c - NA_COLS // 2, 0, GRID_W - NA_COLS)
    cvalid = (c[None, :] >= cs[:, None]) & (c[None, :] < cs[:, None] + NA_COLS)
    dc = jnp.clip(c[None, :] - c[:, None], 1 - NA_COLS, NA_COLS - 1)
    dri = jnp.clip(dr, 1 - NA_ROWS, NA_ROWS - 1) + NA_ROWS - 1
    sel_r = jax.nn.one_hot(dri, 2 * NA_ROWS - 1, dtype=F32)
    sel_c = jax.nn.one_hot(dc + NA_COLS - 1, 2 * NA_COLS - 1, dtype=F32)
    hp = lax.Precision.HIGHEST
    tmp = jnp.einsum("hab,qcb->haqc", rpb.astype(F32), sel_c, precision=hp)
    bias = jnp.einsum("haqc,vgka->vhgqkc", tmp, sel_r, precision=hp)
    valid = rvalid[:, None, :, None, :, None] & cvalid[None, None, None, :, None, :]
    bias = jnp.where(valid, bias, NEG_INF)
    h = rpb.shape[0]
    return bias.reshape(3, h, NA_GROUP * GRID_W, NA_KEY_ROWS * GRID_W)


def _na_kernel(q_ref, kp_ref, k_ref, kn_ref, vp_ref, v_ref, vn_ref, tab_ref, o_ref):
    gq = q_ref.shape[0]
    nt = (((1,), (1,)), ((), ()))
    for h in range(q_ref.shape[1] // HEAD_DIM):
        sl = slice(h * HEAD_DIM, (h + 1) * HEAD_DIM)
        q = q_ref[:, sl]
        s = jnp.concatenate(
            [lax.dot_general(q, kr[:, sl], nt, preferred_element_type=F32) for kr in (kp_ref, k_ref, kn_ref)],
            axis=1)
        s = s * ATTN_SCALE + tab_ref[0, h]
        m = jnp.max(s, axis=-1, keepdims=True)
        p = jnp.exp(s - m)
        den = jnp.sum(p, axis=-1, keepdims=True)
        pb = p.astype(BF16)
        o = (jnp.dot(pb[:, :gq], vp_ref[:, sl], preferred_element_type=F32)
             + jnp.dot(pb[:, gq:2 * gq], v_ref[:, sl], preferred_element_type=F32)
             + jnp.dot(pb[:, 2 * gq:], vn_ref[:, sl], preferred_element_type=F32))
        o_ref[:, sl] = (o / den).astype(o_ref.dtype)


def _two_batch_position(t, first_blocks, per_seq_a, per_seq_b):
    in_a = t < first_blocks
    per_seq = jnp.where(in_a, per_seq_a, per_seq_b)
    return jnp.where(in_a, t, t - first_blocks) % per_seq, per_seq


def _na_attention(qkv, tables, trunks, heads):
    n = qkv.shape[0]
    gq = NA_GROUP * GRID_W
    hb = NA_HEAD_BLOCK
    hblocks = heads // hb
    (off_a, b_a, s_a), (off_b, b_b, s_b) = trunks
    for (_, _, seq) in trunks:
        assert seq % gq == 0 and seq // GRID_W >= NA_ROWS
    assert heads % hb == 0 and off_a == 0 and off_b == b_a * s_a
    first_groups, gps_a, gps_b = off_b // gq, s_a // gq, s_b // gq

    def place(g):
        pos, per_seq = _two_batch_position(g, first_groups, gps_a, gps_b)
        return pos == 0, pos == per_seq - 1

    def variant(g):
        first, last = place(g)
        return jnp.where(first, 0, jnp.where(last, 2, 1))

    def rows_of(which, shift):
        def index(h, g):
            first, last = place(g)
            gg = g + jnp.where((shift < 0) & first, 0, jnp.where((shift > 0) & last, 0, shift))
            return (gg, which * hblocks + h)
        return index

    blk = lambda index: pl.BlockSpec((gq, hb * HEAD_DIM), index)
    return pl.pallas_call(
        _na_kernel,
        grid=(hblocks, n // gq),
        in_specs=[
            blk(rows_of(0, 0)),
            blk(rows_of(1, -1)), blk(rows_of(1, 0)), blk(rows_of(1, 1)),
            blk(rows_of(2, -1)), blk(rows_of(2, 0)), blk(rows_of(2, 1)),
            pl.BlockSpec((1, hb, gq, NA_KEY_ROWS * GRID_W), lambda h, g: (variant(g), h, 0, 0)),
        ],
        out_specs=blk(rows_of(0, 0)),
        out_shape=jax.ShapeDtypeStruct((n, heads * HEAD_DIM), BF16),
        compiler_params=_params("arbitrary", "arbitrary"),
        name="na_attention",
    )(qkv, qkv, qkv, qkv, qkv, qkv, qkv, tables)


def _proj_residual_kernel(a_ref, w_ref, xa_ref, xb_ref, o_ref, *, first_blocks):
    y = jnp.dot(a_ref[...], w_ref[...], preferred_element_type=F32)

    def add(x_ref):
        o_ref[...] = x_ref[...] + y
    _pick_rows(pl.program_id(1), first_blocks, xa_ref, xb_ref, add)


def _proj_residual(a, w, xa, xb):
    n, k = a.shape
    d = w.shape[1]
    tm = min(TOKEN_BLOCK, xa.shape[0], xb.shape[0])
    tn = min(PROJ_COL_BLOCK, d)
    assert xa.shape[0] % tm == 0 and xb.shape[0] % tm == 0 and xa.shape[0] + xb.shape[0] == n
    first_blocks = xa.shape[0] // tm
    return pl.pallas_call(
        functools.partial(_proj_residual_kernel, first_blocks=first_blocks),
        grid=(d // tn, n // tm),
        in_specs=[
            pl.BlockSpec((tm, k), lambda j, i: (i, 0)),
            pl.BlockSpec((k, tn), lambda j, i: (0, j)),
            pl.BlockSpec((tm, tn), lambda j, i: (jnp.minimum(i, first_blocks - 1), j)),
            pl.BlockSpec((tm, tn), lambda j, i: (jnp.maximum(i - first_blocks, 0), j)),
        ],
        out_specs=pl.BlockSpec((tm, tn), lambda j, i: (i, j)),
        out_shape=jax.ShapeDtypeStruct((n, d), F32),
        compiler_params=_params("arbitrary", "arbitrary"),
        name="proj_residual",
    )(a, w, xa, xb)


def _side_cast_step(step, srcs, dsts, in_buf, out_buf, in_sem, out_sem):
    rows = in_buf.shape[1]
    per = srcs[0].shape[0] // rows
    nb = per * len(srcs)
    slot = step % 2

    def block_copy(refs, b, make):
        for k, ref in enumerate(refs):
            @pl.when(b // per == k)
            def _(k=k, ref=ref):
                make(ref.at[pl.ds(pl.multiple_of((b - k * per) * rows, rows), rows), :]).start()

    def in_copy(s):
        return lambda hbm: pltpu.make_async_copy(hbm, in_buf.at[s], in_sem.at[s])

    def out_copy(s):
        return lambda hbm: pltpu.make_async_copy(out_buf.at[s], hbm, out_sem.at[s])

    @pl.when(step == 0)
    def _():
        block_copy(srcs, 0, in_copy(0))

    @pl.when((step >= 2) & (step - 2 < nb))
    def _():
        out_copy(slot)(dsts[0].at[pl.ds(0, rows), :]).wait()

    @pl.when(step < nb)
    def _():
        in_copy(slot)(srcs[0].at[pl.ds(0, rows), :]).wait()

        @pl.when(step + 1 < nb)
        def _():
            block_copy(srcs, step + 1, in_copy(1 - slot))

        out_buf[slot] = in_buf[slot].astype(BF16)
        block_copy(dsts, step, out_copy(slot))


def _side_cast_fits(srcs, rows, n_steps):
    r = srcs[0].shape[0]
    return r % rows == 0 and (r // rows) * len(srcs) + 2 <= n_steps


def _side_cast_specs(srcs, rows, n_steps):
    r, c = srcs[0].shape
    assert all(a.shape == (r, c) for a in srcs) and _side_cast_fits(srcs, rows, n_steps)
    any_spec = pl.BlockSpec(memory_space=pl.ANY)
    return ([any_spec] * len(srcs), [any_spec] * len(srcs),
            [jax.ShapeDtypeStruct((r, c), BF16) for _ in srcs],
            [pltpu.VMEM((2, rows, c), F32), pltpu.VMEM((2, rows, c), BF16),
             pltpu.SemaphoreType.DMA((2,)), pltpu.SemaphoreType.DMA((2,))])


def _silu_mul(g, u):
    return g * (1.0 / (1.0 + jnp.exp(-g))) * u


def _ffn_kernel(x_ref, gain_ref, wg_ref, wu_ref, wd_ref, *rest, n_side):
    srcs, (o_ref, *dsts), (h_ref, acc_ref, *side) = rest[:n_side], rest[n_side:2 * n_side + 1], rest[2 * n_side + 1:]
    j = pl.program_id(1)
    if n_side:
        _side_cast_step(pl.program_id(0) * pl.num_programs(1) + j, srcs, dsts, *side)

    @pl.when(j == 0)
    def _():
        h_ref[...] = _rms_rows(x_ref[...], gain_ref[...]).astype(BF16)
        acc_ref[...] = jnp.zeros_like(acc_ref)

    h = h_ref[...]
    g = jnp.dot(h, wg_ref[...], preferred_element_type=F32)
    u = jnp.dot(h, wu_ref[...], preferred_element_type=F32)
    acc_ref[...] += jnp.dot(_silu_mul(g, u).astype(BF16), wd_ref[...], preferred_element_type=F32)

    @pl.when(j == pl.num_programs(1) - 1)
    def _():
        o_ref[...] = x_ref[...] + acc_ref[...]


def _dense_ffn(x, gain, w_gate, w_up, w_down, side_srcs=(), side_rows=0):
    n, d = x.shape
    f = w_gate.shape[1]
    tm = min(FFN_TOKEN_BLOCK, n)
    tf = FFN_F_BLOCK
    assert f % tf == 0
    grid = (n // tm, f // tf)
    side_in, side_out, side_shapes, side_scratch = (
        _side_cast_specs(side_srcs, side_rows, grid[0] * grid[1]) if side_srcs else ([], [], [], []))
    return pl.pallas_call(
        functools.partial(_ffn_kernel, n_side=len(side_srcs)),
        grid=grid,
        in_specs=[
            pl.BlockSpec((tm, d), lambda i, j: (i, 0), pipeline_mode=pl.Buffered(1)),
            pl.BlockSpec((1, d), lambda i, j: (0, 0)),
            pl.BlockSpec((d, tf), lambda i, j: (0, j)),
            pl.BlockSpec((d, tf), lambda i, j: (0, j)),
            pl.BlockSpec((tf, d), lambda i, j: (j, 0)),
        ] + side_in,
        out_specs=[pl.BlockSpec((tm, d), lambda i, j: (i, 0))] + side_out,
        out_shape=[jax.ShapeDtypeStruct((n, d), F32)] + side_shapes,
        scratch_shapes=[pltpu.VMEM((tm, d), BF16), pltpu.VMEM((tm, d), F32)] + side_scratch,
        compiler_params=_params("arbitrary", "arbitrary", vmem_limit=FFN_VMEM_LIMIT),
        name="dense_ffn",
    )(x, gain.reshape(1, d), w_gate, w_up, w_down, *side_srcs)


def _band_kernel(q_ref, kp_ref, k_ref, kn_ref, vp_ref, v_ref, vn_ref, o_ref, lse_ref, kwin_ref, vwin_ref, *,
                 first_blocks, per_seq_a, per_seq_b, tq, half):
    tqb = q_ref.shape[0]
    kw = tq + 2 * half
    pos, per_seq = _two_batch_position(pl.program_id(1), first_blocks, per_seq_a, per_seq_b)
    q0 = pos * tqb
    length = per_seq * tqb
    for win, parts in ((kwin_ref, (kp_ref, k_ref, kn_ref)), (vwin_ref, (vp_ref, v_ref, vn_ref))):
        win[:half] = parts[0][...]
        win[half:half + tqb] = parts[1][...]
        win[half + tqb:] = parts[2][...]
    lane = lax.broadcasted_iota(jnp.int32, (tq, LANES), 1)

    def body(t, carry):
        r0 = pl.multiple_of(t * tq, tq)
        qpos = q0 + r0 + lax.broadcasted_iota(jnp.int32, (tq, kw), 0)
        kpos = q0 + r0 - half + lax.broadcasted_iota(jnp.int32, (tq, kw), 1)
        ok = (jnp.abs(qpos - kpos) <= half) & (kpos >= 0) & (kpos < length)
        lse_tile = jnp.zeros((tq, LANES), F32)
        for h in range(q_ref.shape[1] // HEAD_DIM):
            sl = slice(h * HEAD_DIM, (h + 1) * HEAD_DIM)
            q = q_ref[pl.ds(r0, tq), sl]
            k = kwin_ref[pl.ds(r0, kw), sl]
            v = vwin_ref[pl.ds(r0, kw), sl]
            s = lax.dot_general(q, k, (((1,), (1,)), ((), ())), preferred_element_type=F32) * ATTN_SCALE
            s = jnp.where(ok, s, NEG_INF)
            m = jnp.max(s, axis=-1, keepdims=True)
            p = jnp.exp(s - m)
            den = jnp.sum(p, axis=-1, keepdims=True)
            o = jnp.dot(p.astype(BF16), v, preferred_element_type=F32) / den
            o_ref[pl.ds(r0, tq), sl] = o
            lse_tile = jnp.where(lane == h, m + jnp.log(den), lse_tile)
        lse_ref[pl.ds(r0, tq), :] = lse_tile
        return carry

    lax.fori_loop(0, tqb // tq, body, 0)


def _band_attention(qkv, trunks, group):
    window, dil = DIL_CONFIGS[group]
    half = window // (2 * dil)
    _, _, rows, width = qkv.shape
    (off_a, b_a, s_a), (off_b, b_b, s_b) = trunks
    len_a, len_b = s_a // dil, s_b // dil
    assert s_a % dil == 0 and s_b % dil == 0 and qkv.shape[1] == dil and off_a == 0 and off_b == b_a * s_a
    tqb = min(BAND_Q_BLOCK, len_a, len_b)
    tq = min(BAND_Q_TILE, tqb)
    assert len_a % tqb == 0 and len_b % tqb == 0 and tqb % tq == 0 and tqb % half == 0
    hpb = tqb // half
    first_blocks, per_seq_a, per_seq_b = off_b // dil // tqb, len_a // tqb, len_b // tqb

    def halo(which, side):
        def index(r, t):
            pos, per_seq = _two_batch_position(t, first_blocks, per_seq_a, per_seq_b)
            edge = (pos == per_seq - 1) if side else (pos == 0)
            blk = t * hpb + (jnp.where(edge, hpb - 1, hpb) if side else jnp.where(edge, 0, -1))
            return (which, r, blk, 0)
        return index

    mspec = lambda which: pl.BlockSpec((None, None, tqb, width), lambda r, t: (which, r, t, 0))
    hspec = lambda which, side: pl.BlockSpec((None, None, half, width), halo(which, side))
    return pl.pallas_call(
        functools.partial(_band_kernel, first_blocks=first_blocks, per_seq_a=per_seq_a, per_seq_b=per_seq_b,
                          tq=tq, half=half),
        grid=(dil, rows // tqb),
        in_specs=[
            mspec(0),
            hspec(1, 0), mspec(1), hspec(1, 1),
            hspec(2, 0), mspec(2), hspec(2, 1),
        ],
        out_specs=[
            pl.BlockSpec((None, tqb, width), lambda r, t: (r, t, 0)),
            pl.BlockSpec((None, tqb, LANES), lambda r, t: (r, t, 0)),
        ],
        out_shape=[jax.ShapeDtypeStruct((dil, rows, width), F32), jax.ShapeDtypeStruct((dil, rows, LANES), F32)],
        scratch_shapes=[pltpu.VMEM((tqb + 2 * half, width), BF16), pltpu.VMEM((tqb + 2 * half, width), BF16)],
        compiler_params=_params("arbitrary", "arbitrary"),
        name="band_attention",
    )(qkv, qkv, qkv, qkv, qkv, qkv, qkv)


def _merge_proj_kernel(*refs, dils):
    ng = len(dils)
    o_refs, l_refs = refs[:ng], refs[ng:2 * ng]
    w_ref, x_ref, out_ref, a_ref, os_ref, ls_ref = refs[2 * ng:]
    tm = x_ref.shape[0]

    heads = a_ref.shape[1] // HEAD_DIM
    for g, dil in enumerate(dils):
        if dil == 1:
            continue
        for r in range(dil):
            ls_ref[g, pl.ds(r, tm // dil, stride=dil), :] = l_refs[g][r]
            for h in range(heads):
                os_ref[g, h, pl.ds(r, tm // dil, stride=dil), :] = o_refs[g][r, :, h * HEAD_DIM:(h + 1) * HEAD_DIM]

    def group_out(g, h):
        if dils[g] == 1:
            return o_refs[g][0, :, h * HEAD_DIM:(h + 1) * HEAD_DIM]
        return os_ref[g, h]

    lse = [l_refs[g][0] if dils[g] == 1 else ls_ref[g] for g in range(ng)]
    m = functools.reduce(jnp.maximum, lse)
    e = [jnp.exp(l - m) for l in lse]
    tot = functools.reduce(lambda p, q: p + q, e)
    wts = [eg / tot for eg in e]
    for h in range(heads):
        terms = [wts[g][:, h:h + 1] * group_out(g, h) for g in range(ng)]
        a_ref[:, h * HEAD_DIM:(h + 1) * HEAD_DIM] = functools.reduce(lambda p, q: p + q, terms).astype(BF16)
    out_ref[...] = x_ref[...] + jnp.dot(a_ref[...], w_ref[...], preferred_element_type=F32)


def _merge_proj(outs, lses, w, x):
    n, d = x.shape
    width = w.shape[0]
    dils = tuple(o.shape[0] for o in outs)
    tm = min(SMALL_TOKEN_BLOCK, n)
    row = lambda i: (i, 0)
    grp = lambda i: (0, i, 0)
    return pl.pallas_call(
        functools.partial(_merge_proj_kernel, dils=dils),
        grid=(n // tm,),
        in_specs=[pl.BlockSpec((dil, tm // dil, width), grp) for dil in dils]
        + [pl.BlockSpec((dil, tm // dil, LANES), grp) for dil in dils]
        + [pl.BlockSpec((width, d), lambda i: (0, 0)), pl.BlockSpec((tm, d), row)],
        out_specs=pl.BlockSpec((tm, d), row),
        out_shape=jax.ShapeDtypeStruct((n, d), F32),
        scratch_shapes=[
            pltpu.VMEM((tm, width), BF16),
            pltpu.VMEM((len(dils), width // HEAD_DIM, tm, HEAD_DIM), F32),
            pltpu.VMEM((len(dils), tm, LANES), F32),
        ],
        compiler_params=_params("arbitrary"),
        name="merge_proj",
    )(*outs, *lses, w, x)


def _store_chunk_rows(ref, x):
    rows, d = x.shape
    nc = d // LANES
    for c in range(nc):
        ref[pl.ds(c, rows, stride=nc), :] = x[:, c * LANES:(c + 1) * LANES]


def _store_packed_rows(ref, x):
    rows, d = x.shape
    nw = d // (2 * LANES)
    for c in range(nw):
        lo = x[:, (2 * c) * LANES:(2 * c + 1) * LANES].astype(BF16).astype(F32)
        hi = x[:, (2 * c + 1) * LANES:(2 * c + 2) * LANES].astype(BF16).astype(F32)
        word = (pltpu.bitcast(lo, jnp.uint32) >> 16) | (pltpu.bitcast(hi, jnp.uint32) & jnp.uint32(0xFFFF0000))
        ref[pl.ds(c, rows, stride=nw), :] = word


def _load_packed_rows(ref, lead, rows, nw, c):
    word = ref[lead, pl.ds(c, rows, stride=nw), :]
    lo = pltpu.bitcast(word << 16, F32)
    hi = pltpu.bitcast(word & jnp.uint32(0xFFFF0000), F32)
    return lo.astype(BF16), hi.astype(BF16)


def _router_kernel(x_ref, gain_ref, wr_ref, h_ref, idx_ref, gate_ref):
    h = _rms_rows(x_ref[...], gain_ref[...])
    _store_packed_rows(h_ref, h)
    w = wr_ref[...]
    h_hi, w_hi = h.astype(BF16), w.astype(BF16)
    h_lo, w_lo = (h - h_hi.astype(F32)).astype(BF16), (w - w_hi.astype(F32)).astype(BF16)
    logits = (jnp.dot(h_hi, w_hi, preferred_element_type=F32)
              + (jnp.dot(h_hi, w_lo, preferred_element_type=F32) + jnp.dot(h_lo, w_hi, preferred_element_type=F32)))
    lane = lax.broadcasted_iota(jnp.int32, logits.shape, 1)
    lane_f = lane.astype(F32)
    logits = jnp.where(lane < N_EXPERTS, logits, -jnp.inf)
    m1 = jnp.max(logits, axis=-1, keepdims=True)
    i1 = jnp.min(jnp.where(logits == m1, lane_f, float(LANES)), axis=-1, keepdims=True)
    rest = jnp.where(lane_f == i1, -jnp.inf, logits)
    m2 = jnp.max(rest, axis=-1, keepdims=True)
    i2 = jnp.min(jnp.where(rest == m2, lane_f, float(LANES)), axis=-1, keepdims=True)
    e = jnp.exp(m2 - m1)
    tot = 1.0 + e
    idx_ref[...] = jnp.where(lane == 0, i1, jnp.where(lane == 1, i2, 0.0)).astype(jnp.int32)
    gate_ref[...] = jnp.where(lane == 0, 1.0 / tot, jnp.where(lane == 1, e / tot, 0.0))


def _router(x, gain, w_router):
    n, d = x.shape
    tm = min(SMALL_TOKEN_BLOCK, n)
    wr = jnp.zeros((d, LANES), F32).at[:, :N_EXPERTS].set(w_router.astype(F32))
    row = lambda i: (i, 0)
    return pl.pallas_call(
        _router_kernel,
        grid=(n // tm,),
        in_specs=[
            pl.BlockSpec((tm, d), row),
            pl.BlockSpec((1, d), lambda i: (0, 0)),
            pl.BlockSpec((d, LANES), lambda i: (0, 0)),
        ],
        out_specs=[pl.BlockSpec((tm * (d // (2 * LANES)), LANES), row), pl.BlockSpec((tm, LANES), row),
                   pl.BlockSpec((tm, LANES), row)],
        out_shape=[
            jax.ShapeDtypeStruct((n * (d // (2 * LANES)), LANES), jnp.uint32),
            jax.ShapeDtypeStruct((n, LANES), jnp.int32),
            jax.ShapeDtypeStruct((n, LANES), F32),
        ],
        compiler_params=_params("arbitrary"),
        name="router",
    )(x, gain.reshape(1, d), wr)


def _expert_kernel(bexp_ref, brows_ref, nused_ref, rtok_ref, h_hbm, wg_ref, wu_ref, wd_ref, y_ref,
                   xs_ref, xb_ref, acc_ref, sem):
    del bexp_ref
    i = pl.program_id(0)
    j = pl.program_id(1)
    tmb, d = xb_ref.shape
    nw = d // (2 * LANES)
    n_used = nused_ref[0]

    def row_copy(tok, slot, r):
        return pltpu.make_async_copy(h_hbm.at[pl.ds(pl.multiple_of(tok * nw, nw), nw), :],
                                     xs_ref.at[slot, pl.ds(pl.multiple_of(r * nw, nw), nw), :], sem.at[slot])

    def start_gather(blk, slot):
        def issue(r, carry):
            row_copy(rtok_ref[blk * tmb + r], slot, r).start()
            return carry
        lax.fori_loop(0, tmb, issue, 0, unroll=8)

    def wait_gather(slot):
        pltpu.make_async_copy(h_hbm.at[pl.ds(0, tmb * nw), :], xs_ref.at[slot], sem.at[slot]).wait()

    @pl.when(i < n_used)
    def _():
        slot = i % 2

        @pl.when(j == 0)
        def _():
            @pl.when(i == 0)
            def _():
                start_gather(0, 0)

            wait_gather(slot)

            @pl.when(i + 1 < n_used)
            def _():
                start_gather(i + 1, 1 - slot)

            for c in range(nw):
                lo, hi = _load_packed_rows(xs_ref, slot, tmb, nw, c)
                xb_ref[:, (2 * c) * LANES:(2 * c + 1) * LANES] = lo
                xb_ref[:, (2 * c + 1) * LANES:(2 * c + 2) * LANES] = hi
            acc_ref[...] = jnp.zeros_like(acc_ref)

        def swiglu_rows(rows):
            x = xb_ref[:rows]
            g = jnp.dot(x, wg_ref[0], preferred_element_type=F32)
            u = jnp.dot(x, wu_ref[0], preferred_element_type=F32)
            acc_ref[:rows] += jnp.dot(_silu_mul(g, u).astype(BF16), wd_ref[0], preferred_element_type=F32)

        half_full = brows_ref[i] <= tmb // 2

        @pl.when(half_full)
        def _():
            swiglu_rows(tmb // 2)

        @pl.when(jnp.logical_not(half_full))
        def _():
            swiglu_rows(tmb)

    last = j == pl.num_programs(1) - 1

    @pl.when(last & (i < n_used))
    def _():
        _store_chunk_rows(y_ref, acc_ref[...])

    @pl.when(last & (i >= n_used))
    def _():
        y_ref[...] = jnp.zeros_like(y_ref)


def _expert_ffn(h_rows, block_expert, block_rows, n_used, row_tok, w_gate, w_up, w_down):
    d = w_gate.shape[1]
    f = w_gate.shape[2]
    tmb = MOE_BLOCK
    tf = MOE_F_BLOCK
    n_blocks = block_expert.shape[0]
    nf = f // tf
    assert f % tf == 0

    def fblk(i, j, nu):
        return jnp.where(i < nu[0], j, nf - 1)

    grid_spec = pltpu.PrefetchScalarGridSpec(
        num_scalar_prefetch=4,
        grid=(n_blocks, nf),
        in_specs=[
            pl.BlockSpec(memory_space=pl.ANY),
            pl.BlockSpec((1, d, tf), lambda i, j, be, br, nu, rt: (be[i], 0, fblk(i, j, nu))),
            pl.BlockSpec((1, d, tf), lambda i, j, be, br, nu, rt: (be[i], 0, fblk(i, j, nu))),
            pl.BlockSpec((1, tf, d), lambda i, j, be, br, nu, rt: (be[i], fblk(i, j, nu), 0)),
        ],
        out_specs=pl.BlockSpec((tmb * (d // LANES), LANES), lambda i, j, be, br, nu, rt: (i, 0)),
        scratch_shapes=[
            pltpu.VMEM((2, tmb * (d // (2 * LANES)), LANES), jnp.uint32),
            pltpu.VMEM((tmb, d), BF16),
            pltpu.VMEM((tmb, d), F32),
            pltpu.SemaphoreType.DMA((2,)),
        ],
    )
    return pl.pallas_call(
        _expert_kernel,
        grid_spec=grid_spec,
        out_shape=jax.ShapeDtypeStruct((n_blocks * tmb * (d // LANES), LANES), F32),
        compiler_params=_params("arbitrary", "arbitrary"),
        name="expert_ffn",
    )(block_expert, block_rows, n_used, row_tok, h_rows, w_gate, w_up, w_down)


def _combine_kernel(dest_ref, ys_hbm, x_ref, gate_ref, oa_ref, ob_ref, buf_ref, sem, *, first_blocks):
    i = pl.program_id(0)
    nsteps = pl.num_programs(0)
    tc, d = x_ref.shape
    nc = d // LANES
    pitch = buf_ref.shape[1] // (2 * tc)

    def row_copy(row, slot, dst):
        return pltpu.make_async_copy(ys_hbm.at[pl.ds(pl.multiple_of(row * nc, nc), nc), :],
                                     buf_ref.at[slot, pl.ds(pl.multiple_of(dst * pitch, 8), nc), :], sem.at[slot])

    def start_gather(step, slot):
        def issue(r, carry):
            row_copy(dest_ref[step * (2 * tc) + r], slot, r).start()
            return carry
        lax.fori_loop(0, 2 * tc, issue, 0, unroll=8)

    def wait_gather(slot):
        pltpu.make_async_copy(ys_hbm.at[pl.ds(0, 2 * tc * nc), :], buf_ref.at[slot, pl.ds(0, 2 * tc * nc), :],
                              sem.at[slot]).wait()

    slot = i % 2

    @pl.when(i == 0)
    def _():
        start_gather(0, 0)

    wait_gather(slot)

    @pl.when(i + 1 < nsteps)
    def _():
        start_gather(i + 1, 1 - slot)

    g0 = gate_ref[:, 0:1]
    g1 = gate_ref[:, 1:2]

    def emit(o_ref):
        for c in range(nc):
            sl = slice(c * LANES, (c + 1) * LANES)
            y0 = buf_ref[slot, pl.ds(c, tc, stride=pitch), :]
            y1 = buf_ref[slot, pl.ds(tc * pitch + c, tc, stride=pitch), :]
            o_ref[:, sl] = x_ref[:, sl] + (y0 * g0 + y1 * g1)
    _pick_rows(i, first_blocks, oa_ref, ob_ref, emit)


def _combine(x, ys, dest, gate, n_first):
    n, d = x.shape
    tc = min(COMBINE_BLOCK, n_first, n - n_first)
    assert n_first % tc == 0 and (n - n_first) % tc == 0
    first_blocks = n_first // tc
    nc = d // LANES
    pitch = nc if (nc // 8) % 2 else nc + 8
    dest = dest.reshape(n // tc, tc, TOP_K).transpose(0, 2, 1).reshape(-1)
    grid_spec = pltpu.PrefetchScalarGridSpec(
        num_scalar_prefetch=1,
        grid=(n // tc,),
        in_specs=[
            pl.BlockSpec(memory_space=pl.ANY),
            pl.BlockSpec((tc, d), lambda i, dr: (i, 0)),
            pl.BlockSpec((tc, LANES), lambda i, dr: (i, 0)),
        ],
        out_specs=_two_batch_specs(tc, d, first_blocks),
        scratch_shapes=[pltpu.VMEM((2, 2 * tc * pitch, LANES), F32), pltpu.SemaphoreType.DMA((2,))],
    )
    return pl.pallas_call(
        functools.partial(_combine_kernel, first_blocks=first_blocks),
        grid_spec=grid_spec,
        out_shape=[jax.ShapeDtypeStruct((n_first, d), F32), jax.ShapeDtypeStruct((n - n_first, d), F32)],
        compiler_params=_params("arbitrary"),
        name="moe_combine",
    )(dest, ys, x, gate)


def _routing_tables(idx, tmb):
    n = idx.shape[0]
    nk = n * TOP_K
    e_flat = idx[:, :TOP_K].reshape(-1)
    onehot = (e_flat[:, None] == jnp.arange(N_EXPERTS, dtype=jnp.int32)[None, :]).astype(jnp.int32)
    csum = jnp.cumsum(onehot, axis=0)
    rank = jnp.sum(onehot * csum, axis=1) - 1
    counts = csum[-1]
    padded = (counts + tmb - 1) // tmb * tmb
    pad_end = jnp.cumsum(padded)
    pad_start = pad_end - padded
    dest = (jnp.sum(onehot * pad_start[None, :], axis=1) + rank).astype(jnp.int32)
    n_blocks = -(-nk // tmb) + N_EXPERTS
    block_expert = jnp.minimum(
        jnp.searchsorted(pad_end, jnp.arange(n_blocks, dtype=jnp.int32) * tmb, side="right"), N_EXPERTS - 1
    ).astype(jnp.int32)
    n_used = (pad_end[-1] // tmb).astype(jnp.int32).reshape(1)
    block_lo = jnp.arange(n_blocks, dtype=jnp.int32) * tmb - pad_start[block_expert]
    block_rows = jnp.clip(counts[block_expert] - block_lo, 0, tmb).astype(jnp.int32)
    row_tok = jnp.zeros((n_blocks * tmb,), jnp.int32).at[dest].set(jnp.arange(nk, dtype=jnp.int32) // TOP_K)
    return dest, block_expert, block_rows, n_used, row_tok


def _rope_tables(seq_len):
    half = HEAD_DIM // 2
    inv_freq = ROPE_THETA ** (-jnp.arange(half, dtype=F32) / half)
    ang = jnp.arange(seq_len, dtype=F32)[:, None] * inv_freq[None, :]
    cos, sin = jnp.cos(ang), jnp.sin(ang)
    return jnp.concatenate([cos, cos], axis=-1), jnp.concatenate([-sin, sin], axis=-1)


def kernel(x_prompt, x_sample, norm_mix, norm_ffn, na_w_qkv, na_q_gain, na_k_gain, na_rpb, na_w_o,
           da_w_qkv, da_q_gain, da_k_gain, da_w_o, ffn_w_gate, ffn_w_up, ffn_w_down,
           moe_w_router, moe_w_gate, moe_w_up, moe_w_down):
    d = x_prompt.shape[-1]
    trunks = []
    off = 0
    for xin in (x_prompt, x_sample):
        b, s, _ = xin.shape
        trunks.append((off, b, s))
        off += b * s
    n = off
    xa, xb = x_prompt.reshape(-1, d), x_sample.reshape(-1, d)
    depth = norm_mix.shape[0]
    assert depth == 2, "layer 0 reads the two request batches directly and layer 1 writes them separately"
    x = None

    for layer in range(depth):
        lj = layer // 2
        if layer % 2 == 0:
            heads = na_w_qkv.shape[2] // (3 * HEAD_DIM)
            per = heads * HEAD_DIM // PROJ_COL_BLOCK
            hg = jnp.concatenate([
                jnp.broadcast_to(na_q_gain[lj], (per, HEAD_DIM)),
                jnp.broadcast_to(na_k_gain[lj], (per, HEAD_DIM)),
                jnp.ones((per, HEAD_DIM), F32),
            ]).reshape(3 * per, 1, HEAD_DIM)
            qkv = _qkv_project(xa, xb, norm_mix[layer], na_w_qkv[lj].astype(BF16), hg, 3 * per, 2 * per)
            tables = _na_bias_tables(na_rpb[lj])
            att = _na_attention(qkv, tables, trunks, heads)
            x = _proj_residual(att, na_w_o[lj].astype(BF16), xa, xb)
            ne, dm, fe = moe_w_gate.shape[1:]
            wide = (moe_w_gate[lj].reshape(ne * dm, fe), moe_w_up[lj].reshape(ne * dm, fe))
            steps = (n // min(FFN_TOKEN_BLOCK, n)) * (ffn_w_gate.shape[2] // FFN_F_BLOCK)
            shadow = _side_cast_fits(wide, SIDE_CAST_ROWS_WIDE, steps)
            x, *cast = _dense_ffn(x, norm_ffn[layer], ffn_w_gate[lj].astype(BF16), ffn_w_up[lj].astype(BF16),
                                  ffn_w_down[lj].astype(BF16), side_srcs=wide if shadow else (),
                                  side_rows=SIDE_CAST_ROWS_WIDE)
            moe_gate_bf, moe_up_bf = cast if shadow else [w.astype(BF16) for w in wide]
        else:
            groups = len(DIL_CONFIGS)
            heads = da_w_qkv.shape[2] // (groups * 3 * HEAD_DIM)
            dils = [dil for _, dil in DIL_CONFIGS]
            ones = jnp.ones((HEAD_DIM,), F32)
            hg = jnp.stack([t for g in range(groups) for t in (da_q_gain[lj, g], da_k_gain[lj, g], ones)])
            hg = hg.reshape(groups * 3, 1, HEAD_DIM)
            (o0, b0, s0), (o1, b1, s1) = trunks
            tq = min(QKV_DIL_TOKEN_BLOCK, n)
            assert s0 % tq == 0 and s1 % tq == 0 and o1 % tq == 0
            nb0, p0, p1 = o1 // tq, s0 // tq, s1 // tq
            pos_map = lambda i: jnp.where(i < nb0, i % p0, (i - nb0) % p1)
            smax = max(s0, s1)
            rope = [jnp.stack([t.reshape(smax // tq, tq // dil, dil, HEAD_DIM).transpose(0, 2, 1, 3)
                               .reshape(smax, HEAD_DIM) for dil in dils]) for t in _rope_tables(smax)]
            narrow = (moe_w_down[lj].reshape(ne * fe, dm),)
            shadow = _side_cast_fits(narrow, SIDE_CAST_ROWS_NARROW, (n // tq) * groups * 3)
            qkvs = list(_qkv_project_dilated(
                x, norm_mix[layer], da_w_qkv[lj].astype(BF16), hg, rope, pos_map, dils, heads, tq,
                side_srcs=narrow if shadow else (), side_rows=SIDE_CAST_ROWS_NARROW))
            moe_down_bf = qkvs.pop() if shadow else narrow[0].astype(BF16)
            outs, lses = [], []
            for g in range(groups):
                o_g, lse_g = _band_attention(qkvs[g], trunks, g)
                outs.append(o_g)
                lses.append(lse_g)
            x = _merge_proj(outs, lses, da_w_o[lj].astype(BF16), x)
            h, idx, gate = _router(x, norm_ffn[layer], moe_w_router[lj])
            dest, block_expert, block_rows, n_used, row_tok = _routing_tables(idx, MOE_BLOCK)
            ys = _expert_ffn(h, block_expert, block_rows, n_used, row_tok, moe_gate_bf.reshape(ne, dm, fe),
                             moe_up_bf.reshape(ne, dm, fe), moe_down_bf.reshape(ne, fe, dm))
            ya, yb = _combine(x, ys, dest, gate, xa.shape[0])

    return ya.reshape(x_prompt.shape), yb.reshape(x_sample.shape)
```

```python
import functools

import jax
import jax.numpy as jnp
from jax import lax
from jax.experimental import pallas as pl
from jax.experimental.pallas import tpu as pltpu

F32 = jnp.float32
BF16 = jnp.bfloat16

HEAD_DIM = 128
GRID_W = 64
NA_ROWS = 8
NA_COLS = 16
NA_GROUP = NA_ROWS // 2
NA_KEY_ROWS = 3 * NA_GROUP
NA_HEAD_BLOCK = 8
DIL_CONFIGS = ((128, 1), (512, 4), (2048, 16))
N_EXPERTS = 8
TOP_K = 2
ROPE_THETA = 10000.0
EPS = 1e-6
NEG_INF = -1e30
ATTN_SCALE = HEAD_DIM ** -0.5

LANES = 128
MXU_COLS = 256
VMEM_LIMIT = 52 * 1024 * 1024
FFN_VMEM_LIMIT = 57 * 1024 * 1024

TOKEN_BLOCK = 1024
PROJ_COL_BLOCK = 1024
SMALL_TOKEN_BLOCK = 512
QKV_DIL_TOKEN_BLOCK = 512
FFN_TOKEN_BLOCK = 1024
FFN_F_BLOCK = 256
BAND_Q_BLOCK = 512
BAND_Q_TILE = 128
MOE_BLOCK = 512
MOE_F_BLOCK = 1024
COMBINE_BLOCK = 256
SIDE_CAST_ROWS_WIDE = 128
SIDE_CAST_ROWS_NARROW = 512


def _params(*sem, vmem_limit=VMEM_LIMIT):
    return pltpu.CompilerParams(dimension_semantics=sem, vmem_limit_bytes=vmem_limit)


def _rms_rows(x, gain):
    ms = jnp.mean(x * x, axis=-1, keepdims=True)
    return x * lax.rsqrt(ms + EPS) * gain


def _pick_rows(i, first_blocks, xa_ref, xb_ref, use):
    @pl.when(i < first_blocks)
    def _():
        use(xa_ref)

    @pl.when(i >= first_blocks)
    def _():
        use(xb_ref)


def _two_batch_specs(tm, d, first_blocks, **kw):
    return [pl.BlockSpec((tm, d), lambda i, *_: (jnp.minimum(i, first_blocks - 1), 0), **kw),
            pl.BlockSpec((tm, d), lambda i, *_: (jnp.maximum(i - first_blocks, 0), 0), **kw)]


def _qkv_kernel(xa_ref, xb_ref, gain_ref, w_ref, hg_ref, o_ref, h_ref, *, first_blocks, v_every, v_from):
    j = pl.program_id(1)

    @pl.when(j == 0)
    def _():
        def norm(x_ref):
            h_ref[...] = _rms_rows(x_ref[...], gain_ref[...]).astype(BF16)
        _pick_rows(pl.program_id(0), first_blocks, xa_ref, xb_ref, norm)

    is_qk = (j % v_every) < v_from
    for c in range(w_ref.shape[1] // MXU_COLS):
        acc = jnp.dot(h_ref[...], w_ref[:, c * MXU_COLS:(c + 1) * MXU_COLS], preferred_element_type=F32)
        for h in range(MXU_COLS // HEAD_DIM):
            seg = acc[:, h * HEAD_DIM:(h + 1) * HEAD_DIM]
            lo = c * MXU_COLS + h * HEAD_DIM
            o_ref[:, lo:lo + HEAD_DIM] = jnp.where(is_qk, _rms_rows(seg, hg_ref[0]), seg).astype(o_ref.dtype)


def _qkv_project(xa, xb, gain, w, head_gain, v_every, v_from):
    d = xa.shape[1]
    n = xa.shape[0] + xb.shape[0]
    n_out = w.shape[1]
    tm = min(TOKEN_BLOCK, xa.shape[0], xb.shape[0])
    assert xa.shape[0] % tm == 0 and xb.shape[0] % tm == 0
    first_blocks = xa.shape[0] // tm
    tn = n_out // head_gain.shape[0]
    return pl.pallas_call(
        functools.partial(_qkv_kernel, first_blocks=first_blocks, v_every=v_every, v_from=v_from),
        grid=(n // tm, n_out // tn),
        in_specs=_two_batch_specs(tm, d, first_blocks, pipeline_mode=pl.Buffered(1)) + [
            pl.BlockSpec((1, d), lambda i, j: (0, 0)),
            pl.BlockSpec((d, tn), lambda i, j: (0, j)),
            pl.BlockSpec((1, 1, HEAD_DIM), lambda i, j: (j, 0, 0)),
        ],
        out_specs=pl.BlockSpec((tm, tn), lambda i, j: (i, j)),
        out_shape=jax.ShapeDtypeStruct((n, n_out), BF16),
        scratch_shapes=[pltpu.VMEM((tm, d), BF16)],
        compiler_params=_params("arbitrary", "arbitrary"),
        name="qkv_project",
    )(xa, xb, gain.reshape(1, d), w, head_gain)


def _qkv_dil_kernel(x_ref, gain_ref, w_ref, hg_ref, cos_ref, sin_ref, *rest, dils, n_side):
    ng = len(dils)
    srcs, out_refs, dsts = rest[:n_side], rest[n_side:n_side + ng], rest[n_side + ng:2 * n_side + ng]
    h_ref, stage_ref, *side = rest[2 * n_side + ng:]
    j = pl.program_id(1)
    if n_side:
        _side_cast_step(pl.program_id(0) * pl.num_programs(1) + j, srcs, dsts, *side)
    tm, d = x_ref.shape
    tn = w_ref.shape[1]

    @pl.when(j == 0)
    def _():
        y = _rms_rows(x_ref[...], gain_ref[...])
        for c in range(d // LANES):
            stage_ref[c] = y[:, c * LANES:(c + 1) * LANES]
        for g, dil in enumerate(dils):
            rows = tm // dil
            for r in range(dil):
                for c in range(d // LANES):
                    h_ref[g, r * rows:(r + 1) * rows, c * LANES:(c + 1) * LANES] = (
                        stage_ref[c, pl.ds(r, rows, stride=dil), :].astype(BF16))

    group = j // 3
    which = j % 3
    acc = jnp.dot(h_ref[group], w_ref[...], preferred_element_type=F32)

    for g, (dil, o_ref) in enumerate(zip(dils, out_refs)):
        rows = tm // dil

        @pl.when((group == g) & (which < 2))
        def _(dil=dil, o_ref=o_ref, rows=rows):
            for h in range(tn // HEAD_DIM):
                sl = slice(h * HEAD_DIM, (h + 1) * HEAD_DIM)
                y = _rms_rows(acc[:, sl], hg_ref[0])
                y = (y * cos_ref[...] + pltpu.roll(y, HEAD_DIM // 2, 1) * sin_ref[...]).astype(o_ref.dtype)
                for r in range(dil):
                    o_ref[r, :, sl] = y[r * rows:(r + 1) * rows]

        @pl.when((group == g) & (which == 2))
        def _(dil=dil, o_ref=o_ref, rows=rows):
            v = acc.astype(o_ref.dtype)
            for r in range(dil):
                o_ref[r] = v[r * rows:(r + 1) * rows]


def _qkv_project_dilated(x, gain, w, head_gain, rope_tabs, pos_map, dils, heads, tm, side_srcs=(), side_rows=0):
    n, d = x.shape
    n_out = w.shape[1]
    width = heads * HEAD_DIM
    ng = len(dils)
    assert n_out == ng * 3 * width and head_gain.shape[0] == ng * 3
    side_in, side_out, side_shapes, side_scratch = (
        _side_cast_specs(side_srcs, side_rows, (n // tm) * ng * 3) if side_srcs else ([], [], [], []))

    def out_map(g):
        def index(i, j):
            return (jnp.clip(j - 3 * g, 0, 2), 0, i, 0)
        return index

    rope_spec = pl.BlockSpec((None, tm, HEAD_DIM), lambda i, j: (j // 3, pos_map(i), 0))
    return pl.pallas_call(
        functools.partial(_qkv_dil_kernel, dils=tuple(dils), n_side=len(side_srcs)),
        grid=(n // tm, ng * 3),
        in_specs=[
            pl.BlockSpec((tm, d), lambda i, j: (i, 0)),
            pl.BlockSpec((1, d), lambda i, j: (0, 0)),
            pl.BlockSpec((d, width), lambda i, j: (0, j)),
            pl.BlockSpec((1, 1, HEAD_DIM), lambda i, j: (j, 0, 0)),
            rope_spec,
            rope_spec,
        ] + side_in,
        out_specs=[pl.BlockSpec((None, dil, tm // dil, width), out_map(g)) for g, dil in enumerate(dils)] + side_out,
        out_shape=[jax.ShapeDtypeStruct((3, dil, n // dil, width), BF16) for dil in dils] + side_shapes,
        scratch_shapes=[pltpu.VMEM((ng, tm, d), BF16), pltpu.VMEM((d // LANES, tm, LANES), F32)] + side_scratch,
        compiler_params=_params("arbitrary", "arbitrary"),
        name="qkv_project_dilated",
    )(x, gain.reshape(1, d), w, head_gain, *rope_tabs, *side_srcs)


def _na_table_kernel(rpb_ref, o_ref):
    h = pl.program_id(0)
    w = GRID_W
    lane = lax.broadcasted_iota(jnp.int32, (w, 2 * w), 1)
    cq = lax.broadcasted_iota(jnp.int32, (w, 2 * w), 0)
    ck = lane % w
    upper = lane >= w
    cs = jnp.clip(cq - NA_COLS // 2, 0, w - NA_COLS)
    col_ok = (ck >= cs) & (ck < cs + NA_COLS)
    dc = ck - cq + (NA_COLS - 1)

    def pair_tile(dr):
        t = jnp.zeros((w, 2 * w), F32)
        for b in range(2 * NA_COLS - 1):
            lo = rpb_ref[h, dr + NA_ROWS - 1, b]
            hi = rpb_ref[h, dr + NA_ROWS, b]
            t = jnp.where(dc == b, jnp.where(upper, hi, lo), t)
        return t

    tiles = {}
    for qi in range(NA_GROUP):
        for kp in range(NA_KEY_ROWS // 2):
            dr = 2 * kp - NA_GROUP - qi
            if dr not in tiles:
                tiles[dr] = pair_tile(dr)
            for v in range(3):
                def row_ok(kj):
                    if v == 0:
                        return kj >= NA_GROUP
                    if v == 1:
                        return 0 <= kj - qi < NA_ROWS
                    return kj < NA_ROWS
                ok_lo, ok_hi = row_ok(2 * kp), row_ok(2 * kp + 1)
                if ok_lo and ok_hi:
                    ok = col_ok
                elif ok_lo:
                    ok = col_ok & jnp.logical_not(upper)
                elif ok_hi:
                    ok = col_ok & upper
                else:
                    ok = None
                tile = jnp.full((w, 2 * w), NEG_INF, F32) if ok is None else jnp.where(ok, tiles[dr], NEG_INF)
                o_ref[v, 0, qi * w:(qi + 1) * w, kp * 2 * w:(kp + 1) * 2 * w] = tile


def _na_bias_tables(rpb):
    assert NA_GROUP == NA_ROWS // 2 and NA_KEY_ROWS % 2 == 0 and 2 * GRID_W == LANES
    heads = rpb.shape[0]
    shape = (3, heads, NA_GROUP * GRID_W, NA_KEY_ROWS * GRID_W)
    return pl.pallas_call(
        _na_table_kernel,
        grid=(heads,),
        in_specs=[pl.BlockSpec(memory_space=pltpu.SMEM)],
        out_specs=pl.BlockSpec((3, 1) + shape[2:], lambda h: (0, h, 0, 0)),
        out_shape=jax.ShapeDtypeStruct(shape, F32),
        compiler_params=_params("arbitrary"),
        name="na_bias_table",
    )(rpb.astype(F32))


def _na_kernel(q_ref, kp_ref, k_ref, kn_ref, vp_ref, v_ref, vn_ref, tab_ref, o_ref):
    gq = q_ref.shape[0]
    nt = (((1,), (1,)), ((), ()))
    for h in range(q_ref.shape[1] // HEAD_DIM):
        sl = slice(h * HEAD_DIM, (h + 1) * HEAD_DIM)
        q = q_ref[:, sl]
        s = jnp.concatenate(
            [lax.dot_general(q, kr[:, sl], nt, preferred_element_type=F32) for kr in (kp_ref, k_ref, kn_ref)],
            axis=1)
        s = s * ATTN_SCALE + tab_ref[0, h]
        m = jnp.max(s, axis=-1, keepdims=True)
        p = jnp.exp(s - m)
        den = jnp.sum(p, axis=-1, keepdims=True)
        pb = p.astype(BF16)
        o = (jnp.dot(pb[:, :gq], vp_ref[:, sl], preferred_element_type=F32)
             + jnp.dot(pb[:, gq:2 * gq], v_ref[:, sl], preferred_element_type=F32)
             + jnp.dot(pb[:, 2 * gq:], vn_ref[:, sl], preferred_element_type=F32))
        o_ref[:, sl] = (o / den).astype(o_ref.dtype)


def _two_batch_position(t, first_blocks, per_seq_a, per_seq_b):
    in_a = t < first_blocks
    per_seq = jnp.where(in_a, per_seq_a, per_seq_b)
    return jnp.where(in_a, t, t - first_blocks) % per_seq, per_seq


def _na_attention(qkv, tables, trunks, heads):
    n = qkv.shape[0]
    gq = NA_GROUP * GRID_W
    hb = NA_HEAD_BLOCK
    hblocks = heads // hb
    (off_a, b_a, s_a), (off_b, b_b, s_b) = trunks
    for (_, _, seq) in trunks:
        assert seq % gq == 0 and seq // GRID_W >= NA_ROWS
    assert heads % hb == 0 and off_a == 0 and off_b == b_a * s_a
    first_groups, gps_a, gps_b = off_b // gq, s_a // gq, s_b // gq

    def place(g):
        pos, per_seq = _two_batch_position(g, first_groups, gps_a, gps_b)
        return pos == 0, pos == per_seq - 1

    def variant(g):
        first, last = place(g)
        return jnp.where(first, 0, jnp.where(last, 2, 1))

    def rows_of(which, shift):
        def index(h, g):
            first, last = place(g)
            gg = g + jnp.where((shift < 0) & first, 0, jnp.where((shift > 0) & last, 0, shift))
            return (gg, which * hblocks + h)
        return index

    blk = lambda index: pl.BlockSpec((gq, hb * HEAD_DIM), index)
    return pl.pallas_call(
        _na_kernel,
        grid=(hblocks, n // gq),
        in_specs=[
            blk(rows_of(0, 0)),
            blk(rows_of(1, -1)), blk(rows_of(1, 0)), blk(rows_of(1, 1)),
            blk(rows_of(2, -1)), blk(rows_of(2, 0)), blk(rows_of(2, 1)),
            pl.BlockSpec((1, hb, gq, NA_KEY_ROWS * GRID_W), lambda h, g: (variant(g), h, 0, 0)),
        ],
        out_specs=blk(rows_of(0, 0)),
        out_shape=jax.ShapeDtypeStruct((n, heads * HEAD_DIM), BF16),
        compiler_params=_params("arbitrary", "arbitrary"),
        name="na_attention",
    )(qkv, qkv, qkv, qkv, qkv, qkv, qkv, tables)


def _proj_residual_kernel(a_ref, w_ref, xa_ref, xb_ref, o_ref, *, first_blocks):
    y = jnp.dot(a_ref[...], w_ref[...], preferred_element_type=F32)

    def add(x_ref):
        o_ref[...] = x_ref[...] + y
    _pick_rows(pl.program_id(1), first_blocks, xa_ref, xb_ref, add)


def _proj_residual(a, w, xa, xb):
    n, k = a.shape
    d = w.shape[1]
    tm = min(TOKEN_BLOCK, xa.shape[0], xb.shape[0])
    tn = min(PROJ_COL_BLOCK, d)
    assert xa.shape[0] % tm == 0 and xb.shape[0] % tm == 0 and xa.shape[0] + xb.shape[0] == n
    first_blocks = xa.shape[0] // tm
    return pl.pallas_call(
        functools.partial(_proj_residual_kernel, first_blocks=first_blocks),
        grid=(d // tn, n // tm),
        in_specs=[
            pl.BlockSpec((tm, k), lambda j, i: (i, 0)),
            pl.BlockSpec((k, tn), lambda j, i: (0, j)),
            pl.BlockSpec((tm, tn), lambda j, i: (jnp.minimum(i, first_blocks - 1), j)),
            pl.BlockSpec((tm, tn), lambda j, i: (jnp.maximum(i - first_blocks, 0), j)),
        ],
        out_specs=pl.BlockSpec((tm, tn), lambda j, i: (i, j)),
        out_shape=jax.ShapeDtypeStruct((n, d), F32),
        compiler_params=_params("arbitrary", "arbitrary"),
        name="proj_residual",
    )(a, w, xa, xb)


def _side_cast_step(step, srcs, dsts, in_buf, out_buf, in_sem, out_sem):
    rows = in_buf.shape[1]
    per = srcs[0].shape[0] // rows
    nb = per * len(srcs)
    slot = step % 2

    def block_copy(refs, b, make):
        for k, ref in enumerate(refs):
            @pl.when(b // per == k)
            def _(k=k, ref=ref):
                make(ref.at[pl.ds(pl.multiple_of((b - k * per) * rows, rows), rows), :]).start()

    def in_copy(s):
        return lambda hbm: pltpu.make_async_copy(hbm, in_buf.at[s], in_sem.at[s])

    def out_copy(s):
        return lambda hbm: pltpu.make_async_copy(out_buf.at[s], hbm, out_sem.at[s])

    @pl.when(step == 0)
    def _():
        block_copy(srcs, 0, in_copy(0))

    @pl.when((step >= 2) & (step - 2 < nb))
    def _():
        out_copy(slot)(dsts[0].at[pl.ds(0, rows), :]).wait()

    @pl.when(step < nb)
    def _():
        in_copy(slot)(srcs[0].at[pl.ds(0, rows), :]).wait()

        @pl.when(step + 1 < nb)
        def _():
            block_copy(srcs, step + 1, in_copy(1 - slot))

        out_buf[slot] = in_buf[slot].astype(BF16)
        block_copy(dsts, step, out_copy(slot))


def _side_cast_fits(srcs, rows, n_steps):
    r = srcs[0].shape[0]
    return r % rows == 0 and (r // rows) * len(srcs) + 2 <= n_steps


def _side_cast_specs(srcs, rows, n_steps):
    r, c = srcs[0].shape
    assert all(a.shape == (r, c) for a in srcs) and _side_cast_fits(srcs, rows, n_steps)
    any_spec = pl.BlockSpec(memory_space=pl.ANY)
    return ([any_spec] * len(srcs), [any_spec] * len(srcs),
            [jax.ShapeDtypeStruct((r, c), BF16) for _ in srcs],
            [pltpu.VMEM((2, rows, c), F32), pltpu.VMEM((2, rows, c), BF16),
             pltpu.SemaphoreType.DMA((2,)), pltpu.SemaphoreType.DMA((2,))])


def _silu_mul(g, u):
    return g * (1.0 / (1.0 + jnp.exp(-g))) * u


def _ffn_kernel(x_ref, gain_ref, wg_ref, wu_ref, wd_ref, *rest, n_side):
    srcs, (o_ref, *dsts), (h_ref, acc_ref, *side) = rest[:n_side], rest[n_side:2 * n_side + 1], rest[2 * n_side + 1:]
    j = pl.program_id(1)
    if n_side:
        _side_cast_step(pl.program_id(0) * pl.num_programs(1) + j, srcs, dsts, *side)

    @pl.when(j == 0)
    def _():
        h_ref[...] = _rms_rows(x_ref[...], gain_ref[...]).astype(BF16)
        acc_ref[...] = jnp.zeros_like(acc_ref)

    h = h_ref[...]
    g = jnp.dot(h, wg_ref[...], preferred_element_type=F32)
    u = jnp.dot(h, wu_ref[...], preferred_element_type=F32)
    acc_ref[...] += jnp.dot(_silu_mul(g, u).astype(BF16), wd_ref[...], preferred_element_type=F32)

    @pl.when(j == pl.num_programs(1) - 1)
    def _():
        o_ref[...] = x_ref[...] + acc_ref[...]


def _dense_ffn(x, gain, w_gate, w_up, w_down, side_srcs=(), side_rows=0):
    n, d = x.shape
    f = w_gate.shape[1]
    tm = min(FFN_TOKEN_BLOCK, n)
    tf = FFN_F_BLOCK
    assert f % tf == 0
    grid = (n // tm, f // tf)
    side_in, side_out, side_shapes, side_scratch = (
        _side_cast_specs(side_srcs, side_rows, grid[0] * grid[1]) if side_srcs else ([], [], [], []))
    return pl.pallas_call(
        functools.partial(_ffn_kernel, n_side=len(side_srcs)),
        grid=grid,
        in_specs=[
            pl.BlockSpec((tm, d), lambda i, j: (i, 0), pipeline_mode=pl.Buffered(1)),
            pl.BlockSpec((1, d), lambda i, j: (0, 0)),
            pl.BlockSpec((d, tf), lambda i, j: (0, j)),
            pl.BlockSpec((d, tf), lambda i, j: (0, j)),
            pl.BlockSpec((tf, d), lambda i, j: (j, 0)),
        ] + side_in,
        out_specs=[pl.BlockSpec((tm, d), lambda i, j: (i, 0))] + side_out,
        out_shape=[jax.ShapeDtypeStruct((n, d), F32)] + side_shapes,
        scratch_shapes=[pltpu.VMEM((tm, d), BF16), pltpu.VMEM((tm, d), F32)] + side_scratch,
        compiler_params=_params("arbitrary", "arbitrary", vmem_limit=FFN_VMEM_LIMIT),
        name="dense_ffn",
    )(x, gain.reshape(1, d), w_gate, w_up, w_down, *side_srcs)


def _band_kernel(q_ref, kp_ref, k_ref, kn_ref, vp_ref, v_ref, vn_ref, o_ref, lse_ref, kwin_ref, vwin_ref, *,
                 first_blocks, per_seq_a, per_seq_b, tq, half):
    tqb = q_ref.shape[0]
    kw = tq + 2 * half
    pos, per_seq = _two_batch_position(pl.program_id(1), first_blocks, per_seq_a, per_seq_b)
    q0 = pos * tqb
    length = per_seq * tqb
    for win, parts in ((kwin_ref, (kp_ref, k_ref, kn_ref)), (vwin_ref, (vp_ref, v_ref, vn_ref))):
        win[:half] = parts[0][...]
        win[half:half + tqb] = parts[1][...]
        win[half + tqb:] = parts[2][...]
    lane = lax.broadcasted_iota(jnp.int32, (tq, LANES), 1)

    def body(t, carry):
        r0 = pl.multiple_of(t * tq, tq)
        qpos = q0 + r0 + lax.broadcasted_iota(jnp.int32, (tq, kw), 0)
        kpos = q0 + r0 - half + lax.broadcasted_iota(jnp.int32, (tq, kw), 1)
        ok = (jnp.abs(qpos - kpos) <= half) & (kpos >= 0) & (kpos < length)
        lse_tile = jnp.zeros((tq, LANES), F32)
        for h in range(q_ref.shape[1] // HEAD_DIM):
            sl = slice(h * HEAD_DIM, (h + 1) * HEAD_DIM)
            q = q_ref[pl.ds(r0, tq), sl]
            k = kwin_ref[pl.ds(r0, kw), sl]
            v = vwin_ref[pl.ds(r0, kw), sl]
            s = lax.dot_general(q, k, (((1,), (1,)), ((), ())), preferred_element_type=F32) * ATTN_SCALE
            s = jnp.where(ok, s, NEG_INF)
            m = jnp.max(s, axis=-1, keepdims=True)
            p = jnp.exp(s - m)
            den = jnp.sum(p, axis=-1, keepdims=True)
            o = jnp.dot(p.astype(BF16), v, preferred_element_type=F32) / den
            o_ref[pl.ds(r0, tq), sl] = o
            lse_tile = jnp.where(lane == h, m + jnp.log(den), lse_tile)
        lse_ref[pl.ds(r0, tq), :] = lse_tile
        return carry

    lax.fori_loop(0, tqb // tq, body, 0)


def _band_attention(qkv, trunks, group):
    window, dil = DIL_CONFIGS[group]
    half = window // (2 * dil)
    _, _, rows, width = qkv.shape
    (off_a, b_a, s_a), (off_b, b_b, s_b) = trunks
    len_a, len_b = s_a // dil, s_b // dil
    assert s_a % dil == 0 and s_b % dil == 0 and qkv.shape[1] == dil and off_a == 0 and off_b == b_a * s_a
    tqb = min(BAND_Q_BLOCK, len_a, len_b)
    tq = min(BAND_Q_TILE, tqb)
    assert len_a % tqb == 0 and len_b % tqb == 0 and tqb % tq == 0 and tqb % half == 0
    hpb = tqb // half
    first_blocks, per_seq_a, per_seq_b = off_b // dil // tqb, len_a // tqb, len_b // tqb

    def halo(which, side):
        def index(r, t):
            pos, per_seq = _two_batch_position(t, first_blocks, per_seq_a, per_seq_b)
            edge = (pos == per_seq - 1) if side else (pos == 0)
            blk = t * hpb + (jnp.where(edge, hpb - 1, hpb) if side else jnp.where(edge, 0, -1))
            return (which, r, blk, 0)
        return index

    mspec = lambda which: pl.BlockSpec((None, None, tqb, width), lambda r, t: (which, r, t, 0))
    hspec = lambda which, side: pl.BlockSpec((None, None, half, width), halo(which, side))
    return pl.pallas_call(
        functools.partial(_band_kernel, first_blocks=first_blocks, per_seq_a=per_seq_a, per_seq_b=per_seq_b,
                          tq=tq, half=half),
        grid=(dil, rows // tqb),
        in_specs=[
            mspec(0),
            hspec(1, 0), mspec(1), hspec(1, 1),
            hspec(2, 0), mspec(2), hspec(2, 1),
        ],
        out_specs=[
            pl.BlockSpec((None, tqb, width), lambda r, t: (r, t, 0)),
            pl.BlockSpec((None, tqb, LANES), lambda r, t: (r, t, 0)),
        ],
        out_shape=[jax.ShapeDtypeStruct((dil, rows, width), F32), jax.ShapeDtypeStruct((dil, rows, LANES), F32)],
        scratch_shapes=[pltpu.VMEM((tqb + 2 * half, width), BF16), pltpu.VMEM((tqb + 2 * half, width), BF16)],
        compiler_params=_params("arbitrary", "arbitrary"),
        name="band_attention",
    )(qkv, qkv, qkv, qkv, qkv, qkv, qkv)


def _merge_proj_kernel(*refs, dils):
    ng = len(dils)
    o_refs, l_refs = refs[:ng], refs[ng:2 * ng]
    w_ref, x_ref, out_ref, a_ref, os_ref, ls_ref = refs[2 * ng:]
    tm = x_ref.shape[0]

    heads = a_ref.shape[1] // HEAD_DIM
    for g, dil in enumerate(dils):
        if dil == 1:
            continue
        for r in range(dil):
            ls_ref[g, pl.ds(r, tm // dil, stride=dil), :] = l_refs[g][r]
            for h in range(heads):
                os_ref[g, h, pl.ds(r, tm // dil, stride=dil), :] = o_refs[g][r, :, h * HEAD_DIM:(h + 1) * HEAD_DIM]

    def group_out(g, h):
        if dils[g] == 1:
            return o_refs[g][0, :, h * HEAD_DIM:(h + 1) * HEAD_DIM]
        return os_ref[g, h]

    lse = [l_refs[g][0] if dils[g] == 1 else ls_ref[g] for g in range(ng)]
    m = functools.reduce(jnp.maximum, lse)
    e = [jnp.exp(l - m) for l in lse]
    tot = functools.reduce(lambda p, q: p + q, e)
    wts = [eg / tot for eg in e]
    for h in range(heads):
        terms = [wts[g][:, h:h + 1] * group_out(g, h) for g in range(ng)]
        a_ref[:, h * HEAD_DIM:(h + 1) * HEAD_DIM] = functools.reduce(lambda p, q: p + q, terms).astype(BF16)
    out_ref[...] = x_ref[...] + jnp.dot(a_ref[...], w_ref[...], preferred_element_type=F32)


def _merge_proj(outs, lses, w, x):
    n, d = x.shape
    width = w.shape[0]
    dils = tuple(o.shape[0] for o in outs)
    tm = min(SMALL_TOKEN_BLOCK, n)
    row = lambda i: (i, 0)
    grp = lambda i: (0, i, 0)
    return pl.pallas_call(
        functools.partial(_merge_proj_kernel, dils=dils),
        grid=(n // tm,),
        in_specs=[pl.BlockSpec((dil, tm // dil, width), grp) for dil in dils]
        + [pl.BlockSpec((dil, tm // dil, LANES), grp) for dil in dils]
        + [pl.BlockSpec((width, d), lambda i: (0, 0)), pl.BlockSpec((tm, d), row)],
        out_specs=pl.BlockSpec((tm, d), row),
        out_shape=jax.ShapeDtypeStruct((n, d), F32),
        scratch_shapes=[
            pltpu.VMEM((tm, width), BF16),
            pltpu.VMEM((len(dils), width // HEAD_DIM, tm, HEAD_DIM), F32),
            pltpu.VMEM((len(dils), tm, LANES), F32),
        ],
        compiler_params=_params("arbitrary"),
        name="merge_proj",
    )(*outs, *lses, w, x)


def _store_chunk_rows(ref, x):
    rows, d = x.shape
    nc = d // LANES
    for c in range(nc):
        ref[pl.ds(c, rows, stride=nc), :] = x[:, c * LANES:(c + 1) * LANES]


def _store_packed_rows(ref, x):
    rows, d = x.shape
    nw = d // (2 * LANES)
    for c in range(nw):
        lo = x[:, (2 * c) * LANES:(2 * c + 1) * LANES].astype(BF16).astype(F32)
        hi = x[:, (2 * c + 1) * LANES:(2 * c + 2) * LANES].astype(BF16).astype(F32)
        word = (pltpu.bitcast(lo, jnp.uint32) >> 16) | (pltpu.bitcast(hi, jnp.uint32) & jnp.uint32(0xFFFF0000))
        ref[pl.ds(c, rows, stride=nw), :] = word


def _load_packed_rows(ref, lead, rows, nw, c):
    word = ref[lead, pl.ds(c, rows, stride=nw), :]
    lo = pltpu.bitcast(word << 16, F32)
    hi = pltpu.bitcast(word & jnp.uint32(0xFFFF0000), F32)
    return lo.astype(BF16), hi.astype(BF16)


def _router_kernel(x_ref, gain_ref, wr_ref, h_ref, idx_ref, gate_ref):
    h = _rms_rows(x_ref[...], gain_ref[...])
    _store_packed_rows(h_ref, h)
    w = wr_ref[...]
    h_hi, w_hi = h.astype(BF16), w.astype(BF16)
    h_lo, w_lo = (h - h_hi.astype(F32)).astype(BF16), (w - w_hi.astype(F32)).astype(BF16)
    logits = (jnp.dot(h_hi, w_hi, preferred_element_type=F32)
              + (jnp.dot(h_hi, w_lo, preferred_element_type=F32) + jnp.dot(h_lo, w_hi, preferred_element_type=F32)))
    lane = lax.broadcasted_iota(jnp.int32, logits.shape, 1)
    lane_f = lane.astype(F32)
    logits = jnp.where(lane < N_EXPERTS, logits, -jnp.inf)
    m1 = jnp.max(logits, axis=-1, keepdims=True)
    i1 = jnp.min(jnp.where(logits == m1, lane_f, float(LANES)), axis=-1, keepdims=True)
    rest = jnp.where(lane_f == i1, -jnp.inf, logits)
    m2 = jnp.max(rest, axis=-1, keepdims=True)
    i2 = jnp.min(jnp.where(rest == m2, lane_f, float(LANES)), axis=-1, keepdims=True)
    e = jnp.exp(m2 - m1)
    tot = 1.0 + e
    idx_ref[...] = jnp.where(lane == 0, i1, jnp.where(lane == 1, i2, 0.0)).astype(jnp.int32)
    gate_ref[...] = jnp.where(lane == 0, 1.0 / tot, jnp.where(lane == 1, e / tot, 0.0))


def _router(x, gain, w_router):
    n, d = x.shape
    tm = min(SMALL_TOKEN_BLOCK, n)
    wr = jnp.zeros((d, LANES), F32).at[:, :N_EXPERTS].set(w_router.astype(F32))
    row = lambda i: (i, 0)
    return pl.pallas_call(
        _router_kernel,
        grid=(n // tm,),
        in_specs=[
            pl.BlockSpec((tm, d), row),
            pl.BlockSpec((1, d), lambda i: (0, 0)),
            pl.BlockSpec((d, LANES), lambda i: (0, 0)),
        ],
        out_specs=[pl.BlockSpec((tm * (d // (2 * LANES)), LANES), row), pl.BlockSpec((tm, LANES), row),
                   pl.BlockSpec((tm, LANES), row)],
        out_shape=[
            jax.ShapeDtypeStruct((n * (d // (2 * LANES)), LANES), jnp.uint32),
            jax.ShapeDtypeStruct((n, LANES), jnp.int32),
            jax.ShapeDtypeStruct((n, LANES), F32),
        ],
        compiler_params=_params("arbitrary"),
        name="router",
    )(x, gain.reshape(1, d), wr)


def _expert_kernel(bexp_ref, brows_ref, nused_ref, rtok_ref, h_hbm, wg_ref, wu_ref, wd_ref, y_ref,
                   xs_ref, xb_ref, acc_ref, sem):
    del bexp_ref
    i = pl.program_id(0)
    j = pl.program_id(1)
    tmb, d = xb_ref.shape
    nw = d // (2 * LANES)
    n_used = nused_ref[0]

    def row_copy(tok, slot, r):
        return pltpu.make_async_copy(h_hbm.at[pl.ds(pl.multiple_of(tok * nw, nw), nw), :],
                                     xs_ref.at[slot, pl.ds(pl.multiple_of(r * nw, nw), nw), :], sem.at[slot])

    def start_gather(blk, slot):
        def issue(r, carry):
            row_copy(rtok_ref[blk * tmb + r], slot, r).start()
            return carry
        lax.fori_loop(0, tmb, issue, 0, unroll=8)

    def wait_gather(slot):
        pltpu.make_async_copy(h_hbm.at[pl.ds(0, tmb * nw), :], xs_ref.at[slot], sem.at[slot]).wait()

    @pl.when(i < n_used)
    def _():
        slot = i % 2

        @pl.when(j == 0)
        def _():
            @pl.when(i == 0)
            def _():
                start_gather(0, 0)

            wait_gather(slot)

            @pl.when(i + 1 < n_used)
            def _():
                start_gather(i + 1, 1 - slot)

            for c in range(nw):
                lo, hi = _load_packed_rows(xs_ref, slot, tmb, nw, c)
                xb_ref[:, (2 * c) * LANES:(2 * c + 1) * LANES] = lo
                xb_ref[:, (2 * c + 1) * LANES:(2 * c + 2) * LANES] = hi
            acc_ref[...] = jnp.zeros_like(acc_ref)

        def swiglu_rows(rows):
            x = xb_ref[:rows]
            g = jnp.dot(x, wg_ref[0], preferred_element_type=F32)
            u = jnp.dot(x, wu_ref[0], preferred_element_type=F32)
            acc_ref[:rows] += jnp.dot(_silu_mul(g, u).astype(BF16), wd_ref[0], preferred_element_type=F32)

        half_full = brows_ref[i] <= tmb // 2

        @pl.when(half_full)
        def _():
            swiglu_rows(tmb // 2)

        @pl.when(jnp.logical_not(half_full))
        def _():
            swiglu_rows(tmb)

    last = j == pl.num_programs(1) - 1

    @pl.when(last & (i < n_used))
    def _():
        _store_chunk_rows(y_ref, acc_ref[...])

    @pl.when(last & (i >= n_used))
    def _():
        y_ref[...] = jnp.zeros_like(y_ref)


def _expert_ffn(h_rows, block_expert, block_rows, n_used, row_tok, w_gate, w_up, w_down):
    d = w_gate.shape[1]
    f = w_gate.shape[2]
    tmb = MOE_BLOCK
    tf = MOE_F_BLOCK
    n_blocks = block_expert.shape[0]
    nf = f // tf
    assert f % tf == 0

    def fblk(i, j, nu):
        return jnp.where(i < nu[0], j, nf - 1)

    grid_spec = pltpu.PrefetchScalarGridSpec(
        num_scalar_prefetch=4,
        grid=(n_blocks, nf),
        in_specs=[
            pl.BlockSpec(memory_space=pl.ANY),
            pl.BlockSpec((1, d, tf), lambda i, j, be, br, nu, rt: (be[i], 0, fblk(i, j, nu))),
            pl.BlockSpec((1, d, tf), lambda i, j, be, br, nu, rt: (be[i], 0, fblk(i, j, nu))),
            pl.BlockSpec((1, tf, d), lambda i, j, be, br, nu, rt: (be[i], fblk(i, j, nu), 0)),
        ],
        out_specs=pl.BlockSpec((tmb * (d // LANES), LANES), lambda i, j, be, br, nu, rt: (i, 0)),
        scratch_shapes=[
            pltpu.VMEM((2, tmb * (d // (2 * LANES)), LANES), jnp.uint32),
            pltpu.VMEM((tmb, d), BF16),
            pltpu.VMEM((tmb, d), F32),
            pltpu.SemaphoreType.DMA((2,)),
        ],
    )
    return pl.pallas_call(
        _expert_kernel,
        grid_spec=grid_spec,
        out_shape=jax.ShapeDtypeStruct((n_blocks * tmb * (d // LANES), LANES), F32),
        compiler_params=_params("arbitrary", "arbitrary"),
        name="expert_ffn",
    )(block_expert, block_rows, n_used, row_tok, h_rows, w_gate, w_up, w_down)


def _combine_kernel(dest_ref, ys_hbm, x_ref, gate_ref, oa_ref, ob_ref, buf_ref, sem, *, first_blocks):
    i = pl.program_id(0)
    nsteps = pl.num_programs(0)
    tc, d = x_ref.shape
    nc = d // LANES
    pitch = buf_ref.shape[1] // (2 * tc)

    def row_copy(row, slot, dst):
        return pltpu.make_async_copy(ys_hbm.at[pl.ds(pl.multiple_of(row * nc, nc), nc), :],
                                     buf_ref.at[slot, pl.ds(pl.multiple_of(dst * pitch, 8), nc), :], sem.at[slot])

    def start_gather(step, slot):
        def issue(r, carry):
            row_copy(dest_ref[step * (2 * tc) + r], slot, r).start()
            return carry
        lax.fori_loop(0, 2 * tc, issue, 0, unroll=8)

    def wait_gather(slot):
        pltpu.make_async_copy(ys_hbm.at[pl.ds(0, 2 * tc * nc), :], buf_ref.at[slot, pl.ds(0, 2 * tc * nc), :],
                              sem.at[slot]).wait()

    slot = i % 2

    @pl.when(i == 0)
    def _():
        start_gather(0, 0)

    wait_gather(slot)

    @pl.when(i + 1 < nsteps)
    def _():
        start_gather(i + 1, 1 - slot)

    g0 = gate_ref[:, 0:1]
    g1 = gate_ref[:, 1:2]

    def emit(o_ref):
        for c in range(nc):
            sl = slice(c * LANES, (c + 1) * LANES)
            y0 = buf_ref[slot, pl.ds(c, tc, stride=pitch), :]
            y1 = buf_ref[slot, pl.ds(tc * pitch + c, tc, stride=pitch), :]
            o_ref[:, sl] = x_ref[:, sl] + (y0 * g0 + y1 * g1)
    _pick_rows(i, first_blocks, oa_ref, ob_ref, emit)


def _combine(x, ys, dest, gate, n_first):
    n, d = x.shape
    tc = min(COMBINE_BLOCK, n_first, n - n_first)
    assert n_first % tc == 0 and (n - n_first) % tc == 0
    first_blocks = n_first // tc
    nc = d // LANES
    pitch = nc if (nc // 8) % 2 else nc + 8
    dest = dest.reshape(n // tc, tc, TOP_K).transpose(0, 2, 1).reshape(-1)
    grid_spec = pltpu.PrefetchScalarGridSpec(
        num_scalar_prefetch=1,
        grid=(n // tc,),
        in_specs=[
            pl.BlockSpec(memory_space=pl.ANY),
            pl.BlockSpec((tc, d), lambda i, dr: (i, 0)),
            pl.BlockSpec((tc, LANES), lambda i, dr: (i, 0)),
        ],
        out_specs=_two_batch_specs(tc, d, first_blocks),
        scratch_shapes=[pltpu.VMEM((2, 2 * tc * pitch, LANES), F32), pltpu.SemaphoreType.DMA((2,))],
    )
    return pl.pallas_call(
        functools.partial(_combine_kernel, first_blocks=first_blocks),
        grid_spec=grid_spec,
        out_shape=[jax.ShapeDtypeStruct((n_first, d), F32), jax.ShapeDtypeStruct((n - n_first, d), F32)],
        compiler_params=_params("arbitrary"),
        name="moe_combine",
    )(dest, ys, x, gate)


def _routing_tables(idx, tmb):
    n = idx.shape[0]
    nk = n * TOP_K
    e_flat = idx[:, :TOP_K].reshape(-1)
    onehot = (e_flat[:, None] == jnp.arange(N_EXPERTS, dtype=jnp.int32)[None, :]).astype(jnp.int32)
    csum = jnp.cumsum(onehot, axis=0)
    rank = jnp.sum(onehot * csum, axis=1) - 1
    counts = csum[-1]
    padded = (counts + tmb - 1) // tmb * tmb
    pad_end = jnp.cumsum(padded)
    pad_start = pad_end - padded
    dest = (jnp.sum(onehot * pad_start[None, :], axis=1) + rank).astype(jnp.int32)
    n_blocks = -(-nk // tmb) + N_EXPERTS
    block_expert = jnp.minimum(
        jnp.searchsorted(pad_end, jnp.arange(n_blocks, dtype=jnp.int32) * tmb, side="right"), N_EXPERTS - 1
    ).astype(jnp.int32)
    n_used = (pad_end[-1] // tmb).astype(jnp.int32).reshape(1)
    block_lo = jnp.arange(n_blocks, dtype=jnp.int32) * tmb - pad_start[block_expert]
    block_rows = jnp.clip(counts[block_expert] - block_lo, 0, tmb).astype(jnp.int32)
    row_tok = jnp.zeros((n_blocks * tmb,), jnp.int32).at[dest].set(jnp.arange(nk, dtype=jnp.int32) // TOP_K)
    return dest, block_expert, block_rows, n_used, row_tok


def _rope_tables(seq_len):
    half = HEAD_DIM // 2
    inv_freq = ROPE_THETA ** (-jnp.arange(half, dtype=F32) / half)
    ang = jnp.arange(seq_len, dtype=F32)[:, None] * inv_freq[None, :]
    cos, sin = jnp.cos(ang), jnp.sin(ang)
    return jnp.concatenate([cos, cos], axis=-1), jnp.concatenate([-sin, sin], axis=-1)


def kernel(x_prompt, x_sample, norm_mix, norm_ffn, na_w_qkv, na_q_gain, na_k_gain, na_rpb, na_w_o,
           da_w_qkv, da_q_gain, da_k_gain, da_w_o, ffn_w_gate, ffn_w_up, ffn_w_down,
           moe_w_router, moe_w_gate, moe_w_up, moe_w_down):
    d = x_prompt.shape[-1]
    trunks = []
    off = 0
    for xin in (x_prompt, x_sample):
        b, s, _ = xin.shape
        trunks.append((off, b, s))
        off += b * s
    n = off
    xa, xb = x_prompt.reshape(-1, d), x_sample.reshape(-1, d)
    depth = norm_mix.shape[0]
    assert depth == 2, "layer 0 reads the two request batches directly and layer 1 writes them separately"
    x = None

    for layer in range(depth):
        lj = layer // 2
        if layer % 2 == 0:
            heads = na_w_qkv.shape[2] // (3 * HEAD_DIM)
            per = heads * HEAD_DIM // PROJ_COL_BLOCK
            hg = jnp.concatenate([
                jnp.broadcast_to(na_q_gain[lj], (per, HEAD_DIM)),
                jnp.broadcast_to(na_k_gain[lj], (per, HEAD_DIM)),
                jnp.ones((per, HEAD_DIM), F32),
            ]).reshape(3 * per, 1, HEAD_DIM)
            qkv = _qkv_project(xa, xb, norm_mix[layer], na_w_qkv[lj].astype(BF16), hg, 3 * per, 2 * per)
            tables = _na_bias_tables(na_rpb[lj])
            att = _na_attention(qkv, tables, trunks, heads)
            x = _proj_residual(att, na_w_o[lj].astype(BF16), xa, xb)
            ne, dm, fe = moe_w_gate.shape[1:]
            wide = (moe_w_gate[lj].reshape(ne * dm, fe), moe_w_up[lj].reshape(ne * dm, fe))
            steps = (n // min(FFN_TOKEN_BLOCK, n)) * (ffn_w_gate.shape[2] // FFN_F_BLOCK)
            shadow = _side_cast_fits(wide, SIDE_CAST_ROWS_WIDE, steps)
            x, *cast = _dense_ffn(x, norm_ffn[layer], ffn_w_gate[lj].astype(BF16), ffn_w_up[lj].astype(BF16),
                                  ffn_w_down[lj].astype(BF16), side_srcs=wide if shadow else (),
                                  side_rows=SIDE_CAST_ROWS_WIDE)
            moe_gate_bf, moe_up_bf = cast if shadow else [w.astype(BF16) for w in wide]
        else:
            groups = len(DIL_CONFIGS)
            heads = da_w_qkv.shape[2] // (groups * 3 * HEAD_DIM)
            dils = [dil for _, dil in DIL_CONFIGS]
            ones = jnp.ones((HEAD_DIM,), F32)
            hg = jnp.stack([t for g in range(groups) for t in (da_q_gain[lj, g], da_k_gain[lj, g], ones)])
            hg = hg.reshape(groups * 3, 1, HEAD_DIM)
            (o0, b0, s0), (o1, b1, s1) = trunks
            tq = min(QKV_DIL_TOKEN_BLOCK, n)
            assert s0 % tq == 0 and s1 % tq == 0 and o1 % tq == 0
            nb0, p0, p1 = o1 // tq, s0 // tq, s1 // tq
            pos_map = lambda i: jnp.where(i < nb0, i % p0, (i - nb0) % p1)
            smax = max(s0, s1)
            rope = [jnp.stack([t.reshape(smax // tq, tq // dil, dil, HEAD_DIM).transpose(0, 2, 1, 3)
                               .reshape(smax, HEAD_DIM) for dil in dils]) for t in _rope_tables(smax)]
            narrow = (moe_w_down[lj].reshape(ne * fe, dm),)
            shadow = _side_cast_fits(narrow, SIDE_CAST_ROWS_NARROW, (n // tq) * groups * 3)
            qkvs = list(_qkv_project_dilated(
                x, norm_mix[layer], da_w_qkv[lj].astype(BF16), hg, rope, pos_map, dils, heads, tq,
                side_srcs=narrow if shadow else (), side_rows=SIDE_CAST_ROWS_NARROW))
            moe_down_bf = qkvs.pop() if shadow else narrow[0].astype(BF16)
            outs, lses = [], []
            for g in range(groups):
                o_g, lse_g = _band_attention(qkvs[g], trunks, g)
                outs.append(o_g)
                lses.append(lse_g)
            x = _merge_proj(outs, lses, da_w_o[lj].astype(BF16), x)
            h, idx, gate = _router(x, norm_ffn[layer], moe_w_router[lj])
            dest, block_expert, block_rows, n_used, row_tok = _routing_tables(idx, MOE_BLOCK)
            ys = _expert_ffn(h, block_expert, block_rows, n_used, row_tok, moe_gate_bf.reshape(ne, dm, fe),
                             moe_up_bf.reshape(ne, dm, fe), moe_down_bf.reshape(ne, fe, dm))
            ya, yb = _combine(x, ys, dest, gate, xa.shape[0])

    return ya.reshape(x_prompt.shape), yb.reshape(x_sample.shape)
```

```python
import functools

import jax
import jax.numpy as jnp
from jax import lax
from jax.experimental import pallas as pl
from jax.experimental.pallas import tpu as pltpu

F32 = jnp.float32
BF16 = jnp.bfloat16

HEAD_DIM = 128
GRID_W = 64
NA_ROWS = 8
NA_COLS = 16
NA_GROUP = NA_ROWS // 2
NA_KEY_ROWS = 3 * NA_GROUP
NA_HEAD_BLOCK = 16
DIL_CONFIGS = ((128, 1), (512, 4), (2048, 16))
N_EXPERTS = 8
TOP_K = 2
ROPE_THETA = 10000.0
EPS = 1e-6
NEG_INF = -1e30
ATTN_SCALE = HEAD_DIM ** -0.5

LANES = 128
MXU_COLS = 256
VMEM_LIMIT = 52 * 1024 * 1024
FFN_VMEM_LIMIT = 57 * 1024 * 1024

TOKEN_BLOCK = 1024
PROJ_COL_BLOCK = 1024
SMALL_TOKEN_BLOCK = 512
QKV_DIL_TOKEN_BLOCK = 512
FFN_TOKEN_BLOCK = 1024
FFN_F_BLOCK = 256
BAND_Q_BLOCK = 512
BAND_Q_TILE = 128
MOE_BLOCK = 512
MOE_F_BLOCK = 1024
COMBINE_BLOCK = 256
SIDE_CAST_ROWS_WIDE = 128
SIDE_CAST_ROWS_NARROW = 512


def _params(*sem, vmem_limit=VMEM_LIMIT):
    return pltpu.CompilerParams(dimension_semantics=sem, vmem_limit_bytes=vmem_limit)


def _rms_rows(x, gain):
    ms = jnp.mean(x * x, axis=-1, keepdims=True)
    return x * lax.rsqrt(ms + EPS) * gain


def _pick_rows(i, first_blocks, xa_ref, xb_ref, use):
    @pl.when(i < first_blocks)
    def _():
        use(xa_ref)

    @pl.when(i >= first_blocks)
    def _():
        use(xb_ref)


def _two_batch_specs(tm, d, first_blocks, **kw):
    return [pl.BlockSpec((tm, d), lambda i, *_: (jnp.minimum(i, first_blocks - 1), 0), **kw),
            pl.BlockSpec((tm, d), lambda i, *_: (jnp.maximum(i - first_blocks, 0), 0), **kw)]


def _qkv_kernel(xa_ref, xb_ref, gain_ref, w_ref, hg_ref, o_ref, h_ref, *, first_blocks, v_every, v_from):
    j = pl.program_id(1)

    @pl.when(j == 0)
    def _():
        def norm(x_ref):
            h_ref[...] = _rms_rows(x_ref[...], gain_ref[...]).astype(BF16)
        _pick_rows(pl.program_id(0), first_blocks, xa_ref, xb_ref, norm)

    is_qk = (j % v_every) < v_from
    for c in range(w_ref.shape[1] // MXU_COLS):
        acc = jnp.dot(h_ref[...], w_ref[:, c * MXU_COLS:(c + 1) * MXU_COLS], preferred_element_type=F32)
        for h in range(MXU_COLS // HEAD_DIM):
            seg = acc[:, h * HEAD_DIM:(h + 1) * HEAD_DIM]
            lo = c * MXU_COLS + h * HEAD_DIM
            o_ref[:, lo:lo + HEAD_DIM] = jnp.where(is_qk, _rms_rows(seg, hg_ref[0]), seg).astype(o_ref.dtype)


def _qkv_project(xa, xb, gain, w, head_gain, v_every, v_from):
    d = xa.shape[1]
    n = xa.shape[0] + xb.shape[0]
    n_out = w.shape[1]
    tm = min(TOKEN_BLOCK, xa.shape[0], xb.shape[0])
    assert xa.shape[0] % tm == 0 and xb.shape[0] % tm == 0
    first_blocks = xa.shape[0] // tm
    tn = n_out // head_gain.shape[0]
    return pl.pallas_call(
        functools.partial(_qkv_kernel, first_blocks=first_blocks, v_every=v_every, v_from=v_from),
        grid=(n // tm, n_out // tn),
        in_specs=_two_batch_specs(tm, d, first_blocks, pipeline_mode=pl.Buffered(1)) + [
            pl.BlockSpec((1, d), lambda i, j: (0, 0)),
            pl.BlockSpec((d, tn), lambda i, j: (0, j)),
            pl.BlockSpec((1, 1, HEAD_DIM), lambda i, j: (j, 0, 0)),
        ],
        out_specs=pl.BlockSpec((tm, tn), lambda i, j: (i, j)),
        out_shape=jax.ShapeDtypeStruct((n, n_out), BF16),
        scratch_shapes=[pltpu.VMEM((tm, d), BF16)],
        compiler_params=_params("arbitrary", "arbitrary"),
        name="qkv_project",
    )(xa, xb, gain.reshape(1, d), w, head_gain)


def _qkv_dil_kernel(x_ref, gain_ref, w_ref, hg_ref, cos_ref, sin_ref, *rest, dils, n_side):
    ng = len(dils)
    srcs, out_refs, dsts = rest[:n_side], rest[n_side:n_side + ng], rest[n_side + ng:2 * n_side + ng]
    h_ref, stage_ref, *side = rest[2 * n_side + ng:]
    j = pl.program_id(1)
    if n_side:
        _side_cast_step(pl.program_id(0) * pl.num_programs(1) + j, srcs, dsts, *side)
    tm, d = x_ref.shape
    tn = w_ref.shape[1]

    @pl.when(j == 0)
    def _():
        y = _rms_rows(x_ref[...], gain_ref[...])
        for c in range(d // LANES):
            stage_ref[c] = y[:, c * LANES:(c + 1) * LANES]
        for g, dil in enumerate(dils):
            rows = tm // dil
            for r in range(dil):
                for c in range(d // LANES):
                    h_ref[g, r * rows:(r + 1) * rows, c * LANES:(c + 1) * LANES] = (
                        stage_ref[c, pl.ds(r, rows, stride=dil), :].astype(BF16))

    group = j // 3
    which = j % 3
    acc = jnp.dot(h_ref[group], w_ref[...], preferred_element_type=F32)

    for g, (dil, o_ref) in enumerate(zip(dils, out_refs)):
        rows = tm // dil

        @pl.when((group == g) & (which < 2))
        def _(dil=dil, o_ref=o_ref, rows=rows):
            for h in range(tn // HEAD_DIM):
                sl = slice(h * HEAD_DIM, (h + 1) * HEAD_DIM)
                y = _rms_rows(acc[:, sl], hg_ref[0])
                y = (y * cos_ref[...] + pltpu.roll(y, HEAD_DIM // 2, 1) * sin_ref[...]).astype(o_ref.dtype)
                for r in range(dil):
                    o_ref[r, :, sl] = y[r * rows:(r + 1) * rows]

        @pl.when((group == g) & (which == 2))
        def _(dil=dil, o_ref=o_ref, rows=rows):
            v = acc.astype(o_ref.dtype)
            for r in range(dil):
                o_ref[r] = v[r * rows:(r + 1) * rows]


def _qkv_project_dilated(x, gain, w, head_gain, rope_tabs, pos_map, dils, heads, tm, side_srcs=(), side_rows=0):
    n, d = x.shape
    n_out = w.shape[1]
    width = heads * HEAD_DIM
    ng = len(dils)
    assert n_out == ng * 3 * width and head_gain.shape[0] == ng * 3
    side_in, side_out, side_shapes, side_scratch = (
        _side_cast_specs(side_srcs, side_rows, (n // tm) * ng * 3) if side_srcs else ([], [], [], []))

    def out_map(g):
        def index(i, j):
            return (jnp.clip(j - 3 * g, 0, 2), 0, i, 0)
        return index

    rope_spec = pl.BlockSpec((None, tm, HEAD_DIM), lambda i, j: (j // 3, pos_map(i), 0))
    return pl.pallas_call(
        functools.partial(_qkv_dil_kernel, dils=tuple(dils), n_side=len(side_srcs)),
        grid=(n // tm, ng * 3),
        in_specs=[
            pl.BlockSpec((tm, d), lambda i, j: (i, 0)),
            pl.BlockSpec((1, d), lambda i, j: (0, 0)),
            pl.BlockSpec((d, width), lambda i, j: (0, j)),
            pl.BlockSpec((1, 1, HEAD_DIM), lambda i, j: (j, 0, 0)),
            rope_spec,
            rope_spec,
        ] + side_in,
        out_specs=[pl.BlockSpec((None, dil, tm // dil, width), out_map(g)) for g, dil in enumerate(dils)] + side_out,
        out_shape=[jax.ShapeDtypeStruct((3, dil, n // dil, width), BF16) for dil in dils] + side_shapes,
        scratch_shapes=[pltpu.VMEM((ng, tm, d), BF16), pltpu.VMEM((d // LANES, tm, LANES), F32)] + side_scratch,
        compiler_params=_params("arbitrary", "arbitrary"),
        name="qkv_project_dilated",
    )(x, gain.reshape(1, d), w, head_gain, *rope_tabs, *side_srcs)


def _na_table_kernel(rpb_ref, o_ref):
    h = pl.program_id(0)
    w = GRID_W
    lane = lax.broadcasted_iota(jnp.int32, (w, 2 * w), 1)
    cq = lax.broadcasted_iota(jnp.int32, (w, 2 * w), 0)
    ck = lane % w
    upper = lane >= w
    cs = jnp.clip(cq - NA_COLS // 2, 0, w - NA_COLS)
    col_ok = (ck >= cs) & (ck < cs + NA_COLS)
    dc = ck - cq + (NA_COLS - 1)

    def pair_tile(dr):
        t = jnp.zeros((w, 2 * w), F32)
        for b in range(2 * NA_COLS - 1):
            lo = rpb_ref[h, dr + NA_ROWS - 1, b]
            hi = rpb_ref[h, dr + NA_ROWS, b]
            t = jnp.where(dc == b, jnp.where(upper, hi, lo), t)
        return t

    tiles = {}
    for qi in range(NA_GROUP):
        for kp in range(NA_KEY_ROWS // 2):
            dr = 2 * kp - NA_GROUP - qi
            if dr not in tiles:
                tiles[dr] = pair_tile(dr)
            for v in range(3):
                def row_ok(kj):
                    if v == 0:
                        return kj >= NA_GROUP
                    if v == 1:
                        return 0 <= kj - qi < NA_ROWS
                    return kj < NA_ROWS
                ok_lo, ok_hi = row_ok(2 * kp), row_ok(2 * kp + 1)
                if ok_lo and ok_hi:
                    ok = col_ok
                elif ok_lo:
                    ok = col_ok & jnp.logical_not(upper)
                elif ok_hi:
                    ok = col_ok & upper
                else:
                    ok = None
                tile = jnp.full((w, 2 * w), NEG_INF, F32) if ok is None else jnp.where(ok, tiles[dr], NEG_INF)
                o_ref[v, 0, qi * w:(qi + 1) * w, kp * 2 * w:(kp + 1) * 2 * w] = tile


def _na_bias_tables(rpb):
    assert NA_GROUP == NA_ROWS // 2 and NA_KEY_ROWS % 2 == 0 and 2 * GRID_W == LANES
    heads = rpb.shape[0]
    shape = (3, heads, NA_GROUP * GRID_W, NA_KEY_ROWS * GRID_W)
    return pl.pallas_call(
        _na_table_kernel,
        grid=(heads,),
        in_specs=[pl.BlockSpec(memory_space=pltpu.SMEM)],
        out_specs=pl.BlockSpec((3, 1) + shape[2:], lambda h: (0, h, 0, 0)),
        out_shape=jax.ShapeDtypeStruct(shape, F32),
        compiler_params=_params("arbitrary"),
        name="na_bias_table",
    )(rpb.astype(F32))


def _na_kernel(q_ref, kp_ref, k_ref, kn_ref, vp_ref, v_ref, vn_ref, tab_ref, o_ref):
    gq = q_ref.shape[0]
    nt = (((1,), (1,)), ((), ()))
    for h in range(q_ref.shape[1] // HEAD_DIM):
        sl = slice(h * HEAD_DIM, (h + 1) * HEAD_DIM)
        q = q_ref[:, sl]
        s = jnp.concatenate(
            [lax.dot_general(q, kr[:, sl], nt, preferred_element_type=F32) for kr in (kp_ref, k_ref, kn_ref)],
            axis=1)
        s = s * ATTN_SCALE + tab_ref[0, h]
        m = jnp.max(s, axis=-1, keepdims=True)
        p = jnp.exp(s - m)
        den = jnp.sum(p, axis=-1, keepdims=True)
        pb = p.astype(BF16)
        o = (jnp.dot(pb[:, :gq], vp_ref[:, sl], preferred_element_type=F32)
             + jnp.dot(pb[:, gq:2 * gq], v_ref[:, sl], preferred_element_type=F32)
             + jnp.dot(pb[:, 2 * gq:], vn_ref[:, sl], preferred_element_type=F32))
        o_ref[:, sl] = (o / den).astype(o_ref.dtype)


def _two_batch_position(t, first_blocks, per_seq_a, per_seq_b):
    in_a = t < first_blocks
    per_seq = jnp.where(in_a, per_seq_a, per_seq_b)
    return jnp.where(in_a, t, t - first_blocks) % per_seq, per_seq


def _na_attention(qkv, tables, trunks, heads):
    n = qkv.shape[0]
    gq = NA_GROUP * GRID_W
    hb = NA_HEAD_BLOCK
    hblocks = heads // hb
    (off_a, b_a, s_a), (off_b, b_b, s_b) = trunks
    for (_, _, seq) in trunks:
        assert seq % gq == 0 and seq // GRID_W >= NA_ROWS
    assert heads % hb == 0 and off_a == 0 and off_b == b_a * s_a
    first_groups, gps_a, gps_b = off_b // gq, s_a // gq, s_b // gq

    def place(g):
        pos, per_seq = _two_batch_position(g, first_groups, gps_a, gps_b)
        return pos == 0, pos == per_seq - 1

    def variant(g):
        first, last = place(g)
        return jnp.where(first, 0, jnp.where(last, 2, 1))

    def rows_of(which, shift):
        def index(h, g):
            first, last = place(g)
            gg = g + jnp.where((shift < 0) & first, 0, jnp.where((shift > 0) & last, 0, shift))
            return (gg, which * hblocks + h)
        return index

    blk = lambda index: pl.BlockSpec((gq, hb * HEAD_DIM), index)
    return pl.pallas_call(
        _na_kernel,
        grid=(hblocks, n // gq),
        in_specs=[
            blk(rows_of(0, 0)),
            blk(rows_of(1, -1)), blk(rows_of(1, 0)), blk(rows_of(1, 1)),
            blk(rows_of(2, -1)), blk(rows_of(2, 0)), blk(rows_of(2, 1)),
            pl.BlockSpec((1, hb, gq, NA_KEY_ROWS * GRID_W), lambda h, g: (variant(g), h, 0, 0)),
        ],
        out_specs=blk(rows_of(0, 0)),
        out_shape=jax.ShapeDtypeStruct((n, heads * HEAD_DIM), BF16),
        compiler_params=_params("arbitrary", "arbitrary"),
        name="na_attention",
    )(qkv, qkv, qkv, qkv, qkv, qkv, qkv, tables)


def _proj_residual_kernel(a_ref, w_ref, xa_ref, xb_ref, o_ref, *, first_blocks):
    y = jnp.dot(a_ref[...], w_ref[...], preferred_element_type=F32)

    def add(x_ref):
        o_ref[...] = x_ref[...] + y
    _pick_rows(pl.program_id(1), first_blocks, xa_ref, xb_ref, add)


def _proj_residual(a, w, xa, xb):
    n, k = a.shape
    d = w.shape[1]
    tm = min(TOKEN_BLOCK, xa.shape[0], xb.shape[0])
    tn = min(PROJ_COL_BLOCK, d)
    assert xa.shape[0] % tm == 0 and xb.shape[0] % tm == 0 and xa.shape[0] + xb.shape[0] == n
    first_blocks = xa.shape[0] // tm
    return pl.pallas_call(
        functools.partial(_proj_residual_kernel, first_blocks=first_blocks),
        grid=(d // tn, n // tm),
        in_specs=[
            pl.BlockSpec((tm, k), lambda j, i: (i, 0)),
            pl.BlockSpec((k, tn), lambda j, i: (0, j)),
            pl.BlockSpec((tm, tn), lambda j, i: (jnp.minimum(i, first_blocks - 1), j)),
            pl.BlockSpec((tm, tn), lambda j, i: (jnp.maximum(i - first_blocks, 0), j)),
        ],
        out_specs=pl.BlockSpec((tm, tn), lambda j, i: (i, j)),
        out_shape=jax.ShapeDtypeStruct((n, d), F32),
        compiler_params=_params("arbitrary", "arbitrary"),
        name="proj_residual",
    )(a, w, xa, xb)


def _side_cast_step(step, srcs, dsts, in_buf, out_buf, in_sem, out_sem):
    rows = in_buf.shape[1]
    per = srcs[0].shape[0] // rows
    nb = per * len(srcs)
    slot = step % 2

    def block_copy(refs, b, make):
        for k, ref in enumerate(refs):
            @pl.when(b // per == k)
            def _(k=k, ref=ref):
                make(ref.at[pl.ds(pl.multiple_of((b - k * per) * rows, rows), rows), :]).start()

    def in_copy(s):
        return lambda hbm: pltpu.make_async_copy(hbm, in_buf.at[s], in_sem.at[s])

    def out_copy(s):
        return lambda hbm: pltpu.make_async_copy(out_buf.at[s], hbm, out_sem.at[s])

    @pl.when(step == 0)
    def _():
        block_copy(srcs, 0, in_copy(0))

    @pl.when((step >= 2) & (step - 2 < nb))
    def _():
        out_copy(slot)(dsts[0].at[pl.ds(0, rows), :]).wait()

    @pl.when(step < nb)
    def _():
        in_copy(slot)(srcs[0].at[pl.ds(0, rows), :]).wait()

        @pl.when(step + 1 < nb)
        def _():
            block_copy(srcs, step + 1, in_copy(1 - slot))

        out_buf[slot] = in_buf[slot].astype(BF16)
        block_copy(dsts, step, out_copy(slot))


def _side_cast_fits(srcs, rows, n_steps):
    r = srcs[0].shape[0]
    return r % rows == 0 and (r // rows) * len(srcs) + 2 <= n_steps


def _side_cast_specs(srcs, rows, n_steps):
    r, c = srcs[0].shape
    assert all(a.shape == (r, c) for a in srcs) and _side_cast_fits(srcs, rows, n_steps)
    any_spec = pl.BlockSpec(memory_space=pl.ANY)
    return ([any_spec] * len(srcs), [any_spec] * len(srcs),
            [jax.ShapeDtypeStruct((r, c), BF16) for _ in srcs],
            [pltpu.VMEM((2, rows, c), F32), pltpu.VMEM((2, rows, c), BF16),
             pltpu.SemaphoreType.DMA((2,)), pltpu.SemaphoreType.DMA((2,))])


def _silu_mul(g, u):
    return g * (1.0 / (1.0 + jnp.exp(-g))) * u


def _ffn_kernel(x_ref, gain_ref, wg_ref, wu_ref, wd_ref, *rest, n_side):
    srcs, (o_ref, *dsts), (h_ref, acc_ref, *side) = rest[:n_side], rest[n_side:2 * n_side + 1], rest[2 * n_side + 1:]
    j = pl.program_id(1)
    if n_side:
        _side_cast_step(pl.program_id(0) * pl.num_programs(1) + j, srcs, dsts, *side)

    @pl.when(j == 0)
    def _():
        h_ref[...] = _rms_rows(x_ref[...], gain_ref[...]).astype(BF16)
        acc_ref[...] = jnp.zeros_like(acc_ref)

    h = h_ref[...]
    g = jnp.dot(h, wg_ref[...], preferred_element_type=F32)
    u = jnp.dot(h, wu_ref[...], preferred_element_type=F32)
    acc_ref[...] += jnp.dot(_silu_mul(g, u).astype(BF16), wd_ref[...], preferred_element_type=F32)

    @pl.when(j == pl.num_programs(1) - 1)
    def _():
        o_ref[...] = x_ref[...] + acc_ref[...]


def _dense_ffn(x, gain, w_gate, w_up, w_down, side_srcs=(), side_rows=0):
    n, d = x.shape
    f = w_gate.shape[1]
    tm = min(FFN_TOKEN_BLOCK, n)
    tf = FFN_F_BLOCK
    assert f % tf == 0
    grid = (n // tm, f // tf)
    side_in, side_out, side_shapes, side_scratch = (
        _side_cast_specs(side_srcs, side_rows, grid[0] * grid[1]) if side_srcs else ([], [], [], []))
    return pl.pallas_call(
        functools.partial(_ffn_kernel, n_side=len(side_srcs)),
        grid=grid,
        in_specs=[
            pl.BlockSpec((tm, d), lambda i, j: (i, 0), pipeline_mode=pl.Buffered(1)),
            pl.BlockSpec((1, d), lambda i, j: (0, 0)),
            pl.BlockSpec((d, tf), lambda i, j: (0, j)),
            pl.BlockSpec((d, tf), lambda i, j: (0, j)),
            pl.BlockSpec((tf, d), lambda i, j: (j, 0)),
        ] + side_in,
        out_specs=[pl.BlockSpec((tm, d), lambda i, j: (i, 0))] + side_out,
        out_shape=[jax.ShapeDtypeStruct((n, d), F32)] + side_shapes,
        scratch_shapes=[pltpu.VMEM((tm, d), BF16), pltpu.VMEM((tm, d), F32)] + side_scratch,
        compiler_params=_params("arbitrary", "arbitrary", vmem_limit=FFN_VMEM_LIMIT),
        name="dense_ffn",
    )(x, gain.reshape(1, d), w_gate, w_up, w_down, *side_srcs)


def _band_kernel(q_ref, kp_ref, k_ref, kn_ref, vp_ref, v_ref, vn_ref, o_ref, lse_ref, kwin_ref, vwin_ref, *,
                 first_blocks, per_seq_a, per_seq_b, tq, half):
    tqb = q_ref.shape[0]
    kw = tq + 2 * half
    pos, per_seq = _two_batch_position(pl.program_id(1), first_blocks, per_seq_a, per_seq_b)
    q0 = pos * tqb
    length = per_seq * tqb
    for win, parts in ((kwin_ref, (kp_ref, k_ref, kn_ref)), (vwin_ref, (vp_ref, v_ref, vn_ref))):
        win[:half] = parts[0][...]
        win[half:half + tqb] = parts[1][...]
        win[half + tqb:] = parts[2][...]
    lane = lax.broadcasted_iota(jnp.int32, (tq, LANES), 1)

    def body(t, carry):
        r0 = pl.multiple_of(t * tq, tq)
        qpos = q0 + r0 + lax.broadcasted_iota(jnp.int32, (tq, kw), 0)
        kpos = q0 + r0 - half + lax.broadcasted_iota(jnp.int32, (tq, kw), 1)
        ok = (jnp.abs(qpos - kpos) <= half) & (kpos >= 0) & (kpos < length)
        lse_tile = jnp.zeros((tq, LANES), F32)
        for h in range(q_ref.shape[1] // HEAD_DIM):
            sl = slice(h * HEAD_DIM, (h + 1) * HEAD_DIM)
            q = q_ref[pl.ds(r0, tq), sl]
            k = kwin_ref[pl.ds(r0, kw), sl]
            v = vwin_ref[pl.ds(r0, kw), sl]
            s = lax.dot_general(q, k, (((1,), (1,)), ((), ())), preferred_element_type=F32) * ATTN_SCALE
            s = jnp.where(ok, s, NEG_INF)
            m = jnp.max(s, axis=-1, keepdims=True)
            p = jnp.exp(s - m)
            den = jnp.sum(p, axis=-1, keepdims=True)
            o = jnp.dot(p.astype(BF16), v, preferred_element_type=F32) / den
            o_ref[pl.ds(r0, tq), sl] = o
            lse_tile = jnp.where(lane == h, m + jnp.log(den), lse_tile)
        lse_ref[pl.ds(r0, tq), :] = lse_tile
        return carry

    lax.fori_loop(0, tqb // tq, body, 0)


def _band_attention(qkv, trunks, group):
    window, dil = DIL_CONFIGS[group]
    half = window // (2 * dil)
    _, _, rows, width = qkv.shape
    (off_a, b_a, s_a), (off_b, b_b, s_b) = trunks
    len_a, len_b = s_a // dil, s_b // dil
    assert s_a % dil == 0 and s_b % dil == 0 and qkv.shape[1] == dil and off_a == 0 and off_b == b_a * s_a
    tqb = min(BAND_Q_BLOCK, len_a, len_b)
    tq = min(BAND_Q_TILE, tqb)
    assert len_a % tqb == 0 and len_b % tqb == 0 and tqb % tq == 0 and tqb % half == 0
    hpb = tqb // half
    first_blocks, per_seq_a, per_seq_b = off_b // dil // tqb, len_a // tqb, len_b // tqb

    def halo(which, side):
        def index(r, t):
            pos, per_seq = _two_batch_position(t, first_blocks, per_seq_a, per_seq_b)
            edge = (pos == per_seq - 1) if side else (pos == 0)
            blk = t * hpb + (jnp.where(edge, hpb - 1, hpb) if side else jnp.where(edge, 0, -1))
            return (which, r, blk, 0)
        return index

    mspec = lambda which: pl.BlockSpec((None, None, tqb, width), lambda r, t: (which, r, t, 0))
    hspec = lambda which, side: pl.BlockSpec((None, None, half, width), halo(which, side))
    return pl.pallas_call(
        functools.partial(_band_kernel, first_blocks=first_blocks, per_seq_a=per_seq_a, per_seq_b=per_seq_b,
                          tq=tq, half=half),
        grid=(dil, rows // tqb),
        in_specs=[
            mspec(0),
            hspec(1, 0), mspec(1), hspec(1, 1),
            hspec(2, 0), mspec(2), hspec(2, 1),
        ],
        out_specs=[
            pl.BlockSpec((None, tqb, width), lambda r, t: (r, t, 0)),
            pl.BlockSpec((None, tqb, LANES), lambda r, t: (r, t, 0)),
        ],
        out_shape=[jax.ShapeDtypeStruct((dil, rows, width), F32), jax.ShapeDtypeStruct((dil, rows, LANES), F32)],
        scratch_shapes=[pltpu.VMEM((tqb + 2 * half, width), BF16), pltpu.VMEM((tqb + 2 * half, width), BF16)],
        compiler_params=_params("arbitrary", "arbitrary"),
        name="band_attention",
    )(qkv, qkv, qkv, qkv, qkv, qkv, qkv)


def _merge_proj_kernel(*refs, dils):
    ng = len(dils)
    o_refs, l_refs = refs[:ng], refs[ng:2 * ng]
    w_ref, x_ref, out_ref, a_ref, os_ref, ls_ref = refs[2 * ng:]
    tm = x_ref.shape[0]

    heads = a_ref.shape[1] // HEAD_DIM
    for g, dil in enumerate(dils):
        if dil == 1:
            continue
        for r in range(dil):
            ls_ref[g, pl.ds(r, tm // dil, stride=dil), :] = l_refs[g][r]
            for h in range(heads):
                os_ref[g, h, pl.ds(r, tm // dil, stride=dil), :] = o_refs[g][r, :, h * HEAD_DIM:(h + 1) * HEAD_DIM]

    def group_out(g, h):
        if dils[g] == 1:
            return o_refs[g][0, :, h * HEAD_DIM:(h + 1) * HEAD_DIM]
        return os_ref[g, h]

    lse = [l_refs[g][0] if dils[g] == 1 else ls_ref[g] for g in range(ng)]
    m = functools.reduce(jnp.maximum, lse)
    e = [jnp.exp(l - m) for l in lse]
    tot = functools.reduce(lambda p, q: p + q, e)
    wts = [eg / tot for eg in e]
    for h in range(heads):
        terms = [wts[g][:, h:h + 1] * group_out(g, h) for g in range(ng)]
        a_ref[:, h * HEAD_DIM:(h + 1) * HEAD_DIM] = functools.reduce(lambda p, q: p + q, terms).astype(BF16)
    out_ref[...] = x_ref[...] + jnp.dot(a_ref[...], w_ref[...], preferred_element_type=F32)


def _merge_proj(outs, lses, w, x):
    n, d = x.shape
    width = w.shape[0]
    dils = tuple(o.shape[0] for o in outs)
    tm = min(SMALL_TOKEN_BLOCK, n)
    row = lambda i: (i, 0)
    grp = lambda i: (0, i, 0)
    return pl.pallas_call(
        functools.partial(_merge_proj_kernel, dils=dils),
        grid=(n // tm,),
        in_specs=[pl.BlockSpec((dil, tm // dil, width), grp) for dil in dils]
        + [pl.BlockSpec((dil, tm // dil, LANES), grp) for dil in dils]
        + [pl.BlockSpec((width, d), lambda i: (0, 0)), pl.BlockSpec((tm, d), row)],
        out_specs=pl.BlockSpec((tm, d), row),
        out_shape=jax.ShapeDtypeStruct((n, d), F32),
        scratch_shapes=[
            pltpu.VMEM((tm, width), BF16),
            pltpu.VMEM((len(dils), width // HEAD_DIM, tm, HEAD_DIM), F32),
            pltpu.VMEM((len(dils), tm, LANES), F32),
        ],
        compiler_params=_params("arbitrary"),
        name="merge_proj",
    )(*outs, *lses, w, x)


def _store_chunk_rows(ref, x):
    rows, d = x.shape
    nc = d // LANES
    for c in range(nc):
        ref[pl.ds(c, rows, stride=nc), :] = x[:, c * LANES:(c + 1) * LANES]


def _store_packed_rows(ref, x):
    rows, d = x.shape
    nw = d // (2 * LANES)
    for c in range(nw):
        lo = x[:, (2 * c) * LANES:(2 * c + 1) * LANES].astype(BF16).astype(F32)
        hi = x[:, (2 * c + 1) * LANES:(2 * c + 2) * LANES].astype(BF16).astype(F32)
        word = (pltpu.bitcast(lo, jnp.uint32) >> 16) | (pltpu.bitcast(hi, jnp.uint32) & jnp.uint32(0xFFFF0000))
        ref[pl.ds(c, rows, stride=nw), :] = word


def _load_packed_rows(ref, lead, rows, nw, c):
    word = ref[lead, pl.ds(c, rows, stride=nw), :]
    lo = pltpu.bitcast(word << 16, F32)
    hi = pltpu.bitcast(word & jnp.uint32(0xFFFF0000), F32)
    return lo.astype(BF16), hi.astype(BF16)


def _router_kernel(x_ref, gain_ref, wr_ref, h_ref, idx_ref, gate_ref):
    h = _rms_rows(x_ref[...], gain_ref[...])
    _store_packed_rows(h_ref, h)
    w = wr_ref[...]
    h_hi, w_hi = h.astype(BF16), w.astype(BF16)
    h_lo, w_lo = (h - h_hi.astype(F32)).astype(BF16), (w - w_hi.astype(F32)).astype(BF16)
    logits = (jnp.dot(h_hi, w_hi, preferred_element_type=F32)
              + (jnp.dot(h_hi, w_lo, preferred_element_type=F32) + jnp.dot(h_lo, w_hi, preferred_element_type=F32)))
    lane = lax.broadcasted_iota(jnp.int32, logits.shape, 1)
    lane_f = lane.astype(F32)
    logits = jnp.where(lane < N_EXPERTS, logits, -jnp.inf)
    m1 = jnp.max(logits, axis=-1, keepdims=True)
    i1 = jnp.min(jnp.where(logits == m1, lane_f, float(LANES)), axis=-1, keepdims=True)
    rest = jnp.where(lane_f == i1, -jnp.inf, logits)
    m2 = jnp.max(rest, axis=-1, keepdims=True)
    i2 = jnp.min(jnp.where(rest == m2, lane_f, float(LANES)), axis=-1, keepdims=True)
    e = jnp.exp(m2 - m1)
    tot = 1.0 + e
    idx_ref[...] = jnp.where(lane == 0, i1, jnp.where(lane == 1, i2, 0.0)).astype(jnp.int32)
    gate_ref[...] = jnp.where(lane == 0, 1.0 / tot, jnp.where(lane == 1, e / tot, 0.0))


def _router(x, gain, w_router):
    n, d = x.shape
    tm = min(SMALL_TOKEN_BLOCK, n)
    wr = jnp.zeros((d, LANES), F32).at[:, :N_EXPERTS].set(w_router.astype(F32))
    row = lambda i: (i, 0)
    return pl.pallas_call(
        _router_kernel,
        grid=(n // tm,),
        in_specs=[
            pl.BlockSpec((tm, d), row),
            pl.BlockSpec((1, d), lambda i: (0, 0)),
            pl.BlockSpec((d, LANES), lambda i: (0, 0)),
        ],
        out_specs=[pl.BlockSpec((tm * (d // (2 * LANES)), LANES), row), pl.BlockSpec((tm, LANES), row),
                   pl.BlockSpec((tm, LANES), row)],
        out_shape=[
            jax.ShapeDtypeStruct((n * (d // (2 * LANES)), LANES), jnp.uint32),
            jax.ShapeDtypeStruct((n, LANES), jnp.int32),
            jax.ShapeDtypeStruct((n, LANES), F32),
        ],
        compiler_params=_params("arbitrary"),
        name="router",
    )(x, gain.reshape(1, d), wr)


def _expert_kernel(bexp_ref, brows_ref, nused_ref, rtok_ref, h_hbm, wg_ref, wu_ref, wd_ref, y_ref,
                   xs_ref, xb_ref, acc_ref, sem):
    del bexp_ref
    i = pl.program_id(0)
    j = pl.program_id(1)
    tmb, d = xb_ref.shape
    nw = d // (2 * LANES)
    n_used = nused_ref[0]

    def row_copy(tok, slot, r):
        return pltpu.make_async_copy(h_hbm.at[pl.ds(pl.multiple_of(tok * nw, nw), nw), :],
                                     xs_ref.at[slot, pl.ds(pl.multiple_of(r * nw, nw), nw), :], sem.at[slot])

    def start_gather(blk, slot):
        def issue(r, carry):
            row_copy(rtok_ref[blk * tmb + r], slot, r).start()
            return carry
        lax.fori_loop(0, tmb, issue, 0, unroll=16)

    def wait_gather(slot):
        pltpu.make_async_copy(h_hbm.at[pl.ds(0, tmb * nw), :], xs_ref.at[slot], sem.at[slot]).wait()

    @pl.when(i < n_used)
    def _():
        slot = i % 2

        @pl.when(j == 0)
        def _():
            @pl.when(i == 0)
            def _():
                start_gather(0, 0)

            wait_gather(slot)

            @pl.when(i + 1 < n_used)
            def _():
                start_gather(i + 1, 1 - slot)

            for c in range(nw):
                lo, hi = _load_packed_rows(xs_ref, slot, tmb, nw, c)
                xb_ref[:, (2 * c) * LANES:(2 * c + 1) * LANES] = lo
                xb_ref[:, (2 * c + 1) * LANES:(2 * c + 2) * LANES] = hi
            acc_ref[...] = jnp.zeros_like(acc_ref)

        def swiglu_rows(rows):
            x = xb_ref[:rows]
            g = jnp.dot(x, wg_ref[0], preferred_element_type=F32)
            u = jnp.dot(x, wu_ref[0], preferred_element_type=F32)
            acc_ref[:rows] += jnp.dot(_silu_mul(g, u).astype(BF16), wd_ref[0], preferred_element_type=F32)

        half_full = brows_ref[i] <= tmb // 2

        @pl.when(half_full)
        def _():
            swiglu_rows(tmb // 2)

        @pl.when(jnp.logical_not(half_full))
        def _():
            swiglu_rows(tmb)

    last = j == pl.num_programs(1) - 1

    @pl.when(last & (i < n_used))
    def _():
        _store_chunk_rows(y_ref, acc_ref[...])

    @pl.when(last & (i >= n_used))
    def _():
        y_ref[...] = jnp.zeros_like(y_ref)


def _expert_ffn(h_rows, block_expert, block_rows, n_used, row_tok, w_gate, w_up, w_down):
    d = w_gate.shape[1]
    f = w_gate.shape[2]
    tmb = MOE_BLOCK
    tf = MOE_F_BLOCK
    n_blocks = block_expert.shape[0]
    nf = f // tf
    assert f % tf == 0

    def fblk(i, j, nu):
        return jnp.where(i < nu[0], j, nf - 1)

    grid_spec = pltpu.PrefetchScalarGridSpec(
        num_scalar_prefetch=4,
        grid=(n_blocks, nf),
        in_specs=[
            pl.BlockSpec(memory_space=pl.ANY),
            pl.BlockSpec((1, d, tf), lambda i, j, be, br, nu, rt: (be[i], 0, fblk(i, j, nu))),
            pl.BlockSpec((1, d, tf), lambda i, j, be, br, nu, rt: (be[i], 0, fblk(i, j, nu))),
            pl.BlockSpec((1, tf, d), lambda i, j, be, br, nu, rt: (be[i], fblk(i, j, nu), 0)),
        ],
        out_specs=pl.BlockSpec((tmb * (d // LANES), LANES), lambda i, j, be, br, nu, rt: (i, 0)),
        scratch_shapes=[
            pltpu.VMEM((2, tmb * (d // (2 * LANES)), LANES), jnp.uint32),
            pltpu.VMEM((tmb, d), BF16),
            pltpu.VMEM((tmb, d), F32),
            pltpu.SemaphoreType.DMA((2,)),
        ],
    )
    return pl.pallas_call(
        _expert_kernel,
        grid_spec=grid_spec,
        out_shape=jax.ShapeDtypeStruct((n_blocks * tmb * (d // LANES), LANES), F32),
        compiler_params=_params("arbitrary", "arbitrary"),
        name="expert_ffn",
    )(block_expert, block_rows, n_used, row_tok, h_rows, w_gate, w_up, w_down)


def _combine_kernel(dest_ref, ys_hbm, x_ref, gate_ref, oa_ref, ob_ref, buf_ref, sem, *, first_blocks):
    i = pl.program_id(0)
    nsteps = pl.num_programs(0)
    tc, d = x_ref.shape
    nc = d // LANES
    pitch = buf_ref.shape[1] // (2 * tc)

    def row_copy(row, slot, dst):
        return pltpu.make_async_copy(ys_hbm.at[pl.ds(pl.multiple_of(row * nc, nc), nc), :],
                                     buf_ref.at[slot, pl.ds(pl.multiple_of(dst * pitch, 8), nc), :], sem.at[slot])

    def start_gather(step, slot):
        def issue(r, carry):
            row_copy(dest_ref[step * (2 * tc) + r], slot, r).start()
            return carry
        lax.fori_loop(0, 2 * tc, issue, 0, unroll=8)

    def wait_gather(slot):
        pltpu.make_async_copy(ys_hbm.at[pl.ds(0, 2 * tc * nc), :], buf_ref.at[slot, pl.ds(0, 2 * tc * nc), :],
                              sem.at[slot]).wait()

    slot = i % 2

    @pl.when(i == 0)
    def _():
        start_gather(0, 0)

    wait_gather(slot)

    @pl.when(i + 1 < nsteps)
    def _():
        start_gather(i + 1, 1 - slot)

    g0 = gate_ref[:, 0:1]
    g1 = gate_ref[:, 1:2]

    def emit(o_ref):
        for c in range(nc):
            sl = slice(c * LANES, (c + 1) * LANES)
            y0 = buf_ref[slot, pl.ds(c, tc, stride=pitch), :]
            y1 = buf_ref[slot, pl.ds(tc * pitch + c, tc, stride=pitch), :]
            o_ref[:, sl] = x_ref[:, sl] + (y0 * g0 + y1 * g1)
    _pick_rows(i, first_blocks, oa_ref, ob_ref, emit)


def _combine(x, ys, dest, gate, n_first):
    n, d = x.shape
    tc = min(COMBINE_BLOCK, n_first, n - n_first)
    assert n_first % tc == 0 and (n - n_first) % tc == 0
    first_blocks = n_first // tc
    nc = d // LANES
    pitch = nc if (nc // 8) % 2 else nc + 8
    dest = dest.reshape(n // tc, tc, TOP_K).transpose(0, 2, 1).reshape(-1)
    grid_spec = pltpu.PrefetchScalarGridSpec(
        num_scalar_prefetch=1,
        grid=(n // tc,),
        in_specs=[
            pl.BlockSpec(memory_space=pl.ANY),
            pl.BlockSpec((tc, d), lambda i, dr: (i, 0)),
            pl.BlockSpec((tc, LANES), lambda i, dr: (i, 0)),
        ],
        out_specs=_two_batch_specs(tc, d, first_blocks),
        scratch_shapes=[pltpu.VMEM((2, 2 * tc * pitch, LANES), F32), pltpu.SemaphoreType.DMA((2,))],
    )
    return pl.pallas_call(
        functools.partial(_combine_kernel, first_blocks=first_blocks),
        grid_spec=grid_spec,
        out_shape=[jax.ShapeDtypeStruct((n_first, d), F32), jax.ShapeDtypeStruct((n - n_first, d), F32)],
        compiler_params=_params("arbitrary"),
        name="moe_combine",
    )(dest, ys, x, gate)


def _routing_tables(idx, tmb):
    n = idx.shape[0]
    nk = n * TOP_K
    e_flat = idx[:, :TOP_K].reshape(-1)
    onehot = (e_flat[:, None] == jnp.arange(N_EXPERTS, dtype=jnp.int32)[None, :]).astype(jnp.int32)
    csum = jnp.cumsum(onehot, axis=0)
    rank = jnp.sum(onehot * csum, axis=1) - 1
    counts = csum[-1]
    padded = (counts + tmb - 1) // tmb * tmb
    pad_end = jnp.cumsum(padded)
    pad_start = pad_end - padded
    dest = (jnp.sum(onehot * pad_start[None, :], axis=1) + rank).astype(jnp.int32)
    n_blocks = -(-nk // tmb) + N_EXPERTS
    block_expert = jnp.minimum(
        jnp.searchsorted(pad_end, jnp.arange(n_blocks, dtype=jnp.int32) * tmb, side="right"), N_EXPERTS - 1
    ).astype(jnp.int32)
    n_used = (pad_end[-1] // tmb).astype(jnp.int32).reshape(1)
    block_lo = jnp.arange(n_blocks, dtype=jnp.int32) * tmb - pad_start[block_expert]
    block_rows = jnp.clip(counts[block_expert] - block_lo, 0, tmb).astype(jnp.int32)
    row_tok = jnp.zeros((n_blocks * tmb,), jnp.int32).at[dest].set(jnp.arange(nk, dtype=jnp.int32) // TOP_K)
    return dest, block_expert, block_rows, n_used, row_tok


def _rope_tables(seq_len):
    half = HEAD_DIM // 2
    inv_freq = ROPE_THETA ** (-jnp.arange(half, dtype=F32) / half)
    ang = jnp.arange(seq_len, dtype=F32)[:, None] * inv_freq[None, :]
    cos, sin = jnp.cos(ang), jnp.sin(ang)
    return jnp.concatenate([cos, cos], axis=-1), jnp.concatenate([-sin, sin], axis=-1)


def kernel(x_prompt, x_sample, norm_mix, norm_ffn, na_w_qkv, na_q_gain, na_k_gain, na_rpb, na_w_o,
           da_w_qkv, da_q_gain, da_k_gain, da_w_o, ffn_w_gate, ffn_w_up, ffn_w_down,
           moe_w_router, moe_w_gate, moe_w_up, moe_w_down):
    d = x_prompt.shape[-1]
    trunks = []
    off = 0
    for xin in (x_prompt, x_sample):
        b, s, _ = xin.shape
        trunks.append((off, b, s))
        off += b * s
    n = off
    xa, xb = x_prompt.reshape(-1, d), x_sample.reshape(-1, d)
    depth = norm_mix.shape[0]
    assert depth == 2, "layer 0 reads the two request batches directly and layer 1 writes them separately"
    x = None

    for layer in range(depth):
        lj = layer // 2
        if layer % 2 == 0:
            heads = na_w_qkv.shape[2] // (3 * HEAD_DIM)
            per = heads * HEAD_DIM // PROJ_COL_BLOCK
            hg = jnp.concatenate([
                jnp.broadcast_to(na_q_gain[lj], (per, HEAD_DIM)),
                jnp.broadcast_to(na_k_gain[lj], (per, HEAD_DIM)),
                jnp.ones((per, HEAD_DIM), F32),
            ]).reshape(3 * per, 1, HEAD_DIM)
            qkv = _qkv_project(xa, xb, norm_mix[layer], na_w_qkv[lj].astype(BF16), hg, 3 * per, 2 * per)
            tables = _na_bias_tables(na_rpb[lj])
            att = _na_attention(qkv, tables, trunks, heads)
            x = _proj_residual(att, na_w_o[lj].astype(BF16), xa, xb)
            ne, dm, fe = moe_w_gate.shape[1:]
            wide = (moe_w_gate[lj].reshape(ne * dm, fe), moe_w_up[lj].reshape(ne * dm, fe))
            steps = (n // min(FFN_TOKEN_BLOCK, n)) * (ffn_w_gate.shape[2] // FFN_F_BLOCK)
            shadow = _side_cast_fits(wide, SIDE_CAST_ROWS_WIDE, steps)
            x, *cast = _dense_ffn(x, norm_ffn[layer], ffn_w_gate[lj].astype(BF16), ffn_w_up[lj].astype(BF16),
                                  ffn_w_down[lj].astype(BF16), side_srcs=wide if shadow else (),
                                  side_rows=SIDE_CAST_ROWS_WIDE)
            moe_gate_bf, moe_up_bf = cast if shadow else [w.astype(BF16) for w in wide]
        else:
            groups = len(DIL_CONFIGS)
            heads = da_w_qkv.shape[2] // (groups * 3 * HEAD_DIM)
            dils = [dil for _, dil in DIL_CONFIGS]
            ones = jnp.ones((HEAD_DIM,), F32)
            hg = jnp.stack([t for g in range(groups) for t in (da_q_gain[lj, g], da_k_gain[lj, g], ones)])
            hg = hg.reshape(groups * 3, 1, HEAD_DIM)
            (o0, b0, s0), (o1, b1, s1) = trunks
            tq = min(QKV_DIL_TOKEN_BLOCK, n)
            assert s0 % tq == 0 and s1 % tq == 0 and o1 % tq == 0
            nb0, p0, p1 = o1 // tq, s0 // tq, s1 // tq
            pos_map = lambda i: jnp.where(i < nb0, i % p0, (i - nb0) % p1)
            smax = max(s0, s1)
            rope = [jnp.stack([t.reshape(smax // tq, tq // dil, dil, HEAD_DIM).transpose(0, 2, 1, 3)
                               .reshape(smax, HEAD_DIM) for dil in dils]) for t in _rope_tables(smax)]
            narrow = (moe_w_down[lj].reshape(ne * fe, dm),)
            shadow = _side_cast_fits(narrow, SIDE_CAST_ROWS_NARROW, (n // tq) * groups * 3)
            qkvs = list(_qkv_project_dilated(
                x, norm_mix[layer], da_w_qkv[lj].astype(BF16), hg, rope, pos_map, dils, heads, tq,
                side_srcs=narrow if shadow else (), side_rows=SIDE_CAST_ROWS_NARROW))
            moe_down_bf = qkvs.pop() if shadow else narrow[0].astype(BF16)
            outs, lses = [], []
            for g in range(groups):
                o_g, lse_g = _band_attention(qkvs[g], trunks, g)
                outs.append(o_g)
                lses.append(lse_g)
            x = _merge_proj(outs, lses, da_w_o[lj].astype(BF16), x)
            h, idx, gate = _router(x, norm_ffn[layer], moe_w_router[lj])
            dest, block_expert, block_rows, n_used, row_tok = _routing_tables(idx, MOE_BLOCK)
            ys = _expert_ffn(h, block_expert, block_rows, n_used, row_tok, moe_gate_bf.reshape(ne, dm, fe),
                             moe_up_bf.reshape(ne, dm, fe), moe_down_bf.reshape(ne, fe, dm))
            ya, yb = _combine(x, ys, dest, gate, xa.shape[0])

    return ya.reshape(x_prompt.shape), yb.reshape(x_sample.shape)
```

```python
import functools

import jax
import jax.numpy as jnp
from jax import lax
from jax.experimental import pallas as pl
from jax.experimental.pallas import tpu as pltpu

F32 = jnp.float32
BF16 = jnp.bfloat16

HEAD_DIM = 128
GRID_W = 64
NA_ROWS = 8
NA_COLS = 16
NA_GROUP = NA_ROWS // 2
NA_KEY_ROWS = 3 * NA_GROUP
NA_HEAD_BLOCK = 16
DIL_CONFIGS = ((128, 1), (512, 4), (2048, 16))
N_EXPERTS = 8
TOP_K = 2
ROPE_THETA = 10000.0
EPS = 1e-6
NEG_INF = -1e30
ATTN_SCALE = HEAD_DIM ** -0.5

LANES = 128
MXU_COLS = 256
VMEM_LIMIT = 52 * 1024 * 1024
FFN_VMEM_LIMIT = 57 * 1024 * 1024

TOKEN_BLOCK = 1024
PROJ_COL_BLOCK = 1024
SMALL_TOKEN_BLOCK = 512
QKV_DIL_TOKEN_BLOCK = 512
FFN_TOKEN_BLOCK = 1024
FFN_F_BLOCK = 256
BAND_Q_BLOCK = 512
BAND_Q_TILE = 128
MOE_BLOCK = 512
MOE_F_BLOCK = 1024
COMBINE_BLOCK = 512
SIDE_CAST_ROWS_WIDE = 128
SIDE_CAST_ROWS_NARROW = 512


def _params(*sem, vmem_limit=VMEM_LIMIT):
    return pltpu.CompilerParams(dimension_semantics=sem, vmem_limit_bytes=vmem_limit)


def _rms_rows(x, gain):
    ms = jnp.mean(x * x, axis=-1, keepdims=True)
    return x * lax.rsqrt(ms + EPS) * gain


def _pick_rows(i, first_blocks, xa_ref, xb_ref, use):
    @pl.when(i < first_blocks)
    def _():
        use(xa_ref)

    @pl.when(i >= first_blocks)
    def _():
        use(xb_ref)


def _two_batch_specs(tm, d, first_blocks, **kw):
    return [pl.BlockSpec((tm, d), lambda i, *_: (jnp.minimum(i, first_blocks - 1), 0), **kw),
            pl.BlockSpec((tm, d), lambda i, *_: (jnp.maximum(i - first_blocks, 0), 0), **kw)]


def _qkv_kernel(xa_ref, xb_ref, gain_ref, w_ref, hg_ref, o_ref, h_ref, *, first_blocks, v_every, v_from):
    j = pl.program_id(1)

    @pl.when(j == 0)
    def _():
        def norm(x_ref):
            h_ref[...] = _rms_rows(x_ref[...], gain_ref[...]).astype(BF16)
        _pick_rows(pl.program_id(0), first_blocks, xa_ref, xb_ref, norm)

    is_qk = (j % v_every) < v_from
    for c in range(w_ref.shape[1] // MXU_COLS):
        acc = jnp.dot(h_ref[...], w_ref[:, c * MXU_COLS:(c + 1) * MXU_COLS], preferred_element_type=F32)
        for h in range(MXU_COLS // HEAD_DIM):
            seg = acc[:, h * HEAD_DIM:(h + 1) * HEAD_DIM]
            lo = c * MXU_COLS + h * HEAD_DIM
            o_ref[:, lo:lo + HEAD_DIM] = jnp.where(is_qk, _rms_rows(seg, hg_ref[0]), seg).astype(o_ref.dtype)


def _qkv_project(xa, xb, gain, w, head_gain, v_every, v_from):
    d = xa.shape[1]
    n = xa.shape[0] + xb.shape[0]
    n_out = w.shape[1]
    tm = min(TOKEN_BLOCK, xa.shape[0], xb.shape[0])
    assert xa.shape[0] % tm == 0 and xb.shape[0] % tm == 0
    first_blocks = xa.shape[0] // tm
    tn = n_out // head_gain.shape[0]
    return pl.pallas_call(
        functools.partial(_qkv_kernel, first_blocks=first_blocks, v_every=v_every, v_from=v_from),
        grid=(n // tm, n_out // tn),
        in_specs=_two_batch_specs(tm, d, first_blocks) + [
            pl.BlockSpec((1, d), lambda i, j: (0, 0)),
            pl.BlockSpec((d, tn), lambda i, j: (0, j)),
            pl.BlockSpec((1, 1, HEAD_DIM), lambda i, j: (j, 0, 0)),
        ],
        out_specs=pl.BlockSpec((tm, tn), lambda i, j: (i, j)),
        out_shape=jax.ShapeDtypeStruct((n, n_out), BF16),
        scratch_shapes=[pltpu.VMEM((tm, d), BF16)],
        compiler_params=_params("arbitrary", "arbitrary", vmem_limit=FFN_VMEM_LIMIT),
        name="qkv_project",
    )(xa, xb, gain.reshape(1, d), w, head_gain)


def _qkv_dil_kernel(x_ref, gain_ref, w_ref, hg_ref, cos_ref, sin_ref, *rest, dils, n_side):
    ng = len(dils)
    srcs, out_refs, dsts = rest[:n_side], rest[n_side:n_side + ng], rest[n_side + ng:2 * n_side + ng]
    h_ref, stage_ref, *side = rest[2 * n_side + ng:]
    j = pl.program_id(1)
    if n_side:
        _side_cast_step(pl.program_id(0) * pl.num_programs(1) + j, srcs, dsts, *side)
    tm, d = x_ref.shape
    tn = w_ref.shape[1]

    @pl.when(j == 0)
    def _():
        y = _rms_rows(x_ref[...], gain_ref[...])
        for c in range(d // LANES):
            stage_ref[c] = y[:, c * LANES:(c + 1) * LANES]
        for g, dil in enumerate(dils):
            rows = tm // dil
            for r in range(dil):
                for c in range(d // LANES):
                    h_ref[g, r * rows:(r + 1) * rows, c * LANES:(c + 1) * LANES] = (
                        stage_ref[c, pl.ds(r, rows, stride=dil), :].astype(BF16))

    group = j // 3
    which = j % 3
    acc = jnp.dot(h_ref[group], w_ref[...], preferred_element_type=F32)

    for g, (dil, o_ref) in enumerate(zip(dils, out_refs)):
        rows = tm // dil

        @pl.when((group == g) & (which < 2))
        def _(dil=dil, o_ref=o_ref, rows=rows):
            for h in range(tn // HEAD_DIM):
                sl = slice(h * HEAD_DIM, (h + 1) * HEAD_DIM)
                y = _rms_rows(acc[:, sl], hg_ref[0])
                y = (y * cos_ref[...] + pltpu.roll(y, HEAD_DIM // 2, 1) * sin_ref[...]).astype(o_ref.dtype)
                for r in range(dil):
                    o_ref[r, :, sl] = y[r * rows:(r + 1) * rows]

        @pl.when((group == g) & (which == 2))
        def _(dil=dil, o_ref=o_ref, rows=rows):
            v = acc.astype(o_ref.dtype)
            for r in range(dil):
                o_ref[r] = v[r * rows:(r + 1) * rows]


def _qkv_project_dilated(x, gain, w, head_gain, rope_tabs, pos_map, dils, heads, tm, side_srcs=(), side_rows=0):
    n, d = x.shape
    n_out = w.shape[1]
    width = heads * HEAD_DIM
    ng = len(dils)
    assert n_out == ng * 3 * width and head_gain.shape[0] == ng * 3
    side_in, side_out, side_shapes, side_scratch = (
        _side_cast_specs(side_srcs, side_rows, (n // tm) * ng * 3) if side_srcs else ([], [], [], []))

    def out_map(g):
        def index(i, j):
            return (jnp.clip(j - 3 * g, 0, 2), 0, i, 0)
        return index

    rope_spec = pl.BlockSpec((None, tm, HEAD_DIM), lambda i, j: (j // 3, pos_map(i), 0))
    return pl.pallas_call(
        functools.partial(_qkv_dil_kernel, dils=tuple(dils), n_side=len(side_srcs)),
        grid=(n // tm, ng * 3),
        in_specs=[
            pl.BlockSpec((tm, d), lambda i, j: (i, 0)),
            pl.BlockSpec((1, d), lambda i, j: (0, 0)),
            pl.BlockSpec((d, width), lambda i, j: (0, j)),
            pl.BlockSpec((1, 1, HEAD_DIM), lambda i, j: (j, 0, 0)),
            rope_spec,
            rope_spec,
        ] + side_in,
        out_specs=[pl.BlockSpec((None, dil, tm // dil, width), out_map(g)) for g, dil in enumerate(dils)] + side_out,
        out_shape=[jax.ShapeDtypeStruct((3, dil, n // dil, width), BF16) for dil in dils] + side_shapes,
        scratch_shapes=[pltpu.VMEM((ng, tm, d), BF16), pltpu.VMEM((d // LANES, tm, LANES), F32)] + side_scratch,
        compiler_params=_params("arbitrary", "arbitrary"),
        name="qkv_project_dilated",
    )(x, gain.reshape(1, d), w, head_gain, *rope_tabs, *side_srcs)


def _na_table_kernel(rpb_ref, o_ref):
    h = pl.program_id(0)
    w = GRID_W
    lane = lax.broadcasted_iota(jnp.int32, (w, 2 * w), 1)
    cq = lax.broadcasted_iota(jnp.int32, (w, 2 * w), 0)
    ck = lane % w
    upper = lane >= w
    cs = jnp.clip(cq - NA_COLS // 2, 0, w - NA_COLS)
    col_ok = (ck >= cs) & (ck < cs + NA_COLS)
    dc = ck - cq + (NA_COLS - 1)

    def pair_tile(dr):
        t = jnp.zeros((w, 2 * w), F32)
        for b in range(2 * NA_COLS - 1):
            lo = rpb_ref[h, dr + NA_ROWS - 1, b]
            hi = rpb_ref[h, dr + NA_ROWS, b]
            t = jnp.where(dc == b, jnp.where(upper, hi, lo), t)
        return t

    tiles = {}
    for qi in range(NA_GROUP):
        for kp in range(NA_KEY_ROWS // 2):
            dr = 2 * kp - NA_GROUP - qi
            if dr not in tiles:
                tiles[dr] = pair_tile(dr)
            for v in range(3):
                def row_ok(kj):
                    if v == 0:
                        return kj >= NA_GROUP
                    if v == 1:
                        return 0 <= kj - qi < NA_ROWS
                    return kj < NA_ROWS
                ok_lo, ok_hi = row_ok(2 * kp), row_ok(2 * kp + 1)
                if ok_lo and ok_hi:
                    ok = col_ok
                elif ok_lo:
                    ok = col_ok & jnp.logical_not(upper)
                elif ok_hi:
                    ok = col_ok & upper
                else:
                    ok = None
                tile = jnp.full((w, 2 * w), NEG_INF, F32) if ok is None else jnp.where(ok, tiles[dr], NEG_INF)
                o_ref[v, 0, qi * w:(qi + 1) * w, kp * 2 * w:(kp + 1) * 2 * w] = tile


def _na_bias_tables(rpb):
    assert NA_GROUP == NA_ROWS // 2 and NA_KEY_ROWS % 2 == 0 and 2 * GRID_W == LANES
    heads = rpb.shape[0]
    shape = (3, heads, NA_GROUP * GRID_W, NA_KEY_ROWS * GRID_W)
    return pl.pallas_call(
        _na_table_kernel,
        grid=(heads,),
        in_specs=[pl.BlockSpec(memory_space=pltpu.SMEM)],
        out_specs=pl.BlockSpec((3, 1) + shape[2:], lambda h: (0, h, 0, 0)),
        out_shape=jax.ShapeDtypeStruct(shape, F32),
        compiler_params=_params("arbitrary"),
        name="na_bias_table",
    )(rpb.astype(F32))


def _na_kernel(q_ref, kp_ref, k_ref, kn_ref, vp_ref, v_ref, vn_ref, tab_ref, o_ref):
    gq = q_ref.shape[0]
    nt = (((1,), (1,)), ((), ()))
    for h in range(q_ref.shape[1] // HEAD_DIM):
        sl = slice(h * HEAD_DIM, (h + 1) * HEAD_DIM)
        q = q_ref[:, sl]
        s = jnp.concatenate(
            [lax.dot_general(q, kr[:, sl], nt, preferred_element_type=F32) for kr in (kp_ref, k_ref, kn_ref)],
            axis=1)
        s = s * ATTN_SCALE + tab_ref[0, h]
        m = jnp.max(s, axis=-1, keepdims=True)
        p = jnp.exp(s - m)
        den = jnp.sum(p, axis=-1, keepdims=True)
        pb = p.astype(BF16)
        o = (jnp.dot(pb[:, :gq], vp_ref[:, sl], preferred_element_type=F32)
             + jnp.dot(pb[:, gq:2 * gq], v_ref[:, sl], preferred_element_type=F32)
             + jnp.dot(pb[:, 2 * gq:], vn_ref[:, sl], preferred_element_type=F32))
        o_ref[:, sl] = (o / den).astype(o_ref.dtype)


def _two_batch_position(t, first_blocks, per_seq_a, per_seq_b):
    in_a = t < first_blocks
    per_seq = jnp.where(in_a, per_seq_a, per_seq_b)
    return jnp.where(in_a, t, t - first_blocks) % per_seq, per_seq


def _na_attention(qkv, tables, trunks, heads):
    n = qkv.shape[0]
    gq = NA_GROUP * GRID_W
    hb = NA_HEAD_BLOCK
    hblocks = heads // hb
    (off_a, b_a, s_a), (off_b, b_b, s_b) = trunks
    for (_, _, seq) in trunks:
        assert seq % gq == 0 and seq // GRID_W >= NA_ROWS
    assert heads % hb == 0 and off_a == 0 and off_b == b_a * s_a
    first_groups, gps_a, gps_b = off_b // gq, s_a // gq, s_b // gq

    def place(g):
        pos, per_seq = _two_batch_position(g, first_groups, gps_a, gps_b)
        return pos == 0, pos == per_seq - 1

    def variant(g):
        first, last = place(g)
        return jnp.where(first, 0, jnp.where(last, 2, 1))

    def rows_of(which, shift):
        def index(h, g):
            first, last = place(g)
            gg = g + jnp.where((shift < 0) & first, 0, jnp.where((shift > 0) & last, 0, shift))
            return (gg, which * hblocks + h)
        return index

    blk = lambda index: pl.BlockSpec((gq, hb * HEAD_DIM), index)
    return pl.pallas_call(
        _na_kernel,
        grid=(hblocks, n // gq),
        in_specs=[
            blk(rows_of(0, 0)),
            blk(rows_of(1, -1)), blk(rows_of(1, 0)), blk(rows_of(1, 1)),
            blk(rows_of(2, -1)), blk(rows_of(2, 0)), blk(rows_of(2, 1)),
            pl.BlockSpec((1, hb, gq, NA_KEY_ROWS * GRID_W), lambda h, g: (variant(g), h, 0, 0)),
        ],
        out_specs=blk(rows_of(0, 0)),
        out_shape=jax.ShapeDtypeStruct((n, heads * HEAD_DIM), BF16),
        compiler_params=_params("arbitrary", "arbitrary"),
        name="na_attention",
    )(qkv, qkv, qkv, qkv, qkv, qkv, qkv, tables)


def _proj_residual_kernel(a_ref, w_ref, xa_ref, xb_ref, o_ref, *, first_blocks):
    y = jnp.dot(a_ref[...], w_ref[...], preferred_element_type=F32)

    def add(x_ref):
        o_ref[...] = x_ref[...] + y
    _pick_rows(pl.program_id(1), first_blocks, xa_ref, xb_ref, add)


def _proj_residual(a, w, xa, xb):
    n, k = a.shape
    d = w.shape[1]
    tm = min(TOKEN_BLOCK, xa.shape[0], xb.shape[0])
    tn = min(PROJ_COL_BLOCK, d)
    assert xa.shape[0] % tm == 0 and xb.shape[0] % tm == 0 and xa.shape[0] + xb.shape[0] == n
    first_blocks = xa.shape[0] // tm
    return pl.pallas_call(
        functools.partial(_proj_residual_kernel, first_blocks=first_blocks),
        grid=(d // tn, n // tm),
        in_specs=[
            pl.BlockSpec((tm, k), lambda j, i: (i, 0)),
            pl.BlockSpec((k, tn), lambda j, i: (0, j)),
            pl.BlockSpec((tm, tn), lambda j, i: (jnp.minimum(i, first_blocks - 1), j)),
            pl.BlockSpec((tm, tn), lambda j, i: (jnp.maximum(i - first_blocks, 0), j)),
        ],
        out_specs=pl.BlockSpec((tm, tn), lambda j, i: (i, j)),
        out_shape=jax.ShapeDtypeStruct((n, d), F32),
        compiler_params=_params("arbitrary", "arbitrary"),
        name="proj_residual",
    )(a, w, xa, xb)


def _side_cast_step(step, srcs, dsts, in_buf, out_buf, in_sem, out_sem):
    rows = in_buf.shape[1]
    per = srcs[0].shape[0] // rows
    nb = per * len(srcs)
    slot = step % 2

    def block_copy(refs, b, make):
        for k, ref in enumerate(refs):
            @pl.when(b // per == k)
            def _(k=k, ref=ref):
                make(ref.at[pl.ds(pl.multiple_of((b - k * per) * rows, rows), rows), :]).start()

    def in_copy(s):
        return lambda hbm: pltpu.make_async_copy(hbm, in_buf.at[s], in_sem.at[s])

    def out_copy(s):
        return lambda hbm: pltpu.make_async_copy(out_buf.at[s], hbm, out_sem.at[s])

    @pl.when(step == 0)
    def _():
        block_copy(srcs, 0, in_copy(0))

    @pl.when((step >= 2) & (step - 2 < nb))
    def _():
        out_copy(slot)(dsts[0].at[pl.ds(0, rows), :]).wait()

    @pl.when(step < nb)
    def _():
        in_copy(slot)(srcs[0].at[pl.ds(0, rows), :]).wait()

        @pl.when(step + 1 < nb)
        def _():
            block_copy(srcs, step + 1, in_copy(1 - slot))

        out_buf[slot] = in_buf[slot].astype(BF16)
        block_copy(dsts, step, out_copy(slot))


def _side_cast_fits(srcs, rows, n_steps):
    r = srcs[0].shape[0]
    return r % rows == 0 and (r // rows) * len(srcs) + 2 <= n_steps


def _side_cast_specs(srcs, rows, n_steps):
    r, c = srcs[0].shape
    assert all(a.shape == (r, c) for a in srcs) and _side_cast_fits(srcs, rows, n_steps)
    any_spec = pl.BlockSpec(memory_space=pl.ANY)
    return ([any_spec] * len(srcs), [any_spec] * len(srcs),
            [jax.ShapeDtypeStruct((r, c), BF16) for _ in srcs],
            [pltpu.VMEM((2, rows, c), F32), pltpu.VMEM((2, rows, c), BF16),
             pltpu.SemaphoreType.DMA((2,)), pltpu.SemaphoreType.DMA((2,))])


def _silu_mul(g, u):
    return g * (1.0 / (1.0 + jnp.exp(-g))) * u


def _ffn_kernel(x_ref, gain_ref, wg_ref, wu_ref, wd_ref, *rest, n_side):
    srcs, (o_ref, *dsts), (h_ref, acc_ref, *side) = rest[:n_side], rest[n_side:2 * n_side + 1], rest[2 * n_side + 1:]
    j = pl.program_id(1)
    if n_side:
        _side_cast_step(pl.program_id(0) * pl.num_programs(1) + j, srcs, dsts, *side)

    @pl.when(j == 0)
    def _():
        h_ref[...] = _rms_rows(x_ref[...], gain_ref[...]).astype(BF16)
        acc_ref[...] = jnp.zeros_like(acc_ref)

    h = h_ref[...]
    g = jnp.dot(h, wg_ref[...], preferred_element_type=F32)
    u = jnp.dot(h, wu_ref[...], preferred_element_type=F32)
    acc_ref[...] += jnp.dot(_silu_mul(g, u).astype(BF16), wd_ref[...], preferred_element_type=F32)

    @pl.when(j == pl.num_programs(1) - 1)
    def _():
        o_ref[...] = x_ref[...] + acc_ref[...]


def _dense_ffn(x, gain, w_gate, w_up, w_down, side_srcs=(), side_rows=0):
    n, d = x.shape
    f = w_gate.shape[1]
    tm = min(FFN_TOKEN_BLOCK, n)
    tf = FFN_F_BLOCK
    assert f % tf == 0
    grid = (n // tm, f // tf)
    side_in, side_out, side_shapes, side_scratch = (
        _side_cast_specs(side_srcs, side_rows, grid[0] * grid[1]) if side_srcs else ([], [], [], []))
    return pl.pallas_call(
        functools.partial(_ffn_kernel, n_side=len(side_srcs)),
        grid=grid,
        in_specs=[
            pl.BlockSpec((tm, d), lambda i, j: (i, 0), pipeline_mode=pl.Buffered(1)),
            pl.BlockSpec((1, d), lambda i, j: (0, 0)),
            pl.BlockSpec((d, tf), lambda i, j: (0, j)),
            pl.BlockSpec((d, tf), lambda i, j: (0, j)),
            pl.BlockSpec((tf, d), lambda i, j: (j, 0)),
        ] + side_in,
        out_specs=[pl.BlockSpec((tm, d), lambda i, j: (i, 0))] + side_out,
        out_shape=[jax.ShapeDtypeStruct((n, d), F32)] + side_shapes,
        scratch_shapes=[pltpu.VMEM((tm, d), BF16), pltpu.VMEM((tm, d), F32)] + side_scratch,
        compiler_params=_params("arbitrary", "arbitrary", vmem_limit=FFN_VMEM_LIMIT),
        name="dense_ffn",
    )(x, gain.reshape(1, d), w_gate, w_up, w_down, *side_srcs)


def _band_kernel(q_ref, kp_ref, k_ref, kn_ref, vp_ref, v_ref, vn_ref, o_ref, lse_ref, kwin_ref, vwin_ref, *,
                 first_blocks, per_seq_a, per_seq_b, tq, half):
    tqb = q_ref.shape[0]
    kw = tq + 2 * half
    pos, per_seq = _two_batch_position(pl.program_id(1), first_blocks, per_seq_a, per_seq_b)
    q0 = pos * tqb
    length = per_seq * tqb
    for win, parts in ((kwin_ref, (kp_ref, k_ref, kn_ref)), (vwin_ref, (vp_ref, v_ref, vn_ref))):
        win[:half] = parts[0][...]
        win[half:half + tqb] = parts[1][...]
        win[half + tqb:] = parts[2][...]
    lane = lax.broadcasted_iota(jnp.int32, (tq, LANES), 1)

    def body(t, carry):
        r0 = pl.multiple_of(t * tq, tq)
        qpos = q0 + r0 + lax.broadcasted_iota(jnp.int32, (tq, kw), 0)
        kpos = q0 + r0 - half + lax.broadcasted_iota(jnp.int32, (tq, kw), 1)
        ok = (jnp.abs(qpos - kpos) <= half) & (kpos >= 0) & (kpos < length)
        lse_tile = jnp.zeros((tq, LANES), F32)
        for h in range(q_ref.shape[1] // HEAD_DIM):
            sl = slice(h * HEAD_DIM, (h + 1) * HEAD_DIM)
            q = q_ref[pl.ds(r0, tq), sl]
            k = kwin_ref[pl.ds(r0, kw), sl]
            v = vwin_ref[pl.ds(r0, kw), sl]
            s = lax.dot_general(q, k, (((1,), (1,)), ((), ())), preferred_element_type=F32) * ATTN_SCALE
            s = jnp.where(ok, s, NEG_INF)
            m = jnp.max(s, axis=-1, keepdims=True)
            p = jnp.exp(s - m)
            den = jnp.sum(p, axis=-1, keepdims=True)
            o = jnp.dot(p.astype(BF16), v, preferred_element_type=F32) / den
            o_ref[pl.ds(r0, tq), sl] = o
            lse_tile = jnp.where(lane == h, m + jnp.log(den), lse_tile)
        lse_ref[pl.ds(r0, tq), :] = lse_tile
        return carry

    lax.fori_loop(0, tqb // tq, body, 0)


def _band_attention(qkv, trunks, group):
    window, dil = DIL_CONFIGS[group]
    half = window // (2 * dil)
    _, _, rows, width = qkv.shape
    (off_a, b_a, s_a), (off_b, b_b, s_b) = trunks
    len_a, len_b = s_a // dil, s_b // dil
    assert s_a % dil == 0 and s_b % dil == 0 and qkv.shape[1] == dil and off_a == 0 and off_b == b_a * s_a
    tqb = min(BAND_Q_BLOCK, len_a, len_b)
    tq = min(BAND_Q_TILE, tqb)
    assert len_a % tqb == 0 and len_b % tqb == 0 and tqb % tq == 0 and tqb % half == 0
    hpb = tqb // half
    first_blocks, per_seq_a, per_seq_b = off_b // dil // tqb, len_a // tqb, len_b // tqb

    def halo(which, side):
        def index(r, t):
            pos, per_seq = _two_batch_position(t, first_blocks, per_seq_a, per_seq_b)
            edge = (pos == per_seq - 1) if side else (pos == 0)
            blk = t * hpb + (jnp.where(edge, hpb - 1, hpb) if side else jnp.where(edge, 0, -1))
            return (which, r, blk, 0)
        return index

    mspec = lambda which: pl.BlockSpec((None, None, tqb, width), lambda r, t: (which, r, t, 0))
    hspec = lambda which, side: pl.BlockSpec((None, None, half, width), halo(which, side))
    return pl.pallas_call(
        functools.partial(_band_kernel, first_blocks=first_blocks, per_seq_a=per_seq_a, per_seq_b=per_seq_b,
                          tq=tq, half=half),
        grid=(dil, rows // tqb),
        in_specs=[
            mspec(0),
            hspec(1, 0), mspec(1), hspec(1, 1),
            hspec(2, 0), mspec(2), hspec(2, 1),
        ],
        out_specs=[
            pl.BlockSpec((None, tqb, width), lambda r, t: (r, t, 0)),
            pl.BlockSpec((None, tqb, LANES), lambda r, t: (r, t, 0)),
        ],
        out_shape=[jax.ShapeDtypeStruct((dil, rows, width), F32), jax.ShapeDtypeStruct((dil, rows, LANES), F32)],
        scratch_shapes=[pltpu.VMEM((tqb + 2 * half, width), BF16), pltpu.VMEM((tqb + 2 * half, width), BF16)],
        compiler_params=_params("arbitrary", "arbitrary"),
        name="band_attention",
    )(qkv, qkv, qkv, qkv, qkv, qkv, qkv)


def _merge_proj_kernel(*refs, dils):
    ng = len(dils)
    o_refs, l_refs = refs[:ng], refs[ng:2 * ng]
    w_ref, x_ref, out_ref, a_ref, os_ref, ls_ref = refs[2 * ng:]
    tm = x_ref.shape[0]

    heads = a_ref.shape[1] // HEAD_DIM
    for g, dil in enumerate(dils):
        if dil == 1:
            continue
        for r in range(dil):
            ls_ref[g, pl.ds(r, tm // dil, stride=dil), :] = l_refs[g][r]
            for h in range(heads):
                os_ref[g, h, pl.ds(r, tm // dil, stride=dil), :] = o_refs[g][r, :, h * HEAD_DIM:(h + 1) * HEAD_DIM]

    def group_out(g, h):
        if dils[g] == 1:
            return o_refs[g][0, :, h * HEAD_DIM:(h + 1) * HEAD_DIM]
        return os_ref[g, h]

    lse = [l_refs[g][0] if dils[g] == 1 else ls_ref[g] for g in range(ng)]
    m = functools.reduce(jnp.maximum, lse)
    e = [jnp.exp(l - m) for l in lse]
    tot = functools.reduce(lambda p, q: p + q, e)
    wts = [eg / tot for eg in e]
    for h in range(heads):
        terms = [wts[g][:, h:h + 1] * group_out(g, h) for g in range(ng)]
        a_ref[:, h * HEAD_DIM:(h + 1) * HEAD_DIM] = functools.reduce(lambda p, q: p + q, terms).astype(BF16)
    out_ref[...] = x_ref[...] + jnp.dot(a_ref[...], w_ref[...], preferred_element_type=F32)


def _merge_proj(outs, lses, w, x):
    n, d = x.shape
    width = w.shape[0]
    dils = tuple(o.shape[0] for o in outs)
    tm = min(SMALL_TOKEN_BLOCK, n)
    row = lambda i: (i, 0)
    grp = lambda i: (0, i, 0)
    return pl.pallas_call(
        functools.partial(_merge_proj_kernel, dils=dils),
        grid=(n // tm,),
        in_specs=[pl.BlockSpec((dil, tm // dil, width), grp) for dil in dils]
        + [pl.BlockSpec((dil, tm // dil, LANES), grp) for dil in dils]
        + [pl.BlockSpec((width, d), lambda i: (0, 0)), pl.BlockSpec((tm, d), row)],
        out_specs=pl.BlockSpec((tm, d), row),
        out_shape=jax.ShapeDtypeStruct((n, d), F32),
        scratch_shapes=[
            pltpu.VMEM((tm, width), BF16),
            pltpu.VMEM((len(dils), width // HEAD_DIM, tm, HEAD_DIM), F32),
            pltpu.VMEM((len(dils), tm, LANES), F32),
        ],
        compiler_params=_params("arbitrary"),
        name="merge_proj",
    )(*outs, *lses, w, x)


def _store_chunk_rows(ref, x):
    rows, d = x.shape
    nc = d // LANES
    for c in range(nc):
        ref[pl.ds(c, rows, stride=nc), :] = x[:, c * LANES:(c + 1) * LANES]


def _store_packed_rows(ref, x):
    rows, d = x.shape
    nw = d // (2 * LANES)
    for c in range(nw):
        lo = x[:, (2 * c) * LANES:(2 * c + 1) * LANES].astype(BF16).astype(F32)
        hi = x[:, (2 * c + 1) * LANES:(2 * c + 2) * LANES].astype(BF16).astype(F32)
        word = (pltpu.bitcast(lo, jnp.uint32) >> 16) | (pltpu.bitcast(hi, jnp.uint32) & jnp.uint32(0xFFFF0000))
        ref[pl.ds(c, rows, stride=nw), :] = word


def _load_packed_rows(ref, lead, rows, nw, c):
    word = ref[lead, pl.ds(c, rows, stride=nw), :]
    lo = pltpu.bitcast(word << 16, F32)
    hi = pltpu.bitcast(word & jnp.uint32(0xFFFF0000), F32)
    return lo.astype(BF16), hi.astype(BF16)


def _router_kernel(x_ref, gain_ref, wr_ref, h_ref, idx_ref, gate_ref):
    h = _rms_rows(x_ref[...], gain_ref[...])
    _store_packed_rows(h_ref, h)
    w = wr_ref[...]
    h_hi, w_hi = h.astype(BF16), w.astype(BF16)
    h_lo, w_lo = (h - h_hi.astype(F32)).astype(BF16), (w - w_hi.astype(F32)).astype(BF16)
    logits = (jnp.dot(h_hi, w_hi, preferred_element_type=F32)
              + (jnp.dot(h_hi, w_lo, preferred_element_type=F32) + jnp.dot(h_lo, w_hi, preferred_element_type=F32)))
    lane = lax.broadcasted_iota(jnp.int32, logits.shape, 1)
    lane_f = lane.astype(F32)
    logits = jnp.where(lane < N_EXPERTS, logits, -jnp.inf)
    m1 = jnp.max(logits, axis=-1, keepdims=True)
    i1 = jnp.min(jnp.where(logits == m1, lane_f, float(LANES)), axis=-1, keepdims=True)
    rest = jnp.where(lane_f == i1, -jnp.inf, logits)
    m2 = jnp.max(rest, axis=-1, keepdims=True)
    i2 = jnp.min(jnp.where(rest == m2, lane_f, float(LANES)), axis=-1, keepdims=True)
    e = jnp.exp(m2 - m1)
    tot = 1.0 + e
    idx_ref[...] = jnp.where(lane == 0, i1, jnp.where(lane == 1, i2, 0.0)).astype(jnp.int32)
    gate_ref[...] = jnp.where(lane == 0, 1.0 / tot, jnp.where(lane == 1, e / tot, 0.0))


def _router(x, gain, w_router):
    n, d = x.shape
    tm = min(SMALL_TOKEN_BLOCK, n)
    wr = jnp.zeros((d, LANES), F32).at[:, :N_EXPERTS].set(w_router.astype(F32))
    row = lambda i: (i, 0)
    return pl.pallas_call(
        _router_kernel,
        grid=(n // tm,),
        in_specs=[
            pl.BlockSpec((tm, d), row),
            pl.BlockSpec((1, d), lambda i: (0, 0)),
            pl.BlockSpec((d, LANES), lambda i: (0, 0)),
        ],
        out_specs=[pl.BlockSpec((tm * (d // (2 * LANES)), LANES), row), pl.BlockSpec((tm, LANES), row),
                   pl.BlockSpec((tm, LANES), row)],
        out_shape=[
            jax.ShapeDtypeStruct((n * (d // (2 * LANES)), LANES), jnp.uint32),
            jax.ShapeDtypeStruct((n, LANES), jnp.int32),
            jax.ShapeDtypeStruct((n, LANES), F32),
        ],
        compiler_params=_params("arbitrary"),
        name="router",
    )(x, gain.reshape(1, d), wr)


def _expert_kernel(bexp_ref, brows_ref, nused_ref, rtok_ref, h_hbm, wg_ref, wu_ref, wd_ref, y_ref,
                   xs_ref, xb_ref, acc_ref, sem):
    del bexp_ref
    i = pl.program_id(0)
    j = pl.program_id(1)
    tmb, d = xb_ref.shape
    nw = d // (2 * LANES)
    n_used = nused_ref[0]

    def row_copy(tok, slot, r):
        return pltpu.make_async_copy(h_hbm.at[pl.ds(pl.multiple_of(tok * nw, nw), nw), :],
                                     xs_ref.at[slot, pl.ds(pl.multiple_of(r * nw, nw), nw), :], sem.at[slot])

    def start_gather(blk, slot):
        def issue(r, carry):
            row_copy(rtok_ref[blk * tmb + r], slot, r).start()
            return carry
        lax.fori_loop(0, tmb, issue, 0, unroll=16)

    def wait_gather(slot):
        pltpu.make_async_copy(h_hbm.at[pl.ds(0, tmb * nw), :], xs_ref.at[slot], sem.at[slot]).wait()

    @pl.when(i < n_used)
    def _():
        slot = i % 2

        @pl.when(j == 0)
        def _():
            @pl.when(i == 0)
            def _():
                start_gather(0, 0)

            wait_gather(slot)

            @pl.when(i + 1 < n_used)
            def _():
                start_gather(i + 1, 1 - slot)

            for c in range(nw):
                lo, hi = _load_packed_rows(xs_ref, slot, tmb, nw, c)
                xb_ref[:, (2 * c) * LANES:(2 * c + 1) * LANES] = lo
                xb_ref[:, (2 * c + 1) * LANES:(2 * c + 2) * LANES] = hi
            acc_ref[...] = jnp.zeros_like(acc_ref)

        def swiglu_rows(rows):
            x = xb_ref[:rows]
            g = jnp.dot(x, wg_ref[0], preferred_element_type=F32)
            u = jnp.dot(x, wu_ref[0], preferred_element_type=F32)
            acc_ref[:rows] += jnp.dot(_silu_mul(g, u).astype(BF16), wd_ref[0], preferred_element_type=F32)

        half_full = brows_ref[i] <= tmb // 2

        @pl.when(half_full)
        def _():
            swiglu_rows(tmb // 2)

        @pl.when(jnp.logical_not(half_full))
        def _():
            swiglu_rows(tmb)

    last = j == pl.num_programs(1) - 1

    @pl.when(last & (i < n_used))
    def _():
        _store_chunk_rows(y_ref, acc_ref[...])

    @pl.when(last & (i >= n_used))
    def _():
        y_ref[...] = jnp.zeros_like(y_ref)


def _expert_ffn(h_rows, block_expert, block_rows, n_used, row_tok, w_gate, w_up, w_down):
    d = w_gate.shape[1]
    f = w_gate.shape[2]
    tmb = MOE_BLOCK
    tf = MOE_F_BLOCK
    n_blocks = block_expert.shape[0]
    nf = f // tf
    assert f % tf == 0

    def fblk(i, j, nu):
        return jnp.where(i < nu[0], j, nf - 1)

    grid_spec = pltpu.PrefetchScalarGridSpec(
        num_scalar_prefetch=4,
        grid=(n_blocks, nf),
        in_specs=[
            pl.BlockSpec(memory_space=pl.ANY),
            pl.BlockSpec((1, d, tf), lambda i, j, be, br, nu, rt: (be[i], 0, fblk(i, j, nu))),
            pl.BlockSpec((1, d, tf), lambda i, j, be, br, nu, rt: (be[i], 0, fblk(i, j, nu))),
            pl.BlockSpec((1, tf, d), lambda i, j, be, br, nu, rt: (be[i], fblk(i, j, nu), 0)),
        ],
        out_specs=pl.BlockSpec((tmb * (d // LANES), LANES), lambda i, j, be, br, nu, rt: (i, 0)),
        scratch_shapes=[
            pltpu.VMEM((2, tmb * (d // (2 * LANES)), LANES), jnp.uint32),
            pltpu.VMEM((tmb, d), BF16),
            pltpu.VMEM((tmb, d), F32),
            pltpu.SemaphoreType.DMA((2,)),
        ],
    )
    return pl.pallas_call(
        _expert_kernel,
        grid_spec=grid_spec,
        out_shape=jax.ShapeDtypeStruct((n_blocks * tmb * (d // LANES), LANES), F32),
        compiler_params=_params("arbitrary", "arbitrary"),
        name="expert_ffn",
    )(block_expert, block_rows, n_used, row_tok, h_rows, w_gate, w_up, w_down)


def _combine_kernel(dest_ref, ys_hbm, x_ref, gate_ref, oa_ref, ob_ref, buf_ref, sem, *, first_blocks):
    i = pl.program_id(0)
    nsteps = pl.num_programs(0)
    tc, d = x_ref.shape
    nc = d // LANES
    pitch = buf_ref.shape[1] // (2 * tc)

    def row_copy(row, slot, dst):
        return pltpu.make_async_copy(ys_hbm.at[pl.ds(pl.multiple_of(row * nc, nc), nc), :],
                                     buf_ref.at[slot, pl.ds(pl.multiple_of(dst * pitch, 8), nc), :], sem.at[slot])

    def start_gather(step, slot):
        def issue(r, carry):
            row_copy(dest_ref[step * (2 * tc) + r], slot, r).start()
            return carry
        lax.fori_loop(0, 2 * tc, issue, 0, unroll=8)

    def wait_gather(slot):
        pltpu.make_async_copy(ys_hbm.at[pl.ds(0, 2 * tc * nc), :], buf_ref.at[slot, pl.ds(0, 2 * tc * nc), :],
                              sem.at[slot]).wait()

    slot = i % 2

    @pl.when(i == 0)
    def _():
        start_gather(0, 0)

    wait_gather(slot)

    @pl.when(i + 1 < nsteps)
    def _():
        start_gather(i + 1, 1 - slot)

    g0 = gate_ref[:, 0:1]
    g1 = gate_ref[:, 1:2]

    def emit(o_ref):
        for c in range(nc):
            sl = slice(c * LANES, (c + 1) * LANES)
            y0 = buf_ref[slot, pl.ds(c, tc, stride=pitch), :]
            y1 = buf_ref[slot, pl.ds(tc * pitch + c, tc, stride=pitch), :]
            o_ref[:, sl] = x_ref[:, sl] + (y0 * g0 + y1 * g1)
    _pick_rows(i, first_blocks, oa_ref, ob_ref, emit)


def _combine(x, ys, dest, gate, n_first):
    n, d = x.shape
    tc = min(COMBINE_BLOCK, n_first, n - n_first)
    assert n_first % tc == 0 and (n - n_first) % tc == 0
    first_blocks = n_first // tc
    nc = d // LANES
    pitch = nc if (nc // 8) % 2 else nc + 8
    dest = dest.reshape(n // tc, tc, TOP_K).transpose(0, 2, 1).reshape(-1)
    grid_spec = pltpu.PrefetchScalarGridSpec(
        num_scalar_prefetch=1,
        grid=(n // tc,),
        in_specs=[
            pl.BlockSpec(memory_space=pl.ANY),
            pl.BlockSpec((tc, d), lambda i, dr: (i, 0)),
            pl.BlockSpec((tc, LANES), lambda i, dr: (i, 0)),
        ],
        out_specs=_two_batch_specs(tc, d, first_blocks),
        scratch_shapes=[pltpu.VMEM((2, 2 * tc * pitch, LANES), F32), pltpu.SemaphoreType.DMA((2,))],
    )
    return pl.pallas_call(
        functools.partial(_combine_kernel, first_blocks=first_blocks),
        grid_spec=grid_spec,
        out_shape=[jax.ShapeDtypeStruct((n_first, d), F32), jax.ShapeDtypeStruct((n - n_first, d), F32)],
        compiler_params=_params("arbitrary"),
        name="moe_combine",
    )(dest, ys, x, gate)


def _routing_tables(idx, tmb):
    n = idx.shape[0]
    nk = n * TOP_K
    e_flat = idx[:, :TOP_K].reshape(-1)
    onehot = (e_flat[:, None] == jnp.arange(N_EXPERTS, dtype=jnp.int32)[None, :]).astype(jnp.int32)
    csum = jnp.cumsum(onehot, axis=0)
    rank = jnp.sum(onehot * csum, axis=1) - 1
    counts = csum[-1]
    padded = (counts + tmb - 1) // tmb * tmb
    pad_end = jnp.cumsum(padded)
    pad_start = pad_end - padded
    dest = (jnp.sum(onehot * pad_start[None, :], axis=1) + rank).astype(jnp.int32)
    n_blocks = -(-nk // tmb) + N_EXPERTS
    block_expert = jnp.minimum(
        jnp.searchsorted(pad_end, jnp.arange(n_blocks, dtype=jnp.int32) * tmb, side="right"), N_EXPERTS - 1
    ).astype(jnp.int32)
    n_used = (pad_end[-1] // tmb).astype(jnp.int32).reshape(1)
    block_lo = jnp.arange(n_blocks, dtype=jnp.int32) * tmb - pad_start[block_expert]
    block_rows = jnp.clip(counts[block_expert] - block_lo, 0, tmb).astype(jnp.int32)
    row_tok = jnp.zeros((n_blocks * tmb,), jnp.int32).at[dest].set(jnp.arange(nk, dtype=jnp.int32) // TOP_K)
    return dest, block_expert, block_rows, n_used, row_tok


def _rope_tables(seq_len):
    half = HEAD_DIM // 2
    inv_freq = ROPE_THETA ** (-jnp.arange(half, dtype=F32) / half)
    ang = jnp.arange(seq_len, dtype=F32)[:, None] * inv_freq[None, :]
    cos, sin = jnp.cos(ang), jnp.sin(ang)
    return jnp.concatenate([cos, cos], axis=-1), jnp.concatenate([-sin, sin], axis=-1)


def kernel(x_prompt, x_sample, norm_mix, norm_ffn, na_w_qkv, na_q_gain, na_k_gain, na_rpb, na_w_o,
           da_w_qkv, da_q_gain, da_k_gain, da_w_o, ffn_w_gate, ffn_w_up, ffn_w_down,
           moe_w_router, moe_w_gate, moe_w_up, moe_w_down):
    d = x_prompt.shape[-1]
    trunks = []
    off = 0
    for xin in (x_prompt, x_sample):
        b, s, _ = xin.shape
        trunks.append((off, b, s))
        off += b * s
    n = off
    xa, xb = x_prompt.reshape(-1, d), x_sample.reshape(-1, d)
    depth = norm_mix.shape[0]
    assert depth == 2, "layer 0 reads the two request batches directly and layer 1 writes them separately"
    x = None

    for layer in range(depth):
        lj = layer // 2
        if layer % 2 == 0:
            heads = na_w_qkv.shape[2] // (3 * HEAD_DIM)
            per = heads * HEAD_DIM // PROJ_COL_BLOCK
            hg = jnp.concatenate([
                jnp.broadcast_to(na_q_gain[lj], (per, HEAD_DIM)),
                jnp.broadcast_to(na_k_gain[lj], (per, HEAD_DIM)),
                jnp.ones((per, HEAD_DIM), F32),
            ]).reshape(3 * per, 1, HEAD_DIM)
            qkv = _qkv_project(xa, xb, norm_mix[layer], na_w_qkv[lj].astype(BF16), hg, 3 * per, 2 * per)
            tables = _na_bias_tables(na_rpb[lj])
            att = _na_attention(qkv, tables, trunks, heads)
            x = _proj_residual(att, na_w_o[lj].astype(BF16), xa, xb)
            ne, dm, fe = moe_w_gate.shape[1:]
            wide = (moe_w_gate[lj].reshape(ne * dm, fe), moe_w_up[lj].reshape(ne * dm, fe))
            steps = (n // min(FFN_TOKEN_BLOCK, n)) * (ffn_w_gate.shape[2] // FFN_F_BLOCK)
            shadow = _side_cast_fits(wide, SIDE_CAST_ROWS_WIDE, steps)
            x, *cast = _dense_ffn(x, norm_ffn[layer], ffn_w_gate[lj].astype(BF16), ffn_w_up[lj].astype(BF16),
                                  ffn_w_down[lj].astype(BF16), side_srcs=wide if shadow else (),
                                  side_rows=SIDE_CAST_ROWS_WIDE)
            moe_gate_bf, moe_up_bf = cast if shadow else [w.astype(BF16) for w in wide]
        else:
            groups = len(DIL_CONFIGS)
            heads = da_w_qkv.shape[2] // (groups * 3 * HEAD_DIM)
            dils = [dil for _, dil in DIL_CONFIGS]
            ones = jnp.ones((HEAD_DIM,), F32)
            hg = jnp.stack([t for g in range(groups) for t in (da_q_gain[lj, g], da_k_gain[lj, g], ones)])
            hg = hg.reshape(groups * 3, 1, HEAD_DIM)
            (o0, b0, s0), (o1, b1, s1) = trunks
            tq = min(QKV_DIL_TOKEN_BLOCK, n)
            assert s0 % tq == 0 and s1 % tq == 0 and o1 % tq == 0
            nb0, p0, p1 = o1 // tq, s0 // tq, s1 // tq
            pos_map = lambda i: jnp.where(i < nb0, i % p0, (i - nb0) % p1)
            smax = max(s0, s1)
            rope = [jnp.stack([t.reshape(smax // tq, tq // dil, dil, HEAD_DIM).transpose(0, 2, 1, 3)
                               .reshape(smax, HEAD_DIM) for dil in dils]) for t in _rope_tables(smax)]
            narrow = (moe_w_down[lj].reshape(ne * fe, dm),)
            shadow = _side_cast_fits(narrow, SIDE_CAST_ROWS_NARROW, (n // tq) * groups * 3)
            qkvs = list(_qkv_project_dilated(
                x, norm_mix[layer], da_w_qkv[lj].astype(BF16), hg, rope, pos_map, dils, heads, tq,
                side_srcs=narrow if shadow else (), side_rows=SIDE_CAST_ROWS_NARROW))
            moe_down_bf = qkvs.pop() if shadow else narrow[0].astype(BF16)
            outs, lses = [], []
            for g in range(groups):
                o_g, lse_g = _band_attention(qkvs[g], trunks, g)
                outs.append(o_g)
                lses.append(lse_g)
            x = _merge_proj(outs, lses, da_w_o[lj].astype(BF16), x)
            h, idx, gate = _router(x, norm_ffn[layer], moe_w_router[lj])
            dest, block_expert, block_rows, n_used, row_tok = _routing_tables(idx, MOE_BLOCK)
            ys = _expert_ffn(h, block_expert, block_rows, n_used, row_tok, moe_gate_bf.reshape(ne, dm, fe),
                             moe_up_bf.reshape(ne, dm, fe), moe_down_bf.reshape(ne, fe, dm))
            ya, yb = _combine(x, ys, dest, gate, xa.shape[0])

    return ya.reshape(x_prompt.shape), yb.reshape(x_sample.shape)
```

```python
import functools

import jax
import jax.numpy as jnp
from jax import lax
from jax.experimental import pallas as pl
from jax.experimental.pallas import tpu as pltpu

F32 = jnp.float32
BF16 = jnp.bfloat16

HEAD_DIM = 128
GRID_W = 64
NA_ROWS = 8
NA_COLS = 16
NA_GROUP = NA_ROWS // 2
NA_KEY_ROWS = 3 * NA_GROUP
NA_HEAD_BLOCK = 16
DIL_CONFIGS = ((128, 1), (512, 4), (2048, 16))
N_EXPERTS = 8
TOP_K = 2
ROPE_THETA = 10000.0
EPS = 1e-6
NEG_INF = -1e30
ATTN_SCALE = HEAD_DIM ** -0.5

LANES = 128
MXU_COLS = 256
VMEM_LIMIT = 52 * 1024 * 1024
FFN_VMEM_LIMIT = 57 * 1024 * 1024

TOKEN_BLOCK = 1024
PROJ_COL_BLOCK = 1024
SMALL_TOKEN_BLOCK = 512
QKV_DIL_TOKEN_BLOCK = 512
FFN_TOKEN_BLOCK = 1024
FFN_F_BLOCK = 256
BAND_Q_BLOCK = 512
BAND_Q_TILE = 128
MOE_BLOCK = 512
MOE_F_BLOCK = 1024
COMBINE_BLOCK = 512
SIDE_CAST_ROWS_WIDE = 128
SIDE_CAST_ROWS_NARROW = 512


def _params(*sem, vmem_limit=VMEM_LIMIT):
    return pltpu.CompilerParams(dimension_semantics=sem, vmem_limit_bytes=vmem_limit)


def _rms_rows(x, gain):
    ms = jnp.mean(x * x, axis=-1, keepdims=True)
    return x * lax.rsqrt(ms + EPS) * gain


def _pick_rows(i, first_blocks, xa_ref, xb_ref, use):
    @pl.when(i < first_blocks)
    def _():
        use(xa_ref)

    @pl.when(i >= first_blocks)
    def _():
        use(xb_ref)


def _two_batch_specs(tm, d, first_blocks, **kw):
    return [pl.BlockSpec((tm, d), lambda i, *_: (jnp.minimum(i, first_blocks - 1), 0), **kw),
            pl.BlockSpec((tm, d), lambda i, *_: (jnp.maximum(i - first_blocks, 0), 0), **kw)]


def _qkv_kernel(xa_ref, xb_ref, gain_ref, w_ref, hg_ref, o_ref, h_ref, *, first_blocks, v_every, v_from):
    j = pl.program_id(1)

    @pl.when(j == 0)
    def _():
        def norm(x_ref):
            h_ref[...] = _rms_rows(x_ref[...], gain_ref[...]).astype(BF16)
        _pick_rows(pl.program_id(0), first_blocks, xa_ref, xb_ref, norm)

    is_qk = (j % v_every) < v_from
    for c in range(w_ref.shape[1] // MXU_COLS):
        acc = jnp.dot(h_ref[...], w_ref[:, c * MXU_COLS:(c + 1) * MXU_COLS], preferred_element_type=F32)
        for h in range(MXU_COLS // HEAD_DIM):
            seg = acc[:, h * HEAD_DIM:(h + 1) * HEAD_DIM]
            lo = c * MXU_COLS + h * HEAD_DIM
            o_ref[:, lo:lo + HEAD_DIM] = jnp.where(is_qk, _rms_rows(seg, hg_ref[0]), seg).astype(o_ref.dtype)


def _qkv_project(xa, xb, gain, w, head_gain, v_every, v_from):
    d = xa.shape[1]
    n = xa.shape[0] + xb.shape[0]
    n_out = w.shape[1]
    tm = min(TOKEN_BLOCK, xa.shape[0], xb.shape[0])
    assert xa.shape[0] % tm == 0 and xb.shape[0] % tm == 0
    first_blocks = xa.shape[0] // tm
    tn = n_out // head_gain.shape[0]
    return pl.pallas_call(
        functools.partial(_qkv_kernel, first_blocks=first_blocks, v_every=v_every, v_from=v_from),
        grid=(n // tm, n_out // tn),
        in_specs=_two_batch_specs(tm, d, first_blocks) + [
            pl.BlockSpec((1, d), lambda i, j: (0, 0)),
            pl.BlockSpec((d, tn), lambda i, j: (0, j)),
            pl.BlockSpec((1, 1, HEAD_DIM), lambda i, j: (j, 0, 0)),
        ],
        out_specs=pl.BlockSpec((tm, tn), lambda i, j: (i, j)),
        out_shape=jax.ShapeDtypeStruct((n, n_out), BF16),
        scratch_shapes=[pltpu.VMEM((tm, d), BF16)],
        compiler_params=_params("arbitrary", "arbitrary", vmem_limit=FFN_VMEM_LIMIT),
        name="qkv_project",
    )(xa, xb, gain.reshape(1, d), w, head_gain)


def _qkv_dil_kernel(x_ref, gain_ref, w_ref, hg_ref, cos_ref, sin_ref, *rest, dils, n_side):
    ng = len(dils)
    srcs, out_refs, dsts = rest[:n_side], rest[n_side:n_side + ng], rest[n_side + ng:2 * n_side + ng]
    h_ref, stage_ref, *side = rest[2 * n_side + ng:]
    j = pl.program_id(1)
    if n_side:
        _side_cast_step(pl.program_id(0) * pl.num_programs(1) + j, srcs, dsts, *side)
    tm, d = x_ref.shape
    tn = w_ref.shape[1]

    @pl.when(j == 0)
    def _():
        y = _rms_rows(x_ref[...], gain_ref[...])
        for c in range(d // LANES):
            stage_ref[c] = y[:, c * LANES:(c + 1) * LANES]
        for g, dil in enumerate(dils):
            rows = tm // dil
            for r in range(dil):
                for c in range(d // LANES):
                    h_ref[g, r * rows:(r + 1) * rows, c * LANES:(c + 1) * LANES] = (
                        stage_ref[c, pl.ds(r, rows, stride=dil), :].astype(BF16))

    group = j // 3
    which = j % 3
    acc = jnp.dot(h_ref[group], w_ref[...], preferred_element_type=F32)

    for g, (dil, o_ref) in enumerate(zip(dils, out_refs)):
        rows = tm // dil

        @pl.when((group == g) & (which < 2))
        def _(dil=dil, o_ref=o_ref, rows=rows):
            for h in range(tn // HEAD_DIM):
                sl = slice(h * HEAD_DIM, (h + 1) * HEAD_DIM)
                y = _rms_rows(acc[:, sl], hg_ref[0])
                y = (y * cos_ref[...] + pltpu.roll(y, HEAD_DIM // 2, 1) * sin_ref[...]).astype(o_ref.dtype)
                for r in range(dil):
                    o_ref[r, :, sl] = y[r * rows:(r + 1) * rows]

        @pl.when((group == g) & (which == 2))
        def _(dil=dil, o_ref=o_ref, rows=rows):
            v = acc.astype(o_ref.dtype)
            for r in range(dil):
                o_ref[r] = v[r * rows:(r + 1) * rows]


def _qkv_project_dilated(x, gain, w, head_gain, rope_tabs, pos_map, dils, heads, tm, side_srcs=(), side_rows=0):
    n, d = x.shape
    n_out = w.shape[1]
    width = heads * HEAD_DIM
    ng = len(dils)
    assert n_out == ng * 3 * width and head_gain.shape[0] == ng * 3
    side_in, side_out, side_shapes, side_scratch = (
        _side_cast_specs(side_srcs, side_rows, (n // tm) * ng * 3) if side_srcs else ([], [], [], []))

    def out_map(g):
        def index(i, j):
            return (jnp.clip(j - 3 * g, 0, 2), 0, i, 0)
        return index

    rope_spec = pl.BlockSpec((None, tm, HEAD_DIM), lambda i, j: (j // 3, pos_map(i), 0))
    return pl.pallas_call(
        functools.partial(_qkv_dil_kernel, dils=tuple(dils), n_side=len(side_srcs)),
        grid=(n // tm, ng * 3),
        in_specs=[
            pl.BlockSpec((tm, d), lambda i, j: (i, 0)),
            pl.BlockSpec((1, d), lambda i, j: (0, 0)),
            pl.BlockSpec((d, width), lambda i, j: (0, j)),
            pl.BlockSpec((1, 1, HEAD_DIM), lambda i, j: (j, 0, 0)),
            rope_spec,
            rope_spec,
        ] + side_in,
        out_specs=[pl.BlockSpec((None, dil, tm // dil, width), out_map(g)) for g, dil in enumerate(dils)] + side_out,
        out_shape=[jax.ShapeDtypeStruct((3, dil, n // dil, width), BF16) for dil in dils] + side_shapes,
        scratch_shapes=[pltpu.VMEM((ng, tm, d), BF16), pltpu.VMEM((d // LANES, tm, LANES), F32)] + side_scratch,
        compiler_params=_params("arbitrary", "arbitrary"),
        name="qkv_project_dilated",
    )(x, gain.reshape(1, d), w, head_gain, *rope_tabs, *side_srcs)


def _na_table_kernel(rpb_ref, o_ref):
    h = pl.program_id(0)
    w = GRID_W
    lane = lax.broadcasted_iota(jnp.int32, (w, 2 * w), 1)
    cq = lax.broadcasted_iota(jnp.int32, (w, 2 * w), 0)
    ck = lane % w
    upper = lane >= w
    cs = jnp.clip(cq - NA_COLS // 2, 0, w - NA_COLS)
    col_ok = (ck >= cs) & (ck < cs + NA_COLS)
    dc = ck - cq + (NA_COLS - 1)

    def pair_tile(dr):
        t = jnp.zeros((w, 2 * w), F32)
        for b in range(2 * NA_COLS - 1):
            lo = rpb_ref[h, dr + NA_ROWS - 1, b]
            hi = rpb_ref[h, dr + NA_ROWS, b]
            t = jnp.where(dc == b, jnp.where(upper, hi, lo), t)
        return t

    tiles = {}
    for qi in range(NA_GROUP):
        for kp in range(NA_KEY_ROWS // 2):
            dr = 2 * kp - NA_GROUP - qi
            if dr not in tiles:
                tiles[dr] = pair_tile(dr)
            for v in range(3):
                def row_ok(kj):
                    if v == 0:
                        return kj >= NA_GROUP
                    if v == 1:
                        return 0 <= kj - qi < NA_ROWS
                    return kj < NA_ROWS
                ok_lo, ok_hi = row_ok(2 * kp), row_ok(2 * kp + 1)
                if ok_lo and ok_hi:
                    ok = col_ok
                elif ok_lo:
                    ok = col_ok & jnp.logical_not(upper)
                elif ok_hi:
                    ok = col_ok & upper
                else:
                    ok = None
                tile = jnp.full((w, 2 * w), NEG_INF, F32) if ok is None else jnp.where(ok, tiles[dr], NEG_INF)
                o_ref[v, 0, qi * w:(qi + 1) * w, kp * 2 * w:(kp + 1) * 2 * w] = tile


def _na_bias_tables(rpb):
    assert NA_GROUP == NA_ROWS // 2 and NA_KEY_ROWS % 2 == 0 and 2 * GRID_W == LANES
    heads = rpb.shape[0]
    shape = (3, heads, NA_GROUP * GRID_W, NA_KEY_ROWS * GRID_W)
    return pl.pallas_call(
        _na_table_kernel,
        grid=(heads,),
        in_specs=[pl.BlockSpec(memory_space=pltpu.SMEM)],
        out_specs=pl.BlockSpec((3, 1) + shape[2:], lambda h: (0, h, 0, 0)),
        out_shape=jax.ShapeDtypeStruct(shape, F32),
        compiler_params=_params("arbitrary"),
        name="na_bias_table",
    )(rpb.astype(F32))


def _na_kernel(q_ref, kp_ref, k_ref, kn_ref, vp_ref, v_ref, vn_ref, tab_ref, o_ref):
    gq = q_ref.shape[0]
    nt = (((1,), (1,)), ((), ()))
    for h in range(q_ref.shape[1] // HEAD_DIM):
        sl = slice(h * HEAD_DIM, (h + 1) * HEAD_DIM)
        q = q_ref[:, sl]
        s = jnp.concatenate(
            [lax.dot_general(q, kr[:, sl], nt, preferred_element_type=F32) for kr in (kp_ref, k_ref, kn_ref)],
            axis=1)
        s = s * ATTN_SCALE + tab_ref[0, h]
        m = jnp.max(s, axis=-1, keepdims=True)
        p = jnp.exp(s - m)
        den = jnp.sum(p, axis=-1, keepdims=True)
        pb = p.astype(BF16)
        o = (jnp.dot(pb[:, :gq], vp_ref[:, sl], preferred_element_type=F32)
             + jnp.dot(pb[:, gq:2 * gq], v_ref[:, sl], preferred_element_type=F32)
             + jnp.dot(pb[:, 2 * gq:], vn_ref[:, sl], preferred_element_type=F32))
        o_ref[:, sl] = (o / den).astype(o_ref.dtype)


def _two_batch_position(t, first_blocks, per_seq_a, per_seq_b):
    in_a = t < first_blocks
    per_seq = jnp.where(in_a, per_seq_a, per_seq_b)
    return jnp.where(in_a, t, t - first_blocks) % per_seq, per_seq


def _na_attention(qkv, tables, trunks, heads):
    n = qkv.shape[0]
    gq = NA_GROUP * GRID_W
    hb = NA_HEAD_BLOCK
    hblocks = heads // hb
    (off_a, b_a, s_a), (off_b, b_b, s_b) = trunks
    for (_, _, seq) in trunks:
        assert seq % gq == 0 and seq // GRID_W >= NA_ROWS
    assert heads % hb == 0 and off_a == 0 and off_b == b_a * s_a
    first_groups, gps_a, gps_b = off_b // gq, s_a // gq, s_b // gq

    def place(g):
        pos, per_seq = _two_batch_position(g, first_groups, gps_a, gps_b)
        return pos == 0, pos == per_seq - 1

    def variant(g):
        first, last = place(g)
        return jnp.where(first, 0, jnp.where(last, 2, 1))

    def rows_of(which, shift):
        def index(h, g):
            first, last = place(g)
            gg = g + jnp.where((shift < 0) & first, 0, jnp.where((shift > 0) & last, 0, shift))
            return (gg, which * hblocks + h)
        return index

    blk = lambda index: pl.BlockSpec((gq, hb * HEAD_DIM), index)
    return pl.pallas_call(
        _na_kernel,
        grid=(hblocks, n // gq),
        in_specs=[
            blk(rows_of(0, 0)),
            blk(rows_of(1, -1)), blk(rows_of(1, 0)), blk(rows_of(1, 1)),
            blk(rows_of(2, -1)), blk(rows_of(2, 0)), blk(rows_of(2, 1)),
            pl.BlockSpec((1, hb, gq, NA_KEY_ROWS * GRID_W), lambda h, g: (variant(g), h, 0, 0)),
        ],
        out_specs=blk(rows_of(0, 0)),
        out_shape=jax.ShapeDtypeStruct((n, heads * HEAD_DIM), BF16),
        compiler_params=_params("arbitrary", "arbitrary"),
        name="na_attention",
    )(qkv, qkv, qkv, qkv, qkv, qkv, qkv, tables)


def _proj_residual_kernel(a_ref, w_ref, xa_ref, xb_ref, o_ref, *, first_blocks):
    y = jnp.dot(a_ref[...], w_ref[...], preferred_element_type=F32)

    def add(x_ref):
        o_ref[...] = x_ref[...] + y
    _pick_rows(pl.program_id(1), first_blocks, xa_ref, xb_ref, add)


def _proj_residual(a, w, xa, xb):
    n, k = a.shape
    d = w.shape[1]
    tm = min(TOKEN_BLOCK, xa.shape[0], xb.shape[0])
    tn = min(PROJ_COL_BLOCK, d)
    assert xa.shape[0] % tm == 0 and xb.shape[0] % tm == 0 and xa.shape[0] + xb.shape[0] == n
    first_blocks = xa.shape[0] // tm
    return pl.pallas_call(
        functools.partial(_proj_residual_kernel, first_blocks=first_blocks),
        grid=(d // tn, n // tm),
        in_specs=[
            pl.BlockSpec((tm, k), lambda j, i: (i, 0)),
            pl.BlockSpec((k, tn), lambda j, i: (0, j)),
            pl.BlockSpec((tm, tn), lambda j, i: (jnp.minimum(i, first_blocks - 1), j)),
            pl.BlockSpec((tm, tn), lambda j, i: (jnp.maximum(i - first_blocks, 0), j)),
        ],
        out_specs=pl.BlockSpec((tm, tn), lambda j, i: (i, j)),
        out_shape=jax.ShapeDtypeStruct((n, d), F32),
        compiler_params=_params("arbitrary", "arbitrary"),
        name="proj_residual",
    )(a, w, xa, xb)


def _side_cast_step(step, srcs, dsts, in_buf, out_buf, in_sem, out_sem):
    rows = in_buf.shape[1]
    per = srcs[0].shape[0] // rows
    nb = per * len(srcs)
    slot = step % 2

    def block_copy(refs, b, make):
        for k, ref in enumerate(refs):
            @pl.when(b // per == k)
            def _(k=k, ref=ref):
                make(ref.at[pl.ds(pl.multiple_of((b - k * per) * rows, rows), rows), :]).start()

    def in_copy(s):
        return lambda hbm: pltpu.make_async_copy(hbm, in_buf.at[s], in_sem.at[s])

    def out_copy(s):
        return lambda hbm: pltpu.make_async_copy(out_buf.at[s], hbm, out_sem.at[s])

    @pl.when(step == 0)
    def _():
        block_copy(srcs, 0, in_copy(0))

    @pl.when((step >= 2) & (step - 2 < nb))
    def _():
        out_copy(slot)(dsts[0].at[pl.ds(0, rows), :]).wait()

    @pl.when(step < nb)
    def _():
        in_copy(slot)(srcs[0].at[pl.ds(0, rows), :]).wait()

        @pl.when(step + 1 < nb)
        def _():
            block_copy(srcs, step + 1, in_copy(1 - slot))

        out_buf[slot] = in_buf[slot].astype(BF16)
        block_copy(dsts, step, out_copy(slot))


def _side_cast_fits(srcs, rows, n_steps):
    r = srcs[0].shape[0]
    return r % rows == 0 and (r // rows) * len(srcs) + 2 <= n_steps


def _side_cast_specs(srcs, rows, n_steps):
    r, c = srcs[0].shape
    assert all(a.shape == (r, c) for a in srcs) and _side_cast_fits(srcs, rows, n_steps)
    any_spec = pl.BlockSpec(memory_space=pl.ANY)
    return ([any_spec] * len(srcs), [any_spec] * len(srcs),
            [jax.ShapeDtypeStruct((r, c), BF16) for _ in srcs],
            [pltpu.VMEM((2, rows, c), F32), pltpu.VMEM((2, rows, c), BF16),
             pltpu.SemaphoreType.DMA((2,)), pltpu.SemaphoreType.DMA((2,))])


def _silu_mul(g, u):
    return g * (1.0 / (1.0 + jnp.exp(-g))) * u


def _ffn_kernel(x_ref, gain_ref, wg_ref, wu_ref, wd_ref, *rest, n_side):
    srcs, (o_ref, *dsts), (h_ref, acc_ref, *side) = rest[:n_side], rest[n_side:2 * n_side + 1], rest[2 * n_side + 1:]
    j = pl.program_id(1)
    if n_side:
        _side_cast_step(pl.program_id(0) * pl.num_programs(1) + j, srcs, dsts, *side)

    @pl.when(j == 0)
    def _():
        h_ref[...] = _rms_rows(x_ref[...], gain_ref[...]).astype(BF16)
        acc_ref[...] = jnp.zeros_like(acc_ref)

    h = h_ref[...]
    g = jnp.dot(h, wg_ref[...], preferred_element_type=F32)
    u = jnp.dot(h, wu_ref[...], preferred_element_type=F32)
    acc_ref[...] += jnp.dot(_silu_mul(g, u).astype(BF16), wd_ref[...], preferred_element_type=F32)

    @pl.when(j == pl.num_programs(1) - 1)
    def _():
        o_ref[...] = x_ref[...] + acc_ref[...]


def _dense_ffn(x, gain, w_gate, w_up, w_down, side_srcs=(), side_rows=0):
    n, d = x.shape
    f = w_gate.shape[1]
    tm = min(FFN_TOKEN_BLOCK, n)
    tf = FFN_F_BLOCK
    assert f % tf == 0
    grid = (n // tm, f // tf)
    side_in, side_out, side_shapes, side_scratch = (
        _side_cast_specs(side_srcs, side_rows, grid[0] * grid[1]) if side_srcs else ([], [], [], []))
    return pl.pallas_call(
        functools.partial(_ffn_kernel, n_side=len(side_srcs)),
        grid=grid,
        in_specs=[
            pl.BlockSpec((tm, d), lambda i, j: (i, 0), pipeline_mode=pl.Buffered(1)),
            pl.BlockSpec((1, d), lambda i, j: (0, 0)),
            pl.BlockSpec((d, tf), lambda i, j: (0, j)),
            pl.BlockSpec((d, tf), lambda i, j: (0, j)),
            pl.BlockSpec((tf, d), lambda i, j: (j, 0)),
        ] + side_in,
        out_specs=[pl.BlockSpec((tm, d), lambda i, j: (i, 0))] + side_out,
        out_shape=[jax.ShapeDtypeStruct((n, d), F32)] + side_shapes,
        scratch_shapes=[pltpu.VMEM((tm, d), BF16), pltpu.VMEM((tm, d), F32)] + side_scratch,
        compiler_params=_params("arbitrary", "arbitrary", vmem_limit=FFN_VMEM_LIMIT),
        name="dense_ffn",
    )(x, gain.reshape(1, d), w_gate, w_up, w_down, *side_srcs)


def _band_kernel(q_ref, kp_ref, k_ref, kn_ref, vp_ref, v_ref, vn_ref, o_ref, lse_ref, kwin_ref, vwin_ref, *,
                 first_blocks, per_seq_a, per_seq_b, tq, half):
    tqb = q_ref.shape[0]
    kw = tq + 2 * half
    pos, per_seq = _two_batch_position(pl.program_id(1), first_blocks, per_seq_a, per_seq_b)
    q0 = pos * tqb
    length = per_seq * tqb
    for win, parts in ((kwin_ref, (kp_ref, k_ref, kn_ref)), (vwin_ref, (vp_ref, v_ref, vn_ref))):
        win[:half] = parts[0][...]
        win[half:half + tqb] = parts[1][...]
        win[half + tqb:] = parts[2][...]
    lane = lax.broadcasted_iota(jnp.int32, (tq, LANES), 1)

    def body(t, carry):
        r0 = pl.multiple_of(t * tq, tq)
        qpos = q0 + r0 + lax.broadcasted_iota(jnp.int32, (tq, kw), 0)
        kpos = q0 + r0 - half + lax.broadcasted_iota(jnp.int32, (tq, kw), 1)
        ok = (jnp.abs(qpos - kpos) <= half) & (kpos >= 0) & (kpos < length)
        lse_tile = jnp.zeros((tq, LANES), F32)
        for h in range(q_ref.shape[1] // HEAD_DIM):
            sl = slice(h * HEAD_DIM, (h + 1) * HEAD_DIM)
            q = q_ref[pl.ds(r0, tq), sl]
            k = kwin_ref[pl.ds(r0, kw), sl]
            v = vwin_ref[pl.ds(r0, kw), sl]
            s = lax.dot_general(q, k, (((1,), (1,)), ((), ())), preferred_element_type=F32) * ATTN_SCALE
            s = jnp.where(ok, s, NEG_INF)
            m = jnp.max(s, axis=-1, keepdims=True)
            p = jnp.exp(s - m)
            den = jnp.sum(p, axis=-1, keepdims=True)
            o = jnp.dot(p.astype(BF16), v, preferred_element_type=F32) / den
            o_ref[pl.ds(r0, tq), sl] = o
            lse_tile = jnp.where(lane == h, m + jnp.log(den), lse_tile)
        lse_ref[pl.ds(r0, tq), :] = lse_tile
        return carry

    lax.fori_loop(0, tqb // tq, body, 0)


def _band_attention(qkv, trunks, group):
    window, dil = DIL_CONFIGS[group]
    half = window // (2 * dil)
    _, _, rows, width = qkv.shape
    (off_a, b_a, s_a), (off_b, b_b, s_b) = trunks
    len_a, len_b = s_a // dil, s_b // dil
    assert s_a % dil == 0 and s_b % dil == 0 and qkv.shape[1] == dil and off_a == 0 and off_b == b_a * s_a
    tqb = min(BAND_Q_BLOCK, len_a, len_b)
    tq = min(BAND_Q_TILE, tqb)
    assert len_a % tqb == 0 and len_b % tqb == 0 and tqb % tq == 0 and tqb % half == 0
    hpb = tqb // half
    first_blocks, per_seq_a, per_seq_b = off_b // dil // tqb, len_a // tqb, len_b // tqb

    def halo(which, side):
        def index(r, t):
            pos, per_seq = _two_batch_position(t, first_blocks, per_seq_a, per_seq_b)
            edge = (pos == per_seq - 1) if side else (pos == 0)
            blk = t * hpb + (jnp.where(edge, hpb - 1, hpb) if side else jnp.where(edge, 0, -1))
            return (which, r, blk, 0)
        return index

    mspec = lambda which: pl.BlockSpec((None, None, tqb, width), lambda r, t: (which, r, t, 0))
    hspec = lambda which, side: pl.BlockSpec((None, None, half, width), halo(which, side))
    return pl.pallas_call(
        functools.partial(_band_kernel, first_blocks=first_blocks, per_seq_a=per_seq_a, per_seq_b=per_seq_b,
                          tq=tq, half=half),
        grid=(dil, rows // tqb),
        in_specs=[
            mspec(0),
            hspec(1, 0), mspec(1), hspec(1, 1),
            hspec(2, 0), mspec(2), hspec(2, 1),
        ],
        out_specs=[
            pl.BlockSpec((None, tqb, width), lambda r, t: (r, t, 0)),
            pl.BlockSpec((None, tqb, LANES), lambda r, t: (r, t, 0)),
        ],
        out_shape=[jax.ShapeDtypeStruct((dil, rows, width), F32), jax.ShapeDtypeStruct((dil, rows, LANES), F32)],
        scratch_shapes=[pltpu.VMEM((tqb + 2 * half, width), BF16), pltpu.VMEM((tqb + 2 * half, width), BF16)],
        compiler_params=_params("arbitrary", "arbitrary"),
        name="band_attention",
    )(qkv, qkv, qkv, qkv, qkv, qkv, qkv)


def _merge_proj_kernel(*refs, dils):
    ng = len(dils)
    o_refs, l_refs = refs[:ng], refs[ng:2 * ng]
    w_ref, x_ref, out_ref, a_ref, os_ref, ls_ref = refs[2 * ng:]
    tm = x_ref.shape[0]

    heads = a_ref.shape[1] // HEAD_DIM
    for g, dil in enumerate(dils):
        if dil == 1:
            continue
        for r in range(dil):
            ls_ref[g, pl.ds(r, tm // dil, stride=dil), :] = l_refs[g][r]
            for h in range(heads):
                os_ref[g, h, pl.ds(r, tm // dil, stride=dil), :] = o_refs[g][r, :, h * HEAD_DIM:(h + 1) * HEAD_DIM]

    def group_out(g, h):
        if dils[g] == 1:
            return o_refs[g][0, :, h * HEAD_DIM:(h + 1) * HEAD_DIM]
        return os_ref[g, h]

    lse = [l_refs[g][0] if dils[g] == 1 else ls_ref[g] for g in range(ng)]
    m = functools.reduce(jnp.maximum, lse)
    e = [jnp.exp(l - m) for l in lse]
    tot = functools.reduce(lambda p, q: p + q, e)
    wts = [eg / tot for eg in e]
    for h in range(heads):
        terms = [wts[g][:, h:h + 1] * group_out(g, h) for g in range(ng)]
        a_ref[:, h * HEAD_DIM:(h + 1) * HEAD_DIM] = functools.reduce(lambda p, q: p + q, terms).astype(BF16)
    out_ref[...] = x_ref[...] + jnp.dot(a_ref[...], w_ref[...], preferred_element_type=F32)


def _merge_proj(outs, lses, w, x):
    n, d = x.shape
    width = w.shape[0]
    dils = tuple(o.shape[0] for o in outs)
    tm = min(SMALL_TOKEN_BLOCK, n)
    row = lambda i: (i, 0)
    grp = lambda i: (0, i, 0)
    return pl.pallas_call(
        functools.partial(_merge_proj_kernel, dils=dils),
        grid=(n // tm,),
        in_specs=[pl.BlockSpec((dil, tm // dil, width), grp) for dil in dils]
        + [pl.BlockSpec((dil, tm // dil, LANES), grp) for dil in dils]
        + [pl.BlockSpec((width, d), lambda i: (0, 0)), pl.BlockSpec((tm, d), row)],
        out_specs=pl.BlockSpec((tm, d), row),
        out_shape=jax.ShapeDtypeStruct((n, d), F32),
        scratch_shapes=[
            pltpu.VMEM((tm, width), BF16),
            pltpu.VMEM((len(dils), width // HEAD_DIM, tm, HEAD_DIM), F32),
            pltpu.VMEM((len(dils), tm, LANES), F32),
        ],
        compiler_params=_params("arbitrary"),
        name="merge_proj",
    )(*outs, *lses, w, x)


def _store_chunk_rows(ref, x):
    rows, d = x.shape
    nc = d // LANES
    for c in range(nc):
        ref[pl.ds(c, rows, stride=nc), :] = x[:, c * LANES:(c + 1) * LANES]


def _store_packed_rows(ref, x):
    rows, d = x.shape
    nw = d // (2 * LANES)
    for c in range(nw):
        lo = x[:, (2 * c) * LANES:(2 * c + 1) * LANES].astype(BF16).astype(F32)
        hi = x[:, (2 * c + 1) * LANES:(2 * c + 2) * LANES].astype(BF16).astype(F32)
        word = (pltpu.bitcast(lo, jnp.uint32) >> 16) | (pltpu.bitcast(hi, jnp.uint32) & jnp.uint32(0xFFFF0000))
        ref[pl.ds(c, rows, stride=nw), :] = word


def _load_packed_rows(ref, lead, rows, nw, c):
    word = ref[lead, pl.ds(c, rows, stride=nw), :]
    lo = pltpu.bitcast(word << 16, F32)
    hi = pltpu.bitcast(word & jnp.uint32(0xFFFF0000), F32)
    return lo.astype(BF16), hi.astype(BF16)


def _router_kernel(x_ref, gain_ref, wr_ref, h_ref, idx_ref, gate_ref):
    h = _rms_rows(x_ref[...], gain_ref[...])
    _store_packed_rows(h_ref, h)
    w = wr_ref[...]
    h_hi, w_hi = h.astype(BF16), w.astype(BF16)
    h_lo, w_lo = (h - h_hi.astype(F32)).astype(BF16), (w - w_hi.astype(F32)).astype(BF16)
    logits = (jnp.dot(h_hi, w_hi, preferred_element_type=F32)
              + (jnp.dot(h_hi, w_lo, preferred_element_type=F32) + jnp.dot(h_lo, w_hi, preferred_element_type=F32)))
    lane = lax.broadcasted_iota(jnp.int32, logits.shape, 1)
    lane_f = lane.astype(F32)
    logits = jnp.where(lane < N_EXPERTS, logits, -jnp.inf)
    m1 = jnp.max(logits, axis=-1, keepdims=True)
    i1 = jnp.min(jnp.where(logits == m1, lane_f, float(LANES)), axis=-1, keepdims=True)
    rest = jnp.where(lane_f == i1, -jnp.inf, logits)
    m2 = jnp.max(rest, axis=-1, keepdims=True)
    i2 = jnp.min(jnp.where(rest == m2, lane_f, float(LANES)), axis=-1, keepdims=True)
    e = jnp.exp(m2 - m1)
    tot = 1.0 + e
    idx_ref[...] = jnp.where(lane == 0, i1, jnp.where(lane == 1, i2, 0.0)).astype(jnp.int32)
    gate_ref[...] = jnp.where(lane == 0, 1.0 / tot, jnp.where(lane == 1, e / tot, 0.0))


def _router(x, gain, w_router):
    n, d = x.shape
    tm = min(SMALL_TOKEN_BLOCK, n)
    wr = jnp.zeros((d, LANES), F32).at[:, :N_EXPERTS].set(w_router.astype(F32))
    row = lambda i: (i, 0)
    return pl.pallas_call(
        _router_kernel,
        grid=(n // tm,),
        in_specs=[
            pl.BlockSpec((tm, d), row),
            pl.BlockSpec((1, d), lambda i: (0, 0)),
            pl.BlockSpec((d, LANES), lambda i: (0, 0)),
        ],
        out_specs=[pl.BlockSpec((tm * (d // (2 * LANES)), LANES), row), pl.BlockSpec((tm, LANES), row),
                   pl.BlockSpec((tm, LANES), row)],
        out_shape=[
            jax.ShapeDtypeStruct((n * (d // (2 * LANES)), LANES), jnp.uint32),
            jax.ShapeDtypeStruct((n, LANES), jnp.int32),
            jax.ShapeDtypeStruct((n, LANES), F32),
        ],
        compiler_params=_params("arbitrary"),
        name="router",
    )(x, gain.reshape(1, d), wr)


def _expert_kernel(bexp_ref, brows_ref, nused_ref, rtok_ref, h_hbm, wg_ref, wu_ref, wd_ref, y_ref,
                   xs_ref, xb_ref, acc_ref, sem):
    del bexp_ref
    i = pl.program_id(0)
    j = pl.program_id(1)
    tmb, d = xb_ref.shape
    nw = d // (2 * LANES)
    n_used = nused_ref[0]

    def row_copy(tok, slot, r):
        return pltpu.make_async_copy(h_hbm.at[pl.ds(pl.multiple_of(tok * nw, nw), nw), :],
                                     xs_ref.at[slot, pl.ds(pl.multiple_of(r * nw, nw), nw), :], sem.at[slot])

    def start_gather(blk, slot):
        def issue(r, carry):
            row_copy(rtok_ref[blk * tmb + r], slot, r).start()
            return carry
        lax.fori_loop(0, tmb, issue, 0, unroll=16)

    def wait_gather(slot):
        pltpu.make_async_copy(h_hbm.at[pl.ds(0, tmb * nw), :], xs_ref.at[slot], sem.at[slot]).wait()

    @pl.when(i < n_used)
    def _():
        slot = i % 2

        @pl.when(j == 0)
        def _():
            @pl.when(i == 0)
            def _():
                start_gather(0, 0)

            wait_gather(slot)

            @pl.when(i + 1 < n_used)
            def _():
                start_gather(i + 1, 1 - slot)

            for c in range(nw):
                lo, hi = _load_packed_rows(xs_ref, slot, tmb, nw, c)
                xb_ref[:, (2 * c) * LANES:(2 * c + 1) * LANES] = lo
                xb_ref[:, (2 * c + 1) * LANES:(2 * c + 2) * LANES] = hi
            acc_ref[...] = jnp.zeros_like(acc_ref)

        def swiglu_rows(rows):
            x = xb_ref[:rows]
            g = jnp.dot(x, wg_ref[0], preferred_element_type=F32)
            u = jnp.dot(x, wu_ref[0], preferred_element_type=F32)
            acc_ref[:rows] += jnp.dot(_silu_mul(g, u).astype(BF16), wd_ref[0], preferred_element_type=F32)

        quarter = tmb // 4
        quarters = jnp.maximum((brows_ref[i] + quarter - 1) // quarter, 1)
        for k in range(1, 5):
            @pl.when(quarters == k)
            def _(k=k):
                swiglu_rows(k * quarter)

    last = j == pl.num_programs(1) - 1

    @pl.when(last & (i < n_used))
    def _():
        _store_chunk_rows(y_ref, acc_ref[...])

    @pl.when(last & (i >= n_used))
    def _():
        y_ref[...] = jnp.zeros_like(y_ref)


def _expert_ffn(h_rows, block_expert, block_rows, n_used, row_tok, w_gate, w_up, w_down):
    d = w_gate.shape[1]
    f = w_gate.shape[2]
    tmb = MOE_BLOCK
    tf = MOE_F_BLOCK
    n_blocks = block_expert.shape[0]
    nf = f // tf
    assert f % tf == 0

    def fblk(i, j, nu):
        return jnp.where(i < nu[0], j, nf - 1)

    grid_spec = pltpu.PrefetchScalarGridSpec(
        num_scalar_prefetch=4,
        grid=(n_blocks, nf),
        in_specs=[
            pl.BlockSpec(memory_space=pl.ANY),
            pl.BlockSpec((1, d, tf), lambda i, j, be, br, nu, rt: (be[i], 0, fblk(i, j, nu))),
            pl.BlockSpec((1, d, tf), lambda i, j, be, br, nu, rt: (be[i], 0, fblk(i, j, nu))),
            pl.BlockSpec((1, tf, d), lambda i, j, be, br, nu, rt: (be[i], fblk(i, j, nu), 0)),
        ],
        out_specs=pl.BlockSpec((tmb * (d // LANES), LANES), lambda i, j, be, br, nu, rt: (i, 0)),
        scratch_shapes=[
            pltpu.VMEM((2, tmb * (d // (2 * LANES)), LANES), jnp.uint32),
            pltpu.VMEM((tmb, d), BF16),
            pltpu.VMEM((tmb, d), F32),
            pltpu.SemaphoreType.DMA((2,)),
        ],
    )
    return pl.pallas_call(
        _expert_kernel,
        grid_spec=grid_spec,
        out_shape=jax.ShapeDtypeStruct((n_blocks * tmb * (d // LANES), LANES), F32),
        compiler_params=_params("arbitrary", "arbitrary"),
        name="expert_ffn",
    )(block_expert, block_rows, n_used, row_tok, h_rows, w_gate, w_up, w_down)


def _combine_kernel(dest_ref, ys_hbm, x_ref, gate_ref, oa_ref, ob_ref, buf_ref, sem, *, first_blocks):
    i = pl.program_id(0)
    nsteps = pl.num_programs(0)
    tc, d = x_ref.shape
    nc = d // LANES
    pitch = buf_ref.shape[1] // (2 * tc)

    def row_copy(row, slot, dst):
        return pltpu.make_async_copy(ys_hbm.at[pl.ds(pl.multiple_of(row * nc, nc), nc), :],
                                     buf_ref.at[slot, pl.ds(pl.multiple_of(dst * pitch, 8), nc), :], sem.at[slot])

    def start_gather(step, slot):
        def issue(r, carry):
            row_copy(dest_ref[step * (2 * tc) + r], slot, r).start()
            return carry
        lax.fori_loop(0, 2 * tc, issue, 0, unroll=8)

    def wait_gather(slot):
        pltpu.make_async_copy(ys_hbm.at[pl.ds(0, 2 * tc * nc), :], buf_ref.at[slot, pl.ds(0, 2 * tc * nc), :],
                              sem.at[slot]).wait()

    slot = i % 2

    @pl.when(i == 0)
    def _():
        start_gather(0, 0)

    wait_gather(slot)

    @pl.when(i + 1 < nsteps)
    def _():
        start_gather(i + 1, 1 - slot)

    g0 = gate_ref[:, 0:1]
    g1 = gate_ref[:, 1:2]

    def emit(o_ref):
        for c in range(nc):
            sl = slice(c * LANES, (c + 1) * LANES)
            y0 = buf_ref[slot, pl.ds(c, tc, stride=pitch), :]
            y1 = buf_ref[slot, pl.ds(tc * pitch + c, tc, stride=pitch), :]
            o_ref[:, sl] = x_ref[:, sl] + (y0 * g0 + y1 * g1)
    _pick_rows(i, first_blocks, oa_ref, ob_ref, emit)


def _combine(x, ys, dest, gate, n_first):
    n, d = x.shape
    tc = min(COMBINE_BLOCK, n_first, n - n_first)
    assert n_first % tc == 0 and (n - n_first) % tc == 0
    first_blocks = n_first // tc
    nc = d // LANES
    pitch = nc if (nc // 8) % 2 else nc + 8
    dest = dest.reshape(n // tc, tc, TOP_K).transpose(0, 2, 1).reshape(-1)
    grid_spec = pltpu.PrefetchScalarGridSpec(
        num_scalar_prefetch=1,
        grid=(n // tc,),
        in_specs=[
            pl.BlockSpec(memory_space=pl.ANY),
            pl.BlockSpec((tc, d), lambda i, dr: (i, 0)),
            pl.BlockSpec((tc, LANES), lambda i, dr: (i, 0)),
        ],
        out_specs=_two_batch_specs(tc, d, first_blocks),
        scratch_shapes=[pltpu.VMEM((2, 2 * tc * pitch, LANES), F32), pltpu.SemaphoreType.DMA((2,))],
    )
    return pl.pallas_call(
        functools.partial(_combine_kernel, first_blocks=first_blocks),
        grid_spec=grid_spec,
        out_shape=[jax.ShapeDtypeStruct((n_first, d), F32), jax.ShapeDtypeStruct((n - n_first, d), F32)],
        compiler_params=_params("arbitrary"),
        name="moe_combine",
    )(dest, ys, x, gate)


def _routing_tables(idx, tmb):
    n = idx.shape[0]
    nk = n * TOP_K
    e_flat = idx[:, :TOP_K].reshape(-1)
    onehot = (e_flat[:, None] == jnp.arange(N_EXPERTS, dtype=jnp.int32)[None, :]).astype(jnp.int32)
    csum = jnp.cumsum(onehot, axis=0)
    rank = jnp.sum(onehot * csum, axis=1) - 1
    counts = csum[-1]
    padded = (counts + tmb - 1) // tmb * tmb
    pad_end = jnp.cumsum(padded)
    pad_start = pad_end - padded
    dest = (jnp.sum(onehot * pad_start[None, :], axis=1) + rank).astype(jnp.int32)
    n_blocks = -(-nk // tmb) + N_EXPERTS
    block_expert = jnp.minimum(
        jnp.searchsorted(pad_end, jnp.arange(n_blocks, dtype=jnp.int32) * tmb, side="right"), N_EXPERTS - 1
    ).astype(jnp.int32)
    n_used = (pad_end[-1] // tmb).astype(jnp.int32).reshape(1)
    block_lo = jnp.arange(n_blocks, dtype=jnp.int32) * tmb - pad_start[block_expert]
    block_rows = jnp.clip(counts[block_expert] - block_lo, 0, tmb).astype(jnp.int32)
    row_tok = jnp.zeros((n_blocks * tmb,), jnp.int32).at[dest].set(jnp.arange(nk, dtype=jnp.int32) // TOP_K)
    return dest, block_expert, block_rows, n_used, row_tok


def _rope_tables(seq_len):
    half = HEAD_DIM // 2
    inv_freq = ROPE_THETA ** (-jnp.arange(half, dtype=F32) / half)
    ang = jnp.arange(seq_len, dtype=F32)[:, None] * inv_freq[None, :]
    cos, sin = jnp.cos(ang), jnp.sin(ang)
    return jnp.concatenate([cos, cos], axis=-1), jnp.concatenate([-sin, sin], axis=-1)


def kernel(x_prompt, x_sample, norm_mix, norm_ffn, na_w_qkv, na_q_gain, na_k_gain, na_rpb, na_w_o,
           da_w_qkv, da_q_gain, da_k_gain, da_w_o, ffn_w_gate, ffn_w_up, ffn_w_down,
           moe_w_router, moe_w_gate, moe_w_up, moe_w_down):
    d = x_prompt.shape[-1]
    trunks = []
    off = 0
    for xin in (x_prompt, x_sample):
        b, s, _ = xin.shape
        trunks.append((off, b, s))
        off += b * s
    n = off
    xa, xb = x_prompt.reshape(-1, d), x_sample.reshape(-1, d)
    depth = norm_mix.shape[0]
    assert depth == 2, "layer 0 reads the two request batches directly and layer 1 writes them separately"
    x = None

    for layer in range(depth):
        lj = layer // 2
        if layer % 2 == 0:
            heads = na_w_qkv.shape[2] // (3 * HEAD_DIM)
            per = heads * HEAD_DIM // PROJ_COL_BLOCK
            hg = jnp.concatenate([
                jnp.broadcast_to(na_q_gain[lj], (per, HEAD_DIM)),
                jnp.broadcast_to(na_k_gain[lj], (per, HEAD_DIM)),
                jnp.ones((per, HEAD_DIM), F32),
            ]).reshape(3 * per, 1, HEAD_DIM)
            qkv = _qkv_project(xa, xb, norm_mix[layer], na_w_qkv[lj].astype(BF16), hg, 3 * per, 2 * per)
            tables = _na_bias_tables(na_rpb[lj])
            att = _na_attention(qkv, tables, trunks, heads)
            x = _proj_residual(att, na_w_o[lj].astype(BF16), xa, xb)
            ne, dm, fe = moe_w_gate.shape[1:]
            wide = (moe_w_gate[lj].reshape(ne * dm, fe), moe_w_up[lj].reshape(ne * dm, fe))
            steps = (n // min(FFN_TOKEN_BLOCK, n)) * (ffn_w_gate.shape[2] // FFN_F_BLOCK)
            shadow = _side_cast_fits(wide, SIDE_CAST_ROWS_WIDE, steps)
            x, *cast = _dense_ffn(x, norm_ffn[layer], ffn_w_gate[lj].astype(BF16), ffn_w_up[lj].astype(BF16),
                                  ffn_w_down[lj].astype(BF16), side_srcs=wide if shadow else (),
                                  side_rows=SIDE_CAST_ROWS_WIDE)
            moe_gate_bf, moe_up_bf = cast if shadow else [w.astype(BF16) for w in wide]
        else:
            groups = len(DIL_CONFIGS)
            heads = da_w_qkv.shape[2] // (groups * 3 * HEAD_DIM)
            dils = [dil for _, dil in DIL_CONFIGS]
            ones = jnp.ones((HEAD_DIM,), F32)
            hg = jnp.stack([t for g in range(groups) for t in (da_q_gain[lj, g], da_k_gain[lj, g], ones)])
            hg = hg.reshape(groups * 3, 1, HEAD_DIM)
            (o0, b0, s0), (o1, b1, s1) = trunks
            tq = min(QKV_DIL_TOKEN_BLOCK, n)
            assert s0 % tq == 0 and s1 % tq == 0 and o1 % tq == 0
            nb0, p0, p1 = o1 // tq, s0 // tq, s1 // tq
            pos_map = lambda i: jnp.where(i < nb0, i % p0, (i - nb0) % p1)
            smax = max(s0, s1)
            rope = [jnp.stack([t.reshape(smax // tq, tq // dil, dil, HEAD_DIM).transpose(0, 2, 1, 3)
                               .reshape(smax, HEAD_DIM) for dil in dils]) for t in _rope_tables(smax)]
            narrow = (moe_w_down[lj].reshape(ne * fe, dm),)
            shadow = _side_cast_fits(narrow, SIDE_CAST_ROWS_NARROW, (n // tq) * groups * 3)
            qkvs = list(_qkv_project_dilated(
                x, norm_mix[layer], da_w_qkv[lj].astype(BF16), hg, rope, pos_map, dils, heads, tq,
                side_srcs=narrow if shadow else (), side_rows=SIDE_CAST_ROWS_NARROW))
            moe_down_bf = qkvs.pop() if shadow else narrow[0].astype(BF16)
            outs, lses = [], []
            for g in range(groups):
                o_g, lse_g = _band_attention(qkvs[g], trunks, g)
                outs.append(o_g)
                lses.append(lse_g)
            x = _merge_proj(outs, lses, da_w_o[lj].astype(BF16), x)
            h, idx, gate = _router(x, norm_ffn[layer], moe_w_router[lj])
            dest, block_expert, block_rows, n_used, row_tok = _routing_tables(idx, MOE_BLOCK)
            ys = _expert_ffn(h, block_expert, block_rows, n_used, row_tok, moe_gate_bf.reshape(ne, dm, fe),
                             moe_up_bf.reshape(ne, dm, fe), moe_down_bf.reshape(ne, fe, dm))
            ya, yb = _combine(x, ys, dest, gate, xa.shape[0])

    return ya.reshape(x_prompt.shape), yb.reshape(x_sample.shape)
```

```python
import functools

import jax
import jax.numpy as jnp
from jax import lax
from jax.experimental import pallas as pl
from jax.experimental.pallas import tpu as pltpu

F32 = jnp.float32
BF16 = jnp.bfloat16

HEAD_DIM = 128
GRID_W = 64
NA_ROWS = 8
NA_COLS = 16
NA_GROUP = NA_ROWS // 2
NA_KEY_ROWS = 3 * NA_GROUP
NA_HEAD_BLOCK = 16
DIL_CONFIGS = ((128, 1), (512, 4), (2048, 16))
N_EXPERTS = 8
TOP_K = 2
ROPE_THETA = 10000.0
EPS = 1e-6
NEG_INF = -1e30
ATTN_SCALE = HEAD_DIM ** -0.5

LANES = 128
MXU_COLS = 256
VMEM_LIMIT = 52 * 1024 * 1024
FFN_VMEM_LIMIT = 57 * 1024 * 1024

TOKEN_BLOCK = 1024
PROJ_COL_BLOCK = 1024
SMALL_TOKEN_BLOCK = 512
QKV_DIL_TOKEN_BLOCK = 512
FFN_TOKEN_BLOCK = 1024
FFN_F_BLOCK = 256
BAND_Q_BLOCK = 512
BAND_Q_TILE = 128
MOE_BLOCK = 512
MOE_F_BLOCK = 1024
COMBINE_BLOCK = 512
ISSUE_UNROLL = 8
SIDE_CAST_ROWS_WIDE = 128
SIDE_CAST_ROWS_NARROW = 512


def _params(*sem, vmem_limit=VMEM_LIMIT):
    return pltpu.CompilerParams(dimension_semantics=sem, vmem_limit_bytes=vmem_limit)


def _rms_rows(x, gain):
    ms = jnp.mean(x * x, axis=-1, keepdims=True)
    return x * lax.rsqrt(ms + EPS) * gain


def _pick_rows(i, first_blocks, xa_ref, xb_ref, use):
    @pl.when(i < first_blocks)
    def _():
        use(xa_ref)

    @pl.when(i >= first_blocks)
    def _():
        use(xb_ref)


def _two_batch_specs(tm, d, first_blocks, **kw):
    return [pl.BlockSpec((tm, d), lambda i, *_: (jnp.minimum(i, first_blocks - 1), 0), **kw),
            pl.BlockSpec((tm, d), lambda i, *_: (jnp.maximum(i - first_blocks, 0), 0), **kw)]


def _qkv_kernel(xa_ref, xb_ref, gain_ref, w_ref, hg_ref, o_ref, h_ref, *, first_blocks, v_every, v_from):
    j = pl.program_id(1)

    @pl.when(j == 0)
    def _():
        def norm(x_ref):
            h_ref[...] = _rms_rows(x_ref[...], gain_ref[...]).astype(BF16)
        _pick_rows(pl.program_id(0), first_blocks, xa_ref, xb_ref, norm)

    is_qk = (j % v_every) < v_from
    for c in range(w_ref.shape[1] // MXU_COLS):
        acc = jnp.dot(h_ref[...], w_ref[:, c * MXU_COLS:(c + 1) * MXU_COLS], preferred_element_type=F32)
        for h in range(MXU_COLS // HEAD_DIM):
            seg = acc[:, h * HEAD_DIM:(h + 1) * HEAD_DIM]
            lo = c * MXU_COLS + h * HEAD_DIM
            o_ref[:, lo:lo + HEAD_DIM] = jnp.where(is_qk, _rms_rows(seg, hg_ref[0]), seg).astype(o_ref.dtype)


def _qkv_project(xa, xb, gain, w, head_gain, v_every, v_from):
    d = xa.shape[1]
    n = xa.shape[0] + xb.shape[0]
    n_out = w.shape[1]
    tm = min(TOKEN_BLOCK, xa.shape[0], xb.shape[0])
    assert xa.shape[0] % tm == 0 and xb.shape[0] % tm == 0
    first_blocks = xa.shape[0] // tm
    tn = n_out // head_gain.shape[0]
    return pl.pallas_call(
        functools.partial(_qkv_kernel, first_blocks=first_blocks, v_every=v_every, v_from=v_from),
        grid=(n // tm, n_out // tn),
        in_specs=_two_batch_specs(tm, d, first_blocks) + [
            pl.BlockSpec((1, d), lambda i, j: (0, 0)),
            pl.BlockSpec((d, tn), lambda i, j: (0, j)),
            pl.BlockSpec((1, 1, HEAD_DIM), lambda i, j: (j, 0, 0)),
        ],
        out_specs=pl.BlockSpec((tm, tn), lambda i, j: (i, j)),
        out_shape=jax.ShapeDtypeStruct((n, n_out), BF16),
        scratch_shapes=[pltpu.VMEM((tm, d), BF16)],
        compiler_params=_params("arbitrary", "arbitrary", vmem_limit=FFN_VMEM_LIMIT),
        name="qkv_project",
    )(xa, xb, gain.reshape(1, d), w, head_gain)


def _qkv_dil_kernel(x_ref, gain_ref, w_ref, hg_ref, cos_ref, sin_ref, *rest, dils, n_side):
    ng = len(dils)
    srcs, out_refs, dsts = rest[:n_side], rest[n_side:n_side + ng], rest[n_side + ng:2 * n_side + ng]
    h_ref, stage_ref, *side = rest[2 * n_side + ng:]
    j = pl.program_id(1)
    if n_side:
        _side_cast_step(pl.program_id(0) * pl.num_programs(1) + j, srcs, dsts, *side)
    tm, d = x_ref.shape
    tn = w_ref.shape[1]

    @pl.when(j == 0)
    def _():
        y = _rms_rows(x_ref[...], gain_ref[...])
        for c in range(d // LANES):
            stage_ref[c] = y[:, c * LANES:(c + 1) * LANES]
        for g, dil in enumerate(dils):
            rows = tm // dil
            for r in range(dil):
                for c in range(d // LANES):
                    h_ref[g, r * rows:(r + 1) * rows, c * LANES:(c + 1) * LANES] = (
                        stage_ref[c, pl.ds(r, rows, stride=dil), :].astype(BF16))

    group = j // 3
    which = j % 3
    acc = jnp.dot(h_ref[group], w_ref[...], preferred_element_type=F32)

    for g, (dil, o_ref) in enumerate(zip(dils, out_refs)):
        rows = tm // dil

        @pl.when((group == g) & (which < 2))
        def _(dil=dil, o_ref=o_ref, rows=rows):
            for h in range(tn // HEAD_DIM):
                sl = slice(h * HEAD_DIM, (h + 1) * HEAD_DIM)
                y = _rms_rows(acc[:, sl], hg_ref[0])
                y = (y * cos_ref[...] + pltpu.roll(y, HEAD_DIM // 2, 1) * sin_ref[...]).astype(o_ref.dtype)
                for r in range(dil):
                    o_ref[r, :, sl] = y[r * rows:(r + 1) * rows]

        @pl.when((group == g) & (which == 2))
        def _(dil=dil, o_ref=o_ref, rows=rows):
            v = acc.astype(o_ref.dtype)
            for r in range(dil):
                o_ref[r] = v[r * rows:(r + 1) * rows]


def _qkv_project_dilated(x, gain, w, head_gain, rope_tabs, pos_map, dils, heads, tm, side_srcs=(), side_rows=0):
    n, d = x.shape
    n_out = w.shape[1]
    width = heads * HEAD_DIM
    ng = len(dils)
    assert n_out == ng * 3 * width and head_gain.shape[0] == ng * 3
    side_in, side_out, side_shapes, side_scratch = (
        _side_cast_specs(side_srcs, side_rows, (n // tm) * ng * 3) if side_srcs else ([], [], [], []))

    def out_map(g):
        def index(i, j):
            return (jnp.clip(j - 3 * g, 0, 2), 0, i, 0)
        return index

    rope_spec = pl.BlockSpec((None, tm, HEAD_DIM), lambda i, j: (j // 3, pos_map(i), 0))
    return pl.pallas_call(
        functools.partial(_qkv_dil_kernel, dils=tuple(dils), n_side=len(side_srcs)),
        grid=(n // tm, ng * 3),
        in_specs=[
            pl.BlockSpec((tm, d), lambda i, j: (i, 0)),
            pl.BlockSpec((1, d), lambda i, j: (0, 0)),
            pl.BlockSpec((d, width), lambda i, j: (0, j)),
            pl.BlockSpec((1, 1, HEAD_DIM), lambda i, j: (j, 0, 0)),
            rope_spec,
            rope_spec,
        ] + side_in,
        out_specs=[pl.BlockSpec((None, dil, tm // dil, width), out_map(g)) for g, dil in enumerate(dils)] + side_out,
        out_shape=[jax.ShapeDtypeStruct((3, dil, n // dil, width), BF16) for dil in dils] + side_shapes,
        scratch_shapes=[pltpu.VMEM((ng, tm, d), BF16), pltpu.VMEM((d // LANES, tm, LANES), F32)] + side_scratch,
        compiler_params=_params("arbitrary", "arbitrary"),
        name="qkv_project_dilated",
    )(x, gain.reshape(1, d), w, head_gain, *rope_tabs, *side_srcs)


def _na_table_kernel(rpb_ref, o_ref):
    h = pl.program_id(0)
    w = GRID_W
    lane = lax.broadcasted_iota(jnp.int32, (w, 2 * w), 1)
    cq = lax.broadcasted_iota(jnp.int32, (w, 2 * w), 0)
    ck = lane % w
    upper = lane >= w
    cs = jnp.clip(cq - NA_COLS // 2, 0, w - NA_COLS)
    col_ok = (ck >= cs) & (ck < cs + NA_COLS)
    dc = ck - cq + (NA_COLS - 1)

    def pair_tile(dr):
        t = jnp.zeros((w, 2 * w), F32)
        for b in range(2 * NA_COLS - 1):
            lo = rpb_ref[h, dr + NA_ROWS - 1, b]
            hi = rpb_ref[h, dr + NA_ROWS, b]
            t = jnp.where(dc == b, jnp.where(upper, hi, lo), t)
        return t

    tiles = {}
    for qi in range(NA_GROUP):
        for kp in range(NA_KEY_ROWS // 2):
            dr = 2 * kp - NA_GROUP - qi
            if dr not in tiles:
                tiles[dr] = pair_tile(dr)
            for v in range(3):
                def row_ok(kj):
                    if v == 0:
                        return kj >= NA_GROUP
                    if v == 1:
                        return 0 <= kj - qi < NA_ROWS
                    return kj < NA_ROWS
                ok_lo, ok_hi = row_ok(2 * kp), row_ok(2 * kp + 1)
                if ok_lo and ok_hi:
                    ok = col_ok
                elif ok_lo:
                    ok = col_ok & jnp.logical_not(upper)
                elif ok_hi:
                    ok = col_ok & upper
                else:
                    ok = None
                tile = jnp.full((w, 2 * w), NEG_INF, F32) if ok is None else jnp.where(ok, tiles[dr], NEG_INF)
                o_ref[v, 0, qi * w:(qi + 1) * w, kp * 2 * w:(kp + 1) * 2 * w] = tile


def _na_bias_tables(rpb):
    assert NA_GROUP == NA_ROWS // 2 and NA_KEY_ROWS % 2 == 0 and 2 * GRID_W == LANES
    heads = rpb.shape[0]
    shape = (3, heads, NA_GROUP * GRID_W, NA_KEY_ROWS * GRID_W)
    return pl.pallas_call(
        _na_table_kernel,
        grid=(heads,),
        in_specs=[pl.BlockSpec(memory_space=pltpu.SMEM)],
        out_specs=pl.BlockSpec((3, 1) + shape[2:], lambda h: (0, h, 0, 0)),
        out_shape=jax.ShapeDtypeStruct(shape, F32),
        compiler_params=_params("arbitrary"),
        name="na_bias_table",
    )(rpb.astype(F32))


def _na_kernel(q_ref, kp_ref, k_ref, kn_ref, vp_ref, v_ref, vn_ref, tab_ref, o_ref):
    gq = q_ref.shape[0]
    nt = (((1,), (1,)), ((), ()))
    for h in range(q_ref.shape[1] // HEAD_DIM):
        sl = slice(h * HEAD_DIM, (h + 1) * HEAD_DIM)
        q = q_ref[:, sl]
        s = jnp.concatenate(
            [lax.dot_general(q, kr[:, sl], nt, preferred_element_type=F32) for kr in (kp_ref, k_ref, kn_ref)],
            axis=1)
        s = s * ATTN_SCALE + tab_ref[0, h]
        m = jnp.max(s, axis=-1, keepdims=True)
        p = jnp.exp(s - m)
        den = jnp.sum(p, axis=-1, keepdims=True)
        pb = p.astype(BF16)
        o = (jnp.dot(pb[:, :gq], vp_ref[:, sl], preferred_element_type=F32)
             + jnp.dot(pb[:, gq:2 * gq], v_ref[:, sl], preferred_element_type=F32)
             + jnp.dot(pb[:, 2 * gq:], vn_ref[:, sl], preferred_element_type=F32))
        o_ref[:, sl] = (o / den).astype(o_ref.dtype)


def _two_batch_position(t, first_blocks, per_seq_a, per_seq_b):
    in_a = t < first_blocks
    per_seq = jnp.where(in_a, per_seq_a, per_seq_b)
    return jnp.where(in_a, t, t - first_blocks) % per_seq, per_seq


def _na_attention(qkv, tables, trunks, heads):
    n = qkv.shape[0]
    gq = NA_GROUP * GRID_W
    hb = NA_HEAD_BLOCK
    hblocks = heads // hb
    (off_a, b_a, s_a), (off_b, b_b, s_b) = trunks
    for (_, _, seq) in trunks:
        assert seq % gq == 0 and seq // GRID_W >= NA_ROWS
    assert heads % hb == 0 and off_a == 0 and off_b == b_a * s_a
    first_groups, gps_a, gps_b = off_b // gq, s_a // gq, s_b // gq

    def place(g):
        pos, per_seq = _two_batch_position(g, first_groups, gps_a, gps_b)
        return pos == 0, pos == per_seq - 1

    def variant(g):
        first, last = place(g)
        return jnp.where(first, 0, jnp.where(last, 2, 1))

    def rows_of(which, shift):
        def index(h, g):
            first, last = place(g)
            gg = g + jnp.where((shift < 0) & first, 0, jnp.where((shift > 0) & last, 0, shift))
            return (gg, which * hblocks + h)
        return index

    blk = lambda index: pl.BlockSpec((gq, hb * HEAD_DIM), index)
    return pl.pallas_call(
        _na_kernel,
        grid=(hblocks, n // gq),
        in_specs=[
            blk(rows_of(0, 0)),
            blk(rows_of(1, -1)), blk(rows_of(1, 0)), blk(rows_of(1, 1)),
            blk(rows_of(2, -1)), blk(rows_of(2, 0)), blk(rows_of(2, 1)),
            pl.BlockSpec((1, hb, gq, NA_KEY_ROWS * GRID_W), lambda h, g: (variant(g), h, 0, 0)),
        ],
        out_specs=blk(rows_of(0, 0)),
        out_shape=jax.ShapeDtypeStruct((n, heads * HEAD_DIM), BF16),
        compiler_params=_params("arbitrary", "arbitrary"),
        name="na_attention",
    )(qkv, qkv, qkv, qkv, qkv, qkv, qkv, tables)


def _proj_residual_kernel(a_ref, w_ref, xa_ref, xb_ref, o_ref, *, first_blocks):
    y = jnp.dot(a_ref[...], w_ref[...], preferred_element_type=F32)

    def add(x_ref):
        o_ref[...] = x_ref[...] + y
    _pick_rows(pl.program_id(1), first_blocks, xa_ref, xb_ref, add)


def _proj_residual(a, w, xa, xb):
    n, k = a.shape
    d = w.shape[1]
    tm = min(TOKEN_BLOCK, xa.shape[0], xb.shape[0])
    tn = min(PROJ_COL_BLOCK, d)
    assert xa.shape[0] % tm == 0 and xb.shape[0] % tm == 0 and xa.shape[0] + xb.shape[0] == n
    first_blocks = xa.shape[0] // tm
    return pl.pallas_call(
        functools.partial(_proj_residual_kernel, first_blocks=first_blocks),
        grid=(d // tn, n // tm),
        in_specs=[
            pl.BlockSpec((tm, k), lambda j, i: (i, 0)),
            pl.BlockSpec((k, tn), lambda j, i: (0, j)),
            pl.BlockSpec((tm, tn), lambda j, i: (jnp.minimum(i, first_blocks - 1), j)),
            pl.BlockSpec((tm, tn), lambda j, i: (jnp.maximum(i - first_blocks, 0), j)),
        ],
        out_specs=pl.BlockSpec((tm, tn), lambda j, i: (i, j)),
        out_shape=jax.ShapeDtypeStruct((n, d), F32),
        compiler_params=_params("arbitrary", "arbitrary"),
        name="proj_residual",
    )(a, w, xa, xb)


def _side_cast_step(step, srcs, dsts, in_buf, out_buf, in_sem, out_sem):
    rows = in_buf.shape[1]
    per = srcs[0].shape[0] // rows
    nb = per * len(srcs)
    slot = step % 2

    def block_copy(refs, b, make):
        for k, ref in enumerate(refs):
            @pl.when(b // per == k)
            def _(k=k, ref=ref):
                make(ref.at[pl.ds(pl.multiple_of((b - k * per) * rows, rows), rows), :]).start()

    def in_copy(s):
        return lambda hbm: pltpu.make_async_copy(hbm, in_buf.at[s], in_sem.at[s])

    def out_copy(s):
        return lambda hbm: pltpu.make_async_copy(out_buf.at[s], hbm, out_sem.at[s])

    @pl.when(step == 0)
    def _():
        block_copy(srcs, 0, in_copy(0))

    @pl.when((step >= 2) & (step - 2 < nb))
    def _():
        out_copy(slot)(dsts[0].at[pl.ds(0, rows), :]).wait()

    @pl.when(step < nb)
    def _():
        in_copy(slot)(srcs[0].at[pl.ds(0, rows), :]).wait()

        @pl.when(step + 1 < nb)
        def _():
            block_copy(srcs, step + 1, in_copy(1 - slot))

        out_buf[slot] = in_buf[slot].astype(BF16)
        block_copy(dsts, step, out_copy(slot))


def _side_cast_fits(srcs, rows, n_steps):
    r = srcs[0].shape[0]
    return r % rows == 0 and (r // rows) * len(srcs) + 2 <= n_steps


def _side_cast_specs(srcs, rows, n_steps):
    r, c = srcs[0].shape
    assert all(a.shape == (r, c) for a in srcs) and _side_cast_fits(srcs, rows, n_steps)
    any_spec = pl.BlockSpec(memory_space=pl.ANY)
    return ([any_spec] * len(srcs), [any_spec] * len(srcs),
            [jax.ShapeDtypeStruct((r, c), BF16) for _ in srcs],
            [pltpu.VMEM((2, rows, c), F32), pltpu.VMEM((2, rows, c), BF16),
             pltpu.SemaphoreType.DMA((2,)), pltpu.SemaphoreType.DMA((2,))])


def _silu_mul(g, u):
    return g * (1.0 / (1.0 + jnp.exp(-g))) * u


def _ffn_kernel(x_ref, gain_ref, wg_ref, wu_ref, wd_ref, *rest, n_side):
    srcs, (o_ref, *dsts), (h_ref, acc_ref, *side) = rest[:n_side], rest[n_side:2 * n_side + 1], rest[2 * n_side + 1:]
    j = pl.program_id(1)
    if n_side:
        _side_cast_step(pl.program_id(0) * pl.num_programs(1) + j, srcs, dsts, *side)

    @pl.when(j == 0)
    def _():
        h_ref[...] = _rms_rows(x_ref[...], gain_ref[...]).astype(BF16)
        acc_ref[...] = jnp.zeros_like(acc_ref)

    h = h_ref[...]
    g = jnp.dot(h, wg_ref[...], preferred_element_type=F32)
    u = jnp.dot(h, wu_ref[...], preferred_element_type=F32)
    acc_ref[...] += jnp.dot(_silu_mul(g, u).astype(BF16), wd_ref[...], preferred_element_type=F32)

    @pl.when(j == pl.num_programs(1) - 1)
    def _():
        o_ref[...] = x_ref[...] + acc_ref[...]


def _dense_ffn(x, gain, w_gate, w_up, w_down, side_srcs=(), side_rows=0):
    n, d = x.shape
    f = w_gate.shape[1]
    tm = min(FFN_TOKEN_BLOCK, n)
    tf = FFN_F_BLOCK
    assert f % tf == 0
    grid = (n // tm, f // tf)
    side_in, side_out, side_shapes, side_scratch = (
        _side_cast_specs(side_srcs, side_rows, grid[0] * grid[1]) if side_srcs else ([], [], [], []))
    return pl.pallas_call(
        functools.partial(_ffn_kernel, n_side=len(side_srcs)),
        grid=grid,
        in_specs=[
            pl.BlockSpec((tm, d), lambda i, j: (i, 0), pipeline_mode=pl.Buffered(1)),
            pl.BlockSpec((1, d), lambda i, j: (0, 0)),
            pl.BlockSpec((d, tf), lambda i, j: (0, j)),
            pl.BlockSpec((d, tf), lambda i, j: (0, j)),
            pl.BlockSpec((tf, d), lambda i, j: (j, 0)),
        ] + side_in,
        out_specs=[pl.BlockSpec((tm, d), lambda i, j: (i, 0))] + side_out,
        out_shape=[jax.ShapeDtypeStruct((n, d), F32)] + side_shapes,
        scratch_shapes=[pltpu.VMEM((tm, d), BF16), pltpu.VMEM((tm, d), F32)] + side_scratch,
        compiler_params=_params("arbitrary", "arbitrary", vmem_limit=FFN_VMEM_LIMIT),
        name="dense_ffn",
    )(x, gain.reshape(1, d), w_gate, w_up, w_down, *side_srcs)


def _band_kernel(q_ref, kp_ref, k_ref, kn_ref, vp_ref, v_ref, vn_ref, o_ref, lse_ref, kwin_ref, vwin_ref, *,
                 first_blocks, per_seq_a, per_seq_b, tq, half):
    tqb = q_ref.shape[0]
    kw = tq + 2 * half
    pos, per_seq = _two_batch_position(pl.program_id(1), first_blocks, per_seq_a, per_seq_b)
    q0 = pos * tqb
    length = per_seq * tqb
    for win, parts in ((kwin_ref, (kp_ref, k_ref, kn_ref)), (vwin_ref, (vp_ref, v_ref, vn_ref))):
        win[:half] = parts[0][...]
        win[half:half + tqb] = parts[1][...]
        win[half + tqb:] = parts[2][...]
    lane = lax.broadcasted_iota(jnp.int32, (tq, LANES), 1)

    def body(t, carry):
        r0 = pl.multiple_of(t * tq, tq)
        qpos = q0 + r0 + lax.broadcasted_iota(jnp.int32, (tq, kw), 0)
        kpos = q0 + r0 - half + lax.broadcasted_iota(jnp.int32, (tq, kw), 1)
        ok = (jnp.abs(qpos - kpos) <= half) & (kpos >= 0) & (kpos < length)
        lse_tile = jnp.zeros((tq, LANES), F32)
        for h in range(q_ref.shape[1] // HEAD_DIM):
            sl = slice(h * HEAD_DIM, (h + 1) * HEAD_DIM)
            q = q_ref[pl.ds(r0, tq), sl]
            k = kwin_ref[pl.ds(r0, kw), sl]
            v = vwin_ref[pl.ds(r0, kw), sl]
            s = lax.dot_general(q, k, (((1,), (1,)), ((), ())), preferred_element_type=F32) * ATTN_SCALE
            s = jnp.where(ok, s, NEG_INF)
            m = jnp.max(s, axis=-1, keepdims=True)
            p = jnp.exp(s - m)
            den = jnp.sum(p, axis=-1, keepdims=True)
            o = jnp.dot(p.astype(BF16), v, preferred_element_type=F32) / den
            o_ref[pl.ds(r0, tq), sl] = o
            lse_tile = jnp.where(lane == h, m + jnp.log(den), lse_tile)
        lse_ref[pl.ds(r0, tq), :] = lse_tile
        return carry

    lax.fori_loop(0, tqb // tq, body, 0)


def _band_attention(qkv, trunks, group):
    window, dil = DIL_CONFIGS[group]
    half = window // (2 * dil)
    _, _, rows, width = qkv.shape
    (off_a, b_a, s_a), (off_b, b_b, s_b) = trunks
    len_a, len_b = s_a // dil, s_b // dil
    assert s_a % dil == 0 and s_b % dil == 0 and qkv.shape[1] == dil and off_a == 0 and off_b == b_a * s_a
    tqb = min(BAND_Q_BLOCK, len_a, len_b)
    tq = min(BAND_Q_TILE, tqb)
    assert len_a % tqb == 0 and len_b % tqb == 0 and tqb % tq == 0 and tqb % half == 0
    hpb = tqb // half
    first_blocks, per_seq_a, per_seq_b = off_b // dil // tqb, len_a // tqb, len_b // tqb

    def halo(which, side):
        def index(r, t):
            pos, per_seq = _two_batch_position(t, first_blocks, per_seq_a, per_seq_b)
            edge = (pos == per_seq - 1) if side else (pos == 0)
            blk = t * hpb + (jnp.where(edge, hpb - 1, hpb) if side else jnp.where(edge, 0, -1))
            return (which, r, blk, 0)
        return index

    mspec = lambda which: pl.BlockSpec((None, None, tqb, width), lambda r, t: (which, r, t, 0))
    hspec = lambda which, side: pl.BlockSpec((None, None, half, width), halo(which, side))
    return pl.pallas_call(
        functools.partial(_band_kernel, first_blocks=first_blocks, per_seq_a=per_seq_a, per_seq_b=per_seq_b,
                          tq=tq, half=half),
        grid=(dil, rows // tqb),
        in_specs=[
            mspec(0),
            hspec(1, 0), mspec(1), hspec(1, 1),
            hspec(2, 0), mspec(2), hspec(2, 1),
        ],
        out_specs=[
            pl.BlockSpec((None, tqb, width), lambda r, t: (r, t, 0)),
            pl.BlockSpec((None, tqb, LANES), lambda r, t: (r, t, 0)),
        ],
        out_shape=[jax.ShapeDtypeStruct((dil, rows, width), F32), jax.ShapeDtypeStruct((dil, rows, LANES), F32)],
        scratch_shapes=[pltpu.VMEM((tqb + 2 * half, width), BF16), pltpu.VMEM((tqb + 2 * half, width), BF16)],
        compiler_params=_params("arbitrary", "arbitrary"),
        name="band_attention",
    )(qkv, qkv, qkv, qkv, qkv, qkv, qkv)


def _merge_proj_kernel(*refs, dils):
    ng = len(dils)
    o_refs, l_refs = refs[:ng], refs[ng:2 * ng]
    w_ref, x_ref, out_ref, a_ref, os_ref, ls_ref = refs[2 * ng:]
    tm = x_ref.shape[0]

    heads = a_ref.shape[1] // HEAD_DIM
    for g, dil in enumerate(dils):
        if dil == 1:
            continue
        for r in range(dil):
            ls_ref[g, pl.ds(r, tm // dil, stride=dil), :] = l_refs[g][r]
            for h in range(heads):
                os_ref[g, h, pl.ds(r, tm // dil, stride=dil), :] = o_refs[g][r, :, h * HEAD_DIM:(h + 1) * HEAD_DIM]

    def group_out(g, h):
        if dils[g] == 1:
            return o_refs[g][0, :, h * HEAD_DIM:(h + 1) * HEAD_DIM]
        return os_ref[g, h]

    lse = [l_refs[g][0] if dils[g] == 1 else ls_ref[g] for g in range(ng)]
    m = functools.reduce(jnp.maximum, lse)
    e = [jnp.exp(l - m) for l in lse]
    tot = functools.reduce(lambda p, q: p + q, e)
    wts = [eg / tot for eg in e]
    for h in range(heads):
        terms = [wts[g][:, h:h + 1] * group_out(g, h) for g in range(ng)]
        a_ref[:, h * HEAD_DIM:(h + 1) * HEAD_DIM] = functools.reduce(lambda p, q: p + q, terms).astype(BF16)
    out_ref[...] = x_ref[...] + jnp.dot(a_ref[...], w_ref[...], preferred_element_type=F32)


def _merge_proj(outs, lses, w, x):
    n, d = x.shape
    width = w.shape[0]
    dils = tuple(o.shape[0] for o in outs)
    tm = min(SMALL_TOKEN_BLOCK, n)
    row = lambda i: (i, 0)
    grp = lambda i: (0, i, 0)
    return pl.pallas_call(
        functools.partial(_merge_proj_kernel, dils=dils),
        grid=(n // tm,),
        in_specs=[pl.BlockSpec((dil, tm // dil, width), grp) for dil in dils]
        + [pl.BlockSpec((dil, tm // dil, LANES), grp) for dil in dils]
        + [pl.BlockSpec((width, d), lambda i: (0, 0)), pl.BlockSpec((tm, d), row)],
        out_specs=pl.BlockSpec((tm, d), row),
        out_shape=jax.ShapeDtypeStruct((n, d), F32),
        scratch_shapes=[
            pltpu.VMEM((tm, width), BF16),
            pltpu.VMEM((len(dils), width // HEAD_DIM, tm, HEAD_DIM), F32),
            pltpu.VMEM((len(dils), tm, LANES), F32),
        ],
        compiler_params=_params("arbitrary"),
        name="merge_proj",
    )(*outs, *lses, w, x)


def _store_chunk_rows(ref, x):
    rows, d = x.shape
    nc = d // LANES
    for c in range(nc):
        ref[pl.ds(c, rows, stride=nc), :] = x[:, c * LANES:(c + 1) * LANES]


def _store_packed_rows(ref, x):
    rows, d = x.shape
    nw = d // (2 * LANES)
    for c in range(nw):
        lo = x[:, (2 * c) * LANES:(2 * c + 1) * LANES].astype(BF16).astype(F32)
        hi = x[:, (2 * c + 1) * LANES:(2 * c + 2) * LANES].astype(BF16).astype(F32)
        word = (pltpu.bitcast(lo, jnp.uint32) >> 16) | (pltpu.bitcast(hi, jnp.uint32) & jnp.uint32(0xFFFF0000))
        ref[pl.ds(c, rows, stride=nw), :] = word


def _load_packed_rows(ref, lead, rows, nw, c):
    word = ref[lead, pl.ds(c, rows, stride=nw), :]
    lo = pltpu.bitcast(word << 16, F32)
    hi = pltpu.bitcast(word & jnp.uint32(0xFFFF0000), F32)
    return lo.astype(BF16), hi.astype(BF16)


def _router_kernel(x_ref, gain_ref, wr_ref, h_ref, idx_ref, gate_ref):
    h = _rms_rows(x_ref[...], gain_ref[...])
    _store_packed_rows(h_ref, h)
    w = wr_ref[...]
    h_hi, w_hi = h.astype(BF16), w.astype(BF16)
    h_lo, w_lo = (h - h_hi.astype(F32)).astype(BF16), (w - w_hi.astype(F32)).astype(BF16)
    logits = (jnp.dot(h_hi, w_hi, preferred_element_type=F32)
              + (jnp.dot(h_hi, w_lo, preferred_element_type=F32) + jnp.dot(h_lo, w_hi, preferred_element_type=F32)))
    lane = lax.broadcasted_iota(jnp.int32, logits.shape, 1)
    lane_f = lane.astype(F32)
    logits = jnp.where(lane < N_EXPERTS, logits, -jnp.inf)
    m1 = jnp.max(logits, axis=-1, keepdims=True)
    i1 = jnp.min(jnp.where(logits == m1, lane_f, float(LANES)), axis=-1, keepdims=True)
    rest = jnp.where(lane_f == i1, -jnp.inf, logits)
    m2 = jnp.max(rest, axis=-1, keepdims=True)
    i2 = jnp.min(jnp.where(rest == m2, lane_f, float(LANES)), axis=-1, keepdims=True)
    e = jnp.exp(m2 - m1)
    tot = 1.0 + e
    idx_ref[...] = jnp.where(lane == 0, i1, jnp.where(lane == 1, i2, 0.0)).astype(jnp.int32)
    gate_ref[...] = jnp.where(lane == 0, 1.0 / tot, jnp.where(lane == 1, e / tot, 0.0))


def _router(x, gain, w_router):
    n, d = x.shape
    tm = min(SMALL_TOKEN_BLOCK, n)
    wr = jnp.zeros((d, LANES), F32).at[:, :N_EXPERTS].set(w_router.astype(F32))
    row = lambda i: (i, 0)
    return pl.pallas_call(
        _router_kernel,
        grid=(n // tm,),
        in_specs=[
            pl.BlockSpec((tm, d), row),
            pl.BlockSpec((1, d), lambda i: (0, 0)),
            pl.BlockSpec((d, LANES), lambda i: (0, 0)),
        ],
        out_specs=[pl.BlockSpec((tm * (d // (2 * LANES)), LANES), row), pl.BlockSpec((tm, LANES), row),
                   pl.BlockSpec((tm, LANES), row)],
        out_shape=[
            jax.ShapeDtypeStruct((n * (d // (2 * LANES)), LANES), jnp.uint32),
            jax.ShapeDtypeStruct((n, LANES), jnp.int32),
            jax.ShapeDtypeStruct((n, LANES), F32),
        ],
        compiler_params=_params("arbitrary"),
        name="router",
    )(x, gain.reshape(1, d), wr)


def _expert_kernel(bexp_ref, brows_ref, nused_ref, rtok_ref, h_hbm, wg_ref, wu_ref, wd_ref, y_ref,
                   xs_ref, xb_ref, acc_ref, sem):
    del bexp_ref
    i = pl.program_id(0)
    j = pl.program_id(1)
    tmb, d = xb_ref.shape
    nw = d // (2 * LANES)
    n_used = nused_ref[0]

    def row_copy(tok, slot, r):
        return pltpu.make_async_copy(h_hbm.at[pl.ds(pl.multiple_of(tok * nw, nw), nw), :],
                                     xs_ref.at[slot, pl.ds(pl.multiple_of(r * nw, nw), nw), :], sem.at[slot])

    def start_gather(blk, slot):
        def issue(r, carry):
            row_copy(rtok_ref[blk * tmb + r], slot, r).start()
            return carry
        lax.fori_loop(0, tmb, issue, 0, unroll=16)

    def wait_gather(slot):
        pltpu.make_async_copy(h_hbm.at[pl.ds(0, tmb * nw), :], xs_ref.at[slot], sem.at[slot]).wait()

    @pl.when(i < n_used)
    def _():
        slot = i % 2

        @pl.when(j == 0)
        def _():
            @pl.when(i == 0)
            def _():
                start_gather(0, 0)

            wait_gather(slot)

            @pl.when(i + 1 < n_used)
            def _():
                start_gather(i + 1, 1 - slot)

            for c in range(nw):
                lo, hi = _load_packed_rows(xs_ref, slot, tmb, nw, c)
                xb_ref[:, (2 * c) * LANES:(2 * c + 1) * LANES] = lo
                xb_ref[:, (2 * c + 1) * LANES:(2 * c + 2) * LANES] = hi
            acc_ref[...] = jnp.zeros_like(acc_ref)

        def swiglu_rows(rows):
            x = xb_ref[:rows]
            g = jnp.dot(x, wg_ref[0], preferred_element_type=F32)
            u = jnp.dot(x, wu_ref[0], preferred_element_type=F32)
            acc_ref[:rows] += jnp.dot(_silu_mul(g, u).astype(BF16), wd_ref[0], preferred_element_type=F32)

        half_full = brows_ref[i] <= tmb // 2

        @pl.when(half_full)
        def _():
            swiglu_rows(tmb // 2)

        @pl.when(jnp.logical_not(half_full))
        def _():
            swiglu_rows(tmb)

    last = j == pl.num_programs(1) - 1

    @pl.when(last & (i < n_used))
    def _():
        _store_chunk_rows(y_ref, acc_ref[...])

    @pl.when(last & (i >= n_used))
    def _():
        y_ref[...] = jnp.zeros_like(y_ref)


def _expert_ffn(h_rows, block_expert, block_rows, n_used, row_tok, w_gate, w_up, w_down):
    d = w_gate.shape[1]
    f = w_gate.shape[2]
    tmb = MOE_BLOCK
    tf = MOE_F_BLOCK
    n_blocks = block_expert.shape[0]
    nf = f // tf
    assert f % tf == 0

    def fblk(i, j, nu):
        return jnp.where(i < nu[0], j, nf - 1)

    grid_spec = pltpu.PrefetchScalarGridSpec(
        num_scalar_prefetch=4,
        grid=(n_blocks, nf),
        in_specs=[
            pl.BlockSpec(memory_space=pl.ANY),
            pl.BlockSpec((1, d, tf), lambda i, j, be, br, nu, rt: (be[i], 0, fblk(i, j, nu))),
            pl.BlockSpec((1, d, tf), lambda i, j, be, br, nu, rt: (be[i], 0, fblk(i, j, nu))),
            pl.BlockSpec((1, tf, d), lambda i, j, be, br, nu, rt: (be[i], fblk(i, j, nu), 0)),
        ],
        out_specs=pl.BlockSpec((tmb * (d // LANES), LANES), lambda i, j, be, br, nu, rt: (i, 0)),
        scratch_shapes=[
            pltpu.VMEM((2, tmb * (d // (2 * LANES)), LANES), jnp.uint32),
            pltpu.VMEM((tmb, d), BF16),
            pltpu.VMEM((tmb, d), F32),
            pltpu.SemaphoreType.DMA((2,)),
        ],
    )
    return pl.pallas_call(
        _expert_kernel,
        grid_spec=grid_spec,
        out_shape=jax.ShapeDtypeStruct((n_blocks * tmb * (d // LANES), LANES), F32),
        compiler_params=_params("arbitrary", "arbitrary"),
        name="expert_ffn",
    )(block_expert, block_rows, n_used, row_tok, h_rows, w_gate, w_up, w_down)


def _combine_kernel(dest_ref, ys_hbm, x_ref, gate_ref, oa_ref, ob_ref, buf_ref, sem, *, first_blocks):
    i = pl.program_id(0)
    nsteps = pl.num_programs(0)
    tc, d = x_ref.shape
    nc = d // LANES
    pitch = buf_ref.shape[1] // (2 * tc)

    def row_copy(row, slot, dst):
        return pltpu.make_async_copy(ys_hbm.at[pl.ds(pl.multiple_of(row * nc, nc), nc), :],
                                     buf_ref.at[slot, pl.ds(pl.multiple_of(dst * pitch, 8), nc), :], sem.at[slot])

    def start_gather(step, slot):
        def issue(g, carry):
            for k in range(ISSUE_UNROLL):
                r = g * ISSUE_UNROLL + k
                row_copy(dest_ref[step * (2 * tc) + r], slot, r).start(priority=k % 2)
            return carry
        lax.fori_loop(0, 2 * tc // ISSUE_UNROLL, issue, 0)

    def wait_gather(slot):
        pltpu.make_async_copy(ys_hbm.at[pl.ds(0, 2 * tc * nc), :], buf_ref.at[slot, pl.ds(0, 2 * tc * nc), :],
                              sem.at[slot]).wait()

    slot = i % 2

    @pl.when(i == 0)
    def _():
        start_gather(0, 0)

    wait_gather(slot)

    @pl.when(i + 1 < nsteps)
    def _():
        start_gather(i + 1, 1 - slot)

    g0 = gate_ref[:, 0:1]
    g1 = gate_ref[:, 1:2]

    def emit(o_ref):
        for c in range(nc):
            sl = slice(c * LANES, (c + 1) * LANES)
            y0 = buf_ref[slot, pl.ds(c, tc, stride=pitch), :]
            y1 = buf_ref[slot, pl.ds(tc * pitch + c, tc, stride=pitch), :]
            o_ref[:, sl] = x_ref[:, sl] + (y0 * g0 + y1 * g1)
    _pick_rows(i, first_blocks, oa_ref, ob_ref, emit)


def _combine(x, ys, dest, gate, n_first):
    n, d = x.shape
    tc = min(COMBINE_BLOCK, n_first, n - n_first)
    assert n_first % tc == 0 and (n - n_first) % tc == 0
    first_blocks = n_first // tc
    nc = d // LANES
    pitch = nc if (nc // 8) % 2 else nc + 8
    dest = dest.reshape(n // tc, tc, TOP_K).transpose(0, 2, 1).reshape(-1)
    grid_spec = pltpu.PrefetchScalarGridSpec(
        num_scalar_prefetch=1,
        grid=(n // tc,),
        in_specs=[
            pl.BlockSpec(memory_space=pl.ANY),
            pl.BlockSpec((tc, d), lambda i, dr: (i, 0)),
            pl.BlockSpec((tc, LANES), lambda i, dr: (i, 0)),
        ],
        out_specs=_two_batch_specs(tc, d, first_blocks),
        scratch_shapes=[pltpu.VMEM((2, 2 * tc * pitch, LANES), F32), pltpu.SemaphoreType.DMA((2,))],
    )
    return pl.pallas_call(
        functools.partial(_combine_kernel, first_blocks=first_blocks),
        grid_spec=grid_spec,
        out_shape=[jax.ShapeDtypeStruct((n_first, d), F32), jax.ShapeDtypeStruct((n - n_first, d), F32)],
        compiler_params=_params("arbitrary"),
        name="moe_combine",
    )(dest, ys, x, gate)


def _routing_tables(idx, tmb):
    n = idx.shape[0]
    nk = n * TOP_K
    e_flat = idx[:, :TOP_K].reshape(-1)
    onehot = (e_flat[:, None] == jnp.arange(N_EXPERTS, dtype=jnp.int32)[None, :]).astype(jnp.int32)
    csum = jnp.cumsum(onehot, axis=0)
    rank = jnp.sum(onehot * csum, axis=1) - 1
    counts = csum[-1]
    padded = (counts + tmb - 1) // tmb * tmb
    pad_end = jnp.cumsum(padded)
    pad_start = pad_end - padded
    dest = (jnp.sum(onehot * pad_start[None, :], axis=1) + rank).astype(jnp.int32)
    n_blocks = -(-nk // tmb) + N_EXPERTS
    block_expert = jnp.minimum(
        jnp.searchsorted(pad_end, jnp.arange(n_blocks, dtype=jnp.int32) * tmb, side="right"), N_EXPERTS - 1
    ).astype(jnp.int32)
    n_used = (pad_end[-1] // tmb).astype(jnp.int32).reshape(1)
    block_lo = jnp.arange(n_blocks, dtype=jnp.int32) * tmb - pad_start[block_expert]
    block_rows = jnp.clip(counts[block_expert] - block_lo, 0, tmb).astype(jnp.int32)
    row_tok = jnp.zeros((n_blocks * tmb,), jnp.int32).at[dest].set(jnp.arange(nk, dtype=jnp.int32) // TOP_K)
    return dest, block_expert, block_rows, n_used, row_tok


def _rope_tables(seq_len):
    half = HEAD_DIM // 2
    inv_freq = ROPE_THETA ** (-jnp.arange(half, dtype=F32) / half)
    ang = jnp.arange(seq_len, dtype=F32)[:, None] * inv_freq[None, :]
    cos, sin = jnp.cos(ang), jnp.sin(ang)
    return jnp.concatenate([cos, cos], axis=-1), jnp.concatenate([-sin, sin], axis=-1)


def kernel(x_prompt, x_sample, norm_mix, norm_ffn, na_w_qkv, na_q_gain, na_k_gain, na_rpb, na_w_o,
           da_w_qkv, da_q_gain, da_k_gain, da_w_o, ffn_w_gate, ffn_w_up, ffn_w_down,
           moe_w_router, moe_w_gate, moe_w_up, moe_w_down):
    d = x_prompt.shape[-1]
    trunks = []
    off = 0
    for xin in (x_prompt, x_sample):
        b, s, _ = xin.shape
        trunks.append((off, b, s))
        off += b * s
    n = off
    xa, xb = x_prompt.reshape(-1, d), x_sample.reshape(-1, d)
    depth = norm_mix.shape[0]
    assert depth == 2, "layer 0 reads the two request batches directly and layer 1 writes them separately"
    x = None

    for layer in range(depth):
        lj = layer // 2
        if layer % 2 == 0:
            heads = na_w_qkv.shape[2] // (3 * HEAD_DIM)
            per = heads * HEAD_DIM // PROJ_COL_BLOCK
            hg = jnp.concatenate([
                jnp.broadcast_to(na_q_gain[lj], (per, HEAD_DIM)),
                jnp.broadcast_to(na_k_gain[lj], (per, HEAD_DIM)),
                jnp.ones((per, HEAD_DIM), F32),
            ]).reshape(3 * per, 1, HEAD_DIM)
            qkv = _qkv_project(xa, xb, norm_mix[layer], na_w_qkv[lj].astype(BF16), hg, 3 * per, 2 * per)
            tables = _na_bias_tables(na_rpb[lj])
            att = _na_attention(qkv, tables, trunks, heads)
            x = _proj_residual(att, na_w_o[lj].astype(BF16), xa, xb)
            ne, dm, fe = moe_w_gate.shape[1:]
            wide = (moe_w_gate[lj].reshape(ne * dm, fe), moe_w_up[lj].reshape(ne * dm, fe))
            steps = (n // min(FFN_TOKEN_BLOCK, n)) * (ffn_w_gate.shape[2] // FFN_F_BLOCK)
            shadow = _side_cast_fits(wide, SIDE_CAST_ROWS_WIDE, steps)
            x, *cast = _dense_ffn(x, norm_ffn[layer], ffn_w_gate[lj].astype(BF16), ffn_w_up[lj].astype(BF16),
                                  ffn_w_down[lj].astype(BF16), side_srcs=wide if shadow else (),
                                  side_rows=SIDE_CAST_ROWS_WIDE)
            moe_gate_bf, moe_up_bf = cast if shadow else [w.astype(BF16) for w in wide]
        else:
            groups = len(DIL_CONFIGS)
            heads = da_w_qkv.shape[2] // (groups * 3 * HEAD_DIM)
            dils = [dil for _, dil in DIL_CONFIGS]
            ones = jnp.ones((HEAD_DIM,), F32)
            hg = jnp.stack([t for g in range(groups) for t in (da_q_gain[lj, g], da_k_gain[lj, g], ones)])
            hg = hg.reshape(groups * 3, 1, HEAD_DIM)
            (o0, b0, s0), (o1, b1, s1) = trunks
            tq = min(QKV_DIL_TOKEN_BLOCK, n)
            assert s0 % tq == 0 and s1 % tq == 0 and o1 % tq == 0
            nb0, p0, p1 = o1 // tq, s0 // tq, s1 // tq
            pos_map = lambda i: jnp.where(i < nb0, i % p0, (i - nb0) % p1)
            smax = max(s0, s1)
            rope = [jnp.stack([t.reshape(smax // tq, tq // dil, dil, HEAD_DIM).transpose(0, 2, 1, 3)
                               .reshape(smax, HEAD_DIM) for dil in dils]) for t in _rope_tables(smax)]
            narrow = (moe_w_down[lj].reshape(ne * fe, dm),)
            shadow = _side_cast_fits(narrow, SIDE_CAST_ROWS_NARROW, (n // tq) * groups * 3)
            qkvs = list(_qkv_project_dilated(
                x, norm_mix[layer], da_w_qkv[lj].astype(BF16), hg, rope, pos_map, dils, heads, tq,
                side_srcs=narrow if shadow else (), side_rows=SIDE_CAST_ROWS_NARROW))
            moe_down_bf = qkvs.pop() if shadow else narrow[0].astype(BF16)
            outs, lses = [], []
            for g in range(groups):
                o_g, lse_g = _band_attention(qkvs[g], trunks, g)
                outs.append(o_g)
                lses.append(lse_g)
            x = _merge_proj(outs, lses, da_w_o[lj].astype(BF16), x)
            h, idx, gate = _router(x, norm_ffn[layer], moe_w_router[lj])
            dest, block_expert, block_rows, n_used, row_tok = _routing_tables(idx, MOE_BLOCK)
            ys = _expert_ffn(h, block_expert, block_rows, n_used, row_tok, moe_gate_bf.reshape(ne, dm, fe),
                             moe_up_bf.reshape(ne, dm, fe), moe_down_bf.reshape(ne, fe, dm))
            ya, yb = _combine(x, ys, dest, gate, xa.shape[0])

    return ya.reshape(x_prompt.shape), yb.reshape(x_sample.shape)
```
